```python
import numpy as np
import jax
import jax.numpy as jnp
from jax import lax

D_MODEL = 1024
BATCH = 16
SEQ = 2048
DEPTH = 2

BRANCH_WIDTH = D_MODEL // 2
N_BRANCH = 3
FOX_HEAD_DIM = 64
FOX_HEADS = BRANCH_WIDTH // FOX_HEAD_DIM
FOX_BLOCK = 128
FOX_FORGET_BIAS = 2.0
ML_HEADS = 4
ML_DV = BRANCH_WIDTH // ML_HEADS
ML_DQK = ML_DV // 2
ML_CONV = 4
ML_CHUNK = 128
HG_HEADS = 4
HG_DV = BRANCH_WIDTH // HG_HEADS
HG_DK = 128
HG_CHUNK = 64
N_EXPERTS = 64
N_GROUPS = 8
TOPK_GROUPS = 4
TOP_K = 8
D_FF_EXPERT = D_MODEL // 4
D_FF_SHARED = D_MODEL // 4
ROUTE_SCALE = 2.5
MOE_BLOCK = 256
EPS = 1e-6

IN_SPLITS = (
    FOX_HEADS * FOX_HEAD_DIM, FOX_HEADS * FOX_HEAD_DIM, FOX_HEADS * FOX_HEAD_DIM, FOX_HEADS,
    2 * ML_HEADS * ML_DQK, ML_HEADS * ML_DV, ML_HEADS, ML_HEADS, ML_HEADS * ML_DV,
    HG_HEADS * HG_DK, HG_HEADS * HG_DK, HG_HEADS * HG_DV, HG_HEADS * HG_DV,
    N_BRANCH * D_MODEL,
)
D_IN = 3 * FOX_HEADS * FOX_HEAD_DIM + FOX_HEADS + 2 * ML_HEADS * ML_DQK + 2 * ML_HEADS * ML_DV + 2 * ML_HEADS + 2 * HG_HEADS * HG_DK + 2 * HG_HEADS * HG_DV + N_BRANCH * D_MODEL

kernel_name = "hybrid_fox_mlstm_hgrn2_moe_adaln"


def _rms(x):
    xf = x.astype(jnp.float32)
    return xf * lax.rsqrt(jnp.mean(xf * xf, axis=-1, keepdims=True) + EPS)


def _heads(a, n_heads):
    b, s, _ = a.shape
    return a.reshape(b, s, n_heads, -1).transpose(0, 2, 1, 3)


def _merge_heads(a):
    b, h, s, d = a.shape
    return a.transpose(0, 2, 1, 3).reshape(b, s, h * d)


def _to_chunks(a, chunk):
    b, h, s = a.shape[:3]
    return jnp.moveaxis(a.reshape(b, h, s // chunk, chunk, *a.shape[3:]), 2, 0)


def _from_chunks(a):
    nc, b, h, chunk = a.shape[:4]
    return jnp.moveaxis(a, 0, 2).reshape(b, h, nc * chunk, *a.shape[4:])


def _causal_depthwise_conv(x, w):
    width, ch = w.shape
    return lax.conv_general_dilated(
        x, w.astype(x.dtype)[:, None, :], window_strides=(1,), padding=[(width - 1, 0)],
        dimension_numbers=("NWC", "WIO", "NWC"), feature_group_count=ch)


def _forgetting_attention(q, k, v, logf):
    b, h, s, dh = q.shape
    cum = jnp.cumsum(logf, axis=-1)
    k_pos = jnp.arange(s)
    scale = dh ** -0.5

    def block(i):
        start = i * FOX_BLOCK
        qb = lax.dynamic_slice_in_dim(q, start, FOX_BLOCK, axis=2)
        fq = lax.dynamic_slice_in_dim(cum, start, FOX_BLOCK, axis=2)
        logits = jnp.einsum("bhtd,bhsd->bhts", qb, k).astype(jnp.float32) * scale
        logits = logits + fq[..., :, None] - cum[..., None, :]
        q_pos = start + jnp.arange(FOX_BLOCK)
        logits = jnp.where(k_pos[None, :] <= q_pos[:, None], logits, -jnp.inf)
        p = jax.nn.softmax(logits, axis=-1)
        return jnp.einsum("bhts,bhsd->bhtd", p.astype(v.dtype), v)

    out = lax.map(block, jnp.arange(s // FOX_BLOCK))
    return jnp.moveaxis(out, 0, 2).reshape(b, h, s, dh)


def _mlstm_chunkwise(q, k, v, i_pre, logf):
    b, h, s, dqk = q.shape
    dv = v.shape[-1]
    causal = jnp.tril(jnp.ones((ML_CHUNK, ML_CHUNK), dtype=bool))
    xs = tuple(_to_chunks(a, ML_CHUNK) for a in (q, k, v, i_pre, logf))

    def step(carry, xs_c):
        c_mat, n_vec, m_prev = carry
        qb, kb, vb, ib, fb = xs_c
        bcum = jnp.cumsum(fb, axis=-1)
        log_d = bcum[..., :, None] - bcum[..., None, :] + ib[..., None, :]
        log_d = jnp.where(causal, log_d, -jnp.inf)
        log_inter = bcum + m_prev[..., None]
        m_t = jnp.maximum(jnp.max(log_d, axis=-1), log_inter)
        w_intra = jnp.exp(log_d - m_t[..., None])
        w_inter = jnp.exp(log_inter - m_t)
        sc = jnp.einsum("bhtd,bhsd->bhts", qb, kb) * w_intra
        num = jnp.einsum("bhts,bhsv->bhtv", sc, vb) + w_inter[..., None] * jnp.einsum("bhtd,bhdv->bhtv", qb, c_mat)
        den = jnp.sum(sc, axis=-1) + w_inter * jnp.einsum("bhtd,bhd->bht", qb, n_vec)
        h_out = num / jnp.maximum(jnp.abs(den), jnp.exp(-m_t))[..., None]
        b_last = bcum[..., -1]
        log_w_in = b_last[..., None] - bcum + ib
        m_new = jnp.maximum(b_last + m_prev, jnp.max(log_w_in, axis=-1))
        w_in = jnp.exp(log_w_in - m_new[..., None])
        decay = jnp.exp(b_last + m_prev - m_new)
        c_mat = decay[..., None, None] * c_mat + jnp.einsum("bhs,bhsd,bhsv->bhdv", w_in, kb, vb)
        n_vec = decay[..., None] * n_vec + jnp.einsum("bhs,bhsd->bhd", w_in, kb)
        return (c_mat, n_vec, m_new), h_out

    init = (jnp.zeros((b, h, dqk, dv), jnp.float32), jnp.zeros((b, h, dqk), jnp.float32), jnp.zeros((b, h), jnp.float32))
    _, hs = lax.scan(step, init, xs)
    return _from_chunks(hs)


def _hgrn2_chunkwise(q, k, v, logf):
    b, h, s, dk = q.shape
    dv = v.shape[-1]
    causal = jnp.tril(jnp.ones((HG_CHUNK, HG_CHUNK), dtype=bool))
    xs = tuple(_to_chunks(a, HG_CHUNK) for a in (q, k, v, logf))

    def step(state, xs_c):
        qb, kb, vb, gb = xs_c
        a = jnp.cumsum(gb, axis=2)
        rel = a[:, :, :, None, :] - a[:, :, None, :, :]
        decay = jnp.exp(jnp.where(causal[:, :, None], rel, -jnp.inf))
        sc = jnp.einsum("bhtd,bhsd,bhtsd->bhts", qb, kb, decay)
        out = jnp.einsum("bhts,bhsv->bhtv", sc, vb) + jnp.einsum("bhtd,bhdv->bhtv", qb * jnp.exp(a), state)
        a_last = a[:, :, -1]
        state = jnp.exp(a_last)[..., None] * state + jnp.einsum("bhsd,bhsv->bhdv", kb * jnp.exp(a_last[:, :, None] - a), vb)
        return state, out

    _, outs = lax.scan(step, jnp.zeros((b, h, dk, dv), jnp.float32), xs)
    return _from_chunks(outs)


def _hybrid_mixer(h, w_in, fox_bf, fox_q_g, fox_k_g, mlstm_conv, mlstm_bi, mlstm_bf, mlstm_norm_g,
                  hgrn_lb, hgrn_norm_g, w_branch, w_out):
    dt = h.dtype
    b, s, _ = h.shape
    proj = h @ w_in
    split_idx = np.cumsum(IN_SPLITS)[:-1].tolist()
    (fq, fk, fv, ff, mqk, mv, mi, mf, mo, hf, hq, hi, hg, gates) = jnp.split(proj, split_idx, axis=-1)

    qh = (_rms(_heads(fq, FOX_HEADS)) * fox_q_g).astype(dt)
    kh = (_rms(_heads(fk, FOX_HEADS)) * fox_k_g).astype(dt)
    logf_fox = jax.nn.log_sigmoid(ff.astype(jnp.float32) + fox_bf).transpose(0, 2, 1)
    y_fox = _merge_heads(_forgetting_attention(qh, kh, _heads(fv, FOX_HEADS), logf_fox))

    qk = jax.nn.silu(_causal_depthwise_conv(mqk, mlstm_conv))
    mq, mk = jnp.split(qk, 2, axis=-1)
    i_pre = (mi.astype(jnp.float32) + mlstm_bi).transpose(0, 2, 1)
    logf_ml = jax.nn.log_sigmoid(mf.astype(jnp.float32) + mlstm_bf).transpose(0, 2, 1)
    hm = _mlstm_chunkwise(_heads(mq, ML_HEADS), _heads(mk, ML_HEADS) * (ML_DQK ** -0.5), _heads(mv, ML_HEADS), i_pre, logf_ml)
    y_ml = (jax.nn.sigmoid(mo.astype(jnp.float32)) * _merge_heads(_rms(hm)) * mlstm_norm_g).astype(dt)

    fz = _heads(hf, HG_HEADS).astype(jnp.float32)
    lb = hgrn_lb.reshape(HG_HEADS, HG_DK)[None, :, None, :]
    logf_hg = jnp.logaddexp(jnp.log(lb), jnp.log1p(-lb) + jax.nn.log_sigmoid(fz))
    k_hg = (1.0 - lb) * jax.nn.sigmoid(-fz)
    q_hg = jax.nn.silu(_heads(hq, HG_HEADS))
    oh = _hgrn2_chunkwise(q_hg, k_hg, _heads(hi, HG_HEADS), logf_hg)
    y_hg = (_merge_heads(_rms(oh)) * hgrn_norm_g * jax.nn.silu(hg.astype(jnp.float32))).astype(dt)

    g = jax.nn.sigmoid(gates).reshape(b, s, N_BRANCH, D_MODEL)
    merged = g[:, :, 0] * (y_fox @ w_branch[0]) + g[:, :, 1] * (y_ml @ w_branch[1]) + g[:, :, 2] * (y_hg @ w_branch[2])
    return merged @ w_out


def _moe_ffn(h, router_w, router_bias, w1, w3, w2, ws1, ws3, ws2):
    t, d = h.shape
    scores = jax.nn.sigmoid((h @ router_w).astype(jnp.float32))
    choice = scores + router_bias
    grp_score = lax.top_k(choice.reshape(t, N_GROUPS, N_EXPERTS // N_GROUPS), 2)[0].sum(-1)
    _, grp_idx = lax.top_k(grp_score, TOPK_GROUPS)
    grp_mask = jnp.any(grp_idx[:, :, None] == jnp.arange(N_GROUPS)[None, None, :], axis=1)
    masked = jnp.where(jnp.repeat(grp_mask, N_EXPERTS // N_GROUPS, axis=1), choice, -jnp.inf)
    _, idx = lax.top_k(masked, TOP_K)
    wts = jnp.take_along_axis(scores, idx, axis=-1)
    wts = wts / jnp.sum(wts, axis=-1, keepdims=True) * ROUTE_SCALE

    n_assign = t * TOP_K
    flat_e = idx.reshape(n_assign)
    order = jnp.argsort(flat_e)
    sorted_e = flat_e[order]
    tok = (order // TOP_K).astype(jnp.int32)
    counts = jax.ops.segment_sum(jnp.ones((n_assign,), jnp.int32), flat_e, num_segments=N_EXPERTS)
    padded = (counts + MOE_BLOCK - 1) // MOE_BLOCK * MOE_BLOCK
    pad_end = jnp.cumsum(padded)
    pad_start = pad_end - padded
    start = jnp.cumsum(counts) - counts
    dest = pad_start[sorted_e] + jnp.arange(n_assign) - start[sorted_e]
    n_blocks = -(-n_assign // MOE_BLOCK) + N_EXPERTS
    n_rows = n_blocks * MOE_BLOCK
    row_tok = jnp.full((n_rows,), t, jnp.int32).at[dest].set(tok)
    row_w = jnp.zeros((n_rows,), jnp.float32).at[dest].set(wts.reshape(n_assign)[order])
    blk_exp = jnp.minimum(jnp.searchsorted(pad_end, jnp.arange(n_blocks) * MOE_BLOCK, side="right"), N_EXPERTS - 1)
    h_pad = jnp.concatenate([h, jnp.zeros((1, d), h.dtype)], axis=0)

    def body(y, xs_b):
        rt, rw, e = xs_b
        xb = h_pad[rt]
        hid = jax.nn.silu(xb @ w1[e]) * (xb @ w3[e])
        ob = (hid @ w2[e]) * rw[:, None]
        return y.at[rt].add(ob.astype(y.dtype)), None

    y, _ = lax.scan(body, jnp.zeros((t + 1, d), h.dtype),
                    (row_tok.reshape(n_blocks, MOE_BLOCK), row_w.reshape(n_blocks, MOE_BLOCK), blk_exp))
    shared = (jax.nn.silu(h @ ws1) * (h @ ws3)) @ ws2
    return y[:t] + shared


def setup_inputs(seed: int = 0) -> dict:
    key = jax.random.key(seed)
    ks = jax.random.split(key, 32)
    f32 = jnp.float32
    D = D_MODEL

    def nrm(k, shape, scale):
        return jax.random.normal(k, shape, f32) * scale

    return {
        "x": nrm(ks[0], (BATCH, SEQ, D), 1.0),
        "c": nrm(ks[1], (BATCH, D), 1.0),
        "ada_w": nrm(ks[2], (DEPTH, D, 6 * D), 0.5 * D ** -0.5),
        "ada_b": nrm(ks[3], (DEPTH, 6 * D), 0.02),
        "norm1_g": 1.0 + nrm(ks[4], (DEPTH, D), 0.05),
        "norm2_g": 1.0 + nrm(ks[5], (DEPTH, D), 0.05),
        "w_in": nrm(ks[6], (DEPTH, D, D_IN), D ** -0.5),
        "fox_bf": FOX_FORGET_BIAS + nrm(ks[7], (DEPTH, FOX_HEADS), 0.5),
        "fox_q_g": 1.0 + nrm(ks[8], (DEPTH, FOX_HEAD_DIM), 0.05),
        "fox_k_g": 1.0 + nrm(ks[9], (DEPTH, FOX_HEAD_DIM), 0.05),
        "mlstm_conv": nrm(ks[10], (DEPTH, ML_CONV, 2 * ML_HEADS * ML_DQK), ML_CONV ** -0.5),
        "mlstm_bi": nrm(ks[11], (DEPTH, ML_HEADS), 0.1),
        "mlstm_bf": jnp.linspace(3.0, 6.0, ML_HEADS, dtype=f32)[None, :] + nrm(ks[12], (DEPTH, ML_HEADS), 0.1),
        "mlstm_norm_g": 1.0 + nrm(ks[13], (DEPTH, ML_HEADS * ML_DV), 0.05),
        "hgrn_lower_bounds": nrm(ks[14], (DEPTH, HG_HEADS * HG_DK), 0.1),
        "hgrn_norm_g": 1.0 + nrm(ks[15], (DEPTH, HG_HEADS * HG_DV), 0.05),
        "w_branch": nrm(ks[16], (DEPTH, N_BRANCH, BRANCH_WIDTH, D), BRANCH_WIDTH ** -0.5),
        "w_out": nrm(ks[17], (DEPTH, D, D), D ** -0.5),
        "router_w": nrm(ks[18], (DEPTH, D, N_EXPERTS), D ** -0.5),
        "router_bias": nrm(ks[19], (DEPTH, N_EXPERTS), 0.01),
        "exp_w1": nrm(ks[20], (DEPTH, N_EXPERTS, D, D_FF_EXPERT), D ** -0.5),
        "exp_w3": nrm(ks[21], (DEPTH, N_EXPERTS, D, D_FF_EXPERT), D ** -0.5),
        "exp_w2": nrm(ks[22], (DEPTH, N_EXPERTS, D_FF_EXPERT, D), D_FF_EXPERT ** -0.5),
        "sh_w1": nrm(ks[23], (DEPTH, D, D_FF_SHARED), D ** -0.5),
        "sh_w3": nrm(ks[24], (DEPTH, D, D_FF_SHARED), D ** -0.5),
        "sh_w2": nrm(ks[25], (DEPTH, D_FF_SHARED, D), D_FF_SHARED ** -0.5),
    }


def reference(x, c, ada_w, ada_b, norm1_g, norm2_g, w_in, fox_bf, fox_q_g, fox_k_g, mlstm_conv, mlstm_bi,
              mlstm_bf, mlstm_norm_g, hgrn_lower_bounds, hgrn_norm_g, w_branch, w_out, router_w, router_bias,
              exp_w1, exp_w3, exp_w2, sh_w1, sh_w3, sh_w2):
    b, s, d = x.shape
    dt = x.dtype
    cond = jax.nn.silu(c)
    lb_all = jnp.cumsum(jax.nn.softmax(hgrn_lower_bounds.astype(jnp.float32), axis=0), axis=0)
    lb_all = lb_all - lb_all[0]
    for l in range(DEPTH):
        mod = (cond @ ada_w[l] + ada_b[l])[:, None, :]
        sh1, sc1, g1, sh2, sc2, g2 = jnp.split(mod, 6, axis=-1)
        h = (_rms(x) * norm1_g[l] * (1.0 + sc1) + sh1).astype(dt)
        mixed = _hybrid_mixer(h, w_in[l], fox_bf[l], fox_q_g[l], fox_k_g[l], mlstm_conv[l], mlstm_bi[l],
                              mlstm_bf[l], mlstm_norm_g[l], lb_all[l], hgrn_norm_g[l], w_branch[l], w_out[l])
        x = x + g1 * mixed
        h = (_rms(x) * norm2_g[l] * (1.0 + sc2) + sh2).astype(dt)
        y = _moe_ffn(h.reshape(b * s, d), router_w[l], router_bias[l], exp_w1[l], exp_w3[l], exp_w2[l],
                     sh_w1[l], sh_w3[l], sh_w2[l]).reshape(b, s, d)
        x = x + g2 * y
    return x
```

```python
import functools

import jax
import jax.numpy as jnp
import numpy as np
from jax import lax
from jax.experimental import pallas as pl
from jax.experimental.pallas import tpu as pltpu

F32 = jnp.float32
BF16 = jnp.bfloat16
I32 = jnp.int32

LANES = 128
SUBLANES = 8
BF16_ROWS = 16
VMEM_BYTES = 64 * 1024 * 1024

D_MODEL = 1024
BRANCH_WIDTH = D_MODEL // 2
N_BRANCH = 3
FOX_HEAD_DIM = 64
FOX_HEADS = BRANCH_WIDTH // FOX_HEAD_DIM
ML_HEADS = 4
ML_DV = BRANCH_WIDTH // ML_HEADS
ML_DQK = ML_DV // 2
ML_CONV = 4
ML_CHUNK = 128
HG_HEADS = 4
HG_DK = 128
HG_BLOCK = 16
N_EXPERTS = 64
N_GROUPS = 8
GROUP_SIZE = N_EXPERTS // N_GROUPS
TOPK_GROUPS = 4
TOP_K = 8
D_FF = D_MODEL // 4
ROUTE_SCALE = 2.5
MOE_BLOCK = 256
EPS = 1e-6
NEG = -1e30

ROW_TILE = 512
ATTN_TILE = 256
SUB_TOKENS = 512
RUN_ALIGN = BF16_ROWS
SUB_ROWS = ((SUB_TOKENS * TOP_K + N_EXPERTS * (RUN_ALIGN - 1)) + 255) // 256 * 256
RUN_BITS = tuple(1 << b for b in range((SUB_TOKENS // RUN_ALIGN).bit_length()))


def _cp(sem, vmem_mb=48):
    return pltpu.CompilerParams(dimension_semantics=sem, vmem_limit_bytes=vmem_mb * 1024 * 1024)


def _dot(a, b):
    return jnp.dot(a, b, preferred_element_type=F32)


def _dot_nt(a, b):
    return lax.dot_general(a, b, (((1,), (1,)), ((), ())), preferred_element_type=F32)


def _dot_tn(a, b):
    return lax.dot_general(a, b, (((0,), (0,)), ((), ())), preferred_element_type=F32)


def _split3(x):
    hi = x.astype(BF16)
    r = x - hi.astype(F32)
    mid = r.astype(BF16)
    lo = (r - mid.astype(F32)).astype(BF16)
    return hi, mid, lo


def _tri_dot(tri, x):
    hi, mid, lo = _split3(x)
    return (_dot(tri, hi) + _dot(tri, mid)) + _dot(tri, lo)


def _dot_tri(x, tri):
    hi, mid, lo = _split3(x)
    return (_dot(hi, tri) + _dot(mid, tri)) + _dot(lo, tri)


def _log_sigmoid(x):
    return jnp.minimum(x, 0.0) - jnp.log1p(jnp.exp(-jnp.abs(x)))


def _silu(x):
    return x * jax.nn.sigmoid(x)


def _tri_incl(n, dtype=BF16):
    r = lax.broadcasted_iota(I32, (n, n), 0)
    c = lax.broadcasted_iota(I32, (n, n), 1)
    return jnp.where(c <= r, 1.0, 0.0).astype(dtype)


def _mod_kernel(c_ref, w_ref, b_ref, o_ref):
    cond = _silu(c_ref[...])
    hi, mid, lo = _split3(cond)
    w = w_ref[...]
    whi, wmid, wlo = _split3(w)
    acc = _dot(hi, whi) + (_dot(hi, wmid) + _dot(mid, whi))
    acc = acc + (_dot(mid, wmid) + _dot(hi, wlo) + _dot(lo, whi))
    o_ref[...] = acc + b_ref[...]


def adaln_mod(c, ada_w, ada_b):
    depth, d, n = ada_w.shape
    b = c.shape[0]
    tn = 1024
    return pl.pallas_call(
        _mod_kernel,
        grid=(depth, n // tn),
        in_specs=[
            pl.BlockSpec((b, d), lambda l, j: (0, 0)),
            pl.BlockSpec((None, d, tn), lambda l, j: (l, 0, j)),
            pl.BlockSpec((None, 1, tn), lambda l, j: (l, 0, j)),
        ],
        out_specs=pl.BlockSpec((None, b, tn), lambda l, j: (l, 0, j)),
        out_shape=jax.ShapeDtypeStruct((depth, b, n), F32),
        compiler_params=_cp(("parallel", "parallel")),
        name="adaln_mod",
    )(c, ada_w, ada_b.reshape(depth, 1, n))


def _norm_mod(x, g, sc, sh):
    ms = jnp.mean(x * x, axis=-1, keepdims=True)
    return x * lax.rsqrt(ms + EPS) * g * (1.0 + sc) + sh


def _norm_kernel(x_ref, g_ref, sc_ref, sh_ref, h_ref):
    h_ref[...] = _norm_mod(x_ref[...], g_ref[...], sc_ref[...], sh_ref[...]).astype(BF16)


def norm_modulate(x, g, sc, sh):
    b, s, d = x.shape
    tm = ROW_TILE
    return pl.pallas_call(
        _norm_kernel,
        grid=(b, s // tm),
        in_specs=[
            pl.BlockSpec((None, tm, d), lambda i, j: (i, j, 0)),
            pl.BlockSpec((1, d), lambda i, j: (0, 0)),
            pl.BlockSpec((None, 1, d), lambda i, j: (i, 0, 0)),
            pl.BlockSpec((None, 1, d), lambda i, j: (i, 0, 0)),
        ],
        out_specs=pl.BlockSpec((None, tm, d), lambda i, j: (i, j, 0)),
        out_shape=jax.ShapeDtypeStruct((b, s, d), BF16),
        compiler_params=_cp(("parallel", "parallel")),
        name="norm_modulate",
    )(x, g.reshape(1, d), sc.reshape(b, 1, d), sh.reshape(b, 1, d))


FOX_BIAS_PIECES = 3
FOX_QB_LANE = FOX_HEAD_DIM
FOX_KB_LANE = FOX_HEAD_DIM + FOX_BIAS_PIECES


def _pack_pieces(x):
    hi, mid, lo = _split3(x)
    p = hi.astype(F32) + pltpu.roll(mid.astype(F32), FOX_HEADS, axis=1) + pltpu.roll(lo.astype(F32), 2 * FOX_HEADS, axis=1)
    return p.astype(BF16)


def _fox_proj_kernel(h_ref, wq_ref, wk_ref, wv_ref, wf_ref, gq_ref, gk_ref, bf_ref, eq_ref, ek_ref, cq_ref, ck_ref,
                     q_out, k_out, v_out, carry_ref):
    @pl.when(pl.program_id(1) == 0)
    def _():
        carry_ref[...] = jnp.zeros_like(carry_ref)

    h = h_ref[...]
    tm = h.shape[0]
    ones_blk = jnp.full((LANES, LANES), 1.0 / FOX_HEAD_DIM, BF16)

    def head_norm(x):
        outs = []
        for hd in range(FOX_HEADS):
            xh = x[:, LANES * hd:LANES * (hd + 1)]
            ms = _dot((xh * xh).astype(BF16), ones_blk)
            outs.append(xh * lax.rsqrt(ms + EPS))
        return jnp.concatenate(outs, axis=1)

    lane = lax.broadcasted_iota(I32, (tm, LANES), 1)
    logf = jnp.where(lane < FOX_HEADS, _log_sigmoid(_dot(h, wf_ref[...]) + bf_ref[...]), 0.0)
    cs = _dot(_tri_incl(tm), _pack_pieces(logf))
    cum = cs + pltpu.roll(cs, LANES - FOX_HEADS, axis=1) + pltpu.roll(cs, LANES - 2 * FOX_HEADS, axis=1)
    cum = jnp.where(lane < FOX_HEADS, cum, 0.0) + carry_ref[...]
    carry_ref[...] = cum[tm - 1:tm, :]
    pieces = _pack_pieces(cum)

    qn = head_norm(_dot(h, wq_ref[...])) * gq_ref[...]
    q_out[...] = (qn + _dot(pieces, eq_ref[...]) + cq_ref[...]).astype(BF16)
    kn = head_norm(_dot(h, wk_ref[...])) * gk_ref[...]
    k_out[...] = (kn + _dot(pieces, ek_ref[...]) + ck_ref[...]).astype(BF16)
    v_out[...] = _dot(h, wv_ref[...]).astype(BF16)


def _pad_heads(w, heads, dim):
    lead = w.shape[:-1]
    w = w.reshape(*lead, heads, dim)
    w = jnp.pad(w, [(0, 0)] * len(lead) + [(0, 0), (0, LANES - dim)])
    return w.reshape(*lead, heads * LANES)


def _fox_constants():
    eq = np.zeros((LANES, FOX_HEADS * LANES), np.float32)
    ek = np.zeros((LANES, FOX_HEADS * LANES), np.float32)
    cq = np.zeros((1, FOX_HEADS * LANES), np.float32)
    ck = np.zeros((1, FOX_HEADS * LANES), np.float32)
    for hd in range(FOX_HEADS):
        for p in range(FOX_BIAS_PIECES):
            eq[p * FOX_HEADS + hd, LANES * hd + FOX_QB_LANE + p] = 1.0
            ek[p * FOX_HEADS + hd, LANES * hd + FOX_KB_LANE + p] = -1.0
            cq[0, LANES * hd + FOX_KB_LANE + p] = 1.0
            ck[0, LANES * hd + FOX_QB_LANE + p] = 1.0
    return jnp.asarray(eq, BF16), jnp.asarray(ek, BF16), jnp.asarray(cq), jnp.asarray(ck)


def fox_project(h, wq, wk, wv, wf, q_g, k_g, bf):
    b, s, d = h.shape
    tm = ROW_TILE
    hp = FOX_HEADS * LANES
    wq_p = _pad_heads(wq, FOX_HEADS, FOX_HEAD_DIM).astype(BF16)
    wk_p = _pad_heads(wk, FOX_HEADS, FOX_HEAD_DIM).astype(BF16)
    wf_p = jnp.pad(wf, ((0, 0), (0, LANES - FOX_HEADS))).astype(BF16)
    gq = _pad_heads(jnp.tile(q_g * (FOX_HEAD_DIM ** -0.5), FOX_HEADS)[None, :], FOX_HEADS, FOX_HEAD_DIM)
    gk = _pad_heads(jnp.tile(k_g, FOX_HEADS)[None, :], FOX_HEADS, FOX_HEAD_DIM)
    bf_p = jnp.pad(bf, (0, LANES - FOX_HEADS))[None, :]
    eq, ek, cq, ck = _fox_constants()
    const = lambda shape: pl.BlockSpec(shape, lambda i, j: (0,) * len(shape))
    row = lambda n: pl.BlockSpec((None, tm, n), lambda i, j: (i, j, 0))
    return pl.pallas_call(
        _fox_proj_kernel,
        grid=(b, s // tm),
        in_specs=[row(d), const((d, hp)), const((d, hp)), const((d, BRANCH_WIDTH)), const((d, LANES)),
                  const((1, hp)), const((1, hp)), const((1, LANES)), const((LANES, hp)), const((LANES, hp)),
                  const((1, hp)), const((1, hp))],
        out_specs=[row(hp), row(hp), row(BRANCH_WIDTH)],
        out_shape=[jax.ShapeDtypeStruct((b, s, hp), BF16), jax.ShapeDtypeStruct((b, s, hp), BF16),
                   jax.ShapeDtypeStruct((b, s, BRANCH_WIDTH), BF16)],
        scratch_shapes=[pltpu.VMEM((1, LANES), F32)],
        compiler_params=_cp(("parallel", "arbitrary")),
        name="fox_project",
    )(h, wq_p, wk_p, wv.astype(BF16), wf_p, gq, gk, bf_p, eq, ek, cq, ck)


def _fox_attn_kernel(q_ref, k_ref, v_ref, o_ref):
    i = pl.program_id(2)
    t = q_ref.shape[0]
    row = lax.broadcasted_iota(I32, (t, t), 0)
    col = lax.broadcasted_iota(I32, (t, t), 1)
    outs = []
    for a in range(2):
        q = q_ref[:, LANES * a:LANES * (a + 1)]

        def step(j, carry, masked):
            m, l, acc = carry
            start = pl.multiple_of(j * t, t)
            kb = k_ref[pl.ds(start, t), LANES * a:LANES * (a + 1)]
            vb = v_ref[pl.ds(start, t), :]
            s = _dot_nt(q, kb)
            if masked:
                s = jnp.where(col <= row, s, NEG)
            m_new = jnp.maximum(m, jnp.max(s, axis=-1, keepdims=True))
            p = jnp.exp(s - m_new)
            alpha = jnp.exp(m - m_new)
            l = alpha * l + jnp.sum(p, axis=-1, keepdims=True)
            acc = alpha * acc + _dot(p.astype(BF16), vb)
            return m_new, l, acc

        init = (jnp.full((t, 1), NEG, F32), jnp.zeros((t, 1), F32), jnp.zeros((t, LANES), F32))
        carry = lax.fori_loop(0, i, functools.partial(step, masked=False), init)
        m, l, acc = step(i, carry, True)
        outs.append(acc / l)
    lane = lax.broadcasted_iota(I32, (t, LANES), 1)
    o_ref[...] = jnp.where(lane < FOX_HEAD_DIM, outs[0], outs[1]).astype(BF16)


def fox_attention(qp, kp, v):
    b, s, hp = qp.shape
    t = ATTN_TILE
    pairs = FOX_HEADS // 2
    return pl.pallas_call(
        _fox_attn_kernel,
        grid=(b, pairs, s // t),
        in_specs=[
            pl.BlockSpec((None, t, 2 * LANES), lambda bi, p, i: (bi, i, p)),
            pl.BlockSpec((None, s, 2 * LANES), lambda bi, p, i: (bi, 0, p)),
            pl.BlockSpec((None, s, LANES), lambda bi, p, i: (bi, 0, p)),
        ],
        out_specs=pl.BlockSpec((None, t, LANES), lambda bi, p, i: (bi, i, p)),
        out_shape=jax.ShapeDtypeStruct((b, s, BRANCH_WIDTH), BF16),
        compiler_params=_cp(("parallel", "parallel", "arbitrary")),
        name="fox_attention",
    )(qp, kp, v)


CONV_HALO = SUBLANES


def _ml_proj_kernel(h_ref, wqk_ref, wv_ref, wo_ref, wg_ref, conv_ref, gb_ref, q_out, k_out, v_out, og_out, g_out, buf_ref):
    tm = h_ref.shape[0]
    half = ML_HEADS * LANES

    @pl.when(pl.program_id(1) == 0)
    def _():
        buf_ref[0:CONV_HALO, :] = jnp.zeros((CONV_HALO, 2 * half), F32)

    h = h_ref[...]
    buf_ref[CONV_HALO:CONV_HALO + tm, :] = _dot(h, wqk_ref[...])
    acc = jnp.zeros((tm, 2 * half), F32)
    for j in range(ML_CONV):
        off = CONV_HALO - (ML_CONV - 1) + j
        acc = acc + conv_ref[j:j + 1, :] * buf_ref[off:off + tm, :]
    buf_ref[0:CONV_HALO, :] = buf_ref[tm:tm + CONV_HALO, :]
    act = _silu(acc)
    q_out[...] = act[:, :half].astype(BF16)
    k_out[...] = (act[:, half:] * (ML_DQK ** -0.5)).astype(BF16)
    v_out[...] = _dot(h, wv_ref[...]).astype(BF16)
    og_out[...] = jax.nn.sigmoid(_dot(h, wo_ref[...])).astype(BF16)
    g = _dot(h, wg_ref[...]) + gb_ref[...]
    lane = lax.broadcasted_iota(I32, (tm, LANES), 1)
    g_out[...] = jnp.where(lane < ML_HEADS, g, _log_sigmoid(g))


def ml_project(h, wqk, wv, wi, wf, wo, conv, bi, bf):
    b, s, d = h.shape
    tm = ROW_TILE
    half = ML_HEADS * LANES
    nq = ML_HEADS * ML_DQK
    wqk_p = jnp.concatenate([_pad_heads(wqk[:, :nq], ML_HEADS, ML_DQK), _pad_heads(wqk[:, nq:], ML_HEADS, ML_DQK)], axis=1).astype(BF16)
    conv_p = jnp.concatenate([_pad_heads(conv[:, :nq], ML_HEADS, ML_DQK), _pad_heads(conv[:, nq:], ML_HEADS, ML_DQK)], axis=1)
    wg = jnp.pad(jnp.concatenate([wi, wf], axis=1), ((0, 0), (0, LANES - 2 * ML_HEADS))).astype(BF16)
    gb = jnp.pad(jnp.concatenate([bi, bf]), (0, LANES - 2 * ML_HEADS))[None, :]
    const = lambda shape: pl.BlockSpec(shape, lambda i, j: (0,) * len(shape))
    row = lambda n: pl.BlockSpec((None, tm, n), lambda i, j: (i, j, 0))
    return pl.pallas_call(
        _ml_proj_kernel,
        grid=(b, s // tm),
        in_specs=[row(d), const((d, 2 * half)), const((d, BRANCH_WIDTH)), const((d, BRANCH_WIDTH)), const((d, LANES)),
                  const((ML_CONV, 2 * half)), const((1, LANES))],
        out_specs=[row(half), row(half), row(BRANCH_WIDTH), row(BRANCH_WIDTH), row(LANES)],
        out_shape=[jax.ShapeDtypeStruct((b, s, half), BF16), jax.ShapeDtypeStruct((b, s, half), BF16),
                   jax.ShapeDtypeStruct((b, s, BRANCH_WIDTH), BF16), jax.ShapeDtypeStruct((b, s, BRANCH_WIDTH), BF16),
                   jax.ShapeDtypeStruct((b, s, LANES), F32)],
        scratch_shapes=[pltpu.VMEM((tm + CONV_HALO, 2 * half), F32)],
        compiler_params=_cp(("parallel", "arbitrary")),
        name="ml_project",
    )(h, wqk_p, wv.astype(BF16), wo.astype(BF16), wg, conv_p, gb)


def _mlstm_kernel(q_ref, k_ref, v_ref, og_ref, g_ref, gt_ref, ng_ref, o_ref, c_ref, n_ref, m_ref):
    s = q_ref.shape[0]
    L = ML_CHUNK
    c_ref[...] = jnp.zeros_like(c_ref)
    n_ref[...] = jnp.zeros_like(n_ref)
    m_ref[...] = jnp.zeros_like(m_ref)
    tril = _tri_incl(L)
    triu = tril.T
    row = lax.broadcasted_iota(I32, (L, L), 0)
    col = lax.broadcasted_iota(I32, (L, L), 1)
    causal = col <= row

    def chunk(c, carry):
        r0 = pl.multiple_of(c * L, L)
        g = g_ref[pl.ds(r0, L), :]
        gt = gt_ref[:, pl.ds(r0, L)]
        bc = _tri_dot(tril, g)
        br = _dot_tri(gt, triu)
        for hd in range(ML_HEADS):
            sl = slice(LANES * hd, LANES * (hd + 1))
            bcol = bc[:, ML_HEADS + hd:ML_HEADS + hd + 1]
            icol = g[:, hd:hd + 1]
            brow = br[ML_HEADS + hd:ML_HEADS + hd + 1, :]
            irow = gt[hd:hd + 1, :]
            m_prev = m_ref[hd][:, 0:1]
            log_d = jnp.where(causal, bcol - brow + irow, -jnp.inf)
            log_inter = bcol + m_prev
            m_t = jnp.maximum(jnp.max(log_d, axis=-1, keepdims=True), log_inter)
            w_intra = jnp.exp(log_d - m_t)
            w_inter = jnp.exp(log_inter - m_t)
            qc = q_ref[pl.ds(r0, L), sl]
            kc = k_ref[pl.ds(r0, L), sl]
            vc = v_ref[pl.ds(r0, L), sl]
            sc = _dot_nt(qc, kc) * w_intra
            cst = c_ref[hd]
            nst = n_ref[hd]
            num = _dot(sc.astype(BF16), vc) + w_inter * _dot(qc, cst.astype(BF16))
            den = jnp.sum(sc, axis=-1, keepdims=True) + w_inter * jnp.sum(qc.astype(F32) * nst, axis=-1, keepdims=True)
            hout = num / jnp.maximum(jnp.abs(den), jnp.exp(-m_t))
            b_last = bcol[L - 1:L, :]
            lw = b_last - bcol + icol
            m_new = jnp.maximum(b_last + m_prev, jnp.max(lw, axis=0, keepdims=True))
            w_in = jnp.exp(lw - m_new)
            decay = jnp.exp(b_last + m_prev - m_new)
            kw = kc.astype(F32) * w_in
            c_ref[hd] = decay * cst + _dot_tn(kw.astype(BF16), vc)
            n_ref[hd] = decay * nst + jnp.sum(kw, axis=0, keepdims=True)
            m_ref[hd] = jnp.broadcast_to(m_new, (1, LANES))
            ms = jnp.mean(hout * hout, axis=-1, keepdims=True)
            y = og_ref[pl.ds(r0, L), sl].astype(F32) * (hout * lax.rsqrt(ms + EPS)) * ng_ref[:, sl]
            o_ref[pl.ds(r0, L), sl] = y.astype(BF16)
        return carry

    lax.fori_loop(0, s // L, chunk, 0)


def mlstm(q, k, v, og, gates, norm_g):
    b, s, w = v.shape
    gt = jnp.swapaxes(gates[:, :, :2 * ML_HEADS], 1, 2)
    seq = lambda n: pl.BlockSpec((None, s, n), lambda i: (i, 0, 0))
    return pl.pallas_call(
        _mlstm_kernel,
        grid=(b,),
        in_specs=[seq(ML_HEADS * LANES), seq(ML_HEADS * LANES), seq(w), seq(w), seq(LANES),
                  pl.BlockSpec((None, 2 * ML_HEADS, s), lambda i: (i, 0, 0)),
                  pl.BlockSpec((1, w), lambda i: (0, 0))],
        out_specs=seq(w),
        out_shape=jax.ShapeDtypeStruct((b, s, w), BF16),
        scratch_shapes=[pltpu.VMEM((ML_HEADS, LANES, ML_DV), F32), pltpu.VMEM((ML_HEADS, 1, LANES), F32),
                        pltpu.VMEM((ML_HEADS, 1, LANES), F32)],
        compiler_params=_cp(("parallel",)),
        name="mlstm",
    )(q, k, v, og, gates, gt, norm_g.reshape(1, w))


def _hg_proj_kernel(h_ref, wf_ref, wq_ref, wi_ref, wg_ref, lb_ref, q_out, k_out, v_out, og_out, lf_out):
    h = h_ref[...]
    fz = _dot(h, wf_ref[...])
    log_lb = lb_ref[0:1, :]
    log_1m = lb_ref[1:2, :]
    one_m = lb_ref[2:3, :]
    a = log_lb
    bb = log_1m + _log_sigmoid(fz)
    lf_out[...] = jnp.maximum(a, bb) + jnp.log1p(jnp.exp(-jnp.abs(a - bb)))
    k_out[...] = (one_m * jax.nn.sigmoid(-fz)).astype(BF16)
    q_out[...] = _silu(_dot(h, wq_ref[...])).astype(BF16)
    v_out[...] = _dot(h, wi_ref[...]).astype(BF16)
    og_out[...] = _silu(_dot(h, wg_ref[...])).astype(BF16)


def hg_project(h, wf, wq, wi, wg, lb):
    b, s, d = h.shape
    tm = ROW_TILE
    w = BRANCH_WIDTH
    lbp = jnp.stack([jnp.log(lb), jnp.log1p(-lb), 1.0 - lb], axis=0)
    const = lambda shape: pl.BlockSpec(shape, lambda i, j: (0,) * len(shape))
    row = lambda n: pl.BlockSpec((None, tm, n), lambda i, j: (i, j, 0))
    return pl.pallas_call(
        _hg_proj_kernel,
        grid=(b, s // tm),
        in_specs=[row(d), const((d, w)), const((d, w)), const((d, w)), const((d, w)), const((3, w))],
        out_specs=[row(w), row(w), row(w), row(w), row(w)],
        out_shape=[jax.ShapeDtypeStruct((b, s, w), BF16)] * 4 + [jax.ShapeDtypeStruct((b, s, w), F32)],
        compiler_params=_cp(("parallel", "parallel")),
        name="hg_project",
    )(h, wf.astype(BF16), wq.astype(BF16), wi.astype(BF16), wg.astype(BF16), lbp)


def _hgrn_kernel(q_ref, k_ref, v_ref, og_ref, lf_ref, ng_ref, o_ref, st_ref):
    s = q_ref.shape[0]
    L = HG_BLOCK
    st_ref[...] = jnp.zeros_like(st_ref)
    tril = _tri_incl(L)
    rowi = lax.broadcasted_iota(I32, (L, LANES), 0)
    ones = jnp.ones((LANES, LANES), BF16)

    def block(c, carry):
        r0 = pl.multiple_of(c * L, L)
        a = _tri_dot(tril, lf_ref[pl.ds(r0, L), :])
        q = q_ref[pl.ds(r0, L), :].astype(F32)
        k = k_ref[pl.ds(r0, L), :].astype(F32)
        vb = v_ref[pl.ds(r0, L), :]
        v = vb.astype(F32)
        a_last = a[L - 1:L, :]
        st = st_ref[...]
        out = _dot_nt((q * jnp.exp(a)).astype(BF16), st.astype(BF16))
        kt = (k * jnp.exp(a_last - a)).astype(BF16)
        st_ref[...] = st * jnp.exp(a_last) + _dot_tn(vb, kt)
        zs = []
        for j in range(L):
            e = jnp.exp(jnp.where(rowi >= j, a - a[j:j + 1, :], -jnp.inf))
            zs.append(e * (q * k[j:j + 1, :]))
        r = _dot(jnp.concatenate(zs, axis=0).astype(BF16), ones)
        for j in range(L):
            out = out + r[L * j:L * (j + 1), :] * v[j:j + 1, :]
        ms = jnp.mean(out * out, axis=-1, keepdims=True)
        y = (out * lax.rsqrt(ms + EPS)) * ng_ref[...] * og_ref[pl.ds(r0, L), :].astype(F32)
        o_ref[pl.ds(r0, L), :] = y.astype(BF16)
        return carry

    lax.fori_loop(0, s // L, block, 0)


def hgrn(q, k, v, og, logf, norm_g):
    b, s, w = v.shape
    seq = pl.BlockSpec((None, s, LANES), lambda i, hd: (i, 0, hd))
    return pl.pallas_call(
        _hgrn_kernel,
        grid=(b, HG_HEADS),
        in_specs=[seq, seq, seq, seq, seq, pl.BlockSpec((1, LANES), lambda i, hd: (0, hd))],
        out_specs=seq,
        out_shape=jax.ShapeDtypeStruct((b, s, w), BF16),
        scratch_shapes=[pltpu.VMEM((LANES, HG_DK), F32)],
        compiler_params=_cp(("parallel", "parallel")),
        name="hgrn",
    )(q, k, v, og, logf, norm_g.reshape(1, w))


def _merge_kernel(x_ref, h_ref, yf_ref, ym_ref, yh_ref, wg_ref, wb_ref, wo_ref, g1_ref, n2_ref, sc2_ref, sh2_ref,
                  x_out, h2_out):
    d = x_ref.shape[1]
    h = h_ref[...]
    merged = None
    for br, y_ref in enumerate((yf_ref, ym_ref, yh_ref)):
        gate = jax.nn.sigmoid(_dot(h, wg_ref[:, d * br:d * (br + 1)]))
        term = gate * _dot(y_ref[...], wb_ref[br])
        merged = term if merged is None else merged + term
    mixed = _dot(merged.astype(BF16), wo_ref[...])
    x1 = x_ref[...] + g1_ref[...] * mixed
    x_out[...] = x1
    h2_out[...] = _norm_mod(x1, n2_ref[...], sc2_ref[...], sh2_ref[...]).astype(BF16)


def merge_branches(x, h, y_fox, y_ml, y_hg, w_gates, w_branch, w_out, g1, norm2_g, sc2, sh2):
    b, s, d = x.shape
    tm = ROW_TILE // 2
    w = BRANCH_WIDTH
    const = lambda shape: pl.BlockSpec(shape, lambda i, j: (0,) * len(shape))
    row = lambda n: pl.BlockSpec((None, tm, n), lambda i, j: (i, j, 0))
    per_b = pl.BlockSpec((None, 1, d), lambda i, j: (i, 0, 0))
    return pl.pallas_call(
        _merge_kernel,
        grid=(b, s // tm),
        in_specs=[row(d), row(d), row(w), row(w), row(w), const((d, N_BRANCH * d)), const((N_BRANCH, w, d)), const((d, d)),
                  per_b, const((1, d)), per_b, per_b],
        out_specs=[row(d), row(d)],
        out_shape=[jax.ShapeDtypeStruct((b, s, d), F32), jax.ShapeDtypeStruct((b, s, d), BF16)],
        compiler_params=_cp(("parallel", "parallel"), 56),
        name="merge_branches",
    )(x, h, y_fox, y_ml, y_hg, w_gates.astype(BF16), w_branch.astype(BF16), w_out.astype(BF16),
      g1.reshape(b, 1, d), norm2_g.reshape(1, d), sc2.reshape(b, 1, d), sh2.reshape(b, 1, d))


def _first_max(x, iota, size):
    m = jnp.max(x, axis=0, keepdims=True)
    idx = jnp.min(jnp.where(x == m, iota, size), axis=0, keepdims=True)
    return m, idx


def _route_kernel(h_ref, wr_ref, rb_ref, dest_out, w_out, cnt_out):
    n = h_ref.shape[0]
    scores = jax.nn.sigmoid(_dot_nt(wr_ref[...], h_ref[...]))
    choice = scores + rb_ref[...]
    e_iota = lax.broadcasted_iota(I32, (N_EXPERTS, n), 0)
    c3 = choice.reshape(N_GROUPS, GROUP_SIZE, n)
    i3 = lax.broadcasted_iota(I32, (N_GROUPS, GROUP_SIZE, n), 1)
    m1 = jnp.max(c3, axis=1, keepdims=True)
    i1 = jnp.min(jnp.where(c3 == m1, i3, GROUP_SIZE), axis=1, keepdims=True)
    m2 = jnp.max(jnp.where(i3 == i1, -jnp.inf, c3), axis=1, keepdims=True)
    gs = (m1 + m2).reshape(N_GROUPS, n)
    g_iota = lax.broadcasted_iota(I32, (N_GROUPS, n), 0)
    gsel = jnp.zeros((N_GROUPS, n), F32)
    for _ in range(TOPK_GROUPS):
        _, gi = _first_max(gs, g_iota, N_GROUPS)
        hit = g_iota == gi
        gsel = jnp.where(hit, 1.0, gsel)
        gs = jnp.where(hit, -jnp.inf, gs)
    gmask = jnp.broadcast_to(gsel.reshape(N_GROUPS, 1, n), (N_GROUPS, GROUP_SIZE, n)).reshape(N_EXPERTS, n)
    masked = jnp.where(gmask > 0.0, choice, -jnp.inf)
    picks = []
    sel = jnp.zeros((N_EXPERTS, n), F32)
    for _ in range(TOP_K):
        _, ei = _first_max(masked, e_iota, N_EXPERTS)
        hit = e_iota == ei
        picks.append(ei)
        sel = jnp.where(hit, 1.0, sel)
        masked = jnp.where(hit, -jnp.inf, masked)
    tr = lax.broadcasted_iota(I32, (n, n), 0)
    tc = lax.broadcasted_iota(I32, (n, n), 1)
    before = jnp.where(tr < tc, 1.0, 0.0).astype(BF16)
    pos = _dot(sel.astype(BF16), before)
    cnt = jnp.sum(sel, axis=1, keepdims=True)
    units = jnp.floor((cnt + (RUN_ALIGN - 1)) * (1.0 / RUN_ALIGN))
    er = lax.broadcasted_iota(I32, (N_EXPERTS, N_EXPERTS), 0)
    ec = lax.broadcasted_iota(I32, (N_EXPERTS, N_EXPERTS), 1)
    lower = jnp.where(ec < er, 1.0, 0.0).astype(BF16)
    off = _dot(lower, jnp.broadcast_to(units, (N_EXPERTS, LANES)).astype(BF16))[:, 0:1] * RUN_ALIGN
    dest = off + pos
    wsum = jnp.sum(scores * sel, axis=0, keepdims=True)
    dests, wts = [], []
    for ei in picks:
        hit = e_iota == ei
        dests.append(jnp.sum(jnp.where(hit, dest, 0.0), axis=0, keepdims=True))
        wts.append(jnp.sum(jnp.where(hit, scores, 0.0), axis=0, keepdims=True) / wsum * ROUTE_SCALE)
    dest_out[...] = jnp.concatenate(dests, axis=0).astype(I32)
    w_out[...] = jnp.concatenate(wts, axis=0)
    cnt_out[...] = jnp.broadcast_to(cnt, (N_EXPERTS, LANES)).astype(I32)


def moe_route(h2, router_w, router_bias):
    t, d = h2.shape
    n = SUB_TOKENS
    nsub = t // n
    dest, wts, cnt = pl.pallas_call(
        _route_kernel,
        grid=(nsub,),
        in_specs=[pl.BlockSpec((n, d), lambda i: (i, 0)), pl.BlockSpec((N_EXPERTS, d), lambda i: (0, 0)),
                  pl.BlockSpec((N_EXPERTS, 1), lambda i: (0, 0))],
        out_specs=[pl.BlockSpec((None, TOP_K, n), lambda i: (i, 0, 0)), pl.BlockSpec((None, TOP_K, n), lambda i: (i, 0, 0)),
                   pl.BlockSpec((None, N_EXPERTS, LANES), lambda i: (i, 0, 0))],
        out_shape=[jax.ShapeDtypeStruct((nsub, TOP_K, n), I32), jax.ShapeDtypeStruct((nsub, TOP_K, n), F32),
                   jax.ShapeDtypeStruct((nsub, N_EXPERTS, LANES), I32)],
        compiler_params=_cp(("parallel",)),
        name="moe_route",
    )(h2, router_w.T.astype(BF16), router_bias.reshape(N_EXPERTS, 1))
    return dest, wts, cnt[:, :, 0]


def _run_tables(cnt, n_blocks):
    units = (cnt + (RUN_ALIGN - 1)) // RUN_ALIGN
    src = jnp.cumsum(units, axis=1) - units
    per_block = MOE_BLOCK // RUN_ALIGN
    tot = jnp.sum(units, axis=0)
    tot_blocks = (tot + per_block - 1) // per_block
    blk_end = jnp.cumsum(tot_blocks)
    base = (blk_end - tot_blocks) * per_block
    dst = base[None, :] + jnp.cumsum(units, axis=0) - units
    n_used = blk_end[-1]
    tail_units = tot_blocks * per_block - tot
    tail_dst = base + tot
    blk = jnp.minimum(jnp.arange(n_blocks), n_used - 1)
    blk_exp = jnp.minimum(jnp.searchsorted(blk_end, blk, side="right"), N_EXPERTS - 1)
    return (src.reshape(-1).astype(I32), dst.reshape(-1).astype(I32), units.reshape(-1).astype(I32),
            blk_exp.astype(I32), n_used.astype(I32).reshape(1), tail_dst.astype(I32), tail_units.astype(I32))


def _max_blocks(t):
    nsub = t // SUB_TOKENS
    worst_units = t * TOP_K // RUN_ALIGN + nsub * N_EXPERTS
    per_block = MOE_BLOCK // RUN_ALIGN
    return -(-worst_units // per_block) + N_EXPERTS


def _run_copies(src_ref, dst_ref, units_ref, step, buf, hbm, sem, to_hbm, wait):
    def per_expert(e, carry):
        idx = step * N_EXPERTS + e
        u = units_ref[idx]
        s0 = src_ref[idx]
        d0 = dst_ref[idx]
        for bit in RUN_BITS:
            low = u & (bit - 1)

            @pl.when((u & bit) != 0)
            def _():
                rows = bit * RUN_ALIGN
                v = buf.at[pl.ds(pl.multiple_of((s0 + low) * RUN_ALIGN, RUN_ALIGN), rows)]
                g = hbm.at[pl.ds(pl.multiple_of((d0 + low) * RUN_ALIGN, RUN_ALIGN), rows)]
                cp = pltpu.make_async_copy(v, g, sem) if to_hbm else pltpu.make_async_copy(g, v, sem)
                if wait:
                    cp.wait()
                else:
                    cp.start()
        return carry

    lax.fori_loop(0, N_EXPERTS, per_expert, 0)


def _dispatch_kernel(src_ref, dst_ref, units_ref, tdst_ref, tunits_ref, h_ref, dest_ref, xs_in, xs_out, buf_ref, zero_ref, sem):
    del xs_in
    i = pl.program_id(0)
    nsub = pl.num_programs(0)
    slot = i % 2
    n = h_ref.shape[0]
    chunk = MOE_BLOCK

    for sl in range(2):
        @pl.when((slot == sl) & (i >= 2))
        def _():
            _run_copies(src_ref, dst_ref, units_ref, i - 2, buf_ref.at[sl], xs_out, sem.at[sl], True, True)

    h = h_ref[...]
    dest = dest_ref[...]
    r_iota = lax.broadcasted_iota(I32, (chunk, n), 0)
    for sl in range(2):
        @pl.when(slot == sl)
        def _():
            for c in range(SUB_ROWS // chunk):
                p = jnp.zeros((chunk, n), F32)
                for kk in range(TOP_K):
                    p = jnp.where(dest[kk:kk + 1, :] - c * chunk == r_iota, 1.0, p)
                buf_ref[sl, c * chunk:(c + 1) * chunk, :] = _dot(p.astype(BF16), h).astype(BF16)
            _run_copies(src_ref, dst_ref, units_ref, i, buf_ref.at[sl], xs_out, sem.at[sl], True, False)

    @pl.when(i == nsub - 1)
    def _():
        zero_ref[...] = jnp.zeros_like(zero_ref)

        def tails(e, wait):
            u = tunits_ref[e]
            d0 = tdst_ref[e]
            for bit in RUN_BITS:
                low = u & (bit - 1)

                @pl.when((u & bit) != 0)
                def _():
                    rows = bit * RUN_ALIGN
                    cp = pltpu.make_async_copy(zero_ref.at[pl.ds(0, rows)],
                                               xs_out.at[pl.ds(pl.multiple_of((d0 + low) * RUN_ALIGN, RUN_ALIGN), rows)], sem.at[2])
                    if wait:
                        cp.wait()
                    else:
                        cp.start()
            return wait

        lax.fori_loop(0, N_EXPERTS, lambda e, c: (tails(e, False), c)[1], 0)
        for sl in range(2):
            @pl.when((slot != sl) & (i >= 1))
            def _():
                _run_copies(src_ref, dst_ref, units_ref, i - 1, buf_ref.at[sl], xs_out, sem.at[sl], True, True)

            @pl.when(slot == sl)
            def _():
                _run_copies(src_ref, dst_ref, units_ref, i, buf_ref.at[sl], xs_out, sem.at[sl], True, True)
        lax.fori_loop(0, N_EXPERTS, lambda e, c: (tails(e, True), c)[1], 0)


def moe_dispatch(h2, dest, tables, n_rows):
    t, d = h2.shape
    n = SUB_TOKENS
    nsub = t // n
    src, dst, units, _, _, tail_dst, tail_units = tables
    xs0 = jnp.zeros((n_rows, d), BF16)
    grid_spec = pltpu.PrefetchScalarGridSpec(
        num_scalar_prefetch=5,
        grid=(nsub,),
        in_specs=[pl.BlockSpec((n, d), lambda i, *_: (i, 0)), pl.BlockSpec((None, TOP_K, n), lambda i, *_: (i, 0, 0)),
                  pl.BlockSpec(memory_space=pl.ANY)],
        out_specs=pl.BlockSpec(memory_space=pl.ANY),
        scratch_shapes=[pltpu.VMEM((2, SUB_ROWS, d), BF16), pltpu.VMEM((SUB_TOKENS, d), BF16), pltpu.SemaphoreType.DMA((3,))],
    )
    return pl.pallas_call(
        _dispatch_kernel,
        grid_spec=grid_spec,
        out_shape=jax.ShapeDtypeStruct((n_rows, d), BF16),
        input_output_aliases={7: 0},
        compiler_params=_cp(("arbitrary",)),
        name="moe_dispatch",
    )(src, dst, units, tail_dst, tail_units, h2, dest, xs0)


def _expert_kernel(blk_exp_ref, n_used_ref, x_ref, w1_ref, w3_ref, w2_ref, y_ref):
    b = pl.program_id(0)

    @pl.when(b < n_used_ref[0])
    def _():
        x = x_ref[...]
        hid = _silu(_dot(x, w1_ref[...].astype(BF16))) * _dot(x, w3_ref[...].astype(BF16))
        y_ref[...] = _dot(hid.astype(BF16), w2_ref[...].astype(BF16)).astype(BF16)

    @pl.when(b >= n_used_ref[0])
    def _():
        y_ref[...] = jnp.zeros_like(y_ref)


def moe_experts(xs, w1, w3, w2, tables, n_blocks):
    n_rows, d = xs.shape
    _, _, _, blk_exp, n_used, _, _ = tables
    f = w1.shape[-1]

    def x_map(b, be, nu):
        return (jnp.minimum(b, nu[0] - 1), 0)

    def w_map(b, be, nu):
        return (be[b], 0, 0)

    grid_spec = pltpu.PrefetchScalarGridSpec(
        num_scalar_prefetch=2,
        grid=(n_blocks,),
        in_specs=[pl.BlockSpec((MOE_BLOCK, d), x_map), pl.BlockSpec((None, d, f), w_map), pl.BlockSpec((None, d, f), w_map),
                  pl.BlockSpec((None, f, d), w_map)],
        out_specs=pl.BlockSpec((MOE_BLOCK, d), lambda b, be, nu: (b, 0)),
    )
    return pl.pallas_call(
        _expert_kernel,
        grid_spec=grid_spec,
        out_shape=jax.ShapeDtypeStruct((n_rows, d), BF16),
        compiler_params=_cp(("arbitrary",)),
        name="moe_experts",
    )(blk_exp, n_used, xs, w1, w3, w2)


def _combine_kernel(src_ref, dst_ref, units_ref, ys_ref, dcol_ref, wcol_ref, h_ref, x_ref, g2_ref, ws1_ref, ws3_ref, ws2_ref,
                    o_ref, buf_ref, sem):
    i = pl.program_id(0)
    nsub = pl.num_programs(0)
    slot = i % 2
    n = h_ref.shape[0]
    chunk = 512

    def fetch(step, sl):
        buf_ref[sl] = jnp.zeros((SUB_ROWS, buf_ref.shape[2]), BF16)
        _run_copies(src_ref, dst_ref, units_ref, step, buf_ref.at[sl], ys_ref, sem.at[sl], False, False)

    @pl.when(i == 0)
    def _():
        fetch(0, 0)

    for sl in range(2):
        @pl.when((slot != sl) & (i + 1 < nsub))
        def _():
            fetch(i + 1, sl)

    h = h_ref[...]
    shared = _dot((_silu(_dot(h, ws1_ref[...])) * _dot(h, ws3_ref[...])).astype(BF16), ws2_ref[...])
    dcol = dcol_ref[...]
    wcol = wcol_ref[...]
    lane = lax.broadcasted_iota(I32, (n, chunk), 1)
    for sl in range(2):
        @pl.when(slot == sl)
        def _():
            _run_copies(src_ref, dst_ref, units_ref, i, buf_ref.at[sl], ys_ref, sem.at[sl], False, True)
            acc = shared
            for c in range(SUB_ROWS // chunk):
                pw = jnp.zeros((n, chunk), F32)
                for kk in range(TOP_K):
                    pw = jnp.where(dcol[:, kk:kk + 1] - c * chunk == lane, wcol[:, kk:kk + 1], pw)
                acc = acc + _dot(pw.astype(BF16), buf_ref[sl, c * chunk:(c + 1) * chunk, :])
            o_ref[...] = x_ref[...] + g2_ref[...] * acc


def moe_combine(ys, dest, wts, h2, x1, g2, ws1, ws3, ws2, tables, seq):
    t, d = h2.shape
    n = SUB_TOKENS
    nsub = t // n
    src, dst, units = tables[0], tables[1], tables[2]
    dcol = jnp.swapaxes(dest, 1, 2)
    wcol = jnp.swapaxes(wts, 1, 2)
    per_seq = seq // n
    f = ws1.shape[-1]
    grid_spec = pltpu.PrefetchScalarGridSpec(
        num_scalar_prefetch=3,
        grid=(nsub,),
        in_specs=[pl.BlockSpec(memory_space=pl.ANY),
                  pl.BlockSpec((None, n, TOP_K), lambda i, *_: (i, 0, 0)), pl.BlockSpec((None, n, TOP_K), lambda i, *_: (i, 0, 0)),
                  pl.BlockSpec((n, d), lambda i, *_: (i, 0)), pl.BlockSpec((n, d), lambda i, *_: (i, 0)),
                  pl.BlockSpec((None, 1, d), lambda i, *_: (i // per_seq, 0, 0)),
                  pl.BlockSpec((d, f), lambda i, *_: (0, 0)), pl.BlockSpec((d, f), lambda i, *_: (0, 0)),
                  pl.BlockSpec((f, d), lambda i, *_: (0, 0))],
        out_specs=pl.BlockSpec((n, d), lambda i, *_: (i, 0)),
        scratch_shapes=[pltpu.VMEM((2, SUB_ROWS, d), BF16), pltpu.SemaphoreType.DMA((2,))],
    )
    return pl.pallas_call(
        _combine_kernel,
        grid_spec=grid_spec,
        out_shape=jax.ShapeDtypeStruct((t, d), F32),
        compiler_params=_cp(("arbitrary",)),
        name="moe_combine",
    )(src, dst, units, ys, dcol, wcol, h2, x1, g2, ws1.astype(BF16), ws3.astype(BF16), ws2.astype(BF16))


def _split_w_in(w):
    fh = FOX_HEADS * FOX_HEAD_DIM
    sizes = (fh, fh, fh, FOX_HEADS,
             2 * ML_HEADS * ML_DQK, ML_HEADS * ML_DV, ML_HEADS, ML_HEADS, ML_HEADS * ML_DV,
             HG_HEADS * HG_DK, HG_HEADS * HG_DK, BRANCH_WIDTH, BRANCH_WIDTH,
             N_BRANCH * D_MODEL)
    outs, o = [], 0
    for sz in sizes:
        outs.append(w[:, o:o + sz])
        o += sz
    return outs


def moe_ffn(h2, x1, g2, router_w, router_bias, w1, w3, w2, ws1, ws3, ws2, seq):
    t, d = h2.shape
    dest, wts, cnt = moe_route(h2, router_w, router_bias)
    n_blocks = _max_blocks(t)
    tables = _run_tables(cnt, n_blocks)
    xs = moe_dispatch(h2, dest, tables, n_blocks * MOE_BLOCK)
    ys = moe_experts(xs, w1, w3, w2, tables, n_blocks)
    return moe_combine(ys, dest, wts, h2, x1, g2, ws1, ws3, ws2, tables, seq)


def kernel(x, c, ada_w, ada_b, norm1_g, norm2_g, w_in, fox_bf, fox_q_g, fox_k_g, mlstm_conv, mlstm_bi, mlstm_bf, mlstm_norm_g, hgrn_lower_bounds, hgrn_norm_g, w_branch, w_out, router_w, router_bias, exp_w1, exp_w3, exp_w2, sh_w1, sh_w3, sh_w2):
    b, s, d = x.shape
    depth = ada_w.shape[0]
    mod = adaln_mod(c, ada_w, ada_b)
    lb_all = jnp.cumsum(jax.nn.softmax(hgrn_lower_bounds.astype(F32), axis=0), axis=0)
    lb_all = lb_all - lb_all[0]
    for l in range(depth):
        sh1, sc1, g1, sh2, sc2, g2 = [mod[l][:, d * j:d * (j + 1)] for j in range(6)]
        (wfq, wfk, wfv, wff, wmqk, wmv, wmi, wmf, wmo, whf, whq, whi, whg, wgates) = _split_w_in(w_in[l])
        h = norm_modulate(x, norm1_g[l], sc1, sh1)
        qp, kp, fv = fox_project(h, wfq, wfk, wfv, wff, fox_q_g[l], fox_k_g[l], fox_bf[l])
        y_fox = fox_attention(qp, kp, fv)
        mq, mk, mv, mog, mgates = ml_project(h, wmqk, wmv, wmi, wmf, wmo, mlstm_conv[l], mlstm_bi[l], mlstm_bf[l])
        y_ml = mlstm(mq, mk, mv, mog, mgates, mlstm_norm_g[l])
        hq, hk, hv, hog, hlf = hg_project(h, whf, whq, whi, whg, lb_all[l])
        y_hg = hgrn(hq, hk, hv, hog, hlf, hgrn_norm_g[l])
        x1, h2 = merge_branches(x, h, y_fox, y_ml, y_hg, wgates, w_branch[l], w_out[l], g1, norm2_g[l], sc2, sh2)
        x = moe_ffn(h2.reshape(b * s, d), x1.reshape(b * s, d), g2.reshape(b, 1, d), router_w[l], router_bias[l],
                    exp_w1[l], exp_w3[l], exp_w2[l], sh_w1[l], sh_w3[l], sh_w2[l], s).reshape(b, s, d)
    return x
```

```python
import functools

import jax
import jax.numpy as jnp
import numpy as np
from jax import lax
from jax.experimental import pallas as pl
from jax.experimental.pallas import tpu as pltpu

F32 = jnp.float32
BF16 = jnp.bfloat16
I32 = jnp.int32

LANES = 128
SUBLANES = 8
BF16_ROWS = 16
VMEM_BYTES = 64 * 1024 * 1024

D_MODEL = 1024
BRANCH_WIDTH = D_MODEL // 2
N_BRANCH = 3
FOX_HEAD_DIM = 64
FOX_HEADS = BRANCH_WIDTH // FOX_HEAD_DIM
ML_HEADS = 4
ML_DV = BRANCH_WIDTH // ML_HEADS
ML_DQK = ML_DV // 2
ML_CONV = 4
ML_CHUNK = 128
HG_HEADS = 4
HG_DK = 128
HG_CHUNK = 128
N_EXPERTS = 64
N_GROUPS = 8
GROUP_SIZE = N_EXPERTS // N_GROUPS
TOPK_GROUPS = 4
TOP_K = 8
D_FF = D_MODEL // 4
ROUTE_SCALE = 2.5
MOE_BLOCK = 1024
SORT_CHUNK = 256
EPS = 1e-6
NEG = -1e30

ROW_TILE = 512
ATTN_TILE = 256
SUB_TOKENS = 512
RUN_ALIGN = BF16_ROWS
SUB_ROWS = ((SUB_TOKENS * TOP_K + N_EXPERTS * (RUN_ALIGN - 1)) + 255) // 256 * 256
RUN_BITS = tuple(1 << b for b in range((SUB_TOKENS // RUN_ALIGN).bit_length()))
DEST_RADIX = 64
DEST_NONE = DEST_RADIX * 127
assert SUB_ROWS <= DEST_NONE


def _cp(sem, vmem_mb=48):
    return pltpu.CompilerParams(dimension_semantics=sem, vmem_limit_bytes=vmem_mb * 1024 * 1024)


def _dot(a, b):
    return jnp.dot(a, b, preferred_element_type=F32)


def _dot_nt(a, b):
    return lax.dot_general(a, b, (((1,), (1,)), ((), ())), preferred_element_type=F32)


def _dot_tn(a, b):
    return lax.dot_general(a, b, (((0,), (0,)), ((), ())), preferred_element_type=F32)


def _split3(x):
    hi = x.astype(BF16)
    r = x - hi.astype(F32)
    mid = r.astype(BF16)
    lo = (r - mid.astype(F32)).astype(BF16)
    return hi, mid, lo


def _tri_dot(tri, x):
    hi, mid, lo = _split3(x)
    return (_dot(tri, hi) + _dot(tri, mid)) + _dot(tri, lo)


def _dot_tri(x, tri):
    hi, mid, lo = _split3(x)
    return (_dot(hi, tri) + _dot(mid, tri)) + _dot(lo, tri)


def _log_sigmoid(x):
    return jnp.minimum(x, 0.0) - jnp.log1p(jnp.exp(-jnp.abs(x)))


def _silu(x):
    return x * jax.nn.sigmoid(x)


def _tri_incl(n, dtype=BF16):
    r = lax.broadcasted_iota(I32, (n, n), 0)
    c = lax.broadcasted_iota(I32, (n, n), 1)
    return jnp.where(c <= r, 1.0, 0.0).astype(dtype)


def _mod_kernel(c_ref, w_ref, b_ref, o_ref):
    cond = _silu(c_ref[...])
    hi, mid, lo = _split3(cond)
    w = w_ref[...]
    whi, wmid, wlo = _split3(w)
    acc = _dot(hi, whi) + (_dot(hi, wmid) + _dot(mid, whi))
    acc = acc + (_dot(mid, wmid) + _dot(hi, wlo) + _dot(lo, whi))
    o_ref[...] = acc + b_ref[...]


def adaln_mod(c, ada_w, ada_b):
    depth, d, n = ada_w.shape
    b = c.shape[0]
    tn = 1024
    return pl.pallas_call(
        _mod_kernel,
        grid=(depth, n // tn),
        in_specs=[
            pl.BlockSpec((b, d), lambda l, j: (0, 0)),
            pl.BlockSpec((None, d, tn), lambda l, j: (l, 0, j)),
            pl.BlockSpec((None, 1, tn), lambda l, j: (l, 0, j)),
        ],
        out_specs=pl.BlockSpec((None, b, tn), lambda l, j: (l, 0, j)),
        out_shape=jax.ShapeDtypeStruct((depth, b, n), F32),
        compiler_params=_cp(("parallel", "parallel")),
        name="adaln_mod",
    )(c, ada_w, ada_b.reshape(depth, 1, n))


def _norm_mod(x, g, sc, sh):
    ms = jnp.mean(x * x, axis=-1, keepdims=True)
    return x * lax.rsqrt(ms + EPS) * g * (1.0 + sc) + sh


def _norm_kernel(x_ref, g_ref, sc_ref, sh_ref, h_ref):
    h_ref[...] = _norm_mod(x_ref[...], g_ref[...], sc_ref[...], sh_ref[...]).astype(BF16)


def norm_modulate(x, g, sc, sh):
    b, s, d = x.shape
    tm = ROW_TILE
    return pl.pallas_call(
        _norm_kernel,
        grid=(b, s // tm),
        in_specs=[
            pl.BlockSpec((None, tm, d), lambda i, j: (i, j, 0)),
            pl.BlockSpec((1, d), lambda i, j: (0, 0)),
            pl.BlockSpec((None, 1, d), lambda i, j: (i, 0, 0)),
            pl.BlockSpec((None, 1, d), lambda i, j: (i, 0, 0)),
        ],
        out_specs=pl.BlockSpec((None, tm, d), lambda i, j: (i, j, 0)),
        out_shape=jax.ShapeDtypeStruct((b, s, d), BF16),
        compiler_params=_cp(("parallel", "parallel")),
        name="norm_modulate",
    )(x, g.reshape(1, d), sc.reshape(b, 1, d), sh.reshape(b, 1, d))


FOX_BIAS_PIECES = 3
FOX_QB_LANE = FOX_HEAD_DIM
FOX_KB_LANE = FOX_HEAD_DIM + FOX_BIAS_PIECES


def _pack_pieces(x):
    hi, mid, lo = _split3(x)
    p = hi.astype(F32) + pltpu.roll(mid.astype(F32), FOX_HEADS, axis=1) + pltpu.roll(lo.astype(F32), 2 * FOX_HEADS, axis=1)
    return p.astype(BF16)


def _fox_proj_kernel(h_ref, wq_ref, wk_ref, wvt_ref, wf_ref, gq_ref, gk_ref, bf_ref, eq_ref, ek_ref, cq_ref, ck_ref,
                     q_out, k_out, vt_out, carry_ref):
    @pl.when(pl.program_id(1) == 0)
    def _():
        carry_ref[...] = jnp.zeros_like(carry_ref)

    h = h_ref[...]
    tm = h.shape[0]
    ones_blk = jnp.full((LANES, LANES), 1.0 / FOX_HEAD_DIM, BF16)

    def head_norm(x):
        outs = []
        for hd in range(FOX_HEADS):
            xh = x[:, LANES * hd:LANES * (hd + 1)]
            ms = _dot((xh * xh).astype(BF16), ones_blk)
            outs.append(xh * lax.rsqrt(ms + EPS))
        return jnp.concatenate(outs, axis=1)

    lane = lax.broadcasted_iota(I32, (tm, LANES), 1)
    logf = jnp.where(lane < FOX_HEADS, _log_sigmoid(_dot(h, wf_ref[...]) + bf_ref[...]), 0.0)
    cs = _dot(_tri_incl(tm), _pack_pieces(logf))
    cum = cs + pltpu.roll(cs, LANES - FOX_HEADS, axis=1) + pltpu.roll(cs, LANES - 2 * FOX_HEADS, axis=1)
    cum = jnp.where(lane < FOX_HEADS, cum, 0.0) + carry_ref[...]
    carry_ref[...] = cum[tm - 1:tm, :]
    pieces = _pack_pieces(cum)

    qn = head_norm(_dot(h, wq_ref[...])) * gq_ref[...]
    q_out[...] = (qn + _dot(pieces, eq_ref[...]) + cq_ref[...]).astype(BF16)
    kn = head_norm(_dot(h, wk_ref[...])) * gk_ref[...]
    k_out[...] = (kn + _dot(pieces, ek_ref[...]) + ck_ref[...]).astype(BF16)
    vt_out[...] = _dot_nt(wvt_ref[...], h).astype(BF16)


def _pad_heads(w, heads, dim):
    lead = w.shape[:-1]
    w = w.reshape(*lead, heads, dim)
    w = jnp.pad(w, [(0, 0)] * len(lead) + [(0, 0), (0, LANES - dim)])
    return w.reshape(*lead, heads * LANES)


def _fox_constants():
    eq = np.zeros((LANES, FOX_HEADS * LANES), np.float32)
    ek = np.zeros((LANES, FOX_HEADS * LANES), np.float32)
    cq = np.zeros((1, FOX_HEADS * LANES), np.float32)
    ck = np.zeros((1, FOX_HEADS * LANES), np.float32)
    for hd in range(FOX_HEADS):
        for p in range(FOX_BIAS_PIECES):
            eq[p * FOX_HEADS + hd, LANES * hd + FOX_QB_LANE + p] = 1.0
            ek[p * FOX_HEADS + hd, LANES * hd + FOX_KB_LANE + p] = -1.0
            cq[0, LANES * hd + FOX_KB_LANE + p] = 1.0
            ck[0, LANES * hd + FOX_QB_LANE + p] = 1.0
    return jnp.asarray(eq, BF16), jnp.asarray(ek, BF16), jnp.asarray(cq), jnp.asarray(ck)


def fox_project(h, wq, wk, wv, wf, q_g, k_g, bf):
    b, s, d = h.shape
    tm = ROW_TILE
    hp = FOX_HEADS * LANES
    wq_p = _pad_heads(wq, FOX_HEADS, FOX_HEAD_DIM).astype(BF16)
    wk_p = _pad_heads(wk, FOX_HEADS, FOX_HEAD_DIM).astype(BF16)
    wf_p = jnp.pad(wf, ((0, 0), (0, LANES - FOX_HEADS))).astype(BF16)
    gq = _pad_heads(jnp.tile(q_g * (FOX_HEAD_DIM ** -0.5), FOX_HEADS)[None, :], FOX_HEADS, FOX_HEAD_DIM)
    gk = _pad_heads(jnp.tile(k_g, FOX_HEADS)[None, :], FOX_HEADS, FOX_HEAD_DIM)
    bf_p = jnp.pad(bf, (0, LANES - FOX_HEADS))[None, :]
    eq, ek, cq, ck = _fox_constants()
    const = lambda shape: pl.BlockSpec(shape, lambda i, j: (0,) * len(shape))
    row = lambda n: pl.BlockSpec((None, tm, n), lambda i, j: (i, j, 0))
    return pl.pallas_call(
        _fox_proj_kernel,
        grid=(b, s // tm),
        in_specs=[row(d), const((d, hp)), const((d, hp)), const((BRANCH_WIDTH, d)), const((d, LANES)),
                  const((1, hp)), const((1, hp)), const((1, LANES)), const((LANES, hp)), const((LANES, hp)),
                  const((1, hp)), const((1, hp))],
        out_specs=[row(hp), row(hp), pl.BlockSpec((None, BRANCH_WIDTH, tm), lambda i, j: (i, 0, j))],
        out_shape=[jax.ShapeDtypeStruct((b, s, hp), BF16), jax.ShapeDtypeStruct((b, s, hp), BF16),
                   jax.ShapeDtypeStruct((b, BRANCH_WIDTH, s), BF16)],
        scratch_shapes=[pltpu.VMEM((1, LANES), F32)],
        compiler_params=_cp(("parallel", "arbitrary")),
        name="fox_project",
    )(h, wq_p, wk_p, wv.T.astype(BF16), wf_p, gq, gk, bf_p, eq, ek, cq, ck)


def _fox_attn_kernel(q_ref, k_ref, vt_ref, o_ref):
    i = pl.program_id(2)
    t = q_ref.shape[0]
    krow = lax.broadcasted_iota(I32, (t, t), 0)
    qcol = lax.broadcasted_iota(I32, (t, t), 1)
    qs = [q_ref[:, LANES * a:LANES * (a + 1)] for a in range(2)]

    def step(j, carry, masked):
        start = pl.multiple_of(j * t, t)
        vt = vt_ref[:, pl.ds(start, t)]
        new = []
        for a in range(2):
            m, l, acc = carry[a]
            kb = k_ref[pl.ds(start, t), LANES * a:LANES * (a + 1)]
            s = _dot_nt(kb, qs[a])
            if masked:
                s = jnp.where(krow <= qcol, s, NEG)
            m_new = jnp.maximum(m, jnp.max(s, axis=0, keepdims=True))
            p = jnp.exp(s - m_new)
            alpha = jnp.exp(m - m_new)
            l = alpha * l + jnp.sum(p, axis=0, keepdims=True)
            acc = alpha * acc + _dot(vt, p.astype(BF16))
            new.append((m_new, l, acc))
        return tuple(new)

    init = tuple((jnp.full((1, t), NEG, F32), jnp.zeros((1, t), F32), jnp.zeros((LANES, t), F32)) for _ in range(2))
    carry = lax.fori_loop(0, i, functools.partial(step, masked=False), init)
    (_, l0, a0), (_, l1, a1) = step(i, carry, True)
    sub = lax.broadcasted_iota(I32, (LANES, t), 0)
    out_t = jnp.where(sub < FOX_HEAD_DIM, a0 / l0, a1 / l1)
    o_ref[...] = out_t.T.astype(BF16)


def fox_attention(qp, kp, vt):
    b, s, hp = qp.shape
    t = ATTN_TILE
    pairs = FOX_HEADS // 2
    return pl.pallas_call(
        _fox_attn_kernel,
        grid=(b, pairs, s // t),
        in_specs=[
            pl.BlockSpec((None, t, 2 * LANES), lambda bi, p, i: (bi, i, p)),
            pl.BlockSpec((None, s, 2 * LANES), lambda bi, p, i: (bi, 0, p)),
            pl.BlockSpec((None, LANES, s), lambda bi, p, i: (bi, p, 0)),
        ],
        out_specs=pl.BlockSpec((None, t, LANES), lambda bi, p, i: (bi, i, p)),
        out_shape=jax.ShapeDtypeStruct((b, s, BRANCH_WIDTH), BF16),
        compiler_params=_cp(("parallel", "parallel", "arbitrary")),
        name="fox_attention",
    )(qp, kp, vt)


CONV_HALO = SUBLANES


def _ml_proj_kernel(h_ref, wqk_ref, wv_ref, wo_ref, wg_ref, conv_ref, gb_ref, q_out, k_out, v_out, og_out, g_out, buf_ref):
    tm = h_ref.shape[0]
    half = ML_HEADS * LANES

    @pl.when(pl.program_id(1) == 0)
    def _():
        buf_ref[0:CONV_HALO, :] = jnp.zeros((CONV_HALO, 2 * half), F32)

    h = h_ref[...]
    buf_ref[CONV_HALO:CONV_HALO + tm, :] = _dot(h, wqk_ref[...])
    acc = jnp.zeros((tm, 2 * half), F32)
    for j in range(ML_CONV):
        off = CONV_HALO - (ML_CONV - 1) + j
        acc = acc + conv_ref[j:j + 1, :] * buf_ref[off:off + tm, :]
    buf_ref[0:CONV_HALO, :] = buf_ref[tm:tm + CONV_HALO, :]
    act = _silu(acc)
    q_out[...] = act[:, :half].astype(BF16)
    k_out[...] = (act[:, half:] * (ML_DQK ** -0.5)).astype(BF16)
    v_out[...] = _dot(h, wv_ref[...]).astype(BF16)
    og_out[...] = jax.nn.sigmoid(_dot(h, wo_ref[...])).astype(BF16)
    g = _dot(h, wg_ref[...]) + gb_ref[...]
    lane = lax.broadcasted_iota(I32, (tm, LANES), 1)
    g_out[...] = jnp.where(lane < ML_HEADS, g, _log_sigmoid(g))


def ml_project(h, wqk, wv, wi, wf, wo, conv, bi, bf):
    b, s, d = h.shape
    tm = ROW_TILE
    half = ML_HEADS * LANES
    nq = ML_HEADS * ML_DQK
    wqk_p = jnp.concatenate([_pad_heads(wqk[:, :nq], ML_HEADS, ML_DQK), _pad_heads(wqk[:, nq:], ML_HEADS, ML_DQK)], axis=1).astype(BF16)
    conv_p = jnp.concatenate([_pad_heads(conv[:, :nq], ML_HEADS, ML_DQK), _pad_heads(conv[:, nq:], ML_HEADS, ML_DQK)], axis=1)
    wg = jnp.pad(jnp.concatenate([wi, wf], axis=1), ((0, 0), (0, LANES - 2 * ML_HEADS))).astype(BF16)
    gb = jnp.pad(jnp.concatenate([bi, bf]), (0, LANES - 2 * ML_HEADS))[None, :]
    const = lambda shape: pl.BlockSpec(shape, lambda i, j: (0,) * len(shape))
    row = lambda n: pl.BlockSpec((None, tm, n), lambda i, j: (i, j, 0))
    return pl.pallas_call(
        _ml_proj_kernel,
        grid=(b, s // tm),
        in_specs=[row(d), const((d, 2 * half)), const((d, BRANCH_WIDTH)), const((d, BRANCH_WIDTH)), const((d, LANES)),
                  const((ML_CONV, 2 * half)), const((1, LANES))],
        out_specs=[row(half), row(half), row(BRANCH_WIDTH), row(BRANCH_WIDTH), row(LANES)],
        out_shape=[jax.ShapeDtypeStruct((b, s, half), BF16), jax.ShapeDtypeStruct((b, s, half), BF16),
                   jax.ShapeDtypeStruct((b, s, BRANCH_WIDTH), BF16), jax.ShapeDtypeStruct((b, s, BRANCH_WIDTH), BF16),
                   jax.ShapeDtypeStruct((b, s, LANES), F32)],
        scratch_shapes=[pltpu.VMEM((tm + CONV_HALO, 2 * half), F32)],
        compiler_params=_cp(("parallel", "arbitrary")),
        name="ml_project",
    )(h, wqk_p, wv.astype(BF16), wo.astype(BF16), wg, conv_p, gb)


def _mlstm_kernel(q_ref, k_ref, v_ref, og_ref, g_ref, gt_ref, ng_ref, o_ref, c_ref, n_ref, m_ref):
    s = q_ref.shape[0]
    L = ML_CHUNK
    c_ref[...] = jnp.zeros_like(c_ref)
    n_ref[...] = jnp.zeros_like(n_ref)
    m_ref[...] = jnp.zeros_like(m_ref)
    tril = _tri_incl(L)
    triu = tril.T
    row = lax.broadcasted_iota(I32, (L, L), 0)
    col = lax.broadcasted_iota(I32, (L, L), 1)
    causal = col <= row

    def chunk(c, carry):
        r0 = pl.multiple_of(c * L, L)
        g = g_ref[pl.ds(r0, L), :]
        gt = gt_ref[:, pl.ds(r0, L)]
        bc = _tri_dot(tril, g)
        br = _dot_tri(gt, triu)
        for hd in range(ML_HEADS):
            sl = slice(LANES * hd, LANES * (hd + 1))
            bcol = bc[:, ML_HEADS + hd:ML_HEADS + hd + 1]
            icol = g[:, hd:hd + 1]
            brow = br[ML_HEADS + hd:ML_HEADS + hd + 1, :]
            irow = gt[hd:hd + 1, :]
            m_prev = m_ref[hd][:, 0:1]
            log_d = jnp.where(causal, bcol - brow + irow, -jnp.inf)
            log_inter = bcol + m_prev
            m_t = jnp.maximum(jnp.max(log_d, axis=-1, keepdims=True), log_inter)
            w_intra = jnp.exp(log_d - m_t)
            w_inter = jnp.exp(log_inter - m_t)
            qc = q_ref[pl.ds(r0, L), sl]
            kc = k_ref[pl.ds(r0, L), sl]
            vc = v_ref[pl.ds(r0, L), sl]
            sc = _dot_nt(qc, kc) * w_intra
            cst = c_ref[hd]
            nst = n_ref[hd]
            num = _dot(sc.astype(BF16), vc) + w_inter * _dot(qc, cst.astype(BF16))
            den = jnp.sum(sc, axis=-1, keepdims=True) + w_inter * jnp.sum(qc.astype(F32) * nst, axis=-1, keepdims=True)
            hout = num / jnp.maximum(jnp.abs(den), jnp.exp(-m_t))
            b_last = bcol[L - 1:L, :]
            lw = b_last - bcol + icol
            m_new = jnp.maximum(b_last + m_prev, jnp.max(lw, axis=0, keepdims=True))
            w_in = jnp.exp(lw - m_new)
            decay = jnp.exp(b_last + m_prev - m_new)
            kw = kc.astype(F32) * w_in
            c_ref[hd] = decay * cst + _dot_tn(kw.astype(BF16), vc)
            n_ref[hd] = decay * nst + jnp.sum(kw, axis=0, keepdims=True)
            m_ref[hd] = jnp.broadcast_to(m_new, (1, LANES))
            ms = jnp.mean(hout * hout, axis=-1, keepdims=True)
            y = og_ref[pl.ds(r0, L), sl].astype(F32) * (hout * lax.rsqrt(ms + EPS)) * ng_ref[:, sl]
            o_ref[pl.ds(r0, L), sl] = y.astype(BF16)
        return carry

    lax.fori_loop(0, s // L, chunk, 0)


def mlstm(q, k, v, og, gates, norm_g):
    b, s, w = v.shape
    gt = jnp.swapaxes(gates[:, :, :2 * ML_HEADS], 1, 2)
    seq = lambda n: pl.BlockSpec((None, s, n), lambda i: (i, 0, 0))
    return pl.pallas_call(
        _mlstm_kernel,
        grid=(b,),
        in_specs=[seq(ML_HEADS * LANES), seq(ML_HEADS * LANES), seq(w), seq(w), seq(LANES),
                  pl.BlockSpec((None, 2 * ML_HEADS, s), lambda i: (i, 0, 0)),
                  pl.BlockSpec((1, w), lambda i: (0, 0))],
        out_specs=seq(w),
        out_shape=jax.ShapeDtypeStruct((b, s, w), BF16),
        scratch_shapes=[pltpu.VMEM((ML_HEADS, LANES, ML_DV), F32), pltpu.VMEM((ML_HEADS, 1, LANES), F32),
                        pltpu.VMEM((ML_HEADS, 1, LANES), F32)],
        compiler_params=_cp(("parallel",)),
        name="mlstm",
    )(q, k, v, og, gates, gt, norm_g.reshape(1, w))


def _hg_proj_kernel(h_ref, wf_ref, wq_ref, wi_ref, wg_ref, lb_ref, q_out, k_out, v_out, og_out, lf_out):
    h = h_ref[...]
    fz = _dot(h, wf_ref[...])
    log_lb = lb_ref[0:1, :]
    log_1m = lb_ref[1:2, :]
    one_m = lb_ref[2:3, :]
    a = log_lb
    bb = log_1m + _log_sigmoid(fz)
    lf_out[...] = jnp.maximum(a, bb) + jnp.log1p(jnp.exp(-jnp.abs(a - bb)))
    k_out[...] = (one_m * jax.nn.sigmoid(-fz)).astype(BF16)
    q_out[...] = _silu(_dot(h, wq_ref[...])).astype(BF16)
    v_out[...] = _dot(h, wi_ref[...]).astype(BF16)
    og_out[...] = _silu(_dot(h, wg_ref[...])).astype(BF16)


def hg_project(h, wf, wq, wi, wg, lb):
    b, s, d = h.shape
    tm = ROW_TILE
    w = BRANCH_WIDTH
    lbp = jnp.stack([jnp.log(lb), jnp.log1p(-lb), 1.0 - lb], axis=0)
    const = lambda shape: pl.BlockSpec(shape, lambda i, j: (0,) * len(shape))
    row = lambda n: pl.BlockSpec((None, tm, n), lambda i, j: (i, j, 0))
    return pl.pallas_call(
        _hg_proj_kernel,
        grid=(b, s // tm),
        in_specs=[row(d), const((d, w)), const((d, w)), const((d, w)), const((d, w)), const((3, w))],
        out_specs=[row(w), row(w), row(w), row(w), row(w)],
        out_shape=[jax.ShapeDtypeStruct((b, s, w), BF16)] * 4 + [jax.ShapeDtypeStruct((b, s, w), F32)],
        compiler_params=_cp(("parallel", "parallel")),
        name="hg_project",
    )(h, wf.astype(BF16), wq.astype(BF16), wi.astype(BF16), wg.astype(BF16), lbp)


HG_LEVELS = tuple(HG_CHUNK >> (i + 1) for i in range(HG_CHUNK.bit_length() - 1))


def _hg_tables():
    L = HG_CHUNK
    t = np.arange(L)
    tri = (t[None, :] <= t[:, None]).astype(np.float32)
    mats = [tri]
    x = t[:, None] ^ t[None, :]
    lvl = np.full((L, L), -1, np.int32)
    lvl[t[:, None] == t[None, :]] = 0
    for i, m in enumerate(HG_LEVELS):
        mats.append(tri[(t // (2 * m)) * (2 * m) + m - 1])
        lvl[(t[:, None] > t[None, :]) & (x >= m) & (x < 2 * m)] = i + 1
    return jnp.asarray(np.concatenate(mats, axis=0), BF16), jnp.asarray(lvl)


def _hgrn_kernel(q_ref, k_ref, v_ref, og_ref, lf_ref, ng_ref, tall_ref, lvl_ref, o_ref, st_ref):
    s = q_ref.shape[0]
    L = HG_CHUNK
    st_ref[...] = jnp.zeros_like(st_ref)
    rowi = lax.broadcasted_iota(I32, (L, LANES), 0)

    def chunk(c, carry):
        r0 = pl.multiple_of(c * L, L)
        lvl = lvl_ref[...]
        tall = tall_ref[...]
        for hd in range(HG_HEADS):
            sl = slice(LANES * hd, LANES * (hd + 1))
            hi, mid, lo = _split3(lf_ref[pl.ds(r0, L), sl])
            cums = (_dot(tall, hi) + _dot(tall, mid)) + _dot(tall, lo)
            a = cums[0:L]
            qb = q_ref[pl.ds(r0, L), sl]
            kb = k_ref[pl.ds(r0, L), sl]
            vb = v_ref[pl.ds(r0, L), sl]
            qf = qb.astype(F32)
            kf = kb.astype(F32)
            sc = jnp.where(lvl == 0, _dot_nt(qb, kb), 0.0)
            for i, m in enumerate(HG_LEVELS):
                e = jnp.exp(-jnp.abs(a - cums[L * (i + 1):L * (i + 2)]))
                w = (e * jnp.where((rowi & m) != 0, qf, kf)).astype(BF16)
                sc = jnp.where(lvl == i + 1, _dot_nt(w, w), sc)
            st = st_ref[hd]
            a_last = a[L - 1:L, :]
            out = _dot(sc.astype(BF16), vb) + _dot_nt((qf * jnp.exp(a)).astype(BF16), st.astype(BF16))
            kt = (kf * jnp.exp(a_last - a)).astype(BF16)
            st_ref[hd] = st * jnp.exp(a_last) + _dot_tn(vb, kt)
            ms = jnp.mean(out * out, axis=-1, keepdims=True)
            y = (out * lax.rsqrt(ms + EPS)) * ng_ref[:, sl] * og_ref[pl.ds(r0, L), sl].astype(F32)
            o_ref[pl.ds(r0, L), sl] = y.astype(BF16)
        return carry

    lax.fori_loop(0, s // L, chunk, 0)


def hgrn(q, k, v, og, logf, norm_g):
    b, s, w = v.shape
    tall, lvl = _hg_tables()
    seq = pl.BlockSpec((None, s, w), lambda i: (i, 0, 0))
    const = lambda shape: pl.BlockSpec(shape, lambda i: (0,) * len(shape))
    return pl.pallas_call(
        _hgrn_kernel,
        grid=(b,),
        in_specs=[seq, seq, seq, seq, seq, const((1, w)), const(tall.shape), const(lvl.shape)],
        out_specs=seq,
        out_shape=jax.ShapeDtypeStruct((b, s, w), BF16),
        scratch_shapes=[pltpu.VMEM((HG_HEADS, LANES, HG_DK), F32)],
        compiler_params=_cp(("parallel",)),
        name="hgrn",
    )(q, k, v, og, logf, norm_g.reshape(1, w), tall, lvl)


def _merge_kernel(x_ref, h_ref, yf_ref, ym_ref, yh_ref, wg_ref, wb_ref, wo_ref, g1_ref, n2_ref, sc2_ref, sh2_ref,
                  x_out, h2_out):
    d = x_ref.shape[1]
    h = h_ref[...]
    merged = None
    for br, y_ref in enumerate((yf_ref, ym_ref, yh_ref)):
        gate = jax.nn.sigmoid(_dot(h, wg_ref[:, d * br:d * (br + 1)]))
        term = gate * _dot(y_ref[...], wb_ref[br])
        merged = term if merged is None else merged + term
    mixed = _dot(merged.astype(BF16), wo_ref[...])
    x1 = x_ref[...] + g1_ref[...] * mixed
    x_out[...] = x1
    h2_out[...] = _norm_mod(x1, n2_ref[...], sc2_ref[...], sh2_ref[...]).astype(BF16)


def merge_branches(x, h, y_fox, y_ml, y_hg, w_gates, w_branch, w_out, g1, norm2_g, sc2, sh2):
    b, s, d = x.shape
    tm = ROW_TILE // 2
    w = BRANCH_WIDTH
    const = lambda shape: pl.BlockSpec(shape, lambda i, j: (0,) * len(shape))
    row = lambda n: pl.BlockSpec((None, tm, n), lambda i, j: (i, j, 0))
    per_b = pl.BlockSpec((None, 1, d), lambda i, j: (i, 0, 0))
    return pl.pallas_call(
        _merge_kernel,
        grid=(b, s // tm),
        in_specs=[row(d), row(d), row(w), row(w), row(w), const((d, N_BRANCH * d)), const((N_BRANCH, w, d)), const((d, d)),
                  per_b, const((1, d)), per_b, per_b],
        out_specs=[row(d), row(d)],
        out_shape=[jax.ShapeDtypeStruct((b, s, d), F32), jax.ShapeDtypeStruct((b, s, d), BF16)],
        compiler_params=_cp(("parallel", "parallel"), 56),
        name="merge_branches",
    )(x, h, y_fox, y_ml, y_hg, w_gates.astype(BF16), w_branch.astype(BF16), w_out.astype(BF16),
      g1.reshape(b, 1, d), norm2_g.reshape(1, d), sc2.reshape(b, 1, d), sh2.reshape(b, 1, d))


def _first_max(x, iota, size):
    m = jnp.max(x, axis=0, keepdims=True)
    idx = jnp.min(jnp.where(x == m, iota, size), axis=0, keepdims=True)
    return m, idx


def _route_kernel(h_ref, wr_ref, rb_ref, dest_out, w_out, cnt_out):
    n = h_ref.shape[0]
    scores = jax.nn.sigmoid(_dot_nt(wr_ref[...], h_ref[...]))
    choice = scores + rb_ref[...]
    e_iota = lax.broadcasted_iota(I32, (N_EXPERTS, n), 0)
    c3 = choice.reshape(N_GROUPS, GROUP_SIZE, n)
    i3 = lax.broadcasted_iota(I32, (N_GROUPS, GROUP_SIZE, n), 1)
    m1 = jnp.max(c3, axis=1, keepdims=True)
    i1 = jnp.min(jnp.where(c3 == m1, i3, GROUP_SIZE), axis=1, keepdims=True)
    m2 = jnp.max(jnp.where(i3 == i1, -jnp.inf, c3), axis=1, keepdims=True)
    gs = (m1 + m2).reshape(N_GROUPS, n)
    g_iota = lax.broadcasted_iota(I32, (N_GROUPS, n), 0)
    gsel = jnp.zeros((N_GROUPS, n), F32)
    for _ in range(TOPK_GROUPS):
        _, gi = _first_max(gs, g_iota, N_GROUPS)
        hit = g_iota == gi
        gsel = jnp.where(hit, 1.0, gsel)
        gs = jnp.where(hit, -jnp.inf, gs)
    gmask = jnp.broadcast_to(gsel.reshape(N_GROUPS, 1, n), (N_GROUPS, GROUP_SIZE, n)).reshape(N_EXPERTS, n)
    masked = jnp.where(gmask > 0.0, choice, -jnp.inf)
    sel = jnp.zeros((N_EXPERTS, n), F32)
    for _ in range(TOP_K):
        _, ei = _first_max(masked, e_iota, N_EXPERTS)
        hit = e_iota == ei
        sel = jnp.where(hit, 1.0, sel)
        masked = jnp.where(hit, -jnp.inf, masked)
    tr = lax.broadcasted_iota(I32, (n, n), 0)
    tc = lax.broadcasted_iota(I32, (n, n), 1)
    before = jnp.where(tr < tc, 1.0, 0.0).astype(BF16)
    pos = _dot(sel.astype(BF16), before)
    cnt = jnp.sum(sel, axis=1, keepdims=True)
    units = jnp.floor((cnt + (RUN_ALIGN - 1)) * (1.0 / RUN_ALIGN))
    er = lax.broadcasted_iota(I32, (N_EXPERTS, N_EXPERTS), 0)
    ec = lax.broadcasted_iota(I32, (N_EXPERTS, N_EXPERTS), 1)
    lower = jnp.where(ec < er, 1.0, 0.0).astype(BF16)
    off = _dot(lower, jnp.broadcast_to(units, (N_EXPERTS, LANES)).astype(BF16))[:, 0:1] * RUN_ALIGN
    dest = jnp.where(sel > 0.0, off + pos, float(DEST_NONE))
    dhi = jnp.floor(dest * (1.0 / DEST_RADIX))
    dest_out[0:N_EXPERTS, :] = dhi.astype(BF16)
    dest_out[N_EXPERTS:, :] = (dest - dhi * DEST_RADIX).astype(BF16)
    wsum = jnp.sum(scores * sel, axis=0, keepdims=True)
    w_out[0:N_EXPERTS, :] = jnp.zeros((N_EXPERTS, n), BF16)
    w_out[N_EXPERTS:, :] = (scores * sel / wsum * ROUTE_SCALE).astype(BF16)
    cnt_out[...] = jnp.broadcast_to(cnt, (N_EXPERTS, LANES)).astype(I32)


def moe_route(h2, router_w, router_bias):
    t, d = h2.shape
    n = SUB_TOKENS
    nsub = t // n
    mat = pl.BlockSpec((None, 2 * N_EXPERTS, n), lambda i: (i, 0, 0))
    dest, wts, cnt = pl.pallas_call(
        _route_kernel,
        grid=(nsub,),
        in_specs=[pl.BlockSpec((n, d), lambda i: (i, 0)), pl.BlockSpec((N_EXPERTS, d), lambda i: (0, 0)),
                  pl.BlockSpec((N_EXPERTS, 1), lambda i: (0, 0))],
        out_specs=[mat, mat, pl.BlockSpec((None, N_EXPERTS, LANES), lambda i: (i, 0, 0))],
        out_shape=[jax.ShapeDtypeStruct((nsub, 2 * N_EXPERTS, n), BF16), jax.ShapeDtypeStruct((nsub, 2 * N_EXPERTS, n), BF16),
                   jax.ShapeDtypeStruct((nsub, N_EXPERTS, LANES), I32)],
        compiler_params=_cp(("parallel",)),
        name="moe_route",
    )(h2, router_w.T.astype(BF16), router_bias.reshape(N_EXPERTS, 1))
    return dest, wts, cnt[:, :, 0]


def _run_tables(cnt, n_blocks):
    units = (cnt + (RUN_ALIGN - 1)) // RUN_ALIGN
    src = jnp.cumsum(units, axis=1) - units
    per_block = MOE_BLOCK // RUN_ALIGN
    tot = jnp.sum(units, axis=0)
    tot_blocks = (tot + per_block - 1) // per_block
    blk_end = jnp.cumsum(tot_blocks)
    base = (blk_end - tot_blocks) * per_block
    dst = base[None, :] + jnp.cumsum(units, axis=0) - units
    n_used = blk_end[-1]
    tail_units = tot_blocks * per_block - tot
    tail_dst = base + tot
    blk = jnp.minimum(jnp.arange(n_blocks), n_used - 1)
    blk_exp = jnp.minimum(jnp.sum(blk[:, None] >= blk_end[None, :], axis=1), N_EXPERTS - 1)
    return (src.reshape(-1).astype(I32), dst.reshape(-1).astype(I32), units.reshape(-1).astype(I32),
            blk_exp.astype(I32), n_used.astype(I32).reshape(1), tail_dst.astype(I32), tail_units.astype(I32))


def _run_bounds(src, units, nsub):
    lo = (src.reshape(nsub, N_EXPERTS) * RUN_ALIGN).astype(F32)
    hi = lo + (units.reshape(nsub, N_EXPERTS) * RUN_ALIGN).astype(F32)
    return jnp.stack([jnp.concatenate([lo, lo], axis=1), jnp.concatenate([hi, hi], axis=1)], axis=1)


def _max_blocks(t):
    nsub = t // SUB_TOKENS
    worst_units = t * TOP_K // RUN_ALIGN + nsub * N_EXPERTS
    per_block = MOE_BLOCK // RUN_ALIGN
    return -(-worst_units // per_block) + N_EXPERTS


def _run_copies(src_ref, dst_ref, units_ref, step, buf, hbm, sem, to_hbm, wait):
    def per_expert(e, carry):
        idx = step * N_EXPERTS + e
        u = units_ref[idx]
        s0 = src_ref[idx]
        d0 = dst_ref[idx]
        for bit in RUN_BITS:
            low = u & (bit - 1)

            @pl.when((u & bit) != 0)
            def _():
                rows = bit * RUN_ALIGN
                v = buf.at[pl.ds(pl.multiple_of((s0 + low) * RUN_ALIGN, RUN_ALIGN), rows)]
                g = hbm.at[pl.ds(pl.multiple_of((d0 + low) * RUN_ALIGN, RUN_ALIGN), rows)]
                cp = pltpu.make_async_copy(v, g, sem) if to_hbm else pltpu.make_async_copy(g, v, sem)
                if wait:
                    cp.wait()
                else:
                    cp.start()
        return carry

    lax.fori_loop(0, N_EXPERTS, per_expert, 0)


def _dispatch_kernel(src_ref, dst_ref, units_ref, tdst_ref, tunits_ref, nused_ref, h_ref, dest_ref, lohi_ref, xs_out,
                     buf_ref, zero_ref, sem):
    i = pl.program_id(0)
    nsub = pl.num_programs(0)
    slot = i % 2
    n = h_ref.shape[0]
    chunk = SORT_CHUNK

    for sl in range(2):
        @pl.when((slot == sl) & (i >= 2))
        def _():
            _run_copies(src_ref, dst_ref, units_ref, i - 2, buf_ref.at[sl], xs_out, sem.at[sl], True, True)

    h = h_ref[...]
    dest = dest_ref[...]
    lo = lohi_ref[0:1, :]
    hi = lohi_ref[1:2, :]
    radix = jnp.where(lax.broadcasted_iota(I32, (1, 2 * N_EXPERTS), 1) < N_EXPERTS, float(DEST_RADIX), 1.0)
    r_e = lax.broadcasted_iota(I32, (chunk, 2 * N_EXPERTS), 0).astype(F32)
    r_t = lax.broadcasted_iota(I32, (chunk, n), 0).astype(F32)
    for sl in range(2):
        @pl.when(slot == sl)
        def _():
            for c in range(SUB_ROWS // chunk):
                own = jnp.where(r_e + c * chunk >= lo, jnp.where(r_e + c * chunk < hi, radix, 0.0), 0.0)
                row_of = _dot(own.astype(BF16), dest)
                p = jnp.where(row_of == r_t + c * chunk, 1.0, 0.0)
                buf_ref[sl, c * chunk:(c + 1) * chunk, :] = _dot(p.astype(BF16), h).astype(BF16)
            _run_copies(src_ref, dst_ref, units_ref, i, buf_ref.at[sl], xs_out, sem.at[sl], True, False)

    @pl.when(i == nsub - 1)
    def _():
        zero_ref[...] = jnp.zeros_like(zero_ref)

        def tails(e, wait):
            u = tunits_ref[e]
            d0 = tdst_ref[e]
            for bit in RUN_BITS:
                low = u & (bit - 1)

                @pl.when((u & bit) != 0)
                def _():
                    rows = bit * RUN_ALIGN
                    cp = pltpu.make_async_copy(zero_ref.at[pl.ds(0, rows)],
                                               xs_out.at[pl.ds(pl.multiple_of((d0 + low) * RUN_ALIGN, RUN_ALIGN), rows)], sem.at[2])
                    if wait:
                        cp.wait()
                    else:
                        cp.start()
            return wait

        def unused(b, wait):
            cp = pltpu.make_async_copy(zero_ref.at[pl.ds(0, MOE_BLOCK)],
                                       xs_out.at[pl.ds(pl.multiple_of(b * MOE_BLOCK, MOE_BLOCK), MOE_BLOCK)], sem.at[2])
            if wait:
                cp.wait()
            else:
                cp.start()
            return wait

        n_blocks = xs_out.shape[0] // MOE_BLOCK
        lax.fori_loop(0, N_EXPERTS, lambda e, c: (tails(e, False), c)[1], 0)
        lax.fori_loop(nused_ref[0], n_blocks, lambda b, c: (unused(b, False), c)[1], 0)
        for sl in range(2):
            @pl.when((slot != sl) & (i >= 1))
            def _():
                _run_copies(src_ref, dst_ref, units_ref, i - 1, buf_ref.at[sl], xs_out, sem.at[sl], True, True)

            @pl.when(slot == sl)
            def _():
                _run_copies(src_ref, dst_ref, units_ref, i, buf_ref.at[sl], xs_out, sem.at[sl], True, True)
        lax.fori_loop(0, N_EXPERTS, lambda e, c: (tails(e, True), c)[1], 0)
        lax.fori_loop(nused_ref[0], n_blocks, lambda b, c: (unused(b, True), c)[1], 0)


def moe_dispatch(h2, dest, tables, n_rows):
    t, d = h2.shape
    n = SUB_TOKENS
    nsub = t // n
    src, dst, units, _, n_used, tail_dst, tail_units = tables
    grid_spec = pltpu.PrefetchScalarGridSpec(
        num_scalar_prefetch=6,
        grid=(nsub,),
        in_specs=[pl.BlockSpec((n, d), lambda i, *_: (i, 0)), pl.BlockSpec((None, 2 * N_EXPERTS, n), lambda i, *_: (i, 0, 0)),
                  pl.BlockSpec((None, 2, 2 * N_EXPERTS), lambda i, *_: (i, 0, 0))],
        out_specs=pl.BlockSpec(memory_space=pl.ANY),
        scratch_shapes=[pltpu.VMEM((2, SUB_ROWS, d), BF16), pltpu.VMEM((max(SUB_TOKENS, MOE_BLOCK), d), BF16), pltpu.SemaphoreType.DMA((3,))],
    )
    return pl.pallas_call(
        _dispatch_kernel,
        grid_spec=grid_spec,
        out_shape=jax.ShapeDtypeStruct((n_rows, d), BF16),
        compiler_params=_cp(("arbitrary",)),
        name="moe_dispatch",
    )(src, dst, units, tail_dst, tail_units, n_used, h2, dest, _run_bounds(src, units, nsub))


def _expert_kernel(blk_exp_ref, n_used_ref, x_ref, w1_ref, w3_ref, w2_ref, y_ref):
    b = pl.program_id(0)

    @pl.when(b < n_used_ref[0])
    def _():
        x = x_ref[...]
        hid = _silu(_dot(x, w1_ref[...].astype(BF16))) * _dot(x, w3_ref[...].astype(BF16))
        y_ref[...] = _dot(hid.astype(BF16), w2_ref[...].astype(BF16)).astype(BF16)

    @pl.when(b >= n_used_ref[0])
    def _():
        y_ref[...] = jnp.zeros_like(y_ref)


def moe_experts(xs, w1, w3, w2, layer, tables, n_blocks):
    n_rows, d = xs.shape
    _, _, _, blk_exp, n_used, _, _ = tables
    f = w1.shape[-1]

    def x_map(b, be, nu):
        return (jnp.minimum(b, nu[0] - 1), 0)

    def w_map(b, be, nu):
        return (layer, be[b], 0, 0)

    grid_spec = pltpu.PrefetchScalarGridSpec(
        num_scalar_prefetch=2,
        grid=(n_blocks,),
        in_specs=[pl.BlockSpec((MOE_BLOCK, d), x_map), pl.BlockSpec((None, None, d, f), w_map),
                  pl.BlockSpec((None, None, d, f), w_map), pl.BlockSpec((None, None, f, d), w_map)],
        out_specs=pl.BlockSpec((MOE_BLOCK, d), lambda b, be, nu: (b, 0)),
    )
    return pl.pallas_call(
        _expert_kernel,
        grid_spec=grid_spec,
        out_shape=jax.ShapeDtypeStruct((n_rows, d), BF16),
        compiler_params=_cp(("arbitrary",)),
        name="moe_experts",
    )(blk_exp, n_used, xs, w1, w3, w2)


def _combine_kernel(src_ref, dst_ref, units_ref, ys_ref, dcol_ref, wcol_ref, lohi_ref, h_ref, x_ref, g2_ref,
                    ws1_ref, ws3_ref, ws2_ref, o_ref, buf_ref, sem):
    i = pl.program_id(0)
    nsub = pl.num_programs(0)
    slot = i % 2
    n = h_ref.shape[0]
    chunk = 512

    def fetch(step, sl):
        buf_ref[sl] = jnp.zeros((SUB_ROWS, buf_ref.shape[2]), BF16)
        _run_copies(src_ref, dst_ref, units_ref, step, buf_ref.at[sl], ys_ref, sem.at[sl], False, False)

    @pl.when(i == 0)
    def _():
        fetch(0, 0)

    for sl in range(2):
        @pl.when((slot != sl) & (i + 1 < nsub))
        def _():
            fetch(i + 1, sl)

    h = h_ref[...]
    shared = _dot((_silu(_dot(h, ws1_ref[...])) * _dot(h, ws3_ref[...])).astype(BF16), ws2_ref[...])
    dest_t = dcol_ref[...]
    w_t = wcol_ref[...]
    lo = lohi_ref[:, 0:1]
    hi = lohi_ref[:, 1:2]
    radix = jnp.where(lax.broadcasted_iota(I32, (2 * N_EXPERTS, 1), 0) < N_EXPERTS, float(DEST_RADIX), 1.0)
    r_e = lax.broadcasted_iota(I32, (2 * N_EXPERTS, chunk), 1).astype(F32)
    r_t = lax.broadcasted_iota(I32, (n, chunk), 1).astype(F32)
    for sl in range(2):
        @pl.when(slot == sl)
        def _():
            _run_copies(src_ref, dst_ref, units_ref, i, buf_ref.at[sl], ys_ref, sem.at[sl], False, True)
            acc = shared
            for c in range(SUB_ROWS // chunk):
                own = jnp.where(r_e + c * chunk >= lo, jnp.where(r_e + c * chunk < hi, radix, 0.0), 0.0).astype(BF16)
                row_of = _dot(dest_t, own)
                pw = jnp.where(row_of == r_t + c * chunk, _dot(w_t, own), 0.0)
                acc = acc + _dot(pw.astype(BF16), buf_ref[sl, c * chunk:(c + 1) * chunk, :])
            o_ref[...] = x_ref[...] + g2_ref[...] * acc


def moe_combine(ys, dest, wts, h2, x1, g2, ws1, ws3, ws2, tables, seq):
    t, d = h2.shape
    n = SUB_TOKENS
    nsub = t // n
    src, dst, units = tables[0], tables[1], tables[2]
    dcol = jnp.swapaxes(dest, 1, 2)
    wcol = jnp.swapaxes(wts, 1, 2)
    bounds = jnp.swapaxes(_run_bounds(src, units, nsub), 1, 2)
    per_seq = seq // n
    f = ws1.shape[-1]
    pair = pl.BlockSpec((None, n, 2 * N_EXPERTS), lambda i, *_: (i, 0, 0))
    grid_spec = pltpu.PrefetchScalarGridSpec(
        num_scalar_prefetch=3,
        grid=(nsub,),
        in_specs=[pl.BlockSpec(memory_space=pl.ANY), pair, pair,
                  pl.BlockSpec((None, 2 * N_EXPERTS, 2), lambda i, *_: (i, 0, 0)),
                  pl.BlockSpec((n, d), lambda i, *_: (i, 0)), pl.BlockSpec((n, d), lambda i, *_: (i, 0)),
                  pl.BlockSpec((None, 1, d), lambda i, *_: (i // per_seq, 0, 0)),
                  pl.BlockSpec((d, f), lambda i, *_: (0, 0)), pl.BlockSpec((d, f), lambda i, *_: (0, 0)),
                  pl.BlockSpec((f, d), lambda i, *_: (0, 0))],
        out_specs=pl.BlockSpec((n, d), lambda i, *_: (i, 0)),
        scratch_shapes=[pltpu.VMEM((2, SUB_ROWS, d), BF16), pltpu.SemaphoreType.DMA((2,))],
    )
    return pl.pallas_call(
        _combine_kernel,
        grid_spec=grid_spec,
        out_shape=jax.ShapeDtypeStruct((t, d), F32),
        compiler_params=_cp(("arbitrary",)),
        name="moe_combine",
    )(src, dst, units, ys, dcol, wcol, bounds, h2, x1, g2, ws1.astype(BF16), ws3.astype(BF16), ws2.astype(BF16))


def _split_w_in(w):
    fh = FOX_HEADS * FOX_HEAD_DIM
    sizes = (fh, fh, fh, FOX_HEADS,
             2 * ML_HEADS * ML_DQK, ML_HEADS * ML_DV, ML_HEADS, ML_HEADS, ML_HEADS * ML_DV,
             HG_HEADS * HG_DK, HG_HEADS * HG_DK, BRANCH_WIDTH, BRANCH_WIDTH,
             N_BRANCH * D_MODEL)
    outs, o = [], 0
    for sz in sizes:
        outs.append(w[:, o:o + sz])
        o += sz
    return outs


def moe_ffn(h2, x1, g2, router_w, router_bias, w1, w3, w2, layer, ws1, ws3, ws2, seq):
    t, d = h2.shape
    dest, wts, cnt = moe_route(h2, router_w, router_bias)
    n_blocks = _max_blocks(t)
    tables = _run_tables(cnt, n_blocks)
    xs = moe_dispatch(h2, dest, tables, n_blocks * MOE_BLOCK)
    ys = moe_experts(xs, w1, w3, w2, layer, tables, n_blocks)
    return moe_combine(ys, dest, wts, h2, x1, g2, ws1, ws3, ws2, tables, seq)


def kernel(x, c, ada_w, ada_b, norm1_g, norm2_g, w_in, fox_bf, fox_q_g, fox_k_g, mlstm_conv, mlstm_bi, mlstm_bf, mlstm_norm_g, hgrn_lower_bounds, hgrn_norm_g, w_branch, w_out, router_w, router_bias, exp_w1, exp_w3, exp_w2, sh_w1, sh_w3, sh_w2):
    b, s, d = x.shape
    depth = ada_w.shape[0]
    mod = adaln_mod(c, ada_w, ada_b)
    lb_all = jnp.cumsum(jax.nn.softmax(hgrn_lower_bounds.astype(F32), axis=0), axis=0)
    lb_all = lb_all - lb_all[0]
    for l in range(depth):
        sh1, sc1, g1, sh2, sc2, g2 = [mod[l][:, d * j:d * (j + 1)] for j in range(6)]
        (wfq, wfk, wfv, wff, wmqk, wmv, wmi, wmf, wmo, whf, whq, whi, whg, wgates) = _split_w_in(w_in[l])
        h = norm_modulate(x, norm1_g[l], sc1, sh1)
        qp, kp, fv = fox_project(h, wfq, wfk, wfv, wff, fox_q_g[l], fox_k_g[l], fox_bf[l])
        y_fox = fox_attention(qp, kp, fv)
        mq, mk, mv, mog, mgates = ml_project(h, wmqk, wmv, wmi, wmf, wmo, mlstm_conv[l], mlstm_bi[l], mlstm_bf[l])
        y_ml = mlstm(mq, mk, mv, mog, mgates, mlstm_norm_g[l])
        hq, hk, hv, hog, hlf = hg_project(h, whf, whq, whi, whg, lb_all[l])
        y_hg = hgrn(hq, hk, hv, hog, hlf, hgrn_norm_g[l])
        x1, h2 = merge_branches(x, h, y_fox, y_ml, y_hg, wgates, w_branch[l], w_out[l], g1, norm2_g[l], sc2, sh2)
        x = moe_ffn(h2.reshape(b * s, d), x1.reshape(b * s, d), g2.reshape(b, 1, d), router_w[l], router_bias[l],
                    exp_w1, exp_w3, exp_w2, l, sh_w1[l], sh_w3[l], sh_w2[l], s).reshape(b, s, d)
    return x
```

```python
import functools

import jax
import jax.numpy as jnp
import numpy as np
from jax import lax
from jax.experimental import pallas as pl
from jax.experimental.pallas import tpu as pltpu

F32 = jnp.float32
BF16 = jnp.bfloat16
I32 = jnp.int32

LANES = 128
SUBLANES = 8
BF16_ROWS = 16
VMEM_BYTES = 64 * 1024 * 1024

D_MODEL = 1024
BRANCH_WIDTH = D_MODEL // 2
N_BRANCH = 3
FOX_HEAD_DIM = 64
FOX_HEADS = BRANCH_WIDTH // FOX_HEAD_DIM
ML_HEADS = 4
ML_DV = BRANCH_WIDTH // ML_HEADS
ML_DQK = ML_DV // 2
ML_CONV = 4
ML_CHUNK = 128
HG_HEADS = 4
HG_DK = 128
HG_CHUNK = 128
N_EXPERTS = 64
N_GROUPS = 8
GROUP_SIZE = N_EXPERTS // N_GROUPS
TOPK_GROUPS = 4
TOP_K = 8
D_FF = D_MODEL // 4
ROUTE_SCALE = 2.5
MOE_BLOCK = 1024
SORT_CHUNK = 1024
EPS = 1e-6
NEG = -1e30

ROW_TILE = 512
ATTN_TILE = 256
SUB_TOKENS = 512
RUN_ALIGN = BF16_ROWS
SUB_ROWS = ((SUB_TOKENS * TOP_K + N_EXPERTS * (RUN_ALIGN - 1)) + 255) // 256 * 256
RUN_BITS = tuple(1 << b for b in range((SUB_TOKENS // RUN_ALIGN).bit_length()))
DEST_RADIX = 64
DEST_NONE = DEST_RADIX * 127
assert SUB_ROWS <= DEST_NONE


def _cp(sem, vmem_mb=48):
    return pltpu.CompilerParams(dimension_semantics=sem, vmem_limit_bytes=vmem_mb * 1024 * 1024)


def _dot(a, b):
    return jnp.dot(a, b, preferred_element_type=F32)


def _dot_nt(a, b):
    return lax.dot_general(a, b, (((1,), (1,)), ((), ())), preferred_element_type=F32)


def _dot_tn(a, b):
    return lax.dot_general(a, b, (((0,), (0,)), ((), ())), preferred_element_type=F32)


def _split3(x):
    hi = x.astype(BF16)
    r = x - hi.astype(F32)
    mid = r.astype(BF16)
    lo = (r - mid.astype(F32)).astype(BF16)
    return hi, mid, lo


def _tri_dot(tri, x):
    hi, mid, lo = _split3(x)
    return (_dot(tri, hi) + _dot(tri, mid)) + _dot(tri, lo)


def _dot_tri(x, tri):
    hi, mid, lo = _split3(x)
    return (_dot(hi, tri) + _dot(mid, tri)) + _dot(lo, tri)


def _log_sigmoid(x):
    return jnp.minimum(x, 0.0) - jnp.log1p(jnp.exp(-jnp.abs(x)))


def _silu(x):
    return x * jax.nn.sigmoid(x)


def _tri_incl(n, dtype=BF16):
    r = lax.broadcasted_iota(I32, (n, n), 0)
    c = lax.broadcasted_iota(I32, (n, n), 1)
    return jnp.where(c <= r, 1.0, 0.0).astype(dtype)


def _mod_kernel(c_ref, w_ref, b_ref, o_ref):
    cond = _silu(c_ref[...])
    hi, mid, lo = _split3(cond)
    w = w_ref[...]
    whi, wmid, wlo = _split3(w)
    acc = _dot(hi, whi) + (_dot(hi, wmid) + _dot(mid, whi))
    acc = acc + (_dot(mid, wmid) + _dot(hi, wlo) + _dot(lo, whi))
    o_ref[...] = acc + b_ref[...]


def adaln_mod(c, ada_w, ada_b):
    depth, d, n = ada_w.shape
    b = c.shape[0]
    tn = 1024
    return pl.pallas_call(
        _mod_kernel,
        grid=(depth, n // tn),
        in_specs=[
            pl.BlockSpec((b, d), lambda l, j: (0, 0)),
            pl.BlockSpec((None, d, tn), lambda l, j: (l, 0, j)),
            pl.BlockSpec((None, 1, tn), lambda l, j: (l, 0, j)),
        ],
        out_specs=pl.BlockSpec((None, b, tn), lambda l, j: (l, 0, j)),
        out_shape=jax.ShapeDtypeStruct((depth, b, n), F32),
        compiler_params=_cp(("parallel", "parallel")),
        name="adaln_mod",
    )(c, ada_w, ada_b.reshape(depth, 1, n))


def _norm_mod(x, g, sc, sh):
    ms = jnp.mean(x * x, axis=-1, keepdims=True)
    return x * lax.rsqrt(ms + EPS) * g * (1.0 + sc) + sh


def _norm_kernel(x_ref, g_ref, sc_ref, sh_ref, h_ref):
    h_ref[...] = _norm_mod(x_ref[...], g_ref[...], sc_ref[...], sh_ref[...]).astype(BF16)


def norm_modulate(x, g, sc, sh):
    b, s, d = x.shape
    tm = ROW_TILE
    return pl.pallas_call(
        _norm_kernel,
        grid=(b, s // tm),
        in_specs=[
            pl.BlockSpec((None, tm, d), lambda i, j: (i, j, 0)),
            pl.BlockSpec((1, d), lambda i, j: (0, 0)),
            pl.BlockSpec((None, 1, d), lambda i, j: (i, 0, 0)),
            pl.BlockSpec((None, 1, d), lambda i, j: (i, 0, 0)),
        ],
        out_specs=pl.BlockSpec((None, tm, d), lambda i, j: (i, j, 0)),
        out_shape=jax.ShapeDtypeStruct((b, s, d), BF16),
        compiler_params=_cp(("parallel", "parallel")),
        name="norm_modulate",
    )(x, g.reshape(1, d), sc.reshape(b, 1, d), sh.reshape(b, 1, d))


FOX_BIAS_PIECES = 3
FOX_QB_LANE = FOX_HEAD_DIM
FOX_KB_LANE = FOX_HEAD_DIM + FOX_BIAS_PIECES


def _pack_pieces(x):
    hi, mid, lo = _split3(x)
    p = hi.astype(F32) + pltpu.roll(mid.astype(F32), FOX_HEADS, axis=1) + pltpu.roll(lo.astype(F32), 2 * FOX_HEADS, axis=1)
    return p.astype(BF16)


def _fox_proj_kernel(h_ref, wq_ref, wk_ref, wvt_ref, wf_ref, gq_ref, gk_ref, bf_ref, eq_ref, ek_ref, cq_ref, ck_ref,
                     q_out, k_out, vt_out, carry_ref):
    @pl.when(pl.program_id(1) == 0)
    def _():
        carry_ref[...] = jnp.zeros_like(carry_ref)

    h = h_ref[...]
    tm = h.shape[0]
    ones_blk = jnp.full((LANES, LANES), 1.0 / FOX_HEAD_DIM, BF16)

    def head_norm(x):
        outs = []
        for hd in range(FOX_HEADS):
            xh = x[:, LANES * hd:LANES * (hd + 1)]
            ms = _dot((xh * xh).astype(BF16), ones_blk)
            outs.append(xh * lax.rsqrt(ms + EPS))
        return jnp.concatenate(outs, axis=1)

    lane = lax.broadcasted_iota(I32, (tm, LANES), 1)
    logf = jnp.where(lane < FOX_HEADS, _log_sigmoid(_dot(h, wf_ref[...]) + bf_ref[...]), 0.0)
    cs = _dot(_tri_incl(tm), _pack_pieces(logf))
    cum = cs + pltpu.roll(cs, LANES - FOX_HEADS, axis=1) + pltpu.roll(cs, LANES - 2 * FOX_HEADS, axis=1)
    cum = jnp.where(lane < FOX_HEADS, cum, 0.0) + carry_ref[...]
    carry_ref[...] = cum[tm - 1:tm, :]
    pieces = _pack_pieces(cum)

    qn = head_norm(_dot(h, wq_ref[...])) * gq_ref[...]
    q_out[...] = (qn + _dot(pieces, eq_ref[...]) + cq_ref[...]).astype(BF16)
    kn = head_norm(_dot(h, wk_ref[...])) * gk_ref[...]
    k_out[...] = (kn + _dot(pieces, ek_ref[...]) + ck_ref[...]).astype(BF16)
    vt_out[...] = _dot_nt(wvt_ref[...], h).astype(BF16)


def _pad_heads(w, heads, dim):
    lead = w.shape[:-1]
    w = w.reshape(*lead, heads, dim)
    w = jnp.pad(w, [(0, 0)] * len(lead) + [(0, 0), (0, LANES - dim)])
    return w.reshape(*lead, heads * LANES)


def _fox_constants():
    eq = np.zeros((LANES, FOX_HEADS * LANES), np.float32)
    ek = np.zeros((LANES, FOX_HEADS * LANES), np.float32)
    cq = np.zeros((1, FOX_HEADS * LANES), np.float32)
    ck = np.zeros((1, FOX_HEADS * LANES), np.float32)
    for hd in range(FOX_HEADS):
        for p in range(FOX_BIAS_PIECES):
            eq[p * FOX_HEADS + hd, LANES * hd + FOX_QB_LANE + p] = 1.0
            ek[p * FOX_HEADS + hd, LANES * hd + FOX_KB_LANE + p] = -1.0
            cq[0, LANES * hd + FOX_KB_LANE + p] = 1.0
            ck[0, LANES * hd + FOX_QB_LANE + p] = 1.0
    return jnp.asarray(eq, BF16), jnp.asarray(ek, BF16), jnp.asarray(cq), jnp.asarray(ck)


def fox_project(h, wq, wk, wv, wf, q_g, k_g, bf):
    b, s, d = h.shape
    tm = ROW_TILE
    hp = FOX_HEADS * LANES
    wq_p = _pad_heads(wq, FOX_HEADS, FOX_HEAD_DIM).astype(BF16)
    wk_p = _pad_heads(wk, FOX_HEADS, FOX_HEAD_DIM).astype(BF16)
    wf_p = jnp.pad(wf, ((0, 0), (0, LANES - FOX_HEADS))).astype(BF16)
    gq = _pad_heads(jnp.tile(q_g * (FOX_HEAD_DIM ** -0.5), FOX_HEADS)[None, :], FOX_HEADS, FOX_HEAD_DIM)
    gk = _pad_heads(jnp.tile(k_g, FOX_HEADS)[None, :], FOX_HEADS, FOX_HEAD_DIM)
    bf_p = jnp.pad(bf, (0, LANES - FOX_HEADS))[None, :]
    eq, ek, cq, ck = _fox_constants()
    const = lambda shape: pl.BlockSpec(shape, lambda i, j: (0,) * len(shape))
    row = lambda n: pl.BlockSpec((None, tm, n), lambda i, j: (i, j, 0))
    return pl.pallas_call(
        _fox_proj_kernel,
        grid=(b, s // tm),
        in_specs=[row(d), const((d, hp)), const((d, hp)), const((BRANCH_WIDTH, d)), const((d, LANES)),
                  const((1, hp)), const((1, hp)), const((1, LANES)), const((LANES, hp)), const((LANES, hp)),
                  const((1, hp)), const((1, hp))],
        out_specs=[row(hp), row(hp), pl.BlockSpec((None, BRANCH_WIDTH, tm), lambda i, j: (i, 0, j))],
        out_shape=[jax.ShapeDtypeStruct((b, s, hp), BF16), jax.ShapeDtypeStruct((b, s, hp), BF16),
                   jax.ShapeDtypeStruct((b, BRANCH_WIDTH, s), BF16)],
        scratch_shapes=[pltpu.VMEM((1, LANES), F32)],
        compiler_params=_cp(("parallel", "arbitrary")),
        name="fox_project",
    )(h, wq_p, wk_p, wv.T.astype(BF16), wf_p, gq, gk, bf_p, eq, ek, cq, ck)


ATTN_HEADS = 4


def _fox_attn_kernel(q_ref, k_ref, vt_ref, o_ref):
    i = pl.program_id(2)
    t = q_ref.shape[0]
    krow = lax.broadcasted_iota(I32, (t, t), 0)
    qcol = lax.broadcasted_iota(I32, (t, t), 1)
    qs = [q_ref[:, LANES * a:LANES * (a + 1)] for a in range(ATTN_HEADS)]

    def scores(j):
        start = pl.multiple_of(j * t, t)
        return tuple(_dot_nt(k_ref[pl.ds(start, t), LANES * a:LANES * (a + 1)], qs[a]) for a in range(ATTN_HEADS))

    def consume(j, state, ss, masked):
        start = pl.multiple_of(j * t, t)
        new = []
        for a in range(ATTN_HEADS):
            m, l, acc = state[a]
            s = jnp.where(krow <= qcol, ss[a], NEG) if masked else ss[a]
            m_new = jnp.maximum(m, jnp.max(s, axis=0, keepdims=True))
            p = jnp.exp(s - m_new)
            alpha = jnp.exp(m - m_new)
            l = alpha * l + jnp.sum(p, axis=0, keepdims=True)
            vt = vt_ref[LANES * (a // 2):LANES * (a // 2 + 1), pl.ds(start, t)]
            acc = alpha * acc + _dot(vt, p.astype(BF16))
            new.append((m_new, l, acc))
        return tuple(new)

    def body(j, carry):
        state, ss = carry
        nxt = scores(j + 1)
        return consume(j, state, ss, False), nxt

    init = tuple((jnp.full((1, t), NEG, F32), jnp.zeros((1, t), F32), jnp.zeros((LANES, t), F32)) for _ in range(ATTN_HEADS))
    state, ss = lax.fori_loop(0, i, body, (init, scores(0)))
    state = consume(i, state, ss, True)
    sub = lax.broadcasted_iota(I32, (LANES, t), 0)
    for p in range(ATTN_HEADS // 2):
        (_, l0, a0), (_, l1, a1) = state[2 * p], state[2 * p + 1]
        out_t = jnp.where(sub < FOX_HEAD_DIM, a0 / l0, a1 / l1)
        o_ref[:, LANES * p:LANES * (p + 1)] = out_t.T.astype(BF16)


def fox_attention(qp, kp, vt):
    b, s, hp = qp.shape
    t = ATTN_TILE
    groups = FOX_HEADS // ATTN_HEADS
    return pl.pallas_call(
        _fox_attn_kernel,
        grid=(b, groups, s // t),
        in_specs=[
            pl.BlockSpec((None, t, ATTN_HEADS * LANES), lambda bi, p, i: (bi, i, p)),
            pl.BlockSpec((None, s, ATTN_HEADS * LANES), lambda bi, p, i: (bi, 0, p)),
            pl.BlockSpec((None, ATTN_HEADS // 2 * LANES, s), lambda bi, p, i: (bi, p, 0)),
        ],
        out_specs=pl.BlockSpec((None, t, ATTN_HEADS // 2 * LANES), lambda bi, p, i: (bi, i, p)),
        out_shape=jax.ShapeDtypeStruct((b, s, BRANCH_WIDTH), BF16),
        compiler_params=_cp(("parallel", "parallel", "arbitrary")),
        name="fox_attention",
    )(qp, kp, vt)


CONV_HALO = SUBLANES


def _ml_proj_kernel(h_ref, wqk_ref, wv_ref, wo_ref, wg_ref, conv_ref, gb_ref, q_out, k_out, v_out, og_out, g_out, buf_ref):
    tm = h_ref.shape[0]
    half = ML_HEADS * LANES

    @pl.when(pl.program_id(1) == 0)
    def _():
        buf_ref[0:CONV_HALO, :] = jnp.zeros((CONV_HALO, 2 * half), F32)

    h = h_ref[...]
    buf_ref[CONV_HALO:CONV_HALO + tm, :] = _dot(h, wqk_ref[...])
    acc = jnp.zeros((tm, 2 * half), F32)
    for j in range(ML_CONV):
        off = CONV_HALO - (ML_CONV - 1) + j
        acc = acc + conv_ref[j:j + 1, :] * buf_ref[off:off + tm, :]
    buf_ref[0:CONV_HALO, :] = buf_ref[tm:tm + CONV_HALO, :]
    act = _silu(acc)
    q_out[...] = act[:, :half].astype(BF16)
    k_out[...] = (act[:, half:] * (ML_DQK ** -0.5)).astype(BF16)
    v_out[...] = _dot(h, wv_ref[...]).astype(BF16)
    og_out[...] = jax.nn.sigmoid(_dot(h, wo_ref[...])).astype(BF16)
    g = _dot(h, wg_ref[...]) + gb_ref[...]
    lane = lax.broadcasted_iota(I32, (tm, LANES), 1)
    g_out[...] = jnp.where(lane < ML_HEADS, g, _log_sigmoid(g))


def ml_project(h, wqk, wv, wi, wf, wo, conv, bi, bf):
    b, s, d = h.shape
    tm = ROW_TILE
    half = ML_HEADS * LANES
    nq = ML_HEADS * ML_DQK
    wqk_p = jnp.concatenate([_pad_heads(wqk[:, :nq], ML_HEADS, ML_DQK), _pad_heads(wqk[:, nq:], ML_HEADS, ML_DQK)], axis=1).astype(BF16)
    conv_p = jnp.concatenate([_pad_heads(conv[:, :nq], ML_HEADS, ML_DQK), _pad_heads(conv[:, nq:], ML_HEADS, ML_DQK)], axis=1)
    wg = jnp.pad(jnp.concatenate([wi, wf], axis=1), ((0, 0), (0, LANES - 2 * ML_HEADS))).astype(BF16)
    gb = jnp.pad(jnp.concatenate([bi, bf]), (0, LANES - 2 * ML_HEADS))[None, :]
    const = lambda shape: pl.BlockSpec(shape, lambda i, j: (0,) * len(shape))
    row = lambda n: pl.BlockSpec((None, tm, n), lambda i, j: (i, j, 0))
    return pl.pallas_call(
        _ml_proj_kernel,
        grid=(b, s // tm),
        in_specs=[row(d), const((d, 2 * half)), const((d, BRANCH_WIDTH)), const((d, BRANCH_WIDTH)), const((d, LANES)),
                  const((ML_CONV, 2 * half)), const((1, LANES))],
        out_specs=[row(half), row(half), row(BRANCH_WIDTH), row(BRANCH_WIDTH), row(LANES)],
        out_shape=[jax.ShapeDtypeStruct((b, s, half), BF16), jax.ShapeDtypeStruct((b, s, half), BF16),
                   jax.ShapeDtypeStruct((b, s, BRANCH_WIDTH), BF16), jax.ShapeDtypeStruct((b, s, BRANCH_WIDTH), BF16),
                   jax.ShapeDtypeStruct((b, s, LANES), F32)],
        scratch_shapes=[pltpu.VMEM((tm + CONV_HALO, 2 * half), F32)],
        compiler_params=_cp(("parallel", "arbitrary")),
        name="ml_project",
    )(h, wqk_p, wv.astype(BF16), wo.astype(BF16), wg, conv_p, gb)


def _mlstm_kernel(q_ref, k_ref, v_ref, og_ref, g_ref, gt_ref, ng_ref, o_ref, c_ref, n_ref, m_ref):
    s = q_ref.shape[0]
    L = ML_CHUNK
    c_ref[...] = jnp.zeros_like(c_ref)
    n_ref[...] = jnp.zeros_like(n_ref)
    m_ref[...] = jnp.zeros_like(m_ref)
    tril = _tri_incl(L)
    triu = tril.T
    row = lax.broadcasted_iota(I32, (L, L), 0)
    col = lax.broadcasted_iota(I32, (L, L), 1)
    causal = col <= row

    def chunk(c, carry):
        r0 = pl.multiple_of(c * L, L)
        g = g_ref[pl.ds(r0, L), :]
        gt = gt_ref[:, pl.ds(r0, L)]
        bc = _tri_dot(tril, g)
        br = _dot_tri(gt, triu)
        for hd in range(ML_HEADS):
            sl = slice(LANES * hd, LANES * (hd + 1))
            bcol = bc[:, ML_HEADS + hd:ML_HEADS + hd + 1]
            icol = g[:, hd:hd + 1]
            brow = br[ML_HEADS + hd:ML_HEADS + hd + 1, :]
            irow = gt[hd:hd + 1, :]
            m_prev = m_ref[hd][:, 0:1]
            log_d = jnp.where(causal, bcol - brow + irow, -jnp.inf)
            log_inter = bcol + m_prev
            m_t = jnp.maximum(jnp.max(log_d, axis=-1, keepdims=True), log_inter)
            w_intra = jnp.exp(log_d - m_t)
            w_inter = jnp.exp(log_inter - m_t)
            qc = q_ref[pl.ds(r0, L), sl]
            kc = k_ref[pl.ds(r0, L), sl]
            vc = v_ref[pl.ds(r0, L), sl]
            sc = _dot_nt(qc, kc) * w_intra
            cst = c_ref[hd]
            nst = n_ref[hd]
            num = _dot(sc.astype(BF16), vc) + w_inter * _dot(qc, cst.astype(BF16))
            den = jnp.sum(sc, axis=-1, keepdims=True) + w_inter * jnp.sum(qc.astype(F32) * nst, axis=-1, keepdims=True)
            hout = num / jnp.maximum(jnp.abs(den), jnp.exp(-m_t))
            b_last = bcol[L - 1:L, :]
            lw = b_last - bcol + icol
            m_new = jnp.maximum(b_last + m_prev, jnp.max(lw, axis=0, keepdims=True))
            w_in = jnp.exp(lw - m_new)
            decay = jnp.exp(b_last + m_prev - m_new)
            kw = kc.astype(F32) * w_in
            c_ref[hd] = decay * cst + _dot_tn(kw.astype(BF16), vc)
            n_ref[hd] = decay * nst + jnp.sum(kw, axis=0, keepdims=True)
            m_ref[hd] = jnp.broadcast_to(m_new, (1, LANES))
            ms = jnp.mean(hout * hout, axis=-1, keepdims=True)
            y = og_ref[pl.ds(r0, L), sl].astype(F32) * (hout * lax.rsqrt(ms + EPS)) * ng_ref[:, sl]
            o_ref[pl.ds(r0, L), sl] = y.astype(BF16)
        return carry

    lax.fori_loop(0, s // L, chunk, 0)


def mlstm(q, k, v, og, gates, norm_g):
    b, s, w = v.shape
    gt = jnp.swapaxes(gates[:, :, :2 * ML_HEADS], 1, 2)
    seq = lambda n: pl.BlockSpec((None, s, n), lambda i: (i, 0, 0))
    return pl.pallas_call(
        _mlstm_kernel,
        grid=(b,),
        in_specs=[seq(ML_HEADS * LANES), seq(ML_HEADS * LANES), seq(w), seq(w), seq(LANES),
                  pl.BlockSpec((None, 2 * ML_HEADS, s), lambda i: (i, 0, 0)),
                  pl.BlockSpec((1, w), lambda i: (0, 0))],
        out_specs=seq(w),
        out_shape=jax.ShapeDtypeStruct((b, s, w), BF16),
        scratch_shapes=[pltpu.VMEM((ML_HEADS, LANES, ML_DV), F32), pltpu.VMEM((ML_HEADS, 1, LANES), F32),
                        pltpu.VMEM((ML_HEADS, 1, LANES), F32)],
        compiler_params=_cp(("parallel",)),
        name="mlstm",
    )(q, k, v, og, gates, gt, norm_g.reshape(1, w))


def _hg_proj_kernel(h_ref, wf_ref, wq_ref, wi_ref, wg_ref, lb_ref, q_out, k_out, v_out, og_out, lf_out):
    h = h_ref[...]
    fz = _dot(h, wf_ref[...])
    log_lb = lb_ref[0:1, :]
    log_1m = lb_ref[1:2, :]
    one_m = lb_ref[2:3, :]
    a = log_lb
    bb = log_1m + _log_sigmoid(fz)
    lf_out[...] = jnp.maximum(a, bb) + jnp.log1p(jnp.exp(-jnp.abs(a - bb)))
    k_out[...] = (one_m * jax.nn.sigmoid(-fz)).astype(BF16)
    q_out[...] = _silu(_dot(h, wq_ref[...])).astype(BF16)
    v_out[...] = _dot(h, wi_ref[...]).astype(BF16)
    og_out[...] = _silu(_dot(h, wg_ref[...])).astype(BF16)


def hg_project(h, wf, wq, wi, wg, lb):
    b, s, d = h.shape
    tm = ROW_TILE
    w = BRANCH_WIDTH
    lbp = jnp.stack([jnp.log(lb), jnp.log1p(-lb), 1.0 - lb], axis=0)
    const = lambda shape: pl.BlockSpec(shape, lambda i, j: (0,) * len(shape))
    row = lambda n: pl.BlockSpec((None, tm, n), lambda i, j: (i, j, 0))
    return pl.pallas_call(
        _hg_proj_kernel,
        grid=(b, s // tm),
        in_specs=[row(d), const((d, w)), const((d, w)), const((d, w)), const((d, w)), const((3, w))],
        out_specs=[row(w), row(w), row(w), row(w), row(w)],
        out_shape=[jax.ShapeDtypeStruct((b, s, w), BF16)] * 4 + [jax.ShapeDtypeStruct((b, s, w), F32)],
        compiler_params=_cp(("parallel", "parallel")),
        name="hg_project",
    )(h, wf.astype(BF16), wq.astype(BF16), wi.astype(BF16), wg.astype(BF16), lbp)


HG_LEVELS = tuple(HG_CHUNK >> (i + 1) for i in range(HG_CHUNK.bit_length() - 1))


def _hg_tables():
    L = HG_CHUNK
    t = np.arange(L)
    tri = (t[None, :] <= t[:, None]).astype(np.float32)
    mats = [tri]
    x = t[:, None] ^ t[None, :]
    lvl = np.full((L, L), -1, np.int32)
    lvl[t[:, None] == t[None, :]] = 0
    for i, m in enumerate(HG_LEVELS):
        if m < SUBLANES:
            mats.append(tri[(t // (2 * m)) * (2 * m) + m - 1])
        lvl[(t[:, None] > t[None, :]) & (x >= m) & (x < 2 * m)] = i + 1
    return jnp.asarray(np.concatenate(mats, axis=0), BF16), jnp.asarray(lvl)


def _hgrn_kernel(q_ref, k_ref, v_ref, og_ref, lf_ref, ng_ref, tall_ref, lvl_ref, o_ref, st_ref):
    s, wd = q_ref.shape
    L = HG_CHUNK
    st_ref[...] = jnp.zeros_like(st_ref)
    rowi = lax.broadcasted_iota(I32, (L, wd), 0)

    def chunk(c, carry):
        r0 = pl.multiple_of(c * L, L)
        lvl = lvl_ref[...]
        tall = tall_ref[...]
        g = lf_ref[pl.ds(r0, L), :]
        hi = g.astype(BF16)
        mid = (g - hi.astype(F32)).astype(BF16)
        cums = _dot(tall, hi) + _dot(tall, mid)
        a = cums[0:L]
        qb = q_ref[pl.ds(r0, L), :]
        kb = k_ref[pl.ds(r0, L), :]
        qf = qb.astype(F32)
        kf = kb.astype(F32)
        ws = []
        fine = 0
        for m in HG_LEVELS:
            if m >= SUBLANES:
                ref = jnp.concatenate([jnp.broadcast_to(a[g0 + m - 1:g0 + m, :], (2 * m, wd)) for g0 in range(0, L, 2 * m)], axis=0)
            else:
                fine += 1
                ref = cums[L * fine:L * (fine + 1)]
            e = jnp.exp(-jnp.abs(a - ref))
            ws.append((e * jnp.where((rowi & m) != 0, qf, kf)).astype(BF16))
        a_last = a[L - 1:L, :]
        qa = (qf * jnp.exp(a)).astype(BF16)
        kt = (kf * jnp.exp(a_last - a)).astype(BF16)
        decay = jnp.exp(a_last)
        for hd in range(HG_HEADS):
            sl = slice(LANES * hd, LANES * (hd + 1))
            vb = v_ref[pl.ds(r0, L), sl]
            sc = jnp.where(lvl == 0, _dot_nt(qb[:, sl], kb[:, sl]), 0.0)
            for i in range(len(HG_LEVELS)):
                w = ws[i][:, sl]
                sc = jnp.where(lvl == i + 1, _dot_nt(w, w), sc)
            st = st_ref[hd]
            out = _dot(sc.astype(BF16), vb) + _dot_nt(qa[:, sl], st.astype(BF16))
            st_ref[hd] = st * decay[:, sl] + _dot_tn(vb, kt[:, sl])
            ms = jnp.mean(out * out, axis=-1, keepdims=True)
            y = (out * lax.rsqrt(ms + EPS)) * ng_ref[:, sl] * og_ref[pl.ds(r0, L), sl].astype(F32)
            o_ref[pl.ds(r0, L), sl] = y.astype(BF16)
        return carry

    lax.fori_loop(0, s // L, chunk, 0)


def hgrn(q, k, v, og, logf, norm_g):
    b, s, w = v.shape
    tall, lvl = _hg_tables()
    seq = pl.BlockSpec((None, s, w), lambda i: (i, 0, 0))
    const = lambda shape: pl.BlockSpec(shape, lambda i: (0,) * len(shape))
    return pl.pallas_call(
        _hgrn_kernel,
        grid=(b,),
        in_specs=[seq, seq, seq, seq, seq, const((1, w)), const(tall.shape), const(lvl.shape)],
        out_specs=seq,
        out_shape=jax.ShapeDtypeStruct((b, s, w), BF16),
        scratch_shapes=[pltpu.VMEM((HG_HEADS, LANES, HG_DK), F32)],
        compiler_params=_cp(("parallel",)),
        name="hgrn",
    )(q, k, v, og, logf, norm_g.reshape(1, w), tall, lvl)


def _merge_kernel(x_ref, h_ref, yf_ref, ym_ref, yh_ref, wg_ref, wb_ref, wo_ref, g1_ref, n2_ref, sc2_ref, sh2_ref,
                  x_out, h2_out):
    d = x_ref.shape[1]
    h = h_ref[...]
    merged = None
    for br, y_ref in enumerate((yf_ref, ym_ref, yh_ref)):
        gate = jax.nn.sigmoid(_dot(h, wg_ref[:, d * br:d * (br + 1)]))
        term = gate * _dot(y_ref[...], wb_ref[br])
        merged = term if merged is None else merged + term
    mixed = _dot(merged.astype(BF16), wo_ref[...])
    x1 = x_ref[...] + g1_ref[...] * mixed
    x_out[...] = x1
    h2_out[...] = _norm_mod(x1, n2_ref[...], sc2_ref[...], sh2_ref[...]).astype(BF16)


def merge_branches(x, h, y_fox, y_ml, y_hg, w_gates, w_branch, w_out, g1, norm2_g, sc2, sh2):
    b, s, d = x.shape
    tm = ROW_TILE
    w = BRANCH_WIDTH
    const = lambda shape: pl.BlockSpec(shape, lambda i, j: (0,) * len(shape))
    row = lambda n: pl.BlockSpec((None, tm, n), lambda i, j: (i, j, 0))
    per_b = pl.BlockSpec((None, 1, d), lambda i, j: (i, 0, 0))
    return pl.pallas_call(
        _merge_kernel,
        grid=(b, s // tm),
        in_specs=[row(d), row(d), row(w), row(w), row(w), const((d, N_BRANCH * d)), const((N_BRANCH, w, d)), const((d, d)),
                  per_b, const((1, d)), per_b, per_b],
        out_specs=[row(d), row(d)],
        out_shape=[jax.ShapeDtypeStruct((b, s, d), F32), jax.ShapeDtypeStruct((b, s, d), BF16)],
        compiler_params=_cp(("parallel", "parallel"), 56),
        name="merge_branches",
    )(x, h, y_fox, y_ml, y_hg, w_gates.astype(BF16), w_branch.astype(BF16), w_out.astype(BF16),
      g1.reshape(b, 1, d), norm2_g.reshape(1, d), sc2.reshape(b, 1, d), sh2.reshape(b, 1, d))


def _first_max(x, iota, size):
    m = jnp.max(x, axis=0, keepdims=True)
    idx = jnp.min(jnp.where(x == m, iota, size), axis=0, keepdims=True)
    return m, idx


def _route_kernel(h_ref, wr_ref, rb_ref, dest_out, w_out, cnt_out):
    n = h_ref.shape[0]
    scores = jax.nn.sigmoid(_dot_nt(wr_ref[...], h_ref[...]))
    choice = scores + rb_ref[...]
    e_iota = lax.broadcasted_iota(I32, (N_EXPERTS, n), 0)
    c3 = choice.reshape(N_GROUPS, GROUP_SIZE, n)
    i3 = lax.broadcasted_iota(I32, (N_GROUPS, GROUP_SIZE, n), 1)
    m1 = jnp.max(c3, axis=1, keepdims=True)
    i1 = jnp.min(jnp.where(c3 == m1, i3, GROUP_SIZE), axis=1, keepdims=True)
    m2 = jnp.max(jnp.where(i3 == i1, -jnp.inf, c3), axis=1, keepdims=True)
    gs = (m1 + m2).reshape(N_GROUPS, n)
    g_iota = lax.broadcasted_iota(I32, (N_GROUPS, n), 0)
    gsel = jnp.zeros((N_GROUPS, n), F32)
    for _ in range(TOPK_GROUPS):
        _, gi = _first_max(gs, g_iota, N_GROUPS)
        hit = g_iota == gi
        gsel = jnp.where(hit, 1.0, gsel)
        gs = jnp.where(hit, -jnp.inf, gs)
    gmask = jnp.broadcast_to(gsel.reshape(N_GROUPS, 1, n), (N_GROUPS, GROUP_SIZE, n)).reshape(N_EXPERTS, n)
    masked = jnp.where(gmask > 0.0, choice, -jnp.inf)
    sel = jnp.zeros((N_EXPERTS, n), F32)
    for _ in range(TOP_K):
        _, ei = _first_max(masked, e_iota, N_EXPERTS)
        hit = e_iota == ei
        sel = jnp.where(hit, 1.0, sel)
        masked = jnp.where(hit, -jnp.inf, masked)
    tr = lax.broadcasted_iota(I32, (n, n), 0)
    tc = lax.broadcasted_iota(I32, (n, n), 1)
    before = jnp.where(tr < tc, 1.0, 0.0).astype(BF16)
    pos = _dot(sel.astype(BF16), before)
    cnt = jnp.sum(sel, axis=1, keepdims=True)
    units = jnp.floor((cnt + (RUN_ALIGN - 1)) * (1.0 / RUN_ALIGN))
    er = lax.broadcasted_iota(I32, (N_EXPERTS, N_EXPERTS), 0)
    ec = lax.broadcasted_iota(I32, (N_EXPERTS, N_EXPERTS), 1)
    lower = jnp.where(ec < er, 1.0, 0.0).astype(BF16)
    off = _dot(lower, jnp.broadcast_to(units, (N_EXPERTS, LANES)).astype(BF16))[:, 0:1] * RUN_ALIGN
    dest = jnp.where(sel > 0.0, off + pos, float(DEST_NONE))
    dhi = jnp.floor(dest * (1.0 / DEST_RADIX))
    dest_out[0:N_EXPERTS, :] = dhi.astype(BF16)
    dest_out[N_EXPERTS:, :] = (dest - dhi * DEST_RADIX).astype(BF16)
    wsum = jnp.sum(scores * sel, axis=0, keepdims=True)
    w_out[0:N_EXPERTS, :] = jnp.zeros((N_EXPERTS, n), BF16)
    w_out[N_EXPERTS:, :] = (scores * sel / wsum * ROUTE_SCALE).astype(BF16)
    cnt_out[...] = jnp.broadcast_to(cnt, (N_EXPERTS, LANES)).astype(I32)


def moe_route(h2, router_w, router_bias):
    t, d = h2.shape
    n = SUB_TOKENS
    nsub = t // n
    mat = pl.BlockSpec((None, 2 * N_EXPERTS, n), lambda i: (i, 0, 0))
    dest, wts, cnt = pl.pallas_call(
        _route_kernel,
        grid=(nsub,),
        in_specs=[pl.BlockSpec((n, d), lambda i: (i, 0)), pl.BlockSpec((N_EXPERTS, d), lambda i: (0, 0)),
                  pl.BlockSpec((N_EXPERTS, 1), lambda i: (0, 0))],
        out_specs=[mat, mat, pl.BlockSpec((None, N_EXPERTS, LANES), lambda i: (i, 0, 0))],
        out_shape=[jax.ShapeDtypeStruct((nsub, 2 * N_EXPERTS, n), BF16), jax.ShapeDtypeStruct((nsub, 2 * N_EXPERTS, n), BF16),
                   jax.ShapeDtypeStruct((nsub, N_EXPERTS, LANES), I32)],
        compiler_params=_cp(("parallel",)),
        name="moe_route",
    )(h2, router_w.T.astype(BF16), router_bias.reshape(N_EXPERTS, 1))
    return dest, wts, cnt[:, :, 0]


def _run_tables(cnt, n_blocks):
    units = (cnt + (RUN_ALIGN - 1)) // RUN_ALIGN
    src = jnp.cumsum(units, axis=1) - units
    per_block = MOE_BLOCK // RUN_ALIGN
    tot = jnp.sum(units, axis=0)
    tot_blocks = (tot + per_block - 1) // per_block
    blk_end = jnp.cumsum(tot_blocks)
    base = (blk_end - tot_blocks) * per_block
    dst = base[None, :] + jnp.cumsum(units, axis=0) - units
    n_used = blk_end[-1]
    tail_units = tot_blocks * per_block - tot
    tail_dst = base + tot
    blk = jnp.minimum(jnp.arange(n_blocks), n_used - 1)
    blk_exp = jnp.minimum(jnp.sum(blk[:, None] >= blk_end[None, :], axis=1), N_EXPERTS - 1)
    return (src.reshape(-1).astype(I32), dst.reshape(-1).astype(I32), units.reshape(-1).astype(I32),
            blk_exp.astype(I32), n_used.astype(I32).reshape(1), tail_dst.astype(I32), tail_units.astype(I32),
            jnp.sum(units, axis=1).astype(I32))


def _run_bounds(src, units, nsub):
    lo = (src.reshape(nsub, N_EXPERTS) * RUN_ALIGN).astype(F32)
    hi = lo + (units.reshape(nsub, N_EXPERTS) * RUN_ALIGN).astype(F32)
    return jnp.stack([jnp.concatenate([lo, lo], axis=1), jnp.concatenate([hi, hi], axis=1)], axis=1)


def _max_blocks(t):
    nsub = t // SUB_TOKENS
    worst_units = t * TOP_K // RUN_ALIGN + nsub * N_EXPERTS
    per_block = MOE_BLOCK // RUN_ALIGN
    return -(-worst_units // per_block) + N_EXPERTS


def _run_copies(src_ref, dst_ref, units_ref, step, buf, hbm, sem, to_hbm, wait):
    def per_expert(e, carry):
        idx = step * N_EXPERTS + e
        u = units_ref[idx]
        s0 = src_ref[idx]
        d0 = dst_ref[idx]
        for bit in RUN_BITS:
            low = u & (bit - 1)

            @pl.when((u & bit) != 0)
            def _():
                rows = bit * RUN_ALIGN
                v = buf.at[pl.ds(pl.multiple_of((s0 + low) * RUN_ALIGN, RUN_ALIGN), rows)]
                g = hbm.at[pl.ds(pl.multiple_of((d0 + low) * RUN_ALIGN, RUN_ALIGN), rows)]
                cp = pltpu.make_async_copy(v, g, sem) if to_hbm else pltpu.make_async_copy(g, v, sem)
                if wait:
                    cp.wait()
                else:
                    cp.start()
        return carry

    lax.fori_loop(0, N_EXPERTS, per_expert, 0)


TOTAL_BITS = tuple(1 << b for b in range((SUB_ROWS // RUN_ALIGN).bit_length()))


def _wait_runs(total_units, buf, hbm, sem, to_hbm):
    for bit in TOTAL_BITS:
        @pl.when((total_units & bit) != 0)
        def _():
            rows = bit * RUN_ALIGN
            v = buf.at[pl.ds(0, rows)]
            g = hbm.at[pl.ds(0, rows)]
            cp = pltpu.make_async_copy(v, g, sem) if to_hbm else pltpu.make_async_copy(g, v, sem)
            cp.wait()


def _dispatch_kernel(src_ref, dst_ref, units_ref, tot_ref, tdst_ref, tunits_ref, nused_ref, h_ref, dest_ref, lohi_ref, xs_out,
                     buf_ref, zero_ref, sem):
    i = pl.program_id(0)
    nsub = pl.num_programs(0)
    slot = i % 2
    n = h_ref.shape[0]
    chunk = SORT_CHUNK

    for sl in range(2):
        @pl.when((slot == sl) & (i >= 2))
        def _():
            _wait_runs(tot_ref[i - 2], buf_ref.at[sl], xs_out, sem.at[sl], True)

    h = h_ref[...]
    dest = dest_ref[...]
    lo = lohi_ref[0:1, :]
    hi = lohi_ref[1:2, :]
    radix = jnp.where(lax.broadcasted_iota(I32, (1, 2 * N_EXPERTS), 1) < N_EXPERTS, float(DEST_RADIX), 1.0)
    r_e = lax.broadcasted_iota(I32, (chunk, 2 * N_EXPERTS), 0).astype(F32)
    r_t = lax.broadcasted_iota(I32, (chunk, n), 0).astype(F32)
    for sl in range(2):
        @pl.when(slot == sl)
        def _():
            for c in range(SUB_ROWS // chunk):
                own = jnp.where(r_e + c * chunk >= lo, jnp.where(r_e + c * chunk < hi, radix, 0.0), 0.0)
                row_of = _dot(own.astype(BF16), dest)
                p = jnp.where(row_of == r_t + c * chunk, 1.0, 0.0)
                buf_ref[sl, c * chunk:(c + 1) * chunk, :] = _dot(p.astype(BF16), h).astype(BF16)
            _run_copies(src_ref, dst_ref, units_ref, i, buf_ref.at[sl], xs_out, sem.at[sl], True, False)

    @pl.when(i == nsub - 1)
    def _():
        zero_ref[...] = jnp.zeros_like(zero_ref)

        def tails(e, wait):
            u = tunits_ref[e]
            d0 = tdst_ref[e]
            for bit in RUN_BITS:
                low = u & (bit - 1)

                @pl.when((u & bit) != 0)
                def _():
                    rows = bit * RUN_ALIGN
                    cp = pltpu.make_async_copy(zero_ref.at[pl.ds(0, rows)],
                                               xs_out.at[pl.ds(pl.multiple_of((d0 + low) * RUN_ALIGN, RUN_ALIGN), rows)], sem.at[2])
                    if wait:
                        cp.wait()
                    else:
                        cp.start()
            return wait

        def unused(b, wait):
            cp = pltpu.make_async_copy(zero_ref.at[pl.ds(0, MOE_BLOCK)],
                                       xs_out.at[pl.ds(pl.multiple_of(b * MOE_BLOCK, MOE_BLOCK), MOE_BLOCK)], sem.at[2])
            if wait:
                cp.wait()
            else:
                cp.start()
            return wait

        n_blocks = xs_out.shape[0] // MOE_BLOCK
        lax.fori_loop(0, N_EXPERTS, lambda e, c: (tails(e, False), c)[1], 0)
        lax.fori_loop(nused_ref[0], n_blocks, lambda b, c: (unused(b, False), c)[1], 0)
        for sl in range(2):
            @pl.when((slot != sl) & (i >= 1))
            def _():
                _wait_runs(tot_ref[i - 1], buf_ref.at[sl], xs_out, sem.at[sl], True)

            @pl.when(slot == sl)
            def _():
                _wait_runs(tot_ref[i], buf_ref.at[sl], xs_out, sem.at[sl], True)
        lax.fori_loop(0, N_EXPERTS, lambda e, c: (tails(e, True), c)[1], 0)
        lax.fori_loop(nused_ref[0], n_blocks, lambda b, c: (unused(b, True), c)[1], 0)


def moe_dispatch(h2, dest, tables, n_rows):
    t, d = h2.shape
    n = SUB_TOKENS
    nsub = t // n
    src, dst, units, _, n_used, tail_dst, tail_units, tot = tables
    grid_spec = pltpu.PrefetchScalarGridSpec(
        num_scalar_prefetch=7,
        grid=(nsub,),
        in_specs=[pl.BlockSpec((n, d), lambda i, *_: (i, 0)), pl.BlockSpec((None, 2 * N_EXPERTS, n), lambda i, *_: (i, 0, 0)),
                  pl.BlockSpec((None, 2, 2 * N_EXPERTS), lambda i, *_: (i, 0, 0))],
        out_specs=pl.BlockSpec(memory_space=pl.ANY),
        scratch_shapes=[pltpu.VMEM((2, SUB_ROWS, d), BF16), pltpu.VMEM((max(SUB_TOKENS, MOE_BLOCK), d), BF16), pltpu.SemaphoreType.DMA((3,))],
    )
    return pl.pallas_call(
        _dispatch_kernel,
        grid_spec=grid_spec,
        out_shape=jax.ShapeDtypeStruct((n_rows, d), BF16),
        compiler_params=_cp(("arbitrary",)),
        name="moe_dispatch",
    )(src, dst, units, tot, tail_dst, tail_units, n_used, h2, dest, _run_bounds(src, units, nsub))


def _expert_kernel(blk_exp_ref, n_used_ref, x_ref, w1_ref, w3_ref, w2_ref, y_ref):
    b = pl.program_id(0)

    @pl.when(b < n_used_ref[0])
    def _():
        x = x_ref[...]
        hid = _silu(_dot(x, w1_ref[...].astype(BF16))) * _dot(x, w3_ref[...].astype(BF16))
        y_ref[...] = _dot(hid.astype(BF16), w2_ref[...].astype(BF16)).astype(BF16)

    @pl.when(b >= n_used_ref[0])
    def _():
        y_ref[...] = jnp.zeros_like(y_ref)


def moe_experts(xs, w1, w3, w2, layer, tables, n_blocks):
    n_rows, d = xs.shape
    blk_exp, n_used = tables[3], tables[4]
    f = w1.shape[-1]

    def x_map(b, be, nu):
        return (jnp.minimum(b, nu[0] - 1), 0)

    def w_map(b, be, nu):
        return (layer, be[b], 0, 0)

    grid_spec = pltpu.PrefetchScalarGridSpec(
        num_scalar_prefetch=2,
        grid=(n_blocks,),
        in_specs=[pl.BlockSpec((MOE_BLOCK, d), x_map), pl.BlockSpec((None, None, d, f), w_map),
                  pl.BlockSpec((None, None, d, f), w_map), pl.BlockSpec((None, None, f, d), w_map)],
        out_specs=pl.BlockSpec((MOE_BLOCK, d), lambda b, be, nu: (b, 0)),
    )
    return pl.pallas_call(
        _expert_kernel,
        grid_spec=grid_spec,
        out_shape=jax.ShapeDtypeStruct((n_rows, d), BF16),
        compiler_params=_cp(("arbitrary",)),
        name="moe_experts",
    )(blk_exp, n_used, xs, w1, w3, w2)


def _combine_kernel(src_ref, dst_ref, units_ref, tot_ref, ys_ref, dcol_ref, wcol_ref, lohi_ref, h_ref, x_ref, g2_ref,
                    ws1_ref, ws3_ref, ws2_ref, o_ref, buf_ref, sem):
    i = pl.program_id(0)
    nsub = pl.num_programs(0)
    slot = i % 2
    n = h_ref.shape[0]
    chunk = 512

    def fetch(step, sl):
        buf_ref[sl] = jnp.zeros((SUB_ROWS, buf_ref.shape[2]), BF16)
        _run_copies(src_ref, dst_ref, units_ref, step, buf_ref.at[sl], ys_ref, sem.at[sl], False, False)

    @pl.when(i == 0)
    def _():
        fetch(0, 0)

    for sl in range(2):
        @pl.when((slot != sl) & (i + 1 < nsub))
        def _():
            fetch(i + 1, sl)

    h = h_ref[...]
    shared = _dot((_silu(_dot(h, ws1_ref[...])) * _dot(h, ws3_ref[...])).astype(BF16), ws2_ref[...])
    dest_t = dcol_ref[...]
    w_t = wcol_ref[...]
    lo = lohi_ref[:, 0:1]
    hi = lohi_ref[:, 1:2]
    radix = jnp.where(lax.broadcasted_iota(I32, (2 * N_EXPERTS, 1), 0) < N_EXPERTS, float(DEST_RADIX), 1.0)
    r_e = lax.broadcasted_iota(I32, (2 * N_EXPERTS, chunk), 1).astype(F32)
    r_t = lax.broadcasted_iota(I32, (n, chunk), 1).astype(F32)
    for sl in range(2):
        @pl.when(slot == sl)
        def _():
            _wait_runs(tot_ref[i], buf_ref.at[sl], ys_ref, sem.at[sl], False)
            acc = shared
            for c in range(SUB_ROWS // chunk):
                own = jnp.where(r_e + c * chunk >= lo, jnp.where(r_e + c * chunk < hi, radix, 0.0), 0.0).astype(BF16)
                row_of = _dot(dest_t, own)
                pw = jnp.where(row_of == r_t + c * chunk, _dot(w_t, own), 0.0)
                acc = acc + _dot(pw.astype(BF16), buf_ref[sl, c * chunk:(c + 1) * chunk, :])
            o_ref[...] = x_ref[...] + g2_ref[...] * acc


def moe_combine(ys, dest, wts, h2, x1, g2, ws1, ws3, ws2, tables, seq):
    t, d = h2.shape
    n = SUB_TOKENS
    nsub = t // n
    src, dst, units, tot = tables[0], tables[1], tables[2], tables[7]
    dcol = jnp.swapaxes(dest, 1, 2)
    wcol = jnp.swapaxes(wts, 1, 2)
    bounds = jnp.swapaxes(_run_bounds(src, units, nsub), 1, 2)
    per_seq = seq // n
    f = ws1.shape[-1]
    pair = pl.BlockSpec((None, n, 2 * N_EXPERTS), lambda i, *_: (i, 0, 0))
    grid_spec = pltpu.PrefetchScalarGridSpec(
        num_scalar_prefetch=4,
        grid=(nsub,),
        in_specs=[pl.BlockSpec(memory_space=pl.ANY), pair, pair,
                  pl.BlockSpec((None, 2 * N_EXPERTS, 2), lambda i, *_: (i, 0, 0)),
                  pl.BlockSpec((n, d), lambda i, *_: (i, 0)), pl.BlockSpec((n, d), lambda i, *_: (i, 0)),
                  pl.BlockSpec((None, 1, d), lambda i, *_: (i // per_seq, 0, 0)),
                  pl.BlockSpec((d, f), lambda i, *_: (0, 0)), pl.BlockSpec((d, f), lambda i, *_: (0, 0)),
                  pl.BlockSpec((f, d), lambda i, *_: (0, 0))],
        out_specs=pl.BlockSpec((n, d), lambda i, *_: (i, 0)),
        scratch_shapes=[pltpu.VMEM((2, SUB_ROWS, d), BF16), pltpu.SemaphoreType.DMA((2,))],
    )
    return pl.pallas_call(
        _combine_kernel,
        grid_spec=grid_spec,
        out_shape=jax.ShapeDtypeStruct((t, d), F32),
        compiler_params=_cp(("arbitrary",)),
        name="moe_combine",
    )(src, dst, units, tot, ys, dcol, wcol, bounds, h2, x1, g2, ws1.astype(BF16), ws3.astype(BF16), ws2.astype(BF16))


def _split_w_in(w):
    fh = FOX_HEADS * FOX_HEAD_DIM
    sizes = (fh, fh, fh, FOX_HEADS,
             2 * ML_HEADS * ML_DQK, ML_HEADS * ML_DV, ML_HEADS, ML_HEADS, ML_HEADS * ML_DV,
             HG_HEADS * HG_DK, HG_HEADS * HG_DK, BRANCH_WIDTH, BRANCH_WIDTH,
             N_BRANCH * D_MODEL)
    outs, o = [], 0
    for sz in sizes:
        outs.append(w[:, o:o + sz])
        o += sz
    return outs


def moe_ffn(h2, x1, g2, router_w, router_bias, w1, w3, w2, layer, ws1, ws3, ws2, seq):
    t, d = h2.shape
    dest, wts, cnt = moe_route(h2, router_w, router_bias)
    n_blocks = _max_blocks(t)
    tables = _run_tables(cnt, n_blocks)
    xs = moe_dispatch(h2, dest, tables, n_blocks * MOE_BLOCK)
    ys = moe_experts(xs, w1, w3, w2, layer, tables, n_blocks)
    return moe_combine(ys, dest, wts, h2, x1, g2, ws1, ws3, ws2, tables, seq)


def kernel(x, c, ada_w, ada_b, norm1_g, norm2_g, w_in, fox_bf, fox_q_g, fox_k_g, mlstm_conv, mlstm_bi, mlstm_bf, mlstm_norm_g, hgrn_lower_bounds, hgrn_norm_g, w_branch, w_out, router_w, router_bias, exp_w1, exp_w3, exp_w2, sh_w1, sh_w3, sh_w2):
    b, s, d = x.shape
    depth = ada_w.shape[0]
    mod = adaln_mod(c, ada_w, ada_b)
    lb_all = jnp.cumsum(jax.nn.softmax(hgrn_lower_bounds.astype(F32), axis=0), axis=0)
    lb_all = lb_all - lb_all[0]
    for l in range(depth):
        sh1, sc1, g1, sh2, sc2, g2 = [mod[l][:, d * j:d * (j + 1)] for j in range(6)]
        (wfq, wfk, wfv, wff, wmqk, wmv, wmi, wmf, wmo, whf, whq, whi, whg, wgates) = _split_w_in(w_in[l])
        h = norm_modulate(x, norm1_g[l], sc1, sh1)
        qp, kp, fv = fox_project(h, wfq, wfk, wfv, wff, fox_q_g[l], fox_k_g[l], fox_bf[l])
        y_fox = fox_attention(qp, kp, fv)
        mq, mk, mv, mog, mgates = ml_project(h, wmqk, wmv, wmi, wmf, wmo, mlstm_conv[l], mlstm_bi[l], mlstm_bf[l])
        y_ml = mlstm(mq, mk, mv, mog, mgates, mlstm_norm_g[l])
        hq, hk, hv, hog, hlf = hg_project(h, whf, whq, whi, whg, lb_all[l])
        y_hg = hgrn(hq, hk, hv, hog, hlf, hgrn_norm_g[l])
        x1, h2 = merge_branches(x, h, y_fox, y_ml, y_hg, wgates, w_branch[l], w_out[l], g1, norm2_g[l], sc2, sh2)
        x = moe_ffn(h2.reshape(b * s, d), x1.reshape(b * s, d), g2.reshape(b, 1, d), router_w[l], router_bias[l],
                    exp_w1, exp_w3, exp_w2, l, sh_w1[l], sh_w3[l], sh_w2[l], s).reshape(b, s, d)
    return x
```

```python
import functools

import jax
import jax.numpy as jnp
import numpy as np
from jax import lax
from jax.experimental import pallas as pl
from jax.experimental.pallas import tpu as pltpu

F32 = jnp.float32
BF16 = jnp.bfloat16
I32 = jnp.int32

LANES = 128
SUBLANES = 8
BF16_ROWS = 16
VMEM_BYTES = 64 * 1024 * 1024

D_MODEL = 1024
BRANCH_WIDTH = D_MODEL // 2
N_BRANCH = 3
FOX_HEAD_DIM = 64
FOX_HEADS = BRANCH_WIDTH // FOX_HEAD_DIM
ML_HEADS = 4
ML_DV = BRANCH_WIDTH // ML_HEADS
ML_DQK = ML_DV // 2
ML_CONV = 4
ML_CHUNK = 128
HG_HEADS = 4
HG_DK = 128
HG_CHUNK = 128
N_EXPERTS = 64
N_GROUPS = 8
GROUP_SIZE = N_EXPERTS // N_GROUPS
TOPK_GROUPS = 4
TOP_K = 8
D_FF = D_MODEL // 4
ROUTE_SCALE = 2.5
MOE_BLOCK = 1024
SORT_CHUNK = 1024
EPS = 1e-6
NEG = -1e30
LOG2E = 1.4426950408889634

ROW_TILE = 512
PROJ_TILE = 1024
ATTN_TILE = 256
SUB_TOKENS = 512
RUN_ALIGN = BF16_ROWS
SUB_ROWS = ((SUB_TOKENS * TOP_K + N_EXPERTS * (RUN_ALIGN - 1)) + 255) // 256 * 256
RUN_BITS = tuple(1 << b for b in range((SUB_TOKENS // RUN_ALIGN).bit_length()))
DEST_RADIX = 64
DEST_NONE = DEST_RADIX * 127
assert SUB_ROWS <= DEST_NONE


def _cp(sem, vmem_mb=48):
    return pltpu.CompilerParams(dimension_semantics=sem, vmem_limit_bytes=vmem_mb * 1024 * 1024)


def _dot(a, b):
    return jnp.dot(a, b, preferred_element_type=F32)


def _dot_nt(a, b):
    return lax.dot_general(a, b, (((1,), (1,)), ((), ())), preferred_element_type=F32)


def _dot_tn(a, b):
    return lax.dot_general(a, b, (((0,), (0,)), ((), ())), preferred_element_type=F32)


def _split3(x):
    hi = x.astype(BF16)
    r = x - hi.astype(F32)
    mid = r.astype(BF16)
    lo = (r - mid.astype(F32)).astype(BF16)
    return hi, mid, lo


def _tri_dot(tri, x):
    hi, mid, lo = _split3(x)
    return (_dot(tri, hi) + _dot(tri, mid)) + _dot(tri, lo)


def _dot_tri(x, tri):
    hi, mid, lo = _split3(x)
    return (_dot(hi, tri) + _dot(mid, tri)) + _dot(lo, tri)


def _log_sigmoid(x):
    return jnp.minimum(x, 0.0) - jnp.log1p(jnp.exp(-jnp.abs(x)))


def _silu(x):
    return x * jax.nn.sigmoid(x)


def _tri_incl(n, dtype=BF16):
    r = lax.broadcasted_iota(I32, (n, n), 0)
    c = lax.broadcasted_iota(I32, (n, n), 1)
    return jnp.where(c <= r, 1.0, 0.0).astype(dtype)


def _mod_kernel(c_ref, w_ref, b_ref, o_ref):
    cond = _silu(c_ref[...])
    hi, mid, lo = _split3(cond)
    w = w_ref[...]
    whi, wmid, wlo = _split3(w)
    acc = _dot(hi, whi) + (_dot(hi, wmid) + _dot(mid, whi))
    acc = acc + (_dot(mid, wmid) + _dot(hi, wlo) + _dot(lo, whi))
    o_ref[...] = acc + b_ref[...]


def adaln_mod(c, ada_w, ada_b):
    depth, d, n = ada_w.shape
    b = c.shape[0]
    tn = 1024
    return pl.pallas_call(
        _mod_kernel,
        grid=(depth, n // tn),
        in_specs=[
            pl.BlockSpec((b, d), lambda l, j: (0, 0)),
            pl.BlockSpec((None, d, tn), lambda l, j: (l, 0, j)),
            pl.BlockSpec((None, 1, tn), lambda l, j: (l, 0, j)),
        ],
        out_specs=pl.BlockSpec((None, b, tn), lambda l, j: (l, 0, j)),
        out_shape=jax.ShapeDtypeStruct((depth, b, n), F32),
        compiler_params=_cp(("parallel", "parallel")),
        name="adaln_mod",
    )(c, ada_w, ada_b.reshape(depth, 1, n))


def _norm_mod(x, g, sc, sh):
    ms = jnp.mean(x * x, axis=-1, keepdims=True)
    return x * lax.rsqrt(ms + EPS) * g * (1.0 + sc) + sh


def _norm_kernel(x_ref, g_ref, sc_ref, sh_ref, h_ref):
    h_ref[...] = _norm_mod(x_ref[...], g_ref[...], sc_ref[...], sh_ref[...]).astype(BF16)


def norm_modulate(x, g, sc, sh):
    b, s, d = x.shape
    tm = ROW_TILE
    return pl.pallas_call(
        _norm_kernel,
        grid=(b, s // tm),
        in_specs=[
            pl.BlockSpec((None, tm, d), lambda i, j: (i, j, 0)),
            pl.BlockSpec((1, d), lambda i, j: (0, 0)),
            pl.BlockSpec((None, 1, d), lambda i, j: (i, 0, 0)),
            pl.BlockSpec((None, 1, d), lambda i, j: (i, 0, 0)),
        ],
        out_specs=pl.BlockSpec((None, tm, d), lambda i, j: (i, j, 0)),
        out_shape=jax.ShapeDtypeStruct((b, s, d), BF16),
        compiler_params=_cp(("parallel", "parallel")),
        name="norm_modulate",
    )(x, g.reshape(1, d), sc.reshape(b, 1, d), sh.reshape(b, 1, d))


FOX_BIAS_PIECES = 3
FOX_QB_LANE = FOX_HEAD_DIM
FOX_KB_LANE = FOX_HEAD_DIM + FOX_BIAS_PIECES


def _pack_pieces(x):
    hi, mid, lo = _split3(x)
    p = hi.astype(F32) + pltpu.roll(mid.astype(F32), FOX_HEADS, axis=1) + pltpu.roll(lo.astype(F32), 2 * FOX_HEADS, axis=1)
    return p.astype(BF16)


def _fox_proj_kernel(h_ref, wq_ref, wk_ref, wvt_ref, wf_ref, gq_ref, gk_ref, bf_ref, eq_ref, ek_ref, cq_ref, ck_ref,
                     q_out, k_out, vt_out, carry_ref):
    @pl.when(pl.program_id(1) == 0)
    def _():
        carry_ref[...] = jnp.zeros_like(carry_ref)

    h = h_ref[...]
    tm = h.shape[0]
    ones_blk = jnp.full((LANES, LANES), 1.0 / FOX_HEAD_DIM, BF16)

    def head_norm(x):
        outs = []
        for hd in range(FOX_HEADS):
            xh = x[:, LANES * hd:LANES * (hd + 1)]
            ms = _dot((xh * xh).astype(BF16), ones_blk)
            outs.append(xh * lax.rsqrt(ms + EPS))
        return jnp.concatenate(outs, axis=1)

    lane = lax.broadcasted_iota(I32, (tm, LANES), 1)
    logf = jnp.where(lane < FOX_HEADS, _log_sigmoid(_dot(h, wf_ref[...]) + bf_ref[...]), 0.0)
    cs = _dot(_tri_incl(tm), _pack_pieces(logf))
    cum = cs + pltpu.roll(cs, LANES - FOX_HEADS, axis=1) + pltpu.roll(cs, LANES - 2 * FOX_HEADS, axis=1)
    cum = jnp.where(lane < FOX_HEADS, cum, 0.0) + carry_ref[...]
    carry_ref[...] = cum[tm - 1:tm, :]
    pieces = _pack_pieces(cum * LOG2E)

    qn = head_norm(_dot(h, wq_ref[...])) * gq_ref[...]
    q_out[...] = (qn + _dot(pieces, eq_ref[...]) + cq_ref[...]).astype(BF16)
    kn = head_norm(_dot(h, wk_ref[...])) * gk_ref[...]
    k_out[...] = (kn + _dot(pieces, ek_ref[...]) + ck_ref[...]).astype(BF16)
    vt = _dot_nt(wvt_ref[...], h)
    ones = jnp.ones((FOX_HEAD_DIM, tm), F32)
    slots = []
    for hd in range(FOX_HEADS):
        slots += [vt[FOX_HEAD_DIM * hd:FOX_HEAD_DIM * (hd + 1), :], ones]
    vt_out[...] = jnp.concatenate(slots, axis=0).astype(BF16)


def _pad_heads(w, heads, dim):
    lead = w.shape[:-1]
    w = w.reshape(*lead, heads, dim)
    w = jnp.pad(w, [(0, 0)] * len(lead) + [(0, 0), (0, LANES - dim)])
    return w.reshape(*lead, heads * LANES)


def _fox_constants():
    eq = np.zeros((LANES, FOX_HEADS * LANES), np.float32)
    ek = np.zeros((LANES, FOX_HEADS * LANES), np.float32)
    cq = np.zeros((1, FOX_HEADS * LANES), np.float32)
    ck = np.zeros((1, FOX_HEADS * LANES), np.float32)
    for hd in range(FOX_HEADS):
        for p in range(FOX_BIAS_PIECES):
            eq[p * FOX_HEADS + hd, LANES * hd + FOX_QB_LANE + p] = 1.0
            ek[p * FOX_HEADS + hd, LANES * hd + FOX_KB_LANE + p] = -1.0
            cq[0, LANES * hd + FOX_KB_LANE + p] = 1.0
            ck[0, LANES * hd + FOX_QB_LANE + p] = 1.0
    return jnp.asarray(eq, BF16), jnp.asarray(ek, BF16), jnp.asarray(cq), jnp.asarray(ck)


def fox_project(h, wq, wk, wv, wf, q_g, k_g, bf):
    b, s, d = h.shape
    tm = min(PROJ_TILE, s)
    hp = FOX_HEADS * LANES
    wq_p = _pad_heads(wq, FOX_HEADS, FOX_HEAD_DIM).astype(BF16)
    wk_p = _pad_heads(wk, FOX_HEADS, FOX_HEAD_DIM).astype(BF16)
    wf_p = jnp.pad(wf, ((0, 0), (0, LANES - FOX_HEADS))).astype(BF16)
    gq = _pad_heads(jnp.tile(q_g * (FOX_HEAD_DIM ** -0.5 * LOG2E), FOX_HEADS)[None, :], FOX_HEADS, FOX_HEAD_DIM)
    gk = _pad_heads(jnp.tile(k_g, FOX_HEADS)[None, :], FOX_HEADS, FOX_HEAD_DIM)
    bf_p = jnp.pad(bf, (0, LANES - FOX_HEADS))[None, :]
    eq, ek, cq, ck = _fox_constants()
    const = lambda shape: pl.BlockSpec(shape, lambda i, j: (0,) * len(shape))
    row = lambda n: pl.BlockSpec((None, tm, n), lambda i, j: (i, j, 0))
    return pl.pallas_call(
        _fox_proj_kernel,
        grid=(b, s // tm),
        in_specs=[row(d), const((d, hp)), const((d, hp)), const((BRANCH_WIDTH, d)), const((d, LANES)),
                  const((1, hp)), const((1, hp)), const((1, LANES)), const((LANES, hp)), const((LANES, hp)),
                  const((1, hp)), const((1, hp))],
        out_specs=[row(hp), row(hp), pl.BlockSpec((None, hp, tm), lambda i, j: (i, 0, j))],
        out_shape=[jax.ShapeDtypeStruct((b, s, hp), BF16), jax.ShapeDtypeStruct((b, s, hp), BF16),
                   jax.ShapeDtypeStruct((b, hp, s), BF16)],
        scratch_shapes=[pltpu.VMEM((1, LANES), F32)],
        compiler_params=_cp(("parallel", "arbitrary")),
        name="fox_project",
    )(h, wq_p, wk_p, wv.T.astype(BF16), wf_p, gq, gk, bf_p, eq, ek, cq, ck)


ATTN_HEADS = 4


def _fox_attn_kernel(q_ref, k_ref, vt_ref, o_ref):
    i = pl.program_id(2)
    t = q_ref.shape[0]
    krow = lax.broadcasted_iota(I32, (t, t), 0)
    qcol = lax.broadcasted_iota(I32, (t, t), 1)
    qs = [q_ref[:, LANES * a:LANES * (a + 1)] for a in range(ATTN_HEADS)]

    def scores(j):
        start = pl.multiple_of(j * t, t)
        return tuple(_dot_nt(k_ref[pl.ds(start, t), LANES * a:LANES * (a + 1)], qs[a]) for a in range(ATTN_HEADS))

    def consume(j, state, ss, masked):
        start = pl.multiple_of(j * t, t)
        new = []
        for a in range(ATTN_HEADS):
            m, acc = state[a]
            s = jnp.where(krow <= qcol, ss[a], NEG) if masked else ss[a]
            m_new = jnp.maximum(m, jnp.max(s, axis=0, keepdims=True))
            p = jnp.exp2(s - m_new)
            alpha = jnp.exp2(m - m_new)
            vt = vt_ref[LANES * a:LANES * (a + 1), pl.ds(start, t)]
            acc = alpha * acc + _dot(vt, p.astype(BF16))
            new.append((m_new, acc))
        return tuple(new)

    def body(j, carry):
        state, ss = carry
        nxt = scores(j + 1)
        return consume(j, state, ss, False), nxt

    init = tuple((jnp.full((1, t), NEG, F32), jnp.zeros((LANES, t), F32)) for _ in range(ATTN_HEADS))
    state, ss = lax.fori_loop(0, i, body, (init, scores(0)))
    state = consume(i, state, ss, True)
    for p in range(ATTN_HEADS // 2):
        halves = []
        for _, acc in (state[2 * p], state[2 * p + 1]):
            halves.append(acc[:FOX_HEAD_DIM, :] / acc[FOX_HEAD_DIM:FOX_HEAD_DIM + 1, :])
        o_ref[:, LANES * p:LANES * (p + 1)] = jnp.concatenate(halves, axis=0).T.astype(BF16)


def fox_attention(qp, kp, vt):
    b, s, hp = qp.shape
    t = ATTN_TILE
    groups = FOX_HEADS // ATTN_HEADS
    return pl.pallas_call(
        _fox_attn_kernel,
        grid=(b, groups, s // t),
        in_specs=[
            pl.BlockSpec((None, t, ATTN_HEADS * LANES), lambda bi, p, i: (bi, i, p)),
            pl.BlockSpec((None, s, ATTN_HEADS * LANES), lambda bi, p, i: (bi, 0, p)),
            pl.BlockSpec((None, ATTN_HEADS * LANES, s), lambda bi, p, i: (bi, p, 0)),
        ],
        out_specs=pl.BlockSpec((None, t, ATTN_HEADS // 2 * LANES), lambda bi, p, i: (bi, i, p)),
        out_shape=jax.ShapeDtypeStruct((b, s, BRANCH_WIDTH), BF16),
        compiler_params=_cp(("parallel", "parallel", "arbitrary")),
        name="fox_attention",
    )(qp, kp, vt)


CONV_HALO = SUBLANES


def _ml_proj_kernel(h_ref, wqk_ref, wv_ref, wo_ref, wg_ref, conv_ref, gb_ref, q_out, k_out, v_out, og_out, g_out, buf_ref):
    tm = h_ref.shape[0]
    half = ML_HEADS * LANES

    @pl.when(pl.program_id(1) == 0)
    def _():
        buf_ref[0:CONV_HALO, :] = jnp.zeros((CONV_HALO, 2 * half), F32)

    h = h_ref[...]
    buf_ref[CONV_HALO:CONV_HALO + tm, :] = _dot(h, wqk_ref[...])
    acc = jnp.zeros((tm, 2 * half), F32)
    for j in range(ML_CONV):
        off = CONV_HALO - (ML_CONV - 1) + j
        acc = acc + conv_ref[j:j + 1, :] * buf_ref[off:off + tm, :]
    buf_ref[0:CONV_HALO, :] = buf_ref[tm:tm + CONV_HALO, :]
    act = _silu(acc)
    q_out[...] = act[:, :half].astype(BF16)
    k_out[...] = (act[:, half:] * (ML_DQK ** -0.5)).astype(BF16)
    v_out[...] = _dot(h, wv_ref[...]).astype(BF16)
    og_out[...] = jax.nn.sigmoid(_dot(h, wo_ref[...])).astype(BF16)
    g = _dot(h, wg_ref[...]) + gb_ref[...]
    lane = lax.broadcasted_iota(I32, (tm, LANES), 1)
    g_out[...] = jnp.where(lane < ML_HEADS, g, _log_sigmoid(g))


def ml_project(h, wqk, wv, wi, wf, wo, conv, bi, bf):
    b, s, d = h.shape
    tm = min(PROJ_TILE, s)
    half = ML_HEADS * LANES
    nq = ML_HEADS * ML_DQK
    wqk_p = jnp.concatenate([_pad_heads(wqk[:, :nq], ML_HEADS, ML_DQK), _pad_heads(wqk[:, nq:], ML_HEADS, ML_DQK)], axis=1).astype(BF16)
    conv_p = jnp.concatenate([_pad_heads(conv[:, :nq], ML_HEADS, ML_DQK), _pad_heads(conv[:, nq:], ML_HEADS, ML_DQK)], axis=1)
    wg = jnp.pad(jnp.concatenate([wi, wf], axis=1), ((0, 0), (0, LANES - 2 * ML_HEADS))).astype(BF16)
    gb = jnp.pad(jnp.concatenate([bi, bf]), (0, LANES - 2 * ML_HEADS))[None, :]
    const = lambda shape: pl.BlockSpec(shape, lambda i, j: (0,) * len(shape))
    row = lambda n: pl.BlockSpec((None, tm, n), lambda i, j: (i, j, 0))
    return pl.pallas_call(
        _ml_proj_kernel,
        grid=(b, s // tm),
        in_specs=[row(d), const((d, 2 * half)), const((d, BRANCH_WIDTH)), const((d, BRANCH_WIDTH)), const((d, LANES)),
                  const((ML_CONV, 2 * half)), const((1, LANES))],
        out_specs=[row(half), row(half), row(BRANCH_WIDTH), row(BRANCH_WIDTH), row(LANES)],
        out_shape=[jax.ShapeDtypeStruct((b, s, half), BF16), jax.ShapeDtypeStruct((b, s, half), BF16),
                   jax.ShapeDtypeStruct((b, s, BRANCH_WIDTH), BF16), jax.ShapeDtypeStruct((b, s, BRANCH_WIDTH), BF16),
                   jax.ShapeDtypeStruct((b, s, LANES), F32)],
        scratch_shapes=[pltpu.VMEM((tm + CONV_HALO, 2 * half), F32)],
        compiler_params=_cp(("parallel", "arbitrary")),
        name="ml_project",
    )(h, wqk_p, wv.astype(BF16), wo.astype(BF16), wg, conv_p, gb)


def _mlstm_kernel(q_ref, k_ref, v_ref, og_ref, g_ref, gt_ref, ng_ref, o_ref, c_ref, n_ref, m_ref):
    s = q_ref.shape[0]
    L = ML_CHUNK
    c_ref[...] = jnp.zeros_like(c_ref)
    n_ref[...] = jnp.zeros_like(n_ref)
    m_ref[...] = jnp.zeros_like(m_ref)
    tril = _tri_incl(L)
    triu = tril.T
    row = lax.broadcasted_iota(I32, (L, L), 0)
    col = lax.broadcasted_iota(I32, (L, L), 1)
    causal = col <= row

    def chunk(c, carry):
        r0 = pl.multiple_of(c * L, L)
        g = g_ref[pl.ds(r0, L), :]
        gt = gt_ref[:, pl.ds(r0, L)]
        bc = _tri_dot(tril, g)
        br = _dot_tri(gt, triu)
        for hd in range(ML_HEADS):
            sl = slice(LANES * hd, LANES * (hd + 1))
            bcol = bc[:, ML_HEADS + hd:ML_HEADS + hd + 1]
            icol = g[:, hd:hd + 1]
            brow = br[ML_HEADS + hd:ML_HEADS + hd + 1, :]
            irow = gt[hd:hd + 1, :]
            m_prev = m_ref[hd][:, 0:1]
            log_d = jnp.where(causal, bcol - brow + irow, -jnp.inf)
            log_inter = bcol + m_prev
            m_t = jnp.maximum(jnp.max(log_d, axis=-1, keepdims=True), log_inter)
            w_intra = jnp.exp(log_d - m_t)
            w_inter = jnp.exp(log_inter - m_t)
            qc = q_ref[pl.ds(r0, L), sl]
            kc = k_ref[pl.ds(r0, L), sl]
            vc = v_ref[pl.ds(r0, L), sl]
            sc = _dot_nt(qc, kc) * w_intra
            cst = c_ref[hd]
            nst = n_ref[hd]
            num = _dot(sc.astype(BF16), vc) + w_inter * _dot(qc, cst.astype(BF16))
            den = jnp.sum(sc, axis=-1, keepdims=True) + w_inter * jnp.sum(qc.astype(F32) * nst, axis=-1, keepdims=True)
            hout = num / jnp.maximum(jnp.abs(den), jnp.exp(-m_t))
            b_last = bcol[L - 1:L, :]
            lw = b_last - bcol + icol
            m_new = jnp.maximum(b_last + m_prev, jnp.max(lw, axis=0, keepdims=True))
            w_in = jnp.exp(lw - m_new)
            decay = jnp.exp(b_last + m_prev - m_new)
            kw = kc.astype(F32) * w_in
            c_ref[hd] = decay * cst + _dot_tn(kw.astype(BF16), vc)
            n_ref[hd] = decay * nst + jnp.sum(kw, axis=0, keepdims=True)
            m_ref[hd] = jnp.broadcast_to(m_new, (1, LANES))
            ms = jnp.mean(hout * hout, axis=-1, keepdims=True)
            y = og_ref[pl.ds(r0, L), sl].astype(F32) * (hout * lax.rsqrt(ms + EPS)) * ng_ref[:, sl]
            o_ref[pl.ds(r0, L), sl] = y.astype(BF16)
        return carry

    lax.fori_loop(0, s // L, chunk, 0)


def mlstm(q, k, v, og, gates, norm_g):
    b, s, w = v.shape
    gt = jnp.swapaxes(gates[:, :, :2 * ML_HEADS], 1, 2)
    seq = lambda n: pl.BlockSpec((None, s, n), lambda i: (i, 0, 0))
    return pl.pallas_call(
        _mlstm_kernel,
        grid=(b,),
        in_specs=[seq(ML_HEADS * LANES), seq(ML_HEADS * LANES), seq(w), seq(w), seq(LANES),
                  pl.BlockSpec((None, 2 * ML_HEADS, s), lambda i: (i, 0, 0)),
                  pl.BlockSpec((1, w), lambda i: (0, 0))],
        out_specs=seq(w),
        out_shape=jax.ShapeDtypeStruct((b, s, w), BF16),
        scratch_shapes=[pltpu.VMEM((ML_HEADS, LANES, ML_DV), F32), pltpu.VMEM((ML_HEADS, 1, LANES), F32),
                        pltpu.VMEM((ML_HEADS, 1, LANES), F32)],
        compiler_params=_cp(("parallel",)),
        name="mlstm",
    )(q, k, v, og, gates, gt, norm_g.reshape(1, w))


def _hg_proj_kernel(h_ref, wf_ref, wq_ref, wi_ref, wg_ref, lb_ref, q_out, k_out, v_out, og_out, lf_out):
    h = h_ref[...]
    fz = _dot(h, wf_ref[...])
    log_lb = lb_ref[0:1, :]
    log_1m = lb_ref[1:2, :]
    one_m = lb_ref[2:3, :]
    a = log_lb
    bb = log_1m + _log_sigmoid(fz)
    lf_out[...] = jnp.maximum(a, bb) + jnp.log1p(jnp.exp(-jnp.abs(a - bb)))
    k_out[...] = (one_m * jax.nn.sigmoid(-fz)).astype(BF16)
    q_out[...] = _silu(_dot(h, wq_ref[...])).astype(BF16)
    v_out[...] = _dot(h, wi_ref[...]).astype(BF16)
    og_out[...] = _silu(_dot(h, wg_ref[...])).astype(BF16)


def hg_project(h, wf, wq, wi, wg, lb):
    b, s, d = h.shape
    tm = min(PROJ_TILE, s)
    w = BRANCH_WIDTH
    lbp =jnp.stack([jnp.log(lb), jnp.log1p(-lb), 1.0 - lb], axis=0)
    const = lambda shape: pl.BlockSpec(shape, lambda i, j: (0,) * len(shape))
    row = lambda n: pl.BlockSpec((None, tm, n), lambda i, j: (i, j, 0))
    return pl.pallas_call(
        _hg_proj_kernel,
        grid=(b, s // tm),
        in_specs=[row(d), const((d, w)), const((d, w)), const((d, w)), const((d, w)), const((3, w))],
        out_specs=[row(w), row(w), row(w), row(w), row(w)],
        out_shape=[jax.ShapeDtypeStruct((b, s, w), BF16)] * 4 + [jax.ShapeDtypeStruct((b, s, w), F32)],
        compiler_params=_cp(("parallel", "parallel")),
        name="hg_project",
    )(h, wf.astype(BF16), wq.astype(BF16), wi.astype(BF16), wg.astype(BF16), lbp)


HG_LEVELS = tuple(HG_CHUNK >> (i + 1) for i in range(HG_CHUNK.bit_length() - 1))


def _hg_tables():
    L = HG_CHUNK
    t = np.arange(L)
    tri = (t[None, :] <= t[:, None]).astype(np.float32)
    mats = [tri]
    x = t[:, None] ^ t[None, :]
    lvl = np.full((L, L), -1, np.int32)
    lvl[t[:, None] == t[None, :]] = 0
    for i, m in enumerate(HG_LEVELS):
        if m < SUBLANES:
            mats.append(tri[(t // (2 * m)) * (2 * m) + m - 1])
        lvl[(t[:, None] > t[None, :]) & (x >= m) & (x < 2 * m)] = i + 1
    return jnp.asarray(np.concatenate(mats, axis=0), BF16), jnp.asarray(lvl)


def _hgrn_kernel(q_ref, k_ref, v_ref, og_ref, lf_ref, ng_ref, tall_ref, lvl_ref, o_ref, st_ref):
    s, wd = q_ref.shape
    L = HG_CHUNK
    st_ref[...] = jnp.zeros_like(st_ref)
    rowi = lax.broadcasted_iota(I32, (L, wd), 0)

    def chunk(c, carry):
        r0 = pl.multiple_of(c * L, L)
        lvl = lvl_ref[...]
        tall = tall_ref[...]
        g = lf_ref[pl.ds(r0, L), :]
        hi = g.astype(BF16)
        mid = (g - hi.astype(F32)).astype(BF16)
        cums = _dot(tall, hi) + _dot(tall, mid)
        a = cums[0:L]
        qb = q_ref[pl.ds(r0, L), :]
        kb = k_ref[pl.ds(r0, L), :]
        qf = qb.astype(F32)
        kf = kb.astype(F32)
        ws = []
        fine = 0
        for m in HG_LEVELS:
            if m >= SUBLANES:
                ref = jnp.concatenate([jnp.broadcast_to(a[g0 + m - 1:g0 + m, :], (2 * m, wd)) for g0 in range(0, L, 2 * m)], axis=0)
            else:
                fine += 1
                ref = cums[L * fine:L * (fine + 1)]
            e = jnp.exp(-jnp.abs(a - ref))
            ws.append((e * jnp.where((rowi & m) != 0, qf, kf)).astype(BF16))
        a_last = a[L - 1:L, :]
        qa = (qf * jnp.exp(a)).astype(BF16)
        kt = (kf * jnp.exp(a_last - a)).astype(BF16)
        decay = jnp.exp(a_last)
        for hd in range(HG_HEADS):
            sl = slice(LANES * hd, LANES * (hd + 1))
            vb = v_ref[pl.ds(r0, L), sl]
            sc = jnp.where(lvl == 0, _dot_nt(qb[:, sl], kb[:, sl]), 0.0)
            for i in range(len(HG_LEVELS)):
                w = ws[i][:, sl]
                sc = jnp.where(lvl == i + 1, _dot_nt(w, w), sc)
            st = st_ref[hd]
            out = _dot(sc.astype(BF16), vb) + _dot_nt(qa[:, sl], st.astype(BF16))
            st_ref[hd] = st * decay[:, sl] + _dot_tn(vb, kt[:, sl])
            ms = jnp.mean(out * out, axis=-1, keepdims=True)
            y = (out * lax.rsqrt(ms + EPS)) * ng_ref[:, sl] * og_ref[pl.ds(r0, L), sl].astype(F32)
            o_ref[pl.ds(r0, L), sl] = y.astype(BF16)
        return carry

    lax.fori_loop(0, s // L, chunk, 0)


def hgrn(q, k, v, og, logf, norm_g):
    b, s, w = v.shape
    tall, lvl = _hg_tables()
    seq = pl.BlockSpec((None, s, w), lambda i: (i, 0, 0))
    const = lambda shape: pl.BlockSpec(shape, lambda i: (0,) * len(shape))
    return pl.pallas_call(
        _hgrn_kernel,
        grid=(b,),
        in_specs=[seq, seq, seq, seq, seq, const((1, w)), const(tall.shape), const(lvl.shape)],
        out_specs=seq,
        out_shape=jax.ShapeDtypeStruct((b, s, w), BF16),
        scratch_shapes=[pltpu.VMEM((HG_HEADS, LANES, HG_DK), F32)],
        compiler_params=_cp(("parallel",)),
        name="hgrn",
    )(q, k, v, og, logf, norm_g.reshape(1, w), tall, lvl)


def _merge_kernel(x_ref, h_ref, yf_ref, ym_ref, yh_ref, wg_ref, wb_ref, wo_ref, g1_ref, n2_ref, sc2_ref, sh2_ref,
                  x_out, h2_out):
    d = x_ref.shape[1]
    h = h_ref[...]
    merged = None
    for br, y_ref in enumerate((yf_ref, ym_ref, yh_ref)):
        gate = jax.nn.sigmoid(_dot(h, wg_ref[:, d * br:d * (br + 1)]))
        term = gate * _dot(y_ref[...], wb_ref[br])
        merged = term if merged is None else merged + term
    mixed = _dot(merged.astype(BF16), wo_ref[...])
    x1 = x_ref[...] + g1_ref[...] * mixed
    x_out[...] = x1
    h2_out[...] = _norm_mod(x1, n2_ref[...], sc2_ref[...], sh2_ref[...]).astype(BF16)


def merge_branches(x, h, y_fox, y_ml, y_hg, w_gates, w_branch, w_out, g1, norm2_g, sc2, sh2):
    b, s, d = x.shape
    tm = ROW_TILE
    w = BRANCH_WIDTH
    const = lambda shape: pl.BlockSpec(shape, lambda i, j: (0,) * len(shape))
    row = lambda n: pl.BlockSpec((None, tm, n), lambda i, j: (i, j, 0))
    per_b = pl.BlockSpec((None, 1, d), lambda i, j: (i, 0, 0))
    return pl.pallas_call(
        _merge_kernel,
        grid=(b, s // tm),
        in_specs=[row(d), row(d), row(w), row(w), row(w), const((d, N_BRANCH * d)), const((N_BRANCH, w, d)), const((d, d)),
                  per_b, const((1, d)), per_b, per_b],
        out_specs=[row(d), row(d)],
        out_shape=[jax.ShapeDtypeStruct((b, s, d), F32), jax.ShapeDtypeStruct((b, s, d), BF16)],
        compiler_params=_cp(("parallel", "parallel"), 56),
        name="merge_branches",
    )(x, h, y_fox, y_ml, y_hg, w_gates.astype(BF16), w_branch.astype(BF16), w_out.astype(BF16),
      g1.reshape(b, 1, d), norm2_g.reshape(1, d), sc2.reshape(b, 1, d), sh2.reshape(b, 1, d))


def _first_max(x, iota, size):
    m = jnp.max(x, axis=0, keepdims=True)
    idx = jnp.min(jnp.where(x == m, iota, size), axis=0, keepdims=True)
    return m, idx


def _route_kernel(h_ref, wr_ref, rb_ref, dest_out, w_out, cnt_out):
    n = h_ref.shape[0]
    scores = jax.nn.sigmoid(_dot_nt(wr_ref[...], h_ref[...]))
    choice = scores + rb_ref[...]
    e_iota = lax.broadcasted_iota(I32, (N_EXPERTS, n), 0)
    c3 = choice.reshape(N_GROUPS, GROUP_SIZE, n)
    i3 = lax.broadcasted_iota(I32, (N_GROUPS, GROUP_SIZE, n), 1)
    m1 = jnp.max(c3, axis=1, keepdims=True)
    i1 = jnp.min(jnp.where(c3 == m1, i3, GROUP_SIZE), axis=1, keepdims=True)
    m2 = jnp.max(jnp.where(i3 == i1, -jnp.inf, c3), axis=1, keepdims=True)
    gs = (m1 + m2).reshape(N_GROUPS, n)
    g_iota = lax.broadcasted_iota(I32, (N_GROUPS, n), 0)
    gsel = jnp.zeros((N_GROUPS, n), F32)
    for _ in range(TOPK_GROUPS):
        _, gi = _first_max(gs, g_iota, N_GROUPS)
        hit = g_iota == gi
        gsel = jnp.where(hit, 1.0, gsel)
        gs = jnp.where(hit, -jnp.inf, gs)
    gmask = jnp.broadcast_to(gsel.reshape(N_GROUPS, 1, n), (N_GROUPS, GROUP_SIZE, n)).reshape(N_EXPERTS, n)
    masked = jnp.where(gmask > 0.0, choice, -jnp.inf)
    sel = jnp.zeros((N_EXPERTS, n), F32)
    for _ in range(TOP_K):
        _, ei = _first_max(masked, e_iota, N_EXPERTS)
        hit = e_iota == ei
        sel = jnp.where(hit, 1.0, sel)
        masked = jnp.where(hit, -jnp.inf, masked)
    tr = lax.broadcasted_iota(I32, (n, n), 0)
    tc = lax.broadcasted_iota(I32, (n, n), 1)
    before = jnp.where(tr < tc, 1.0, 0.0).astype(BF16)
    pos = _dot(sel.astype(BF16), before)
    cnt = jnp.sum(sel, axis=1, keepdims=True)
    units = jnp.floor((cnt + (RUN_ALIGN - 1)) * (1.0 / RUN_ALIGN))
    er = lax.broadcasted_iota(I32, (N_EXPERTS, N_EXPERTS), 0)
    ec = lax.broadcasted_iota(I32, (N_EXPERTS, N_EXPERTS), 1)
    lower = jnp.where(ec < er, 1.0, 0.0).astype(BF16)
    off = _dot(lower, jnp.broadcast_to(units, (N_EXPERTS, LANES)).astype(BF16))[:, 0:1] * RUN_ALIGN
    dest = jnp.where(sel > 0.0, off + pos, float(DEST_NONE))
    dhi = jnp.floor(dest * (1.0 / DEST_RADIX))
    dest_out[0:N_EXPERTS, :] = dhi.astype(BF16)
    dest_out[N_EXPERTS:, :] = (dest - dhi * DEST_RADIX).astype(BF16)
    wsum = jnp.sum(scores * sel, axis=0, keepdims=True)
    w_out[0:N_EXPERTS, :] = jnp.zeros((N_EXPERTS, n), BF16)
    w_out[N_EXPERTS:, :] = (scores * sel / wsum * ROUTE_SCALE).astype(BF16)
    cnt_out[...] = jnp.broadcast_to(cnt, (N_EXPERTS, LANES)).astype(I32)


def moe_route(h2, router_w, router_bias):
    t, d = h2.shape
    n = SUB_TOKENS
    nsub = t // n
    mat = pl.BlockSpec((None, 2 * N_EXPERTS, n), lambda i: (i, 0, 0))
    dest, wts, cnt = pl.pallas_call(
        _route_kernel,
        grid=(nsub,),
        in_specs=[pl.BlockSpec((n, d), lambda i: (i, 0)), pl.BlockSpec((N_EXPERTS, d), lambda i: (0, 0)),
                  pl.BlockSpec((N_EXPERTS, 1), lambda i: (0, 0))],
        out_specs=[mat, mat, pl.BlockSpec((None, N_EXPERTS, LANES), lambda i: (i, 0, 0))],
        out_shape=[jax.ShapeDtypeStruct((nsub, 2 * N_EXPERTS, n), BF16), jax.ShapeDtypeStruct((nsub, 2 * N_EXPERTS, n), BF16),
                   jax.ShapeDtypeStruct((nsub, N_EXPERTS, LANES), I32)],
        compiler_params=_cp(("parallel",)),
        name="moe_route",
    )(h2, router_w.T.astype(BF16), router_bias.reshape(N_EXPERTS, 1))
    return dest, wts, cnt[:, :, 0]


def _run_tables(cnt, n_blocks):
    units = (cnt + (RUN_ALIGN - 1)) // RUN_ALIGN
    src = jnp.cumsum(units, axis=1) - units
    per_block = MOE_BLOCK // RUN_ALIGN
    tot = jnp.sum(units, axis=0)
    tot_blocks = (tot + per_block - 1) // per_block
    blk_end = jnp.cumsum(tot_blocks)
    base = (blk_end - tot_blocks) * per_block
    dst = base[None, :] + jnp.cumsum(units, axis=0) - units
    n_used = blk_end[-1]
    tail_units = tot_blocks * per_block - tot
    tail_dst = base + tot
    blk = jnp.minimum(jnp.arange(n_blocks), n_used - 1)
    blk_exp = jnp.minimum(jnp.sum(blk[:, None] >= blk_end[None, :], axis=1), N_EXPERTS - 1)
    return (src.reshape(-1).astype(I32), dst.reshape(-1).astype(I32), units.reshape(-1).astype(I32),
            blk_exp.astype(I32), n_used.astype(I32).reshape(1), tail_dst.astype(I32), tail_units.astype(I32),
            jnp.sum(units, axis=1).astype(I32))


def _unit_dst(src, dst, units, nsub):
    src = src.reshape(nsub, 1, N_EXPERTS)
    dst = dst.reshape(nsub, 1, N_EXPERTS)
    end = src + units.reshape(nsub, 1, N_EXPERTS)
    j = jnp.arange(SUB_UNITS, dtype=I32).reshape(1, SUB_UNITS, 1)
    owns = (j >= src) & (j < end)
    return (jnp.sum(jnp.where(owns, dst - src, 0), axis=2) + j[:, :, 0]).reshape(-1).astype(I32)


def _run_bounds(src, units, nsub):
    lo = (src.reshape(nsub, N_EXPERTS) * RUN_ALIGN).astype(F32)
    hi = lo + (units.reshape(nsub, N_EXPERTS) * RUN_ALIGN).astype(F32)
    return jnp.stack([jnp.concatenate([lo, lo], axis=1), jnp.concatenate([hi, hi], axis=1)], axis=1)


def _max_blocks(t):
    nsub = t // SUB_TOKENS
    worst_units = t * TOP_K // RUN_ALIGN + nsub * N_EXPERTS
    per_block = MOE_BLOCK // RUN_ALIGN
    return -(-worst_units // per_block) + N_EXPERTS


SUB_UNITS = SUB_ROWS // RUN_ALIGN
COPY_UNROLL = 4


def _unit_copies(udst_ref, step, total_units, buf, hbm, sem, to_hbm):
    base = step * SUB_UNITS

    def one(j):
        d = udst_ref[base + j]
        v = buf.at[pl.ds(pl.multiple_of(j * RUN_ALIGN, RUN_ALIGN), RUN_ALIGN)]
        g = hbm.at[pl.ds(pl.multiple_of(d * RUN_ALIGN, RUN_ALIGN), RUN_ALIGN)]
        cp = pltpu.make_async_copy(v, g, sem) if to_hbm else pltpu.make_async_copy(g, v, sem)
        cp.start()

    def group(q, carry):
        for r in range(COPY_UNROLL):
            one(q * COPY_UNROLL + r)
        return carry

    groups = lax.shift_right_logical(total_units, COPY_UNROLL.bit_length() - 1)
    lax.fori_loop(0, groups, group, 0)
    lax.fori_loop(groups * COPY_UNROLL, total_units, lambda j, c: (one(j), c)[1], 0)


TOTAL_BITS = tuple(1 << b for b in range(SUB_UNITS.bit_length()))


def _wait_runs(total_units, buf, hbm, sem, to_hbm):
    for bit in TOTAL_BITS:
        @pl.when((total_units & bit) != 0)
        def _():
            rows = bit * RUN_ALIGN
            v = buf.at[pl.ds(0, rows)]
            g = hbm.at[pl.ds(0, rows)]
            cp = pltpu.make_async_copy(v, g, sem) if to_hbm else pltpu.make_async_copy(g, v, sem)
            cp.wait()


def _dispatch_kernel(udst_ref, tot_ref, tdst_ref, tunits_ref, nused_ref, h_ref, dest_ref, lohi_ref, xs_out,
                     buf_ref, zero_ref, sem):
    i = pl.program_id(0)
    nsub = pl.num_programs(0)
    slot = i % 2
    n = h_ref.shape[0]
    chunk = SORT_CHUNK

    for sl in range(2):
        @pl.when((slot == sl) & (i >= 2))
        def _():
            _wait_runs(tot_ref[i - 2], buf_ref.at[sl], xs_out, sem.at[sl], True)

    h = h_ref[...]
    dest = dest_ref[...]
    lo = lohi_ref[0:1, :]
    hi = lohi_ref[1:2, :]
    radix = jnp.where(lax.broadcasted_iota(I32, (1, 2 * N_EXPERTS), 1) < N_EXPERTS, float(DEST_RADIX), 1.0)
    r_e = lax.broadcasted_iota(I32, (chunk, 2 * N_EXPERTS), 0).astype(F32)
    r_t = lax.broadcasted_iota(I32, (chunk, n), 0).astype(F32)
    for sl in range(2):
        @pl.when(slot == sl)
        def _():
            for c in range(SUB_ROWS // chunk):
                own = jnp.where(r_e + c * chunk >= lo, jnp.where(r_e + c * chunk < hi, radix, 0.0), 0.0)
                row_of = _dot(own.astype(BF16), dest)
                p = jnp.where(row_of == r_t + c * chunk, 1.0, 0.0)
                buf_ref[sl, c * chunk:(c + 1) * chunk, :] = _dot(p.astype(BF16), h).astype(BF16)
            _unit_copies(udst_ref, i, tot_ref[i], buf_ref.at[sl], xs_out, sem.at[sl], True)

    @pl.when(i == nsub - 1)
    def _():
        zero_ref[...] = jnp.zeros_like(zero_ref)

        def tails(e, wait):
            u = tunits_ref[e]
            d0 = tdst_ref[e]
            for bit in RUN_BITS:
                low = u & (bit - 1)

                @pl.when((u & bit) != 0)
                def _():
                    rows = bit * RUN_ALIGN
                    cp = pltpu.make_async_copy(zero_ref.at[pl.ds(0, rows)],
                                               xs_out.at[pl.ds(pl.multiple_of((d0 + low) * RUN_ALIGN, RUN_ALIGN), rows)], sem.at[2])
                    if wait:
                        cp.wait()
                    else:
                        cp.start()
            return wait

        def unused(b, wait):
            cp = pltpu.make_async_copy(zero_ref.at[pl.ds(0, MOE_BLOCK)],
                                       xs_out.at[pl.ds(pl.multiple_of(b * MOE_BLOCK, MOE_BLOCK), MOE_BLOCK)], sem.at[2])
            if wait:
                cp.wait()
            else:
                cp.start()
            return wait

        n_blocks = xs_out.shape[0] // MOE_BLOCK
        lax.fori_loop(0, N_EXPERTS, lambda e, c: (tails(e, False), c)[1], 0)
        lax.fori_loop(nused_ref[0], n_blocks, lambda b, c: (unused(b, False), c)[1], 0)
        for sl in range(2):
            @pl.when((slot != sl) & (i >= 1))
            def _():
                _wait_runs(tot_ref[i - 1], buf_ref.at[sl], xs_out, sem.at[sl], True)

            @pl.when(slot == sl)
            def _():
                _wait_runs(tot_ref[i], buf_ref.at[sl], xs_out, sem.at[sl], True)
        lax.fori_loop(0, N_EXPERTS, lambda e, c: (tails(e, True), c)[1], 0)
        lax.fori_loop(nused_ref[0], n_blocks, lambda b, c: (unused(b, True), c)[1], 0)


def moe_dispatch(h2, dest, tables, n_rows):
    t, d = h2.shape
    n = SUB_TOKENS
    nsub = t // n
    src, dst, units, _, n_used, tail_dst, tail_units, tot = tables
    grid_spec = pltpu.PrefetchScalarGridSpec(
        num_scalar_prefetch=5,
        grid=(nsub,),
        in_specs=[pl.BlockSpec((n, d), lambda i, *_: (i, 0)), pl.BlockSpec((None, 2 * N_EXPERTS, n), lambda i, *_: (i, 0, 0)),
                  pl.BlockSpec((None, 2, 2 * N_EXPERTS), lambda i, *_: (i, 0, 0))],
        out_specs=pl.BlockSpec(memory_space=pl.ANY),
        scratch_shapes=[pltpu.VMEM((2, SUB_ROWS, d), BF16), pltpu.VMEM((max(SUB_TOKENS, MOE_BLOCK), d), BF16), pltpu.SemaphoreType.DMA((3,))],
    )
    return pl.pallas_call(
        _dispatch_kernel,
        grid_spec=grid_spec,
        out_shape=jax.ShapeDtypeStruct((n_rows, d), BF16),
        compiler_params=_cp(("arbitrary",)),
        name="moe_dispatch",
    )(_unit_dst(src, dst, units, nsub), tot, tail_dst, tail_units, n_used, h2, dest, _run_bounds(src, units, nsub))


def _expert_kernel(blk_exp_ref, n_used_ref, x_ref, w1_ref, w3_ref, w2_ref, y_ref):
    b = pl.program_id(0)

    @pl.when(b < n_used_ref[0])
    def _():
        x = x_ref[...]
        hid = _silu(_dot(x, w1_ref[...].astype(BF16))) * _dot(x, w3_ref[...].astype(BF16))
        y_ref[...] = _dot(hid.astype(BF16), w2_ref[...].astype(BF16)).astype(BF16)

    @pl.when(b >= n_used_ref[0])
    def _():
        y_ref[...] = jnp.zeros_like(y_ref)


def moe_experts(xs, w1, w3, w2, layer, tables, n_blocks):
    n_rows, d = xs.shape
    blk_exp, n_used = tables[3], tables[4]
    f = w1.shape[-1]

    def x_map(b, be, nu):
        return (jnp.minimum(b, nu[0] - 1), 0)

    def w_map(b, be, nu):
        return (layer, be[b], 0, 0)

    grid_spec = pltpu.PrefetchScalarGridSpec(
        num_scalar_prefetch=2,
        grid=(n_blocks,),
        in_specs=[pl.BlockSpec((MOE_BLOCK, d), x_map), pl.BlockSpec((None, None, d, f), w_map),
                  pl.BlockSpec((None, None, d, f), w_map), pl.BlockSpec((None, None, f, d), w_map)],
        out_specs=pl.BlockSpec((MOE_BLOCK, d), lambda b, be, nu: (b, 0)),
    )
    return pl.pallas_call(
        _expert_kernel,
        grid_spec=grid_spec,
        out_shape=jax.ShapeDtypeStruct((n_rows, d), BF16),
        compiler_params=_cp(("arbitrary",)),
        name="moe_experts",
    )(blk_exp, n_used, xs, w1, w3, w2)


def _combine_kernel(udst_ref, tot_ref, ys_ref, dcol_ref, wcol_ref, lohi_ref, h_ref, x_ref, g2_ref,
                    ws1_ref, ws3_ref, ws2_ref, o_ref, buf_ref, sem):
    i = pl.program_id(0)
    nsub = pl.num_programs(0)
    slot = i % 2
    n = h_ref.shape[0]
    chunk = 512

    def fetch(step, sl):
        buf_ref[sl] = jnp.zeros((SUB_ROWS, buf_ref.shape[2]), BF16)
        _unit_copies(udst_ref, step, tot_ref[step], buf_ref.at[sl], ys_ref, sem.at[sl], False)

    @pl.when(i == 0)
    def _():
        fetch(0, 0)

    for sl in range(2):
        @pl.when((slot != sl) & (i + 1 < nsub))
        def _():
            fetch(i + 1, sl)

    h = h_ref[...]
    shared = _dot((_silu(_dot(h, ws1_ref[...])) * _dot(h, ws3_ref[...])).astype(BF16), ws2_ref[...])
    dest_t = dcol_ref[...]
    w_t = wcol_ref[...]
    lo = lohi_ref[:, 0:1]
    hi = lohi_ref[:, 1:2]
    radix = jnp.where(lax.broadcasted_iota(I32, (2 * N_EXPERTS, 1), 0) < N_EXPERTS, float(DEST_RADIX), 1.0)
    r_e = lax.broadcasted_iota(I32, (2 * N_EXPERTS, chunk), 1).astype(F32)
    r_t = lax.broadcasted_iota(I32, (n, chunk), 1).astype(F32)
    for sl in range(2):
        @pl.when(slot == sl)
        def _():
            _wait_runs(tot_ref[i], buf_ref.at[sl], ys_ref, sem.at[sl], False)
            acc = shared
            for c in range(SUB_ROWS // chunk):
                own = jnp.where(r_e + c * chunk >= lo, jnp.where(r_e + c * chunk < hi, radix, 0.0), 0.0).astype(BF16)
                row_of = _dot(dest_t, own)
                pw = jnp.where(row_of == r_t + c * chunk, _dot(w_t, own), 0.0)
                acc = acc + _dot(pw.astype(BF16), buf_ref[sl, c * chunk:(c + 1) * chunk, :])
            o_ref[...] = x_ref[...] + g2_ref[...] * acc


def moe_combine(ys, dest, wts, h2, x1, g2, ws1, ws3, ws2, tables, seq):
    t, d = h2.shape
    n = SUB_TOKENS
    nsub = t // n
    src, dst, units, tot = tables[0], tables[1], tables[2], tables[7]
    dcol = jnp.swapaxes(dest, 1, 2)
    wcol = jnp.swapaxes(wts, 1, 2)
    bounds = jnp.swapaxes(_run_bounds(src, units, nsub), 1, 2)
    per_seq = seq // n
    f = ws1.shape[-1]
    pair = pl.BlockSpec((None, n, 2 * N_EXPERTS), lambda i, *_: (i, 0, 0))
    grid_spec = pltpu.PrefetchScalarGridSpec(
        num_scalar_prefetch=2,
        grid=(nsub,),
        in_specs=[pl.BlockSpec(memory_space=pl.ANY), pair, pair,
                  pl.BlockSpec((None, 2 * N_EXPERTS, 2), lambda i, *_: (i, 0, 0)),
                  pl.BlockSpec((n, d), lambda i, *_: (i, 0)), pl.BlockSpec((n, d), lambda i, *_: (i, 0)),
                  pl.BlockSpec((None, 1, d), lambda i, *_: (i // per_seq, 0, 0)),
                  pl.BlockSpec((d, f), lambda i, *_: (0, 0)), pl.BlockSpec((d, f), lambda i, *_: (0, 0)),
                  pl.BlockSpec((f, d), lambda i, *_: (0, 0))],
        out_specs=pl.BlockSpec((n, d), lambda i, *_: (i, 0)),
        scratch_shapes=[pltpu.VMEM((2, SUB_ROWS, d), BF16), pltpu.SemaphoreType.DMA((2,))],
    )
    return pl.pallas_call(
        _combine_kernel,
        grid_spec=grid_spec,
        out_shape=jax.ShapeDtypeStruct((t, d), F32),
        compiler_params=_cp(("arbitrary",)),
        name="moe_combine",
    )(_unit_dst(src, dst, units, nsub), tot, ys, dcol, wcol, bounds, h2, x1, g2, ws1.astype(BF16), ws3.astype(BF16), ws2.astype(BF16))


def _split_w_in(w):
    fh = FOX_HEADS * FOX_HEAD_DIM
    sizes = (fh, fh, fh, FOX_HEADS,
             2 * ML_HEADS * ML_DQK, ML_HEADS * ML_DV, ML_HEADS, ML_HEADS, ML_HEADS * ML_DV,
             HG_HEADS * HG_DK, HG_HEADS * HG_DK, BRANCH_WIDTH, BRANCH_WIDTH,
             N_BRANCH * D_MODEL)
    outs, o = [], 0
    for sz in sizes:
        outs.append(w[:, o:o + sz])
        o += sz
    return outs


def moe_ffn(h2, x1, g2, router_w, router_bias, w1, w3, w2, layer, ws1, ws3, ws2, seq):
    t, d = h2.shape
    dest, wts, cnt = moe_route(h2, router_w, router_bias)
    n_blocks = _max_blocks(t)
    tables = _run_tables(cnt, n_blocks)
    xs = moe_dispatch(h2, dest, tables, n_blocks * MOE_BLOCK)
    ys = moe_experts(xs, w1, w3, w2, layer, tables, n_blocks)
    return moe_combine(ys, dest, wts, h2, x1, g2, ws1, ws3, ws2, tables, seq)


def kernel(x, c, ada_w, ada_b, norm1_g, norm2_g, w_in, fox_bf, fox_q_g, fox_k_g, mlstm_conv, mlstm_bi, mlstm_bf, mlstm_norm_g, hgrn_lower_bounds, hgrn_norm_g, w_branch, w_out, router_w, router_bias, exp_w1, exp_w3, exp_w2, sh_w1, sh_w3, sh_w2):
    b, s, d = x.shape
    depth = ada_w.shape[0]
    mod = adaln_mod(c, ada_w, ada_b)
    lb_all = jnp.cumsum(jax.nn.softmax(hgrn_lower_bounds.astype(F32), axis=0), axis=0)
    lb_all = lb_all - lb_all[0]
    for l in range(depth):
        sh1, sc1, g1, sh2, sc2, g2 = [mod[l][:, d * j:d * (j + 1)] for j in range(6)]
        (wfq, wfk, wfv, wff, wmqk, wmv, wmi, wmf, wmo, whf, whq, whi, whg, wgates) = _split_w_in(w_in[l])
        h = norm_modulate(x, norm1_g[l], sc1, sh1)
        qp, kp, fv = fox_project(h, wfq, wfk, wfv, wff, fox_q_g[l], fox_k_g[l], fox_bf[l])
        y_fox = fox_attention(qp, kp, fv)
        mq, mk, mv, mog, mgates = ml_project(h, wmqk, wmv, wmi, wmf, wmo, mlstm_conv[l], mlstm_bi[l], mlstm_bf[l])
        y_ml = mlstm(mq, mk, mv, mog, mgates, mlstm_norm_g[l])
        hq, hk, hv, hog, hlf = hg_project(h, whf, whq, whi, whg, lb_all[l])
        y_hg = hgrn(hq, hk, hv, hog, hlf, hgrn_norm_g[l])
        x1, h2 = merge_branches(x, h, y_fox, y_ml, y_hg, wgates, w_branch[l], w_out[l], g1, norm2_g[l], sc2, sh2)
        x = moe_ffn(h2.reshape(b * s, d), x1.reshape(b * s, d), g2.reshape(b, 1, d), router_w[l], router_bias[l],
                    exp_w1, exp_w3, exp_w2, l, sh_w1[l], sh_w3[l], sh_w2[l], s).reshape(b, s, d)
    return x
```

```python
import functools

import jax
import jax.numpy as jnp
import numpy as np
from jax import lax
from jax.experimental import pallas as pl
from jax.experimental.pallas import tpu as pltpu

F32 = jnp.float32
BF16 = jnp.bfloat16
I32 = jnp.int32

LANES = 128
SUBLANES = 8
BF16_ROWS = 16
VMEM_BYTES = 64 * 1024 * 1024

D_MODEL = 1024
BRANCH_WIDTH = D_MODEL // 2
N_BRANCH = 3
FOX_HEAD_DIM = 64
FOX_HEADS = BRANCH_WIDTH // FOX_HEAD_DIM
ML_HEADS = 4
ML_DV = BRANCH_WIDTH // ML_HEADS
ML_DQK = ML_DV // 2
ML_CONV = 4
ML_CHUNK = 128
HG_HEADS = 4
HG_DK = 128
HG_CHUNK = 128
N_EXPERTS = 64
N_GROUPS = 8
GROUP_SIZE = N_EXPERTS // N_GROUPS
TOPK_GROUPS = 4
TOP_K = 8
D_FF = D_MODEL // 4
ROUTE_SCALE = 2.5
MOE_BLOCK = 1024
SORT_CHUNK = 1024
EPS = 1e-6
NEG = -1e30
LOG2E = 1.4426950408889634

ROW_TILE = 512
PROJ_TILE = 1024
ATTN_TILE = 256
SUB_TOKENS = 512
RUN_ALIGN = BF16_ROWS
SUB_ROWS = ((SUB_TOKENS * TOP_K + N_EXPERTS * (RUN_ALIGN - 1)) + 255) // 256 * 256
RUN_BITS = tuple(1 << b for b in range((SUB_TOKENS // RUN_ALIGN).bit_length()))
DEST_RADIX = 64
DEST_NONE = DEST_RADIX * 127
assert SUB_ROWS <= DEST_NONE


def _cp(sem, vmem_mb=48):
    return pltpu.CompilerParams(dimension_semantics=sem, vmem_limit_bytes=vmem_mb * 1024 * 1024)


def _dot(a, b):
    return jnp.dot(a, b, preferred_element_type=F32)


def _dot_nt(a, b):
    return lax.dot_general(a, b, (((1,), (1,)), ((), ())), preferred_element_type=F32)


def _dot_tn(a, b):
    return lax.dot_general(a, b, (((0,), (0,)), ((), ())), preferred_element_type=F32)


def _split3(x):
    hi = x.astype(BF16)
    r = x - hi.astype(F32)
    mid = r.astype(BF16)
    lo = (r - mid.astype(F32)).astype(BF16)
    return hi, mid, lo


def _tri_dot(tri, x):
    hi, mid, lo = _split3(x)
    return (_dot(tri, hi) + _dot(tri, mid)) + _dot(tri, lo)


def _dot_tri(x, tri):
    hi, mid, lo = _split3(x)
    return (_dot(hi, tri) + _dot(mid, tri)) + _dot(lo, tri)


def _log_sigmoid(x):
    return jnp.minimum(x, 0.0) - jnp.log1p(jnp.exp(-jnp.abs(x)))


def _silu(x):
    return x * jax.nn.sigmoid(x)


def _tri_incl(n, dtype=BF16):
    r = lax.broadcasted_iota(I32, (n, n), 0)
    c = lax.broadcasted_iota(I32, (n, n), 1)
    return jnp.where(c <= r, 1.0, 0.0).astype(dtype)


def _mod_kernel(c_ref, w_ref, b_ref, o_ref):
    cond = _silu(c_ref[...])
    hi, mid, lo = _split3(cond)
    w = w_ref[...]
    whi, wmid, wlo = _split3(w)
    acc = _dot(hi, whi) + (_dot(hi, wmid) + _dot(mid, whi))
    acc = acc + (_dot(mid, wmid) + _dot(hi, wlo) + _dot(lo, whi))
    o_ref[...] = acc + b_ref[...]


def adaln_mod(c, ada_w, ada_b):
    depth, d, n = ada_w.shape
    b = c.shape[0]
    tn = 1024
    return pl.pallas_call(
        _mod_kernel,
        grid=(depth, n // tn),
        in_specs=[
            pl.BlockSpec((b, d), lambda l, j: (0, 0)),
            pl.BlockSpec((None, d, tn), lambda l, j: (l, 0, j)),
            pl.BlockSpec((None, 1, tn), lambda l, j: (l, 0, j)),
        ],
        out_specs=pl.BlockSpec((None, b, tn), lambda l, j: (l, 0, j)),
        out_shape=jax.ShapeDtypeStruct((depth, b, n), F32),
        compiler_params=_cp(("parallel", "parallel")),
        name="adaln_mod",
    )(c, ada_w, ada_b.reshape(depth, 1, n))


def _norm_mod(x, g, sc, sh):
    ms = jnp.mean(x * x, axis=-1, keepdims=True)
    return x * lax.rsqrt(ms + EPS) * g * (1.0 + sc) + sh


def _norm_kernel(x_ref, g_ref, sc_ref, sh_ref, h_ref):
    h_ref[...] = _norm_mod(x_ref[...], g_ref[...], sc_ref[...], sh_ref[...]).astype(BF16)


def norm_modulate(x, g, sc, sh):
    b, s, d = x.shape
    tm = ROW_TILE
    return pl.pallas_call(
        _norm_kernel,
        grid=(b, s // tm),
        in_specs=[
            pl.BlockSpec((None, tm, d), lambda i, j: (i, j, 0)),
            pl.BlockSpec((1, d), lambda i, j: (0, 0)),
            pl.BlockSpec((None, 1, d), lambda i, j: (i, 0, 0)),
            pl.BlockSpec((None, 1, d), lambda i, j: (i, 0, 0)),
        ],
        out_specs=pl.BlockSpec((None, tm, d), lambda i, j: (i, j, 0)),
        out_shape=jax.ShapeDtypeStruct((b, s, d), BF16),
        compiler_params=_cp(("parallel", "parallel")),
        name="norm_modulate",
    )(x, g.reshape(1, d), sc.reshape(b, 1, d), sh.reshape(b, 1, d))


FOX_BIAS_PIECES = 3
FOX_QB_LANE = FOX_HEAD_DIM
FOX_KB_LANE = FOX_HEAD_DIM + FOX_BIAS_PIECES


def _pack_pieces(x):
    hi, mid, lo = _split3(x)
    p = hi.astype(F32) + pltpu.roll(mid.astype(F32), FOX_HEADS, axis=1) + pltpu.roll(lo.astype(F32), 2 * FOX_HEADS, axis=1)
    return p.astype(BF16)


def _fox_proj_kernel(h_ref, wq_ref, wk_ref, wvt_ref, wf_ref, gq_ref, gk_ref, bf_ref, eq_ref, ek_ref, cq_ref, ck_ref,
                     q_out, k_out, vt_out, carry_ref):
    @pl.when(pl.program_id(1) == 0)
    def _():
        carry_ref[...] = jnp.zeros_like(carry_ref)

    h = h_ref[...]
    tm = h.shape[0]
    pr = lax.broadcasted_iota(I32, (2 * LANES, 2 * LANES), 0)
    pc = lax.broadcasted_iota(I32, (2 * LANES, 2 * LANES), 1)
    avg_pair = jnp.where((pr < LANES) == (pc < LANES), 1.0 / FOX_HEAD_DIM, 0.0).astype(BF16)

    def head_norm(x):
        outs = []
        for pair in range(FOX_HEADS // 2):
            xp = x[:, 2 * LANES * pair:2 * LANES * (pair + 1)]
            ms = _dot((xp * xp).astype(BF16), avg_pair)
            outs.append(xp * lax.rsqrt(ms + EPS))
        return jnp.concatenate(outs, axis=1)

    lane = lax.broadcasted_iota(I32, (tm, LANES), 1)
    logf = jnp.where(lane < FOX_HEADS, _log_sigmoid(_dot(h, wf_ref[...]) + bf_ref[...]), 0.0)
    cs = _dot(_tri_incl(tm), _pack_pieces(logf))
    cum = cs + pltpu.roll(cs, LANES - FOX_HEADS, axis=1) + pltpu.roll(cs, LANES - 2 * FOX_HEADS, axis=1)
    cum = jnp.where(lane < FOX_HEADS, cum, 0.0) + carry_ref[...]
    carry_ref[...] = cum[tm - 1:tm, :]
    pieces = _pack_pieces(cum * LOG2E)

    qn = head_norm(_dot(h, wq_ref[...])) * gq_ref[...]
    q_out[...] = (qn + _dot(pieces, eq_ref[...]) + cq_ref[...]).astype(BF16)
    kn = head_norm(_dot(h, wk_ref[...])) * gk_ref[...]
    k_out[...] = (kn + _dot(pieces, ek_ref[...]) + ck_ref[...]).astype(BF16)
    vt = _dot_nt(wvt_ref[...], h)
    ones = jnp.ones((FOX_HEAD_DIM, tm), F32)
    slots = []
    for hd in range(FOX_HEADS):
        slots += [vt[FOX_HEAD_DIM * hd:FOX_HEAD_DIM * (hd + 1), :], ones]
    vt_out[...] = jnp.concatenate(slots, axis=0).astype(BF16)


def _pad_heads(w, heads, dim):
    lead = w.shape[:-1]
    w = w.reshape(*lead, heads, dim)
    w = jnp.pad(w, [(0, 0)] * len(lead) + [(0, 0), (0, LANES - dim)])
    return w.reshape(*lead, heads * LANES)


def _fox_constants():
    eq = np.zeros((LANES, FOX_HEADS * LANES), np.float32)
    ek = np.zeros((LANES, FOX_HEADS * LANES), np.float32)
    cq = np.zeros((1, FOX_HEADS * LANES), np.float32)
    ck = np.zeros((1, FOX_HEADS * LANES), np.float32)
    for hd in range(FOX_HEADS):
        for p in range(FOX_BIAS_PIECES):
            eq[p * FOX_HEADS + hd, LANES * hd + FOX_QB_LANE + p] = 1.0
            ek[p * FOX_HEADS + hd, LANES * hd + FOX_KB_LANE + p] = -1.0
            cq[0, LANES * hd + FOX_KB_LANE + p] = 1.0
            ck[0, LANES * hd + FOX_QB_LANE + p] = 1.0
    return jnp.asarray(eq, BF16), jnp.asarray(ek, BF16), jnp.asarray(cq), jnp.asarray(ck)


def fox_project(h, wq, wk, wv, wf, q_g, k_g, bf):
    b, s, d = h.shape
    tm = min(PROJ_TILE, s)
    hp = FOX_HEADS * LANES
    wq_p = _pad_heads(wq, FOX_HEADS, FOX_HEAD_DIM).astype(BF16)
    wk_p = _pad_heads(wk, FOX_HEADS, FOX_HEAD_DIM).astype(BF16)
    wf_p = jnp.pad(wf, ((0, 0), (0, LANES - FOX_HEADS))).astype(BF16)
    gq = _pad_heads(jnp.tile(q_g * (FOX_HEAD_DIM ** -0.5 * LOG2E), FOX_HEADS)[None, :], FOX_HEADS, FOX_HEAD_DIM)
    gk = _pad_heads(jnp.tile(k_g, FOX_HEADS)[None, :], FOX_HEADS, FOX_HEAD_DIM)
    bf_p = jnp.pad(bf, (0, LANES - FOX_HEADS))[None, :]
    eq, ek, cq, ck = _fox_constants()
    const = lambda shape: pl.BlockSpec(shape, lambda i, j: (0,) * len(shape))
    row = lambda n: pl.BlockSpec((None, tm, n), lambda i, j: (i, j, 0))
    return pl.pallas_call(
        _fox_proj_kernel,
        grid=(b, s // tm),
        in_specs=[row(d), const((d, hp)), const((d, hp)), const((BRANCH_WIDTH, d)), const((d, LANES)),
                  const((1, hp)), const((1, hp)), const((1, LANES)), const((LANES, hp)), const((LANES, hp)),
                  const((1, hp)), const((1, hp))],
        out_specs=[row(hp), row(hp), pl.BlockSpec((None, hp, tm), lambda i, j: (i, 0, j))],
        out_shape=[jax.ShapeDtypeStruct((b, s, hp), BF16), jax.ShapeDtypeStruct((b, s, hp), BF16),
                   jax.ShapeDtypeStruct((b, hp, s), BF16)],
        scratch_shapes=[pltpu.VMEM((1, LANES), F32)],
        compiler_params=_cp(("parallel", "arbitrary")),
        name="fox_project",
    )(h, wq_p, wk_p, wv.T.astype(BF16), wf_p, gq, gk, bf_p, eq, ek, cq, ck)


ATTN_HEADS = 8


def _fox_attn_kernel(q_ref, k_ref, vt_ref, o_ref):
    i = pl.program_id(2)
    t = q_ref.shape[0]
    krow = lax.broadcasted_iota(I32, (t, t), 0)
    qcol = lax.broadcasted_iota(I32, (t, t), 1)
    qs = [q_ref[:, LANES * a:LANES * (a + 1)] for a in range(ATTN_HEADS)]

    def scores(j):
        start = pl.multiple_of(j * t, t)
        return tuple(_dot_nt(k_ref[pl.ds(start, t), LANES * a:LANES * (a + 1)], qs[a]) for a in range(ATTN_HEADS))

    def consume(j, state, ss, masked):
        start = pl.multiple_of(j * t, t)
        new = []
        for a in range(ATTN_HEADS):
            m, acc = state[a]
            s = jnp.where(krow <= qcol, ss[a], NEG) if masked else ss[a]
            m_new = jnp.maximum(m, jnp.max(s, axis=0, keepdims=True))
            p = jnp.exp2(s - m_new)
            alpha = jnp.exp2(m - m_new)
            vt = vt_ref[LANES * a:LANES * (a + 1), pl.ds(start, t)]
            acc = alpha * acc + _dot(vt, p.astype(BF16))
            new.append((m_new, acc))
        return tuple(new)

    def body(j, state):
        return consume(j, state, scores(j), False)

    init = tuple((jnp.full((1, t), NEG, F32), jnp.zeros((LANES, t), F32)) for _ in range(ATTN_HEADS))
    state = lax.fori_loop(0, i, body, init)
    state = consume(i, state, scores(i), True)
    for p in range(ATTN_HEADS // 2):
        halves = []
        for _, acc in (state[2 * p], state[2 * p + 1]):
            halves.append(acc[:FOX_HEAD_DIM, :] / acc[FOX_HEAD_DIM:FOX_HEAD_DIM + 1, :])
        o_ref[:, LANES * p:LANES * (p + 1)] = jnp.concatenate(halves, axis=0).T.astype(BF16)


def fox_attention(qp, kp, vt):
    b, s, hp = qp.shape
    t = ATTN_TILE
    groups = FOX_HEADS // ATTN_HEADS
    return pl.pallas_call(
        _fox_attn_kernel,
        grid=(b, groups, s // t),
        in_specs=[
            pl.BlockSpec((None, t, ATTN_HEADS * LANES), lambda bi, p, i: (bi, i, p)),
            pl.BlockSpec((None, s, ATTN_HEADS * LANES), lambda bi, p, i: (bi, 0, p)),
            pl.BlockSpec((None, ATTN_HEADS * LANES, s), lambda bi, p, i: (bi, p, 0)),
        ],
        out_specs=pl.BlockSpec((None, t, ATTN_HEADS // 2 * LANES), lambda bi, p, i: (bi, i, p)),
        out_shape=jax.ShapeDtypeStruct((b, s, BRANCH_WIDTH), BF16),
        compiler_params=_cp(("parallel", "parallel", "arbitrary")),
        name="fox_attention",
    )(qp, kp, vt)


CONV_HALO = SUBLANES


def _ml_proj_kernel(h_ref, wqk_ref, wvt_ref, wo_ref, wg_ref, conv_ref, gb_ref, q_out, k_out, vt_out, og_out, g_out, buf_ref):
    tm = h_ref.shape[0]
    half = ML_HEADS * LANES

    @pl.when(pl.program_id(1) == 0)
    def _():
        buf_ref[0:CONV_HALO, :] = jnp.zeros((CONV_HALO, 2 * half), F32)

    h = h_ref[...]
    buf_ref[CONV_HALO:CONV_HALO + tm, :] = _dot(h, wqk_ref[...])
    acc = jnp.zeros((tm, 2 * half), F32)
    for j in range(ML_CONV):
        off = CONV_HALO - (ML_CONV - 1) + j
        acc = acc + conv_ref[j:j + 1, :] * buf_ref[off:off + tm, :]
    buf_ref[0:CONV_HALO, :] = buf_ref[tm:tm + CONV_HALO, :]
    act = _silu(acc)
    q_out[...] = act[:, :half].astype(BF16)
    k_out[...] = (act[:, half:] * (ML_DQK ** -0.5)).astype(BF16)
    vt_out[...] = _dot_nt(wvt_ref[...], h).astype(BF16)
    og_out[...] = jax.nn.sigmoid(_dot(h, wo_ref[...])).astype(BF16)
    g = _dot(h, wg_ref[...]) + gb_ref[...]
    lane = lax.broadcasted_iota(I32, (tm, LANES), 1)
    g_out[...] = jnp.where(lane < ML_HEADS, g, _log_sigmoid(g))


def ml_project(h, wqk, wv, wi, wf, wo, conv, bi, bf):
    b, s, d = h.shape
    tm = min(PROJ_TILE, s)
    half = ML_HEADS * LANES
    nq = ML_HEADS * ML_DQK
    wqk_p = jnp.concatenate([_pad_heads(wqk[:, :nq], ML_HEADS, ML_DQK), _pad_heads(wqk[:, nq:], ML_HEADS, ML_DQK)], axis=1).astype(BF16)
    conv_p = jnp.concatenate([_pad_heads(conv[:, :nq], ML_HEADS, ML_DQK), _pad_heads(conv[:, nq:], ML_HEADS, ML_DQK)], axis=1)
    wg = jnp.pad(jnp.concatenate([wi, wf], axis=1), ((0, 0), (0, LANES - 2 * ML_HEADS))).astype(BF16)
    gb = jnp.pad(jnp.concatenate([bi, bf]), (0, LANES - 2 * ML_HEADS))[None, :]
    const = lambda shape: pl.BlockSpec(shape, lambda i, j: (0,) * len(shape))
    row = lambda n: pl.BlockSpec((None, tm, n), lambda i, j: (i, j, 0))
    return pl.pallas_call(
        _ml_proj_kernel,
        grid=(b, s // tm),
        in_specs=[row(d), const((d, 2 * half)), const((BRANCH_WIDTH, d)), const((d, BRANCH_WIDTH)), const((d, LANES)),
                  const((ML_CONV, 2 * half)), const((1, LANES))],
        out_specs=[row(half), row(half), pl.BlockSpec((None, BRANCH_WIDTH, tm), lambda i, j: (i, 0, j)), row(BRANCH_WIDTH), row(LANES)],
        out_shape=[jax.ShapeDtypeStruct((b, s, half), BF16), jax.ShapeDtypeStruct((b, s, half), BF16),
                   jax.ShapeDtypeStruct((b, BRANCH_WIDTH, s), BF16), jax.ShapeDtypeStruct((b, s, BRANCH_WIDTH), BF16),
                   jax.ShapeDtypeStruct((b, s, LANES), F32)],
        scratch_shapes=[pltpu.VMEM((tm + CONV_HALO, 2 * half), F32)],
        compiler_params=_cp(("parallel", "arbitrary")),
        name="ml_project",
    )(h, wqk_p, wv.T.astype(BF16), wo.astype(BF16), wg, conv_p, gb)


def _mlstm_kernel(q_ref, k_ref, vt_ref, og_ref, g_ref, gt_ref, ng_ref, o_ref, c_ref, n_ref, m_ref):
    nb, s = q_ref.shape[0], q_ref.shape[1]
    L = ML_CHUNK
    c_ref[...] = jnp.zeros_like(c_ref)
    n_ref[...] = jnp.zeros_like(n_ref)
    m_ref[...] = jnp.zeros_like(m_ref)
    tril = _tri_incl(L)
    triu = tril.T
    srow = lax.broadcasted_iota(I32, (L, L), 0)
    tcol = lax.broadcasted_iota(I32, (L, L), 1)
    causal = srow <= tcol

    def chunk(c, carry):
        for bi in range(nb):
            one_chunk(c, bi)
        return carry

    def one_chunk(c, bi):
        r0 = pl.multiple_of(c * L, L)
        g = g_ref[bi, pl.ds(r0, L), :]
        gt = gt_ref[bi, :, pl.ds(r0, L)]
        bc = _tri_dot(tril, g)
        br = _dot_tri(gt, triu)
        for hd in range(ML_HEADS):
            sl = slice(LANES * hd, LANES * (hd + 1))
            st = bi * ML_HEADS + hd
            ccol = g[:, hd:hd + 1] - bc[:, ML_HEADS + hd:ML_HEADS + hd + 1]
            brow = br[ML_HEADS + hd:ML_HEADS + hd + 1, :]
            irow = gt[hd:hd + 1, :]
            m_prev = m_ref[st][:, 0:1]
            log_d = jnp.where(causal, brow + ccol, -jnp.inf)
            log_inter = brow + m_prev
            m_t = jnp.maximum(jnp.max(log_d, axis=0, keepdims=True), log_inter)
            w_intra = jnp.exp(log_d - m_t)
            w_inter = jnp.exp(log_inter - m_t)
            qc = q_ref[bi, pl.ds(r0, L), sl]
            kc = k_ref[bi, pl.ds(r0, L), sl]
            vt = vt_ref[bi, sl, pl.ds(r0, L)]
            sc = _dot_nt(kc, qc) * w_intra
            cst = c_ref[st]
            nst = n_ref[st]
            num = _dot(vt, sc.astype(BF16)) + w_inter * _dot_nt(cst.astype(BF16), qc)
            qn = _dot_nt(jnp.broadcast_to(nst, (SUBLANES, LANES)).astype(BF16), qc)[0:1, :]
            den = jnp.sum(sc, axis=0, keepdims=True) + w_inter * qn
            hout = num / jnp.maximum(jnp.abs(den), jnp.exp(-m_t))
            b_last = brow[:, L - 1:L]
            lw = b_last - brow + irow
            m_new = jnp.maximum(b_last + m_prev, jnp.max(lw, axis=1, keepdims=True))
            w_in = jnp.exp(lw - m_new)
            decay = jnp.exp(b_last + m_prev - m_new)
            c_ref[st] = decay * cst + _dot((vt.astype(F32) * w_in).astype(BF16), kc)
            n_ref[st] = decay * nst + _dot(jnp.broadcast_to(w_in, (SUBLANES, L)).astype(BF16), kc)[0:1, :]
            m_ref[st] = jnp.broadcast_to(m_new, (1, LANES))
            ms = jnp.mean(hout * hout, axis=0, keepdims=True)
            y = (hout * lax.rsqrt(ms + EPS)).T * (og_ref[bi, pl.ds(r0, L), sl].astype(F32) * ng_ref[:, sl])
            o_ref[bi, pl.ds(r0, L), sl] = y.astype(BF16)

    lax.fori_loop(0, s // L, chunk, 0)


ML_SEQS = 2


def mlstm(q, k, vt, og, gates, norm_g):
    b, w, s = vt.shape
    nb = ML_SEQS if b % ML_SEQS == 0 else 1
    gt = jnp.swapaxes(gates[:, :, :2 * ML_HEADS], 1, 2)
    seq = lambda n: pl.BlockSpec((nb, s, n), lambda i: (i, 0, 0))
    return pl.pallas_call(
        _mlstm_kernel,
        grid=(b // nb,),
        in_specs=[seq(ML_HEADS * LANES), seq(ML_HEADS * LANES), pl.BlockSpec((nb, w, s), lambda i: (i, 0, 0)), seq(w), seq(LANES),
                  pl.BlockSpec((nb, 2 * ML_HEADS, s), lambda i: (i, 0, 0)),
                  pl.BlockSpec((1, w), lambda i: (0, 0))],
        out_specs=seq(w),
        out_shape=jax.ShapeDtypeStruct((b, s, w), BF16),
        scratch_shapes=[pltpu.VMEM((nb * ML_HEADS, ML_DV, LANES), F32), pltpu.VMEM((nb * ML_HEADS, 1, LANES), F32),
                        pltpu.VMEM((nb * ML_HEADS, 1, LANES), F32)],
        compiler_params=_cp(("parallel",)),
        name="mlstm",
    )(q, k, vt, og, gates, gt, norm_g.reshape(1, w))


def _hg_proj_kernel(h_ref, wf_ref, wq_ref, wi_ref, wg_ref, lb_ref, q_out, k_out, v_out, og_out, lf_out):
    h = h_ref[...]
    fz = _dot(h, wf_ref[...])
    log_lb = lb_ref[0:1, :]
    log_1m = lb_ref[1:2, :]
    one_m = lb_ref[2:3, :]
    a = log_lb
    bb = log_1m + _log_sigmoid(fz)
    lf_out[...] = jnp.maximum(a, bb) + jnp.log1p(jnp.exp(-jnp.abs(a - bb)))
    k_out[...] = (one_m * jax.nn.sigmoid(-fz)).astype(BF16)
    q_out[...] = _silu(_dot(h, wq_ref[...])).astype(BF16)
    v_out[...] = _dot(h, wi_ref[...]).astype(BF16)
    og_out[...] = _silu(_dot(h, wg_ref[...])).astype(BF16)


def hg_project(h, wf, wq, wi, wg, lb):
    b, s, d = h.shape
    tm = min(PROJ_TILE, s)
    w = BRANCH_WIDTH
    lbp =jnp.stack([jnp.log(lb), jnp.log1p(-lb), 1.0 - lb], axis=0)
    const = lambda shape: pl.BlockSpec(shape, lambda i, j: (0,) * len(shape))
    row = lambda n: pl.BlockSpec((None, tm, n), lambda i, j: (i, j, 0))
    return pl.pallas_call(
        _hg_proj_kernel,
        grid=(b, s // tm),
        in_specs=[row(d), const((d, w)), const((d, w)), const((d, w)), const((d, w)), const((3, w))],
        out_specs=[row(w), row(w), row(w), row(w), row(w)],
        out_shape=[jax.ShapeDtypeStruct((b, s, w), BF16)] * 4 + [jax.ShapeDtypeStruct((b, s, w), F32)],
        compiler_params=_cp(("parallel", "parallel")),
        name="hg_project",
    )(h, wf.astype(BF16), wq.astype(BF16), wi.astype(BF16), wg.astype(BF16), lbp)


HG_LEVELS = tuple(HG_CHUNK >> (i + 1) for i in range(HG_CHUNK.bit_length() - 1))


def _hg_tables():
    L = HG_CHUNK
    t = np.arange(L)
    tri = (t[None, :] <= t[:, None]).astype(np.float32)
    mats = [tri]
    x = t[:, None] ^ t[None, :]
    lvl = np.full((L, L), -1, np.int32)
    lvl[t[:, None] == t[None, :]] = 0
    for i, m in enumerate(HG_LEVELS):
        if m < SUBLANES:
            mats.append(tri[(t // (2 * m)) * (2 * m) + m - 1])
        lvl[(t[:, None] > t[None, :]) & (x >= m) & (x < 2 * m)] = i + 1
    return jnp.asarray(np.concatenate(mats, axis=0), BF16), jnp.asarray(lvl)


def _hgrn_kernel(q_ref, k_ref, v_ref, og_ref, lf_ref, ng_ref, tall_ref, lvl_ref, o_ref, st_ref):
    s, wd = q_ref.shape
    L = HG_CHUNK
    st_ref[...] = jnp.zeros_like(st_ref)
    rowi = lax.broadcasted_iota(I32, (L, wd), 0)

    lvl = lvl_ref[...]
    level_masks = [lvl == i for i in range(len(HG_LEVELS) + 1)]

    def chunk(c, carry):
        r0 = pl.multiple_of(c * L, L)
        tall = tall_ref[...]
        g = lf_ref[pl.ds(r0, L), :]
        hi = g.astype(BF16)
        mid = (g - hi.astype(F32)).astype(BF16)
        cums = _dot(tall, hi) + _dot(tall, mid)
        a = cums[0:L]
        qb = q_ref[pl.ds(r0, L), :]
        kb = k_ref[pl.ds(r0, L), :]
        qf = qb.astype(F32)
        kf = kb.astype(F32)
        ws = []
        fine = 0
        for m in HG_LEVELS:
            if m >= SUBLANES:
                ref = jnp.concatenate([jnp.broadcast_to(a[g0 + m - 1:g0 + m, :], (2 * m, wd)) for g0 in range(0, L, 2 * m)], axis=0)
            else:
                fine += 1
                ref = cums[L * fine:L * (fine + 1)]
            e = jnp.exp(-jnp.abs(a - ref))
            ws.append((e * jnp.where((rowi & m) != 0, qf, kf)).astype(BF16))
        a_last = a[L - 1:L, :]
        qa = (qf * jnp.exp(a)).astype(BF16)
        kt = (kf * jnp.exp(a_last - a)).astype(BF16)
        decay = jnp.exp(a_last)
        for hd in range(HG_HEADS):
            sl = slice(LANES * hd, LANES * (hd + 1))
            vb = v_ref[pl.ds(r0, L), sl]
            sc = jnp.where(level_masks[0], _dot_nt(qb[:, sl], kb[:, sl]), 0.0)
            for i in range(len(HG_LEVELS)):
                w = ws[i][:, sl]
                sc = jnp.where(level_masks[i + 1], _dot_nt(w, w), sc)
            st = st_ref[hd]
            out = _dot(sc.astype(BF16), vb) + _dot_nt(qa[:, sl], st.astype(BF16))
            st_ref[hd] = st * decay[:, sl] + _dot_tn(vb, kt[:, sl])
            ms = jnp.mean(out * out, axis=-1, keepdims=True)
            y = (out * lax.rsqrt(ms + EPS)) * ng_ref[:, sl] * og_ref[pl.ds(r0, L), sl].astype(F32)
            o_ref[pl.ds(r0, L), sl] = y.astype(BF16)
        return carry

    lax.fori_loop(0, s // L, chunk, 0)


def hgrn(q, k, v, og, logf, norm_g):
    b, s, w = v.shape
    tall, lvl = _hg_tables()
    seq = pl.BlockSpec((None, s, w), lambda i: (i, 0, 0))
    const = lambda shape: pl.BlockSpec(shape, lambda i: (0,) * len(shape))
    return pl.pallas_call(
        _hgrn_kernel,
        grid=(b,),
        in_specs=[seq, seq, seq, seq, seq, const((1, w)), const(tall.shape), const(lvl.shape)],
        out_specs=seq,
        out_shape=jax.ShapeDtypeStruct((b, s, w), BF16),
        scratch_shapes=[pltpu.VMEM((HG_HEADS, LANES, HG_DK), F32)],
        compiler_params=_cp(("parallel",)),
        name="hgrn",
    )(q, k, v, og, logf, norm_g.reshape(1, w), tall, lvl)


def _merge_kernel(x_ref, h_ref, yf_ref, ym_ref, yh_ref, wg_ref, wb_ref, wo_ref, g1_ref, n2_ref, sc2_ref, sh2_ref,
                  x_out, h2_out):
    d = x_ref.shape[1]
    h = h_ref[...]
    merged = None
    for br, y_ref in enumerate((yf_ref, ym_ref, yh_ref)):
        gate = jax.nn.sigmoid(_dot(h, wg_ref[:, d * br:d * (br + 1)]))
        term = gate * _dot(y_ref[...], wb_ref[br])
        merged = term if merged is None else merged + term
    mixed = _dot(merged.astype(BF16), wo_ref[...])
    x1 = x_ref[...] + g1_ref[...] * mixed
    x_out[...] = x1
    h2_out[...] = _norm_mod(x1, n2_ref[...], sc2_ref[...], sh2_ref[...]).astype(BF16)


def merge_branches(x, h, y_fox, y_ml, y_hg, w_gates, w_branch, w_out, g1, norm2_g, sc2, sh2):
    b, s, d = x.shape
    tm = ROW_TILE
    w = BRANCH_WIDTH
    const = lambda shape: pl.BlockSpec(shape, lambda i, j: (0,) * len(shape))
    row = lambda n: pl.BlockSpec((None, tm, n), lambda i, j: (i, j, 0))
    per_b = pl.BlockSpec((None, 1, d), lambda i, j: (i, 0, 0))
    return pl.pallas_call(
        _merge_kernel,
        grid=(b, s // tm),
        in_specs=[row(d), row(d), row(w), row(w), row(w), const((d, N_BRANCH * d)), const((N_BRANCH, w, d)), const((d, d)),
                  per_b, const((1, d)), per_b, per_b],
        out_specs=[row(d), row(d)],
        out_shape=[jax.ShapeDtypeStruct((b, s, d), F32), jax.ShapeDtypeStruct((b, s, d), BF16)],
        compiler_params=_cp(("parallel", "parallel"), 56),
        name="merge_branches",
    )(x, h, y_fox, y_ml, y_hg, w_gates.astype(BF16), w_branch.astype(BF16), w_out.astype(BF16),
      g1.reshape(b, 1, d), norm2_g.reshape(1, d), sc2.reshape(b, 1, d), sh2.reshape(b, 1, d))


def _first_max(x, iota, size):
    m = jnp.max(x, axis=0, keepdims=True)
    idx = jnp.min(jnp.where(x == m, iota, size), axis=0, keepdims=True)
    return m, idx


def _route_kernel(h_ref, wr_ref, rb_ref, dest_out, w_out, cnt_out):
    n = h_ref.shape[0]
    scores = jax.nn.sigmoid(_dot_nt(wr_ref[...], h_ref[...]))
    choice = scores + rb_ref[...]
    e_iota = lax.broadcasted_iota(I32, (N_EXPERTS, n), 0)
    c3 = choice.reshape(N_GROUPS, GROUP_SIZE, n)
    i3 = lax.broadcasted_iota(I32, (N_GROUPS, GROUP_SIZE, n), 1)
    m1 = jnp.max(c3, axis=1, keepdims=True)
    i1 = jnp.min(jnp.where(c3 == m1, i3, GROUP_SIZE), axis=1, keepdims=True)
    m2 = jnp.max(jnp.where(i3 == i1, -jnp.inf, c3), axis=1, keepdims=True)
    gs = (m1 + m2).reshape(N_GROUPS, n)
    g_iota = lax.broadcasted_iota(I32, (N_GROUPS, n), 0)
    gsel = jnp.zeros((N_GROUPS, n), F32)
    for _ in range(TOPK_GROUPS):
        _, gi = _first_max(gs, g_iota, N_GROUPS)
        hit = g_iota == gi
        gsel = jnp.where(hit, 1.0, gsel)
        gs = jnp.where(hit, -jnp.inf, gs)
    gmask = jnp.broadcast_to(gsel.reshape(N_GROUPS, 1, n), (N_GROUPS, GROUP_SIZE, n)).reshape(N_EXPERTS, n)
    masked = jnp.where(gmask > 0.0, choice, -jnp.inf)
    sel = jnp.zeros((N_EXPERTS, n), F32)
    for _ in range(TOP_K):
        _, ei = _first_max(masked, e_iota, N_EXPERTS)
        hit = e_iota == ei
        sel = jnp.where(hit, 1.0, sel)
        masked = jnp.where(hit, -jnp.inf, masked)
    tr = lax.broadcasted_iota(I32, (n, n), 0)
    tc = lax.broadcasted_iota(I32, (n, n), 1)
    before = jnp.where(tr < tc, 1.0, 0.0).astype(BF16)
    pos = _dot(sel.astype(BF16), before)
    cnt = jnp.sum(sel, axis=1, keepdims=True)
    units = jnp.floor((cnt + (RUN_ALIGN - 1)) * (1.0 / RUN_ALIGN))
    er = lax.broadcasted_iota(I32, (N_EXPERTS, N_EXPERTS), 0)
    ec = lax.broadcasted_iota(I32, (N_EXPERTS, N_EXPERTS), 1)
    lower = jnp.where(ec < er, 1.0, 0.0).astype(BF16)
    off = _dot(lower, jnp.broadcast_to(units, (N_EXPERTS, LANES)).astype(BF16))[:, 0:1] * RUN_ALIGN
    dest = jnp.where(sel > 0.0, off + pos, float(DEST_NONE))
    dhi = jnp.floor(dest * (1.0 / DEST_RADIX))
    dest_out[0:N_EXPERTS, :] = dhi.astype(BF16)
    dest_out[N_EXPERTS:, :] = (dest - dhi * DEST_RADIX).astype(BF16)
    wsum = jnp.sum(scores * sel, axis=0, keepdims=True)
    w_out[0:N_EXPERTS, :] = jnp.zeros((N_EXPERTS, n), BF16)
    w_out[N_EXPERTS:, :] = (scores * sel / wsum * ROUTE_SCALE).astype(BF16)
    cnt_out[...] = jnp.broadcast_to(cnt, (N_EXPERTS, LANES)).astype(I32)


def moe_route(h2, router_w, router_bias):
    t, d = h2.shape
    n = SUB_TOKENS
    nsub = t // n
    mat = pl.BlockSpec((None, 2 * N_EXPERTS, n), lambda i: (i, 0, 0))
    dest, wts, cnt = pl.pallas_call(
        _route_kernel,
        grid=(nsub,),
        in_specs=[pl.BlockSpec((n, d), lambda i: (i, 0)), pl.BlockSpec((N_EXPERTS, d), lambda i: (0, 0)),
                  pl.BlockSpec((N_EXPERTS, 1), lambda i: (0, 0))],
        out_specs=[mat, mat, pl.BlockSpec((None, N_EXPERTS, LANES), lambda i: (i, 0, 0))],
        out_shape=[jax.ShapeDtypeStruct((nsub, 2 * N_EXPERTS, n), BF16), jax.ShapeDtypeStruct((nsub, 2 * N_EXPERTS, n), BF16),
                   jax.ShapeDtypeStruct((nsub, N_EXPERTS, LANES), I32)],
        compiler_params=_cp(("parallel",)),
        name="moe_route",
    )(h2, router_w.T.astype(BF16), router_bias.reshape(N_EXPERTS, 1))
    return dest, wts, cnt[:, :, 0]


def _run_tables(cnt, n_blocks):
    units = (cnt + (RUN_ALIGN - 1)) // RUN_ALIGN
    src = jnp.cumsum(units, axis=1) - units
    per_block = MOE_BLOCK // RUN_ALIGN
    tot = jnp.sum(units, axis=0)
    tot_blocks = (tot + per_block - 1) // per_block
    blk_end = jnp.cumsum(tot_blocks)
    base = (blk_end - tot_blocks) * per_block
    dst = base[None, :] + jnp.cumsum(units, axis=0) - units
    n_used = blk_end[-1]
    tail_units = tot_blocks * per_block - tot
    tail_dst = base + tot
    blk = jnp.minimum(jnp.arange(n_blocks), n_used - 1)
    blk_exp = jnp.minimum(jnp.sum(blk[:, None] >= blk_end[None, :], axis=1), N_EXPERTS - 1)
    return (src.reshape(-1).astype(I32), dst.reshape(-1).astype(I32), units.reshape(-1).astype(I32),
            blk_exp.astype(I32), n_used.astype(I32).reshape(1), tail_dst.astype(I32), tail_units.astype(I32),
            jnp.sum(units, axis=1).astype(I32))


def _unit_dst(src, dst, units, nsub):
    src = src.reshape(nsub, 1, N_EXPERTS)
    dst = dst.reshape(nsub, 1, N_EXPERTS)
    end = src + units.reshape(nsub, 1, N_EXPERTS)
    j = jnp.arange(SUB_UNITS, dtype=I32).reshape(1, SUB_UNITS, 1)
    owns = (j >= src) & (j < end)
    return (jnp.sum(jnp.where(owns, dst - src, 0), axis=2) + j[:, :, 0]).reshape(-1).astype(I32)


def _run_bounds(src, units, nsub):
    lo = (src.reshape(nsub, N_EXPERTS) * RUN_ALIGN).astype(F32)
    hi = lo + (units.reshape(nsub, N_EXPERTS) * RUN_ALIGN).astype(F32)
    return jnp.stack([jnp.concatenate([lo, lo], axis=1), jnp.concatenate([hi, hi], axis=1)], axis=1)


def _max_blocks(t):
    nsub = t // SUB_TOKENS
    worst_units = t * TOP_K // RUN_ALIGN + nsub * N_EXPERTS
    per_block = MOE_BLOCK // RUN_ALIGN
    return -(-worst_units // per_block) + N_EXPERTS


SUB_UNITS = SUB_ROWS // RUN_ALIGN
COPY_UNROLL = 4


def _unit_copies(udst_ref, step, total_units, buf, hbm, sem, to_hbm):
    base = step * SUB_UNITS

    def one(j):
        d = udst_ref[base + j]
        v = buf.at[pl.ds(pl.multiple_of(j * RUN_ALIGN, RUN_ALIGN), RUN_ALIGN)]
        g = hbm.at[pl.ds(pl.multiple_of(d * RUN_ALIGN, RUN_ALIGN), RUN_ALIGN)]
        cp = pltpu.make_async_copy(v, g, sem) if to_hbm else pltpu.make_async_copy(g, v, sem)
        cp.start()

    def group(q, carry):
        for r in range(COPY_UNROLL):
            one(q * COPY_UNROLL + r)
        return carry

    groups = lax.shift_right_logical(total_units, COPY_UNROLL.bit_length() - 1)
    lax.fori_loop(0, groups, group, 0)
    lax.fori_loop(groups * COPY_UNROLL, total_units, lambda j, c: (one(j), c)[1], 0)


TOTAL_BITS = tuple(1 << b for b in range(SUB_UNITS.bit_length()))


def _wait_runs(total_units, buf, hbm, sem, to_hbm):
    for bit in TOTAL_BITS:
        @pl.when((total_units & bit) != 0)
        def _():
            rows = bit * RUN_ALIGN
            v = buf.at[pl.ds(0, rows)]
            g = hbm.at[pl.ds(0, rows)]
            cp = pltpu.make_async_copy(v, g, sem) if to_hbm else pltpu.make_async_copy(g, v, sem)
            cp.wait()


def _dispatch_kernel(udst_ref, tot_ref, tdst_ref, tunits_ref, nused_ref, h_ref, dest_ref, lohi_ref, xs_out,
                     buf_ref, zero_ref, sem):
    i = pl.program_id(0)
    nsub = pl.num_programs(0)
    slot = i % 2
    n = h_ref.shape[0]
    chunk = SORT_CHUNK

    for sl in range(2):
        @pl.when((slot == sl) & (i >= 2))
        def _():
            _wait_runs(tot_ref[i - 2], buf_ref.at[sl], xs_out, sem.at[sl], True)

    h = h_ref[...]
    dest = dest_ref[...]
    lo = lohi_ref[0:1, :]
    hi = lohi_ref[1:2, :]
    radix = jnp.where(lax.broadcasted_iota(I32, (1, 2 * N_EXPERTS), 1) < N_EXPERTS, float(DEST_RADIX), 1.0)
    r_e = lax.broadcasted_iota(I32, (chunk, 2 * N_EXPERTS), 0).astype(F32)
    r_t = lax.broadcasted_iota(I32, (chunk, n), 0).astype(F32)
    for sl in range(2):
        @pl.when(slot == sl)
        def _():
            for c in range(SUB_ROWS // chunk):
                own = jnp.where(r_e + c * chunk >= lo, jnp.where(r_e + c * chunk < hi, radix, 0.0), 0.0)
                row_of = _dot(own.astype(BF16), dest)
                p = jnp.where(row_of == r_t + c * chunk, 1.0, 0.0)
                buf_ref[sl, c * chunk:(c + 1) * chunk, :] = _dot(p.astype(BF16), h).astype(BF16)
            _unit_copies(udst_ref, i, tot_ref[i], buf_ref.at[sl], xs_out, sem.at[sl], True)

    @pl.when(i == nsub - 1)
    def _():
        zero_ref[...] = jnp.zeros_like(zero_ref)

        def tails(e, wait):
            u = tunits_ref[e]
            d0 = tdst_ref[e]
            for bit in RUN_BITS:
                low = u & (bit - 1)

                @pl.when((u & bit) != 0)
                def _():
                    rows = bit * RUN_ALIGN
                    cp = pltpu.make_async_copy(zero_ref.at[pl.ds(0, rows)],
                                               xs_out.at[pl.ds(pl.multiple_of((d0 + low) * RUN_ALIGN, RUN_ALIGN), rows)], sem.at[2])
                    if wait:
                        cp.wait()
                    else:
                        cp.start()
            return wait

        def unused(b, wait):
            cp = pltpu.make_async_copy(zero_ref.at[pl.ds(0, MOE_BLOCK)],
                                       xs_out.at[pl.ds(pl.multiple_of(b * MOE_BLOCK, MOE_BLOCK), MOE_BLOCK)], sem.at[2])
            if wait:
                cp.wait()
            else:
                cp.start()
            return wait

        n_blocks = xs_out.shape[0] // MOE_BLOCK
        lax.fori_loop(0, N_EXPERTS, lambda e, c: (tails(e, False), c)[1], 0)
        lax.fori_loop(nused_ref[0], n_blocks, lambda b, c: (unused(b, False), c)[1], 0)
        for sl in range(2):
            @pl.when((slot != sl) & (i >= 1))
            def _():
                _wait_runs(tot_ref[i - 1], buf_ref.at[sl], xs_out, sem.at[sl], True)

            @pl.when(slot == sl)
            def _():
                _wait_runs(tot_ref[i], buf_ref.at[sl], xs_out, sem.at[sl], True)
        lax.fori_loop(0, N_EXPERTS, lambda e, c: (tails(e, True), c)[1], 0)
        lax.fori_loop(nused_ref[0], n_blocks, lambda b, c: (unused(b, True), c)[1], 0)


def moe_dispatch(h2, dest, tables, n_rows):
    t, d = h2.shape
    n = SUB_TOKENS
    nsub = t // n
    src, dst, units, _, n_used, tail_dst, tail_units, tot = tables
    grid_spec = pltpu.PrefetchScalarGridSpec(
        num_scalar_prefetch=5,
        grid=(nsub,),
        in_specs=[pl.BlockSpec((n, d), lambda i, *_: (i, 0)), pl.BlockSpec((None, 2 * N_EXPERTS, n), lambda i, *_: (i, 0, 0)),
                  pl.BlockSpec((None, 2, 2 * N_EXPERTS), lambda i, *_: (i, 0, 0))],
        out_specs=pl.BlockSpec(memory_space=pl.ANY),
        scratch_shapes=[pltpu.VMEM((2, SUB_ROWS, d), BF16), pltpu.VMEM((max(SUB_TOKENS, MOE_BLOCK), d), BF16), pltpu.SemaphoreType.DMA((3,))],
    )
    return pl.pallas_call(
        _dispatch_kernel,
        grid_spec=grid_spec,
        out_shape=jax.ShapeDtypeStruct((n_rows, d), BF16),
        compiler_params=_cp(("arbitrary",)),
        name="moe_dispatch",
    )(_unit_dst(src, dst, units, nsub), tot, tail_dst, tail_units, n_used, h2, dest, _run_bounds(src, units, nsub))


def _expert_kernel(blk_exp_ref, n_used_ref, x_ref, w1_ref, w3_ref, w2_ref, y_ref):
    b = pl.program_id(0)

    @pl.when(b < n_used_ref[0])
    def _():
        x = x_ref[...]
        hid = _silu(_dot(x, w1_ref[...].astype(BF16))) * _dot(x, w3_ref[...].astype(BF16))
        y_ref[...] = _dot(hid.astype(BF16), w2_ref[...].astype(BF16)).astype(BF16)

    @pl.when(b >= n_used_ref[0])
    def _():
        y_ref[...] = jnp.zeros_like(y_ref)


def moe_experts(xs, w1, w3, w2, layer, tables, n_blocks):
    n_rows, d = xs.shape
    blk_exp, n_used = tables[3], tables[4]
    f = w1.shape[-1]

    def x_map(b, be, nu):
        return (jnp.minimum(b, nu[0] - 1), 0)

    def w_map(b, be, nu):
        return (layer, be[b], 0, 0)

    grid_spec = pltpu.PrefetchScalarGridSpec(
        num_scalar_prefetch=2,
        grid=(n_blocks,),
        in_specs=[pl.BlockSpec((MOE_BLOCK, d), x_map), pl.BlockSpec((None, None, d, f), w_map),
                  pl.BlockSpec((None, None, d, f), w_map), pl.BlockSpec((None, None, f, d), w_map)],
        out_specs=pl.BlockSpec((MOE_BLOCK, d), lambda b, be, nu: (b, 0)),
    )
    return pl.pallas_call(
        _expert_kernel,
        grid_spec=grid_spec,
        out_shape=jax.ShapeDtypeStruct((n_rows, d), BF16),
        compiler_params=_cp(("arbitrary",)),
        name="moe_experts",
    )(blk_exp, n_used, xs, w1, w3, w2)


def _combine_kernel(udst_ref, tot_ref, ys_ref, dcol_ref, wcol_ref, lohi_ref, h_ref, x_ref, g2_ref,
                    ws1_ref, ws3_ref, ws2_ref, o_ref, buf_ref, sem):
    i = pl.program_id(0)
    nsub = pl.num_programs(0)
    slot = i % 2
    n = h_ref.shape[0]
    chunk = 512

    def fetch(step, sl):
        buf_ref[sl] = jnp.zeros((SUB_ROWS, buf_ref.shape[2]), BF16)
        _unit_copies(udst_ref, step, tot_ref[step], buf_ref.at[sl], ys_ref, sem.at[sl], False)

    @pl.when(i == 0)
    def _():
        fetch(0, 0)

    for sl in range(2):
        @pl.when((slot != sl) & (i + 1 < nsub))
        def _():
            fetch(i + 1, sl)

    h = h_ref[...]
    shared = _dot((_silu(_dot(h, ws1_ref[...])) * _dot(h, ws3_ref[...])).astype(BF16), ws2_ref[...])
    dest_t = dcol_ref[...]
    w_t = wcol_ref[...]
    lo = lohi_ref[:, 0:1]
    hi = lohi_ref[:, 1:2]
    radix = jnp.where(lax.broadcasted_iota(I32, (2 * N_EXPERTS, 1), 0) < N_EXPERTS, float(DEST_RADIX), 1.0)
    r_e = lax.broadcasted_iota(I32, (2 * N_EXPERTS, chunk), 1).astype(F32)
    r_t = lax.broadcasted_iota(I32, (n, chunk), 1).astype(F32)
    for sl in range(2):
        @pl.when(slot == sl)
        def _():
            _wait_runs(tot_ref[i], buf_ref.at[sl], ys_ref, sem.at[sl], False)
            acc = shared
            for c in range(SUB_ROWS // chunk):
                own = jnp.where(r_e + c * chunk >= lo, jnp.where(r_e + c * chunk < hi, radix, 0.0), 0.0).astype(BF16)
                row_of = _dot(dest_t, own)
                pw = jnp.where(row_of == r_t + c * chunk, _dot(w_t, own), 0.0)
                acc = acc + _dot(pw.astype(BF16), buf_ref[sl, c * chunk:(c + 1) * chunk, :])
            o_ref[...] = x_ref[...] + g2_ref[...] * acc


def moe_combine(ys, dest, wts, h2, x1, g2, ws1, ws3, ws2, tables, seq):
    t, d = h2.shape
    n = SUB_TOKENS
    nsub = t // n
    src, dst, units, tot = tables[0], tables[1], tables[2], tables[7]
    dcol = jnp.swapaxes(dest, 1, 2)
    wcol = jnp.swapaxes(wts, 1, 2)
    bounds = jnp.swapaxes(_run_bounds(src, units, nsub), 1, 2)
    per_seq = seq // n
    f = ws1.shape[-1]
    pair = pl.BlockSpec((None, n, 2 * N_EXPERTS), lambda i, *_: (i, 0, 0))
    grid_spec = pltpu.PrefetchScalarGridSpec(
        num_scalar_prefetch=2,
        grid=(nsub,),
        in_specs=[pl.BlockSpec(memory_space=pl.ANY), pair, pair,
                  pl.BlockSpec((None, 2 * N_EXPERTS, 2), lambda i, *_: (i, 0, 0)),
                  pl.BlockSpec((n, d), lambda i, *_: (i, 0)), pl.BlockSpec((n, d), lambda i, *_: (i, 0)),
                  pl.BlockSpec((None, 1, d), lambda i, *_: (i // per_seq, 0, 0)),
                  pl.BlockSpec((d, f), lambda i, *_: (0, 0)), pl.BlockSpec((d, f), lambda i, *_: (0, 0)),
                  pl.BlockSpec((f, d), lambda i, *_: (0, 0))],
        out_specs=pl.BlockSpec((n, d), lambda i, *_: (i, 0)),
        scratch_shapes=[pltpu.VMEM((2, SUB_ROWS, d), BF16), pltpu.SemaphoreType.DMA((2,))],
    )
    return pl.pallas_call(
        _combine_kernel,
        grid_spec=grid_spec,
        out_shape=jax.ShapeDtypeStruct((t, d), F32),
        compiler_params=_cp(("arbitrary",)),
        name="moe_combine",
    )(_unit_dst(src, dst, units, nsub), tot, ys, dcol, wcol, bounds, h2, x1, g2, ws1.astype(BF16), ws3.astype(BF16), ws2.astype(BF16))


def _split_w_in(w):
    fh = FOX_HEADS * FOX_HEAD_DIM
    sizes = (fh, fh, fh, FOX_HEADS,
             2 * ML_HEADS * ML_DQK, ML_HEADS * ML_DV, ML_HEADS, ML_HEADS, ML_HEADS * ML_DV,
             HG_HEADS * HG_DK, HG_HEADS * HG_DK, BRANCH_WIDTH, BRANCH_WIDTH,
             N_BRANCH * D_MODEL)
    outs, o = [], 0
    for sz in sizes:
        outs.append(w[:, o:o + sz])
        o += sz
    return outs


def moe_ffn(h2, x1, g2, router_w, router_bias, w1, w3, w2, layer, ws1, ws3, ws2, seq):
    t, d = h2.shape
    dest, wts, cnt = moe_route(h2, router_w, router_bias)
    n_blocks = _max_blocks(t)
    tables = _run_tables(cnt, n_blocks)
    xs = moe_dispatch(h2, dest, tables, n_blocks * MOE_BLOCK)
    ys = moe_experts(xs, w1, w3, w2, layer, tables, n_blocks)
    return moe_combine(ys, dest, wts, h2, x1, g2, ws1, ws3, ws2, tables, seq)


def kernel(x, c, ada_w, ada_b, norm1_g, norm2_g, w_in, fox_bf, fox_q_g, fox_k_g, mlstm_conv, mlstm_bi, mlstm_bf, mlstm_norm_g, hgrn_lower_bounds, hgrn_norm_g, w_branch, w_out, router_w, router_bias, exp_w1, exp_w3, exp_w2, sh_w1, sh_w3, sh_w2):
    b, s, d = x.shape
    depth = ada_w.shape[0]
    mod = adaln_mod(c, ada_w, ada_b)
    lb_all = jnp.cumsum(jax.nn.softmax(hgrn_lower_bounds.astype(F32), axis=0), axis=0)
    lb_all = lb_all - lb_all[0]
    for l in range(depth):
        sh1, sc1, g1, sh2, sc2, g2 = [mod[l][:, d * j:d * (j + 1)] for j in range(6)]
        (wfq, wfk, wfv, wff, wmqk, wmv, wmi, wmf, wmo, whf, whq, whi, whg, wgates) = _split_w_in(w_in[l])
        h = norm_modulate(x, norm1_g[l], sc1, sh1)
        qp, kp, fv = fox_project(h, wfq, wfk, wfv, wff, fox_q_g[l], fox_k_g[l], fox_bf[l])
        y_fox = fox_attention(qp, kp, fv)
        mq, mk, mv, mog, mgates = ml_project(h, wmqk, wmv, wmi, wmf, wmo, mlstm_conv[l], mlstm_bi[l], mlstm_bf[l])
        y_ml = mlstm(mq, mk, mv, mog, mgates, mlstm_norm_g[l])
        hq, hk, hv, hog, hlf = hg_project(h, whf, whq, whi, whg, lb_all[l])
        y_hg = hgrn(hq, hk, hv, hog, hlf, hgrn_norm_g[l])
        x1, h2 = merge_branches(x, h, y_fox, y_ml, y_hg, wgates, w_branch[l], w_out[l], g1, norm2_g[l], sc2, sh2)
        x = moe_ffn(h2.reshape(b * s, d), x1.reshape(b * s, d), g2.reshape(b, 1, d), router_w[l], router_bias[l],
                    exp_w1, exp_w3, exp_w2, l, sh_w1[l], sh_w3[l], sh_w2[l], s).reshape(b, s, d)
    return x
```

```python
import functools

import jax
import jax.numpy as jnp
import numpy as np
from jax import lax
from jax.experimental import pallas as pl
from jax.experimental.pallas import tpu as pltpu

F32 = jnp.float32
BF16 = jnp.bfloat16
I32 = jnp.int32

LANES = 128
SUBLANES = 8
BF16_ROWS = 16
VMEM_BYTES = 64 * 1024 * 1024

D_MODEL = 1024
BRANCH_WIDTH = D_MODEL // 2
N_BRANCH = 3
FOX_HEAD_DIM = 64
FOX_HEADS = BRANCH_WIDTH // FOX_HEAD_DIM
ML_HEADS = 4
ML_DV = BRANCH_WIDTH // ML_HEADS
ML_DQK = ML_DV // 2
ML_CONV = 4
ML_CHUNK = 128
HG_HEADS = 4
HG_DK = 128
HG_CHUNK = 128
N_EXPERTS = 64
N_GROUPS = 8
GROUP_SIZE = N_EXPERTS // N_GROUPS
TOPK_GROUPS = 4
TOP_K = 8
D_FF = D_MODEL // 4
ROUTE_SCALE = 2.5
MOE_BLOCK = 1024
SORT_CHUNK = 1024
EPS = 1e-6
NEG = -1e30
LOG2E = 1.4426950408889634

ROW_TILE = 512
PROJ_TILE = 1024
ATTN_TILE = 256
SUB_TOKENS = 512
RUN_ALIGN = BF16_ROWS
SUB_ROWS = ((SUB_TOKENS * TOP_K + N_EXPERTS * (RUN_ALIGN - 1)) + 255) // 256 * 256
RUN_BITS = tuple(1 << b for b in range((SUB_TOKENS // RUN_ALIGN).bit_length()))
DEST_RADIX = 64
DEST_NONE = DEST_RADIX * 127
assert SUB_ROWS <= DEST_NONE


def _cp(sem, vmem_mb=48):
    return pltpu.CompilerParams(dimension_semantics=sem, vmem_limit_bytes=vmem_mb * 1024 * 1024)


def _dot(a, b):
    return jnp.dot(a, b, preferred_element_type=F32)


def _dot_nt(a, b):
    return lax.dot_general(a, b, (((1,), (1,)), ((), ())), preferred_element_type=F32)


def _dot_tn(a, b):
    return lax.dot_general(a, b, (((0,), (0,)), ((), ())), preferred_element_type=F32)


def _split3(x):
    hi = x.astype(BF16)
    r = x - hi.astype(F32)
    mid = r.astype(BF16)
    lo = (r - mid.astype(F32)).astype(BF16)
    return hi, mid, lo


def _tri_dot(tri, x):
    hi, mid, lo = _split3(x)
    return (_dot(tri, hi) + _dot(tri, mid)) + _dot(tri, lo)


def _dot_tri(x, tri):
    hi, mid, lo = _split3(x)
    return (_dot(hi, tri) + _dot(mid, tri)) + _dot(lo, tri)


def _log_sigmoid(x):
    return jnp.minimum(x, 0.0) - jnp.log1p(jnp.exp(-jnp.abs(x)))


def _silu(x):
    return x * jax.nn.sigmoid(x)


def _tri_incl(n, dtype=BF16):
    r = lax.broadcasted_iota(I32, (n, n), 0)
    c = lax.broadcasted_iota(I32, (n, n), 1)
    return jnp.where(c <= r, 1.0, 0.0).astype(dtype)


def _mod_kernel(c_ref, w_ref, b_ref, o_ref):
    cond = _silu(c_ref[...])
    hi, mid, lo = _split3(cond)
    w = w_ref[...]
    whi, wmid, wlo = _split3(w)
    acc = _dot(hi, whi) + (_dot(hi, wmid) + _dot(mid, whi))
    acc = acc + (_dot(mid, wmid) + _dot(hi, wlo) + _dot(lo, whi))
    o_ref[...] = acc + b_ref[...]


def adaln_mod(c, ada_w, ada_b):
    depth, d, n = ada_w.shape
    b = c.shape[0]
    tn = 1024
    return pl.pallas_call(
        _mod_kernel,
        grid=(depth, n // tn),
        in_specs=[
            pl.BlockSpec((b, d), lambda l, j: (0, 0)),
            pl.BlockSpec((None, d, tn), lambda l, j: (l, 0, j)),
            pl.BlockSpec((None, 1, tn), lambda l, j: (l, 0, j)),
        ],
        out_specs=pl.BlockSpec((None, b, tn), lambda l, j: (l, 0, j)),
        out_shape=jax.ShapeDtypeStruct((depth, b, n), F32),
        compiler_params=_cp(("parallel", "parallel")),
        name="adaln_mod",
    )(c, ada_w, ada_b.reshape(depth, 1, n))


def _norm_mod(x, g, sc, sh):
    ms = jnp.mean(x * x, axis=-1, keepdims=True)
    return x * lax.rsqrt(ms + EPS) * g * (1.0 + sc) + sh


def _norm_kernel(x_ref, g_ref, sc_ref, sh_ref, h_ref):
    h_ref[...] = _norm_mod(x_ref[...], g_ref[...], sc_ref[...], sh_ref[...]).astype(BF16)


def norm_modulate(x, g, sc, sh):
    b, s, d = x.shape
    tm = ROW_TILE
    return pl.pallas_call(
        _norm_kernel,
        grid=(b, s // tm),
        in_specs=[
            pl.BlockSpec((None, tm, d), lambda i, j: (i, j, 0)),
            pl.BlockSpec((1, d), lambda i, j: (0, 0)),
            pl.BlockSpec((None, 1, d), lambda i, j: (i, 0, 0)),
            pl.BlockSpec((None, 1, d), lambda i, j: (i, 0, 0)),
        ],
        out_specs=pl.BlockSpec((None, tm, d), lambda i, j: (i, j, 0)),
        out_shape=jax.ShapeDtypeStruct((b, s, d), BF16),
        compiler_params=_cp(("parallel", "parallel")),
        name="norm_modulate",
    )(x, g.reshape(1, d), sc.reshape(b, 1, d), sh.reshape(b, 1, d))


FOX_BIAS_PIECES = 3
FOX_QB_LANE = FOX_HEAD_DIM
FOX_KB_LANE = FOX_HEAD_DIM + FOX_BIAS_PIECES


def _pack_pieces(x):
    hi, mid, lo = _split3(x)
    p = hi.astype(F32) + pltpu.roll(mid.astype(F32), FOX_HEADS, axis=1) + pltpu.roll(lo.astype(F32), 2 * FOX_HEADS, axis=1)
    return p.astype(BF16)


def _fox_proj_kernel(h_ref, wq_ref, wk_ref, wvt_ref, wf_ref, gq_ref, gk_ref, bf_ref, eq_ref, ek_ref, cq_ref, ck_ref,
                     q_out, k_out, vt_out, carry_ref):
    @pl.when(pl.program_id(1) == 0)
    def _():
        carry_ref[...] = jnp.zeros_like(carry_ref)

    h = h_ref[...]
    tm = h.shape[0]
    pr = lax.broadcasted_iota(I32, (2 * LANES, 2 * LANES), 0)
    pc = lax.broadcasted_iota(I32, (2 * LANES, 2 * LANES), 1)
    avg_pair = jnp.where((pr < LANES) == (pc < LANES), 1.0 / FOX_HEAD_DIM, 0.0).astype(BF16)

    def head_norm(x):
        outs = []
        for pair in range(FOX_HEADS // 2):
            xp = x[:, 2 * LANES * pair:2 * LANES * (pair + 1)]
            ms = _dot((xp * xp).astype(BF16), avg_pair)
            outs.append(xp * lax.rsqrt(ms + EPS))
        return jnp.concatenate(outs, axis=1)

    lane = lax.broadcasted_iota(I32, (tm, LANES), 1)
    logf = jnp.where(lane < FOX_HEADS, _log_sigmoid(_dot(h, wf_ref[...]) + bf_ref[...]), 0.0)
    cs = _dot(_tri_incl(tm), _pack_pieces(logf))
    cum = cs + pltpu.roll(cs, LANES - FOX_HEADS, axis=1) + pltpu.roll(cs, LANES - 2 * FOX_HEADS, axis=1)
    cum = jnp.where(lane < FOX_HEADS, cum, 0.0) + carry_ref[...]
    carry_ref[...] = cum[tm - 1:tm, :]
    pieces = _pack_pieces(cum * LOG2E)

    qn = head_norm(_dot(h, wq_ref[...])) * gq_ref[...]
    q_out[...] = (qn + _dot(pieces, eq_ref[...]) + cq_ref[...]).astype(BF16)
    kn = head_norm(_dot(h, wk_ref[...])) * gk_ref[...]
    k_out[...] = (kn + _dot(pieces, ek_ref[...]) + ck_ref[...]).astype(BF16)
    vt = _dot_nt(wvt_ref[...], h)
    ones = jnp.ones((FOX_HEAD_DIM, tm), F32)
    slots = []
    for hd in range(FOX_HEADS):
        slots += [vt[FOX_HEAD_DIM * hd:FOX_HEAD_DIM * (hd + 1), :], ones]
    vt_out[...] = jnp.concatenate(slots, axis=0).astype(BF16)


def _pad_heads(w, heads, dim):
    lead = w.shape[:-1]
    w = w.reshape(*lead, heads, dim)
    w = jnp.pad(w, [(0, 0)] * len(lead) + [(0, 0), (0, LANES - dim)])
    return w.reshape(*lead, heads * LANES)


def _fox_constants():
    eq = np.zeros((LANES, FOX_HEADS * LANES), np.float32)
    ek = np.zeros((LANES, FOX_HEADS * LANES), np.float32)
    cq = np.zeros((1, FOX_HEADS * LANES), np.float32)
    ck = np.zeros((1, FOX_HEADS * LANES), np.float32)
    for hd in range(FOX_HEADS):
        for p in range(FOX_BIAS_PIECES):
            eq[p * FOX_HEADS + hd, LANES * hd + FOX_QB_LANE + p] = 1.0
            ek[p * FOX_HEADS + hd, LANES * hd + FOX_KB_LANE + p] = -1.0
            cq[0, LANES * hd + FOX_KB_LANE + p] = 1.0
            ck[0, LANES * hd + FOX_QB_LANE + p] = 1.0
    return jnp.asarray(eq, BF16), jnp.asarray(ek, BF16), jnp.asarray(cq), jnp.asarray(ck)


def fox_project(h, wq, wk, wv, wf, q_g, k_g, bf):
    b, s, d = h.shape
    tm = min(PROJ_TILE, s)
    hp = FOX_HEADS * LANES
    wq_p = _pad_heads(wq, FOX_HEADS, FOX_HEAD_DIM).astype(BF16)
    wk_p = _pad_heads(wk, FOX_HEADS, FOX_HEAD_DIM).astype(BF16)
    wf_p = jnp.pad(wf, ((0, 0), (0, LANES - FOX_HEADS))).astype(BF16)
    gq = _pad_heads(jnp.tile(q_g * (FOX_HEAD_DIM ** -0.5 * LOG2E), FOX_HEADS)[None, :], FOX_HEADS, FOX_HEAD_DIM)
    gk = _pad_heads(jnp.tile(k_g, FOX_HEADS)[None, :], FOX_HEADS, FOX_HEAD_DIM)
    bf_p = jnp.pad(bf, (0, LANES - FOX_HEADS))[None, :]
    eq, ek, cq, ck = _fox_constants()
    const = lambda shape: pl.BlockSpec(shape, lambda i, j: (0,) * len(shape))
    row = lambda n: pl.BlockSpec((None, tm, n), lambda i, j: (i, j, 0))
    return pl.pallas_call(
        _fox_proj_kernel,
        grid=(b, s // tm),
        in_specs=[row(d), const((d, hp)), const((d, hp)), const((BRANCH_WIDTH, d)), const((d, LANES)),
                  const((1, hp)), const((1, hp)), const((1, LANES)), const((LANES, hp)), const((LANES, hp)),
                  const((1, hp)), const((1, hp))],
        out_specs=[row(hp), row(hp), pl.BlockSpec((None, hp, tm), lambda i, j: (i, 0, j))],
        out_shape=[jax.ShapeDtypeStruct((b, s, hp), BF16), jax.ShapeDtypeStruct((b, s, hp), BF16),
                   jax.ShapeDtypeStruct((b, hp, s), BF16)],
        scratch_shapes=[pltpu.VMEM((1, LANES), F32)],
        compiler_params=_cp(("parallel", "arbitrary")),
        name="fox_project",
    )(h, wq_p, wk_p, wv.T.astype(BF16), wf_p, gq, gk, bf_p, eq, ek, cq, ck)


ATTN_HEADS = 8


def _fox_attn_kernel(q_ref, k_ref, vt_ref, o_ref):
    i = pl.program_id(2)
    t = q_ref.shape[0]
    krow = lax.broadcasted_iota(I32, (t, t), 0)
    qcol = lax.broadcasted_iota(I32, (t, t), 1)
    qs = [q_ref[:, LANES * a:LANES * (a + 1)] for a in range(ATTN_HEADS)]

    def scores(j):
        start = pl.multiple_of(j * t, t)
        return tuple(_dot_nt(k_ref[pl.ds(start, t), LANES * a:LANES * (a + 1)], qs[a]) for a in range(ATTN_HEADS))

    def consume(j, state, ss, masked):
        start = pl.multiple_of(j * t, t)
        new = []
        for a in range(ATTN_HEADS):
            m, acc = state[a]
            s = jnp.where(krow <= qcol, ss[a], NEG) if masked else ss[a]
            m_new = jnp.maximum(m, jnp.max(s, axis=0, keepdims=True))
            p = jnp.exp2(s - m_new)
            alpha = jnp.exp2(m - m_new)
            vt = vt_ref[LANES * a:LANES * (a + 1), pl.ds(start, t)]
            acc = alpha * acc + _dot(vt, p.astype(BF16))
            new.append((m_new, acc))
        return tuple(new)

    def body(j, state):
        return consume(j, state, scores(j), False)

    init = tuple((jnp.full((1, t), NEG, F32), jnp.zeros((LANES, t), F32)) for _ in range(ATTN_HEADS))
    state = lax.fori_loop(0, i, body, init)
    state = consume(i, state, scores(i), True)
    for p in range(ATTN_HEADS // 2):
        halves = []
        for _, acc in (state[2 * p], state[2 * p + 1]):
            halves.append(acc[:FOX_HEAD_DIM, :] / acc[FOX_HEAD_DIM:FOX_HEAD_DIM + 1, :])
        o_ref[:, LANES * p:LANES * (p + 1)] = jnp.concatenate(halves, axis=0).T.astype(BF16)


def fox_attention(qp, kp, vt):
    b, s, hp = qp.shape
    t = ATTN_TILE
    groups = FOX_HEADS // ATTN_HEADS
    return pl.pallas_call(
        _fox_attn_kernel,
        grid=(b, groups, s // t),
        in_specs=[
            pl.BlockSpec((None, t, ATTN_HEADS * LANES), lambda bi, p, i: (bi, i, p)),
            pl.BlockSpec((None, s, ATTN_HEADS * LANES), lambda bi, p, i: (bi, 0, p)),
            pl.BlockSpec((None, ATTN_HEADS * LANES, s), lambda bi, p, i: (bi, p, 0)),
        ],
        out_specs=pl.BlockSpec((None, t, ATTN_HEADS // 2 * LANES), lambda bi, p, i: (bi, i, p)),
        out_shape=jax.ShapeDtypeStruct((b, s, BRANCH_WIDTH), BF16),
        compiler_params=_cp(("parallel", "parallel", "arbitrary")),
        name="fox_attention",
    )(qp, kp, vt)


CONV_HALO = SUBLANES


def _ml_proj_kernel(h_ref, wqk_ref, wvt_ref, wo_ref, wg_ref, conv_ref, gb_ref, q_out, k_out, vt_out, og_out, g_out, buf_ref):
    tm = h_ref.shape[0]
    half = ML_HEADS * LANES

    @pl.when(pl.program_id(1) == 0)
    def _():
        buf_ref[0:CONV_HALO, :] = jnp.zeros((CONV_HALO, 2 * half), F32)

    h = h_ref[...]
    buf_ref[CONV_HALO:CONV_HALO + tm, :] = _dot(h, wqk_ref[...])
    acc = jnp.zeros((tm, 2 * half), F32)
    for j in range(ML_CONV):
        off = CONV_HALO - (ML_CONV - 1) + j
        acc = acc + conv_ref[j:j + 1, :] * buf_ref[off:off + tm, :]
    buf_ref[0:CONV_HALO, :] = buf_ref[tm:tm + CONV_HALO, :]
    act = _silu(acc)
    q_out[...] = act[:, :half].astype(BF16)
    k_out[...] = (act[:, half:] * (ML_DQK ** -0.5)).astype(BF16)
    vt_out[...] = _dot_nt(wvt_ref[...], h).astype(BF16)
    og_out[...] = jax.nn.sigmoid(_dot(h, wo_ref[...])).astype(BF16)
    g = _dot(h, wg_ref[...]) + gb_ref[...]
    lane = lax.broadcasted_iota(I32, (tm, LANES), 1)
    g_out[...] = jnp.where(lane < ML_HEADS, g, _log_sigmoid(g))


def ml_project(h, wqk, wv, wi, wf, wo, conv, bi, bf):
    b, s, d = h.shape
    tm = min(PROJ_TILE, s)
    half = ML_HEADS * LANES
    nq = ML_HEADS * ML_DQK
    wqk_p = jnp.concatenate([_pad_heads(wqk[:, :nq], ML_HEADS, ML_DQK), _pad_heads(wqk[:, nq:], ML_HEADS, ML_DQK)], axis=1).astype(BF16)
    conv_p = jnp.concatenate([_pad_heads(conv[:, :nq], ML_HEADS, ML_DQK), _pad_heads(conv[:, nq:], ML_HEADS, ML_DQK)], axis=1)
    wg = jnp.pad(jnp.concatenate([wi, wf], axis=1), ((0, 0), (0, LANES - 2 * ML_HEADS))).astype(BF16)
    gb = jnp.pad(jnp.concatenate([bi, bf]), (0, LANES - 2 * ML_HEADS))[None, :]
    const = lambda shape: pl.BlockSpec(shape, lambda i, j: (0,) * len(shape))
    row = lambda n: pl.BlockSpec((None, tm, n), lambda i, j: (i, j, 0))
    return pl.pallas_call(
        _ml_proj_kernel,
        grid=(b, s // tm),
        in_specs=[row(d), const((d, 2 * half)), const((BRANCH_WIDTH, d)), const((d, BRANCH_WIDTH)), const((d, LANES)),
                  const((ML_CONV, 2 * half)), const((1, LANES))],
        out_specs=[row(half), row(half), pl.BlockSpec((None, BRANCH_WIDTH, tm), lambda i, j: (i, 0, j)), row(BRANCH_WIDTH), row(LANES)],
        out_shape=[jax.ShapeDtypeStruct((b, s, half), BF16), jax.ShapeDtypeStruct((b, s, half), BF16),
                   jax.ShapeDtypeStruct((b, BRANCH_WIDTH, s), BF16), jax.ShapeDtypeStruct((b, s, BRANCH_WIDTH), BF16),
                   jax.ShapeDtypeStruct((b, s, LANES), F32)],
        scratch_shapes=[pltpu.VMEM((tm + CONV_HALO, 2 * half), F32)],
        compiler_params=_cp(("parallel", "arbitrary")),
        name="ml_project",
    )(h, wqk_p, wv.T.astype(BF16), wo.astype(BF16), wg, conv_p, gb)


def _mlstm_kernel(q_ref, k_ref, vt_ref, og_ref, g_ref, gt_ref, ng_ref, o_ref, c_ref, n_ref, m_ref):
    nb, s = q_ref.shape[0], q_ref.shape[1]
    L = ML_CHUNK
    c_ref[...] = jnp.zeros_like(c_ref)
    n_ref[...] = jnp.zeros_like(n_ref)
    m_ref[...] = jnp.zeros_like(m_ref)
    tril = _tri_incl(L)
    triu = tril.T
    srow = lax.broadcasted_iota(I32, (L, L), 0)
    tcol = lax.broadcasted_iota(I32, (L, L), 1)
    causal = srow <= tcol

    def chunk(c, carry):
        for bi in range(nb):
            one_chunk(c, bi)
        return carry

    def one_chunk(c, bi):
        r0 = pl.multiple_of(c * L, L)
        g = g_ref[bi, pl.ds(r0, L), :]
        gt = gt_ref[bi, :, pl.ds(r0, L)]
        bc = _tri_dot(tril, g)
        br = _dot_tri(gt, triu)
        for hd in range(ML_HEADS):
            sl = slice(LANES * hd, LANES * (hd + 1))
            st = bi * ML_HEADS + hd
            ccol = g[:, hd:hd + 1] - bc[:, ML_HEADS + hd:ML_HEADS + hd + 1]
            brow = br[ML_HEADS + hd:ML_HEADS + hd + 1, :]
            irow = gt[hd:hd + 1, :]
            m_prev = m_ref[st][:, 0:1]
            log_d = jnp.where(causal, brow + ccol, -jnp.inf)
            log_inter = brow + m_prev
            m_t = jnp.maximum(jnp.max(log_d, axis=0, keepdims=True), log_inter)
            w_intra = jnp.exp(log_d - m_t)
            w_inter = jnp.exp(log_inter - m_t)
            qc = q_ref[bi, pl.ds(r0, L), sl]
            kc = k_ref[bi, pl.ds(r0, L), sl]
            vt = vt_ref[bi, sl, pl.ds(r0, L)]
            sc = _dot_nt(kc, qc) * w_intra
            cst = c_ref[st]
            nst = n_ref[st]
            num = _dot(vt, sc.astype(BF16)) + w_inter * _dot_nt(cst.astype(BF16), qc)
            qn = _dot_nt(jnp.broadcast_to(nst, (SUBLANES, LANES)).astype(BF16), qc)[0:1, :]
            den = jnp.sum(sc, axis=0, keepdims=True) + w_inter * qn
            hout = num / jnp.maximum(jnp.abs(den), jnp.exp(-m_t))
            b_last = brow[:, L - 1:L]
            lw = b_last - brow + irow
            m_new = jnp.maximum(b_last + m_prev, jnp.max(lw, axis=1, keepdims=True))
            w_in = jnp.exp(lw - m_new)
            decay = jnp.exp(b_last + m_prev - m_new)
            c_ref[st] = decay * cst + _dot((vt.astype(F32) * w_in).astype(BF16), kc)
            n_ref[st] = decay * nst + _dot(jnp.broadcast_to(w_in, (SUBLANES, L)).astype(BF16), kc)[0:1, :]
            m_ref[st] = jnp.broadcast_to(m_new, (1, LANES))
            ms = jnp.mean(hout * hout, axis=0, keepdims=True)
            y = (hout * lax.rsqrt(ms + EPS)).T * (og_ref[bi, pl.ds(r0, L), sl].astype(F32) * ng_ref[:, sl])
            o_ref[bi, pl.ds(r0, L), sl] = y.astype(BF16)

    lax.fori_loop(0, s // L, chunk, 0)


ML_SEQS = 2


def mlstm(q, k, vt, og, gates, norm_g):
    b, w, s = vt.shape
    nb = ML_SEQS if b % ML_SEQS == 0 else 1
    gt = jnp.swapaxes(gates[:, :, :2 * ML_HEADS], 1, 2)
    seq = lambda n: pl.BlockSpec((nb, s, n), lambda i: (i, 0, 0))
    return pl.pallas_call(
        _mlstm_kernel,
        grid=(b // nb,),
        in_specs=[seq(ML_HEADS * LANES), seq(ML_HEADS * LANES), pl.BlockSpec((nb, w, s), lambda i: (i, 0, 0)), seq(w), seq(LANES),
                  pl.BlockSpec((nb, 2 * ML_HEADS, s), lambda i: (i, 0, 0)),
                  pl.BlockSpec((1, w), lambda i: (0, 0))],
        out_specs=seq(w),
        out_shape=jax.ShapeDtypeStruct((b, s, w), BF16),
        scratch_shapes=[pltpu.VMEM((nb * ML_HEADS, ML_DV, LANES), F32), pltpu.VMEM((nb * ML_HEADS, 1, LANES), F32),
                        pltpu.VMEM((nb * ML_HEADS, 1, LANES), F32)],
        compiler_params=_cp(("parallel",)),
        name="mlstm",
    )(q, k, vt, og, gates, gt, norm_g.reshape(1, w))


def _hg_proj_kernel(h_ref, wf_ref, wq_ref, wi_ref, wg_ref, lb_ref, q_out, k_out, v_out, og_out, lf_out):
    h = h_ref[...]
    fz = _dot(h, wf_ref[...])
    log_lb = lb_ref[0:1, :]
    log_1m = lb_ref[1:2, :]
    one_m = lb_ref[2:3, :]
    a = log_lb
    bb = log_1m + _log_sigmoid(fz)
    lf_out[...] = jnp.maximum(a, bb) + jnp.log1p(jnp.exp(-jnp.abs(a - bb)))
    k_out[...] = (one_m * jax.nn.sigmoid(-fz)).astype(BF16)
    q_out[...] = _silu(_dot(h, wq_ref[...])).astype(BF16)
    v_out[...] = _dot(h, wi_ref[...]).astype(BF16)
    og_out[...] = _silu(_dot(h, wg_ref[...])).astype(BF16)


def hg_project(h, wf, wq, wi, wg, lb):
    b, s, d = h.shape
    tm = min(PROJ_TILE, s)
    w = BRANCH_WIDTH
    lbp =jnp.stack([jnp.log(lb), jnp.log1p(-lb), 1.0 - lb], axis=0)
    const = lambda shape: pl.BlockSpec(shape, lambda i, j: (0,) * len(shape))
    row = lambda n: pl.BlockSpec((None, tm, n), lambda i, j: (i, j, 0))
    return pl.pallas_call(
        _hg_proj_kernel,
        grid=(b, s // tm),
        in_specs=[row(d), const((d, w)), const((d, w)), const((d, w)), const((d, w)), const((3, w))],
        out_specs=[row(w), row(w), row(w), row(w), row(w)],
        out_shape=[jax.ShapeDtypeStruct((b, s, w), BF16)] * 4 + [jax.ShapeDtypeStruct((b, s, w), F32)],
        compiler_params=_cp(("parallel", "parallel")),
        name="hg_project",
    )(h, wf.astype(BF16), wq.astype(BF16), wi.astype(BF16), wg.astype(BF16), lbp)


HG_LEVELS = tuple(HG_CHUNK >> (i + 1) for i in range(HG_CHUNK.bit_length() - 1))


def _hg_tables():
    L = HG_CHUNK
    t = np.arange(L)
    tri = (t[None, :] <= t[:, None]).astype(np.float32)
    mats = [tri]
    x = t[:, None] ^ t[None, :]
    lvl = np.full((L, L), -1, np.int32)
    lvl[t[:, None] == t[None, :]] = 0
    for i, m in enumerate(HG_LEVELS):
        if m < SUBLANES:
            mats.append(tri[(t // (2 * m)) * (2 * m) + m - 1])
        lvl[(t[:, None] > t[None, :]) & (x >= m) & (x < 2 * m)] = i + 1
    return jnp.asarray(np.concatenate(mats, axis=0), BF16), jnp.asarray(lvl)


def _hgrn_kernel(q_ref, k_ref, v_ref, og_ref, lf_ref, ng_ref, tall_ref, lvl_ref, o_ref, st_ref):
    s, wd = q_ref.shape
    L = HG_CHUNK
    st_ref[...] = jnp.zeros_like(st_ref)
    rowi = lax.broadcasted_iota(I32, (L, LANES), 0)

    lvl = lvl_ref[...]
    level_masks = [lvl == i for i in range(len(HG_LEVELS) + 1)]

    def chunk(c, carry):
        r0 = pl.multiple_of(c * L, L)
        tall = tall_ref[...]
        g = lf_ref[pl.ds(r0, L), :]
        hi = g.astype(BF16)
        mid = (g - hi.astype(F32)).astype(BF16)
        cums = (_dot(tall, hi) + _dot(tall, mid)) * LOG2E
        a = cums[0:L]
        qb = q_ref[pl.ds(r0, L), :]
        kb = k_ref[pl.ds(r0, L), :]
        qf = qb.astype(F32)
        kf = kb.astype(F32)
        ws = []
        fine = 0
        for m in HG_LEVELS:
            if m >= SUBLANES:
                ref = jnp.concatenate([jnp.broadcast_to(a[g0 + m - 1:g0 + m, :], (2 * m, wd)) for g0 in range(0, L, 2 * m)], axis=0)
            else:
                fine += 1
                ref = cums[L * fine:L * (fine + 1)]
            e = jnp.exp2(-jnp.abs(a - ref))
            upper = (rowi & m) != 0
            qk = jnp.concatenate([jnp.where(upper, qf[:, LANES * hd:LANES * (hd + 1)], kf[:, LANES * hd:LANES * (hd + 1)])
                                  for hd in range(HG_HEADS)], axis=1)
            ws.append((e * qk).astype(BF16))
        a_last = a[L - 1:L, :]
        qa = (qf * jnp.exp2(a)).astype(BF16)
        kt = (kf * jnp.exp2(a_last - a)).astype(BF16)
        decay = jnp.exp2(a_last)
        for hd in range(HG_HEADS):
            sl = slice(LANES * hd, LANES * (hd + 1))
            vb = v_ref[pl.ds(r0, L), sl]
            sc = jnp.where(level_masks[0], _dot_nt(qb[:, sl], kb[:, sl]), 0.0)
            for i in range(len(HG_LEVELS)):
                w = ws[i][:, sl]
                sc = jnp.where(level_masks[i + 1], _dot_nt(w, w), sc)
            st = st_ref[hd]
            out = _dot(sc.astype(BF16), vb) + _dot_nt(qa[:, sl], st.astype(BF16))
            st_ref[hd] = st * decay[:, sl] + _dot_tn(vb, kt[:, sl])
            ms = jnp.mean(out * out, axis=-1, keepdims=True)
            y = (out * lax.rsqrt(ms + EPS)) * ng_ref[:, sl] * og_ref[pl.ds(r0, L), sl].astype(F32)
            o_ref[pl.ds(r0, L), sl] = y.astype(BF16)
        return carry

    lax.fori_loop(0, s // L, chunk, 0)


def hgrn(q, k, v, og, logf, norm_g):
    b, s, w = v.shape
    tall, lvl = _hg_tables()
    seq = pl.BlockSpec((None, s, w), lambda i: (i, 0, 0))
    const = lambda shape: pl.BlockSpec(shape, lambda i: (0,) * len(shape))
    return pl.pallas_call(
        _hgrn_kernel,
        grid=(b,),
        in_specs=[seq, seq, seq, seq, seq, const((1, w)), const(tall.shape), const(lvl.shape)],
        out_specs=seq,
        out_shape=jax.ShapeDtypeStruct((b, s, w), BF16),
        scratch_shapes=[pltpu.VMEM((HG_HEADS, LANES, HG_DK), F32)],
        compiler_params=_cp(("parallel",)),
        name="hgrn",
    )(q, k, v, og, logf, norm_g.reshape(1, w), tall, lvl)


def _merge_kernel(x_ref, h_ref, yf_ref, ym_ref, yh_ref, wg_ref, wb_ref, wo_ref, g1_ref, n2_ref, sc2_ref, sh2_ref,
                  x_out, h2_out):
    d = x_ref.shape[1]
    h = h_ref[...]
    merged = None
    for br, y_ref in enumerate((yf_ref, ym_ref, yh_ref)):
        gate = jax.nn.sigmoid(_dot(h, wg_ref[:, d * br:d * (br + 1)]))
        term = gate * _dot(y_ref[...], wb_ref[br])
        merged = term if merged is None else merged + term
    mixed = _dot(merged.astype(BF16), wo_ref[...])
    x1 = x_ref[...] + g1_ref[...] * mixed
    x_out[...] = x1
    h2_out[...] = _norm_mod(x1, n2_ref[...], sc2_ref[...], sh2_ref[...]).astype(BF16)


def merge_branches(x, h, y_fox, y_ml, y_hg, w_gates, w_branch, w_out, g1, norm2_g, sc2, sh2):
    b, s, d = x.shape
    tm = ROW_TILE
    w = BRANCH_WIDTH
    const = lambda shape: pl.BlockSpec(shape, lambda i, j: (0,) * len(shape))
    row = lambda n: pl.BlockSpec((None, tm, n), lambda i, j: (i, j, 0))
    per_b = pl.BlockSpec((None, 1, d), lambda i, j: (i, 0, 0))
    return pl.pallas_call(
        _merge_kernel,
        grid=(b, s // tm),
        in_specs=[row(d), row(d), row(w), row(w), row(w), const((d, N_BRANCH * d)), const((N_BRANCH, w, d)), const((d, d)),
                  per_b, const((1, d)), per_b, per_b],
        out_specs=[row(d), row(d)],
        out_shape=[jax.ShapeDtypeStruct((b, s, d), F32), jax.ShapeDtypeStruct((b, s, d), BF16)],
        compiler_params=_cp(("parallel", "parallel"), 56),
        name="merge_branches",
    )(x, h, y_fox, y_ml, y_hg, w_gates.astype(BF16), w_branch.astype(BF16), w_out.astype(BF16),
      g1.reshape(b, 1, d), norm2_g.reshape(1, d), sc2.reshape(b, 1, d), sh2.reshape(b, 1, d))


def _first_max(x, iota, size):
    m = jnp.max(x, axis=0, keepdims=True)
    idx = jnp.min(jnp.where(x == m, iota, size), axis=0, keepdims=True)
    return m, idx


def _route_kernel(h_ref, wr_ref, rb_ref, dest_out, w_out, cnt_out):
    n = h_ref.shape[0]
    scores = jax.nn.sigmoid(_dot_nt(wr_ref[...], h_ref[...]))
    choice = scores + rb_ref[...]
    e_iota = lax.broadcasted_iota(I32, (N_EXPERTS, n), 0)
    c3 = choice.reshape(N_GROUPS, GROUP_SIZE, n)
    i3 = lax.broadcasted_iota(I32, (N_GROUPS, GROUP_SIZE, n), 1)
    m1 = jnp.max(c3, axis=1, keepdims=True)
    i1 = jnp.min(jnp.where(c3 == m1, i3, GROUP_SIZE), axis=1, keepdims=True)
    m2 = jnp.max(jnp.where(i3 == i1, -jnp.inf, c3), axis=1, keepdims=True)
    gs = (m1 + m2).reshape(N_GROUPS, n)
    g_iota = lax.broadcasted_iota(I32, (N_GROUPS, n), 0)
    gsel = jnp.zeros((N_GROUPS, n), F32)
    for _ in range(TOPK_GROUPS):
        _, gi = _first_max(gs, g_iota, N_GROUPS)
        hit = g_iota == gi
        gsel = jnp.where(hit, 1.0, gsel)
        gs = jnp.where(hit, -jnp.inf, gs)
    gmask = jnp.broadcast_to(gsel.reshape(N_GROUPS, 1, n), (N_GROUPS, GROUP_SIZE, n)).reshape(N_EXPERTS, n)
    masked = jnp.where(gmask > 0.0, choice, -jnp.inf)
    sel = jnp.zeros((N_EXPERTS, n), F32)
    for _ in range(TOP_K):
        _, ei = _first_max(masked, e_iota, N_EXPERTS)
        hit = e_iota == ei
        sel = jnp.where(hit, 1.0, sel)
        masked = jnp.where(hit, -jnp.inf, masked)
    tr = lax.broadcasted_iota(I32, (n, n), 0)
    tc = lax.broadcasted_iota(I32, (n, n), 1)
    before = jnp.where(tr < tc, 1.0, 0.0).astype(BF16)
    pos = _dot(sel.astype(BF16), before)
    cnt = jnp.sum(sel, axis=1, keepdims=True)
    units = jnp.floor((cnt + (RUN_ALIGN - 1)) * (1.0 / RUN_ALIGN))
    er = lax.broadcasted_iota(I32, (N_EXPERTS, N_EXPERTS), 0)
    ec = lax.broadcasted_iota(I32, (N_EXPERTS, N_EXPERTS), 1)
    lower = jnp.where(ec < er, 1.0, 0.0).astype(BF16)
    off = _dot(lower, jnp.broadcast_to(units, (N_EXPERTS, LANES)).astype(BF16))[:, 0:1] * RUN_ALIGN
    dest = jnp.where(sel > 0.0, off + pos, float(DEST_NONE))
    dhi = jnp.floor(dest * (1.0 / DEST_RADIX))
    dest_out[0:N_EXPERTS, :] = dhi.astype(BF16)
    dest_out[N_EXPERTS:, :] = (dest - dhi * DEST_RADIX).astype(BF16)
    wsum = jnp.sum(scores * sel, axis=0, keepdims=True)
    w_out[0:N_EXPERTS, :] = jnp.zeros((N_EXPERTS, n), BF16)
    w_out[N_EXPERTS:, :] = (scores * sel / wsum * ROUTE_SCALE).astype(BF16)
    cnt_out[...] = jnp.broadcast_to(cnt, (N_EXPERTS, LANES)).astype(I32)


def moe_route(h2, router_w, router_bias):
    t, d = h2.shape
    n = SUB_TOKENS
    nsub = t // n
    mat = pl.BlockSpec((None, 2 * N_EXPERTS, n), lambda i: (i, 0, 0))
    dest, wts, cnt = pl.pallas_call(
        _route_kernel,
        grid=(nsub,),
        in_specs=[pl.BlockSpec((n, d), lambda i: (i, 0)), pl.BlockSpec((N_EXPERTS, d), lambda i: (0, 0)),
                  pl.BlockSpec((N_EXPERTS, 1), lambda i: (0, 0))],
        out_specs=[mat, mat, pl.BlockSpec((None, N_EXPERTS, LANES), lambda i: (i, 0, 0))],
        out_shape=[jax.ShapeDtypeStruct((nsub, 2 * N_EXPERTS, n), BF16), jax.ShapeDtypeStruct((nsub, 2 * N_EXPERTS, n), BF16),
                   jax.ShapeDtypeStruct((nsub, N_EXPERTS, LANES), I32)],
        compiler_params=_cp(("parallel",)),
        name="moe_route",
    )(h2, router_w.T.astype(BF16), router_bias.reshape(N_EXPERTS, 1))
    return dest, wts, cnt[:, :, 0]


def _run_tables(cnt, n_blocks):
    units = (cnt + (RUN_ALIGN - 1)) // RUN_ALIGN
    src = jnp.cumsum(units, axis=1) - units
    per_block = MOE_BLOCK // RUN_ALIGN
    tot = jnp.sum(units, axis=0)
    tot_blocks = (tot + per_block - 1) // per_block
    blk_end = jnp.cumsum(tot_blocks)
    base = (blk_end - tot_blocks) * per_block
    dst = base[None, :] + jnp.cumsum(units, axis=0) - units
    n_used = blk_end[-1]
    tail_units = tot_blocks * per_block - tot
    tail_dst = base + tot
    blk = jnp.minimum(jnp.arange(n_blocks), n_used - 1)
    blk_exp = jnp.minimum(jnp.sum(blk[:, None] >= blk_end[None, :], axis=1), N_EXPERTS - 1)
    return (src.reshape(-1).astype(I32), dst.reshape(-1).astype(I32), units.reshape(-1).astype(I32),
            blk_exp.astype(I32), n_used.astype(I32).reshape(1), tail_dst.astype(I32), tail_units.astype(I32),
            jnp.sum(units, axis=1).astype(I32))


def _copy_lists(src, dst, units, nsub):
    s0 = src.reshape(nsub, 1, N_EXPERTS)
    d0 = dst.reshape(nsub, 1, N_EXPERTS)
    u = units.reshape(nsub, 1, N_EXPERTS)
    npair = u // 2
    poff = jnp.cumsum(npair, axis=2) - npair
    p = jnp.arange(SUB_UNITS // 2, dtype=I32).reshape(1, -1, 1)
    own = (p >= poff) & (p < poff + npair)
    psrc = jnp.sum(jnp.where(own, s0 - 2 * poff, 0), axis=2) + 2 * p[:, :, 0]
    pdst = jnp.sum(jnp.where(own, d0 - 2 * poff, 0), axis=2) + 2 * p[:, :, 0]
    odd = u & 1
    soff = jnp.cumsum(odd, axis=2) - odd
    q = jnp.arange(N_EXPERTS, dtype=I32).reshape(1, -1, 1)
    owns = (odd == 1) & (soff == q)
    ssrc = jnp.sum(jnp.where(owns, s0 + u - 1, 0), axis=2)
    sdst = jnp.sum(jnp.where(owns, d0 + u - 1, 0), axis=2)
    flat = lambda a: a.reshape(-1).astype(I32)
    return (flat(psrc), flat(pdst), flat(jnp.sum(npair, axis=2)), flat(ssrc), flat(sdst), flat(jnp.sum(odd, axis=2)))


def _run_bounds(src, units, nsub):
    lo = (src.reshape(nsub, N_EXPERTS) * RUN_ALIGN).astype(F32)
    hi = lo + (units.reshape(nsub, N_EXPERTS) * RUN_ALIGN).astype(F32)
    return jnp.stack([jnp.concatenate([lo, lo], axis=1), jnp.concatenate([hi, hi], axis=1)], axis=1)


def _max_blocks(t):
    nsub = t // SUB_TOKENS
    worst_units = t * TOP_K // RUN_ALIGN + nsub * N_EXPERTS
    per_block = MOE_BLOCK // RUN_ALIGN
    return -(-worst_units // per_block) + N_EXPERTS


SUB_UNITS = SUB_ROWS // RUN_ALIGN
COPY_UNROLL = 4


PAIR_SLOTS = SUB_UNITS // 2


def _piece_copies(lists, step, buf, hbm, sem, to_hbm):
    psrc_ref, pdst_ref, npair_ref, ssrc_ref, sdst_ref, nsingle_ref = lists

    def piece(src_ref, dst_ref, idx, units):
        rows = units * RUN_ALIGN
        v = buf.at[pl.ds(pl.multiple_of(src_ref[idx] * RUN_ALIGN, RUN_ALIGN), rows)]
        g = hbm.at[pl.ds(pl.multiple_of(dst_ref[idx] * RUN_ALIGN, RUN_ALIGN), rows)]
        cp = pltpu.make_async_copy(v, g, sem) if to_hbm else pltpu.make_async_copy(g, v, sem)
        cp.start()

    def issue(src_ref, dst_ref, base, count, units):
        def group(q, carry):
            for r in range(COPY_UNROLL):
                piece(src_ref, dst_ref, base + q * COPY_UNROLL + r, units)
            return carry

        groups = lax.shift_right_logical(count, COPY_UNROLL.bit_length() - 1)
        lax.fori_loop(0, groups, group, 0)
        lax.fori_loop(groups * COPY_UNROLL, count, lambda j, c: (piece(src_ref, dst_ref, base + j, units), c)[1], 0)

    issue(psrc_ref, pdst_ref, step * PAIR_SLOTS, npair_ref[step], 2)
    issue(ssrc_ref, sdst_ref, step * N_EXPERTS, nsingle_ref[step], 1)


TOTAL_BITS = tuple(1 << b for b in range(SUB_UNITS.bit_length()))


def _wait_runs(total_units, buf, hbm, sem, to_hbm):
    for bit in TOTAL_BITS:
        @pl.when((total_units & bit) != 0)
        def _():
            rows = bit * RUN_ALIGN
            v = buf.at[pl.ds(0, rows)]
            g = hbm.at[pl.ds(0, rows)]
            cp = pltpu.make_async_copy(v, g, sem) if to_hbm else pltpu.make_async_copy(g, v, sem)
            cp.wait()


def _dispatch_kernel(psrc_ref, pdst_ref, npair_ref, ssrc_ref, sdst_ref, nsingle_ref, tot_ref, tdst_ref, tunits_ref, nused_ref,
                     h_ref, dest_ref, lohi_ref, xs_out, buf_ref, zero_ref, sem):
    lists = (psrc_ref, pdst_ref, npair_ref, ssrc_ref, sdst_ref, nsingle_ref)
    i = pl.program_id(0)
    nsub = pl.num_programs(0)
    slot = i % 2
    n = h_ref.shape[0]
    chunk = SORT_CHUNK

    for sl in range(2):
        @pl.when((slot == sl) & (i >= 2))
        def _():
            _wait_runs(tot_ref[i - 2], buf_ref.at[sl], xs_out, sem.at[sl], True)

    h = h_ref[...]
    dest = dest_ref[...]
    lo = lohi_ref[0:1, :]
    hi = lohi_ref[1:2, :]
    radix = jnp.where(lax.broadcasted_iota(I32, (1, 2 * N_EXPERTS), 1) < N_EXPERTS, float(DEST_RADIX), 1.0)
    r_e = lax.broadcasted_iota(I32, (chunk, 2 * N_EXPERTS), 0).astype(F32)
    r_t = lax.broadcasted_iota(I32, (chunk, n), 0).astype(F32)
    for sl in range(2):
        @pl.when(slot == sl)
        def _():
            for c in range(SUB_ROWS // chunk):
                own = jnp.where(r_e + c * chunk >= lo, jnp.where(r_e + c * chunk < hi, radix, 0.0), 0.0)
                row_of = _dot(own.astype(BF16), dest)
                p = jnp.where(row_of == r_t + c * chunk, 1.0, 0.0)
                buf_ref[sl, c * chunk:(c + 1) * chunk, :] = _dot(p.astype(BF16), h).astype(BF16)
            _piece_copies(lists, i, buf_ref.at[sl], xs_out, sem.at[sl], True)

    @pl.when(i == nsub - 1)
    def _():
        zero_ref[...] = jnp.zeros_like(zero_ref)

        def tails(e, wait):
            u = tunits_ref[e]
            d0 = tdst_ref[e]
            for bit in RUN_BITS:
                low = u & (bit - 1)

                @pl.when((u & bit) != 0)
                def _():
                    rows = bit * RUN_ALIGN
                    cp = pltpu.make_async_copy(zero_ref.at[pl.ds(0, rows)],
                                               xs_out.at[pl.ds(pl.multiple_of((d0 + low) * RUN_ALIGN, RUN_ALIGN), rows)], sem.at[2])
                    if wait:
                        cp.wait()
                    else:
                        cp.start()
            return wait

        def unused(b, wait):
            cp = pltpu.make_async_copy(zero_ref.at[pl.ds(0, MOE_BLOCK)],
                                       xs_out.at[pl.ds(pl.multiple_of(b * MOE_BLOCK, MOE_BLOCK), MOE_BLOCK)], sem.at[2])
            if wait:
                cp.wait()
            else:
                cp.start()
            return wait

        n_blocks = xs_out.shape[0] // MOE_BLOCK
        lax.fori_loop(0, N_EXPERTS, lambda e, c: (tails(e, False), c)[1], 0)
        lax.fori_loop(nused_ref[0], n_blocks, lambda b, c: (unused(b, False), c)[1], 0)
        for sl in range(2):
            @pl.when((slot != sl) & (i >= 1))
            def _():
                _wait_runs(tot_ref[i - 1], buf_ref.at[sl], xs_out, sem.at[sl], True)

            @pl.when(slot == sl)
            def _():
                _wait_runs(tot_ref[i], buf_ref.at[sl], xs_out, sem.at[sl], True)
        lax.fori_loop(0, N_EXPERTS, lambda e, c: (tails(e, True), c)[1], 0)
        lax.fori_loop(nused_ref[0], n_blocks, lambda b, c: (unused(b, True), c)[1], 0)


def moe_dispatch(h2, dest, tables, n_rows):
    t, d = h2.shape
    n = SUB_TOKENS
    nsub = t // n
    src, dst, units, _, n_used, tail_dst, tail_units, tot = tables
    grid_spec = pltpu.PrefetchScalarGridSpec(
        num_scalar_prefetch=10,
        grid=(nsub,),
        in_specs=[pl.BlockSpec((n, d), lambda i, *_: (i, 0)), pl.BlockSpec((None, 2 * N_EXPERTS, n), lambda i, *_: (i, 0, 0)),
                  pl.BlockSpec((None, 2, 2 * N_EXPERTS), lambda i, *_: (i, 0, 0))],
        out_specs=pl.BlockSpec(memory_space=pl.ANY),
        scratch_shapes=[pltpu.VMEM((2, SUB_ROWS, d), BF16), pltpu.VMEM((max(SUB_TOKENS, MOE_BLOCK), d), BF16), pltpu.SemaphoreType.DMA((3,))],
    )
    return pl.pallas_call(
        _dispatch_kernel,
        grid_spec=grid_spec,
        out_shape=jax.ShapeDtypeStruct((n_rows, d), BF16),
        compiler_params=_cp(("arbitrary",)),
        name="moe_dispatch",
    )(*_copy_lists(src, dst, units, nsub), tot, tail_dst, tail_units, n_used, h2, dest, _run_bounds(src, units, nsub))


X_SLOTS = 3


def _expert_kernel(blk_exp_ref, n_used_ref, x_hbm, w1_ref, w3_ref, w2_ref, y_ref, xbuf, sem):
    b = pl.program_id(0)
    n_used = n_used_ref[0]

    def x_copy(blk, slot):
        rows = pl.ds(pl.multiple_of(blk * MOE_BLOCK, MOE_BLOCK), MOE_BLOCK)
        return pltpu.make_async_copy(x_hbm.at[rows], xbuf.at[slot], sem.at[slot])

    @pl.when(b == 0)
    def _():
        x_copy(0, 0).start()

        @pl.when(n_used > 1)
        def _():
            x_copy(1, 1).start()

    ahead = b + (X_SLOTS - 1)

    @pl.when(ahead < n_used)
    def _():
        x_copy(ahead, lax.rem(ahead, X_SLOTS)).start()

    @pl.when(b < n_used)
    def _():
        slot = lax.rem(b, X_SLOTS)
        x_copy(b, slot).wait()
        x = xbuf[slot]
        hid = _silu(_dot(x, w1_ref[...].astype(BF16))) * _dot(x, w3_ref[...].astype(BF16))
        y_ref[...] = _dot(hid.astype(BF16), w2_ref[...].astype(BF16)).astype(BF16)

    @pl.when(b >= n_used)
    def _():
        y_ref[...] = jnp.zeros_like(y_ref)


def moe_experts(xs, w1, w3, w2, layer, tables, n_blocks):
    n_rows, d = xs.shape
    blk_exp, n_used = tables[3], tables[4]
    f = w1.shape[-1]

    def w_map(b, be, nu):
        return (layer, be[b], 0, 0)

    grid_spec = pltpu.PrefetchScalarGridSpec(
        num_scalar_prefetch=2,
        grid=(n_blocks,),
        in_specs=[pl.BlockSpec(memory_space=pl.ANY), pl.BlockSpec((None, None, d, f), w_map),
                  pl.BlockSpec((None, None, d, f), w_map), pl.BlockSpec((None, None, f, d), w_map)],
        out_specs=pl.BlockSpec((MOE_BLOCK, d), lambda b, be, nu: (b, 0)),
        scratch_shapes=[pltpu.VMEM((X_SLOTS, MOE_BLOCK, d), BF16), pltpu.SemaphoreType.DMA((X_SLOTS,))],
    )
    return pl.pallas_call(
        _expert_kernel,
        grid_spec=grid_spec,
        out_shape=jax.ShapeDtypeStruct((n_rows, d), BF16),
        compiler_params=_cp(("arbitrary",)),
        name="moe_experts",
    )(blk_exp, n_used, xs, w1, w3, w2)


def _combine_kernel(psrc_ref, pdst_ref, npair_ref, ssrc_ref, sdst_ref, nsingle_ref, tot_ref, ys_ref, dcol_ref, wcol_ref, lohi_ref,
                    h_ref, x_ref, g2_ref, ws1_ref, ws3_ref, ws2_ref, o_ref, buf_ref, sem):
    lists = (psrc_ref, pdst_ref, npair_ref, ssrc_ref, sdst_ref, nsingle_ref)
    i = pl.program_id(0)
    nsub = pl.num_programs(0)
    slot = i % 2
    n = h_ref.shape[0]
    chunk = 512

    def fetch(step, sl):
        always = SUB_TOKENS * TOP_K
        buf_ref[sl, always:, :] = jnp.zeros((SUB_ROWS - always, buf_ref.shape[2]), BF16)
        _piece_copies(lists, step, buf_ref.at[sl], ys_ref, sem.at[sl], False)

    @pl.when(i == 0)
    def _():
        fetch(0, 0)

    for sl in range(2):
        @pl.when((slot != sl) & (i + 1 < nsub))
        def _():
            fetch(i + 1, sl)

    h = h_ref[...]
    shared = _dot((_silu(_dot(h, ws1_ref[...])) * _dot(h, ws3_ref[...])).astype(BF16), ws2_ref[...])
    dest_t = dcol_ref[...]
    w_t = wcol_ref[...]
    lo = lohi_ref[:, 0:1]
    hi = lohi_ref[:, 1:2]
    radix = jnp.where(lax.broadcasted_iota(I32, (2 * N_EXPERTS, 1), 0) < N_EXPERTS, float(DEST_RADIX), 1.0)
    r_e = lax.broadcasted_iota(I32, (2 * N_EXPERTS, chunk), 1).astype(F32)
    r_t = lax.broadcasted_iota(I32, (n, chunk), 1).astype(F32)
    for sl in range(2):
        @pl.when(slot == sl)
        def _():
            _wait_runs(tot_ref[i], buf_ref.at[sl], ys_ref, sem.at[sl], False)
            acc = shared
            for c in range(SUB_ROWS // chunk):
                own = jnp.where(r_e + c * chunk >= lo, jnp.where(r_e + c * chunk < hi, radix, 0.0), 0.0).astype(BF16)
                row_of = _dot(dest_t, own)
                pw = jnp.where(row_of == r_t + c * chunk, _dot(w_t, own), 0.0)
                acc = acc + _dot(pw.astype(BF16), buf_ref[sl, c * chunk:(c + 1) * chunk, :])
            o_ref[...] = x_ref[...] + g2_ref[...] * acc


def moe_combine(ys, dest, wts, h2, x1, g2, ws1, ws3, ws2, tables, seq):
    t, d = h2.shape
    n = SUB_TOKENS
    nsub = t // n
    src, dst, units, tot = tables[0], tables[1], tables[2], tables[7]
    dcol = jnp.swapaxes(dest, 1, 2)
    wcol = jnp.swapaxes(wts, 1, 2)
    bounds = jnp.swapaxes(_run_bounds(src, units, nsub), 1, 2)
    per_seq = seq // n
    f = ws1.shape[-1]
    pair = pl.BlockSpec((None, n, 2 * N_EXPERTS), lambda i, *_: (i, 0, 0))
    grid_spec = pltpu.PrefetchScalarGridSpec(
        num_scalar_prefetch=7,
        grid=(nsub,),
        in_specs=[pl.BlockSpec(memory_space=pl.ANY), pair, pair,
                  pl.BlockSpec((None, 2 * N_EXPERTS, 2), lambda i, *_: (i, 0, 0)),
                  pl.BlockSpec((n, d), lambda i, *_: (i, 0)), pl.BlockSpec((n, d), lambda i, *_: (i, 0)),
                  pl.BlockSpec((None, 1, d), lambda i, *_: (i // per_seq, 0, 0)),
                  pl.BlockSpec((d, f), lambda i, *_: (0, 0)), pl.BlockSpec((d, f), lambda i, *_: (0, 0)),
                  pl.BlockSpec((f, d), lambda i, *_: (0, 0))],
        out_specs=pl.BlockSpec((n, d), lambda i, *_: (i, 0)),
        scratch_shapes=[pltpu.VMEM((2, SUB_ROWS, d), BF16), pltpu.SemaphoreType.DMA((2,))],
    )
    return pl.pallas_call(
        _combine_kernel,
        grid_spec=grid_spec,
        out_shape=jax.ShapeDtypeStruct((t, d), F32),
        compiler_params=_cp(("arbitrary",)),
        name="moe_combine",
    )(*_copy_lists(src, dst, units, nsub), tot, ys, dcol, wcol, bounds, h2, x1, g2, ws1.astype(BF16), ws3.astype(BF16), ws2.astype(BF16))


def _split_w_in(w):
    fh = FOX_HEADS * FOX_HEAD_DIM
    sizes = (fh, fh, fh, FOX_HEADS,
             2 * ML_HEADS * ML_DQK, ML_HEADS * ML_DV, ML_HEADS, ML_HEADS, ML_HEADS * ML_DV,
             HG_HEADS * HG_DK, HG_HEADS * HG_DK, BRANCH_WIDTH, BRANCH_WIDTH,
             N_BRANCH * D_MODEL)
    outs, o = [], 0
    for sz in sizes:
        outs.append(w[:, o:o + sz])
        o += sz
    return outs


def moe_ffn(h2, x1, g2, router_w, router_bias, w1, w3, w2, layer, ws1, ws3, ws2, seq):
    t, d = h2.shape
    dest, wts, cnt = moe_route(h2, router_w, router_bias)
    n_blocks = _max_blocks(t)
    tables = _run_tables(cnt, n_blocks)
    xs = moe_dispatch(h2, dest, tables, n_blocks * MOE_BLOCK)
    ys = moe_experts(xs, w1, w3, w2, layer, tables, n_blocks)
    return moe_combine(ys, dest, wts, h2, x1, g2, ws1, ws3, ws2, tables, seq)


def kernel(x, c, ada_w, ada_b, norm1_g, norm2_g, w_in, fox_bf, fox_q_g, fox_k_g, mlstm_conv, mlstm_bi, mlstm_bf, mlstm_norm_g, hgrn_lower_bounds, hgrn_norm_g, w_branch, w_out, router_w, router_bias, exp_w1, exp_w3, exp_w2, sh_w1, sh_w3, sh_w2):
    b, s, d = x.shape
    depth = ada_w.shape[0]
    mod = adaln_mod(c, ada_w, ada_b)
    lb_all = jnp.cumsum(jax.nn.softmax(hgrn_lower_bounds.astype(F32), axis=0), axis=0)
    lb_all = lb_all - lb_all[0]
    for l in range(depth):
        sh1, sc1, g1, sh2, sc2, g2 = [mod[l][:, d * j:d * (j + 1)] for j in range(6)]
        (wfq, wfk, wfv, wff, wmqk, wmv, wmi, wmf, wmo, whf, whq, whi, whg, wgates) = _split_w_in(w_in[l])
        h = norm_modulate(x, norm1_g[l], sc1, sh1)
        qp, kp, fv = fox_project(h, wfq, wfk, wfv, wff, fox_q_g[l], fox_k_g[l], fox_bf[l])
        y_fox = fox_attention(qp, kp, fv)
        mq, mk, mv, mog, mgates = ml_project(h, wmqk, wmv, wmi, wmf, wmo, mlstm_conv[l], mlstm_bi[l], mlstm_bf[l])
        y_ml = mlstm(mq, mk, mv, mog, mgates, mlstm_norm_g[l])
        hq, hk, hv, hog, hlf = hg_project(h, whf, whq, whi, whg, lb_all[l])
        y_hg = hgrn(hq, hk, hv, hog, hlf, hgrn_norm_g[l])
        x1, h2 = merge_branches(x, h, y_fox, y_ml, y_hg, wgates, w_branch[l], w_out[l], g1, norm2_g[l], sc2, sh2)
        x = moe_ffn(h2.reshape(b * s, d), x1.reshape(b * s, d), g2.reshape(b, 1, d), router_w[l], router_bias[l],
                    exp_w1, exp_w3, exp_w2, l, sh_w1[l], sh_w3[l], sh_w2[l], s).reshape(b, s, d)
    return x
```

```python
import functools

import jax
import jax.numpy as jnp
import numpy as np
from jax import lax
from jax.experimental import pallas as pl
from jax.experimental.pallas import tpu as pltpu

F32 = jnp.float32
BF16 = jnp.bfloat16
I32 = jnp.int32

LANES = 128
SUBLANES = 8
BF16_ROWS = 16
VMEM_BYTES = 64 * 1024 * 1024

D_MODEL = 1024
BRANCH_WIDTH = D_MODEL // 2
N_BRANCH = 3
FOX_HEAD_DIM = 64
FOX_HEADS = BRANCH_WIDTH // FOX_HEAD_DIM
ML_HEADS = 4
ML_DV = BRANCH_WIDTH // ML_HEADS
ML_DQK = ML_DV // 2
ML_CONV = 4
ML_CHUNK = 128
HG_HEADS = 4
HG_DK = 128
HG_CHUNK = 128
N_EXPERTS = 64
N_GROUPS = 8
GROUP_SIZE = N_EXPERTS // N_GROUPS
TOPK_GROUPS = 4
TOP_K = 8
D_FF = D_MODEL // 4
ROUTE_SCALE = 2.5
MOE_BLOCK = 1024
SORT_CHUNK = 1024
EPS = 1e-6
NEG = -1e30
LOG2E = 1.4426950408889634

ROW_TILE = 512
PROJ_TILE = 1024
ATTN_TILE = 256
SUB_TOKENS = 512
RUN_ALIGN = BF16_ROWS
SUB_ROWS = ((SUB_TOKENS * TOP_K + N_EXPERTS * (RUN_ALIGN - 1)) + 255) // 256 * 256
RUN_BITS = tuple(1 << b for b in range((SUB_TOKENS // RUN_ALIGN).bit_length()))
DEST_RADIX = 64
DEST_NONE = DEST_RADIX * 127
assert SUB_ROWS <= DEST_NONE


def _cp(sem, vmem_mb=48):
    return pltpu.CompilerParams(dimension_semantics=sem, vmem_limit_bytes=vmem_mb * 1024 * 1024)


def _dot(a, b):
    return jnp.dot(a, b, preferred_element_type=F32)


def _dot_nt(a, b):
    return lax.dot_general(a, b, (((1,), (1,)), ((), ())), preferred_element_type=F32)


def _dot_tn(a, b):
    return lax.dot_general(a, b, (((0,), (0,)), ((), ())), preferred_element_type=F32)


def _split3(x):
    hi = x.astype(BF16)
    r = x - hi.astype(F32)
    mid = r.astype(BF16)
    lo = (r - mid.astype(F32)).astype(BF16)
    return hi, mid, lo


def _tri_dot(tri, x):
    hi, mid, lo = _split3(x)
    return (_dot(tri, hi) + _dot(tri, mid)) + _dot(tri, lo)


def _dot_tri(x, tri):
    hi, mid, lo = _split3(x)
    return (_dot(hi, tri) + _dot(mid, tri)) + _dot(lo, tri)


def _log_sigmoid(x):
    return jnp.minimum(x, 0.0) - jnp.log1p(jnp.exp(-jnp.abs(x)))


def _silu(x):
    return x * jax.nn.sigmoid(x)


def _tri_incl(n, dtype=BF16):
    r = lax.broadcasted_iota(I32, (n, n), 0)
    c = lax.broadcasted_iota(I32, (n, n), 1)
    return jnp.where(c <= r, 1.0, 0.0).astype(dtype)


def _mod_kernel(c_ref, w_ref, b_ref, o_ref):
    cond = _silu(c_ref[...])
    hi, mid, lo = _split3(cond)
    w = w_ref[...]
    whi, wmid, wlo = _split3(w)
    acc = _dot(hi, whi) + (_dot(hi, wmid) + _dot(mid, whi))
    acc = acc + (_dot(mid, wmid) + _dot(hi, wlo) + _dot(lo, whi))
    o_ref[...] = acc + b_ref[...]


def adaln_mod(c, ada_w, ada_b):
    depth, d, n = ada_w.shape
    b = c.shape[0]
    tn = 1024
    return pl.pallas_call(
        _mod_kernel,
        grid=(depth, n // tn),
        in_specs=[
            pl.BlockSpec((b, d), lambda l, j: (0, 0)),
            pl.BlockSpec((None, d, tn), lambda l, j: (l, 0, j)),
            pl.BlockSpec((None, 1, tn), lambda l, j: (l, 0, j)),
        ],
        out_specs=pl.BlockSpec((None, b, tn), lambda l, j: (l, 0, j)),
        out_shape=jax.ShapeDtypeStruct((depth, b, n), F32),
        compiler_params=_cp(("parallel", "parallel")),
        name="adaln_mod",
    )(c, ada_w, ada_b.reshape(depth, 1, n))


def _norm_mod(x, g, sc, sh):
    ms = jnp.mean(x * x, axis=-1, keepdims=True)
    return x * lax.rsqrt(ms + EPS) * g * (1.0 + sc) + sh


def _norm_kernel(x_ref, g_ref, sc_ref, sh_ref, h_ref):
    h_ref[...] = _norm_mod(x_ref[...], g_ref[...], sc_ref[...], sh_ref[...]).astype(BF16)


def norm_modulate(x, g, sc, sh):
    b, s, d = x.shape
    tm = ROW_TILE
    return pl.pallas_call(
        _norm_kernel,
        grid=(b, s // tm),
        in_specs=[
            pl.BlockSpec((None, tm, d), lambda i, j: (i, j, 0)),
            pl.BlockSpec((1, d), lambda i, j: (0, 0)),
            pl.BlockSpec((None, 1, d), lambda i, j: (i, 0, 0)),
            pl.BlockSpec((None, 1, d), lambda i, j: (i, 0, 0)),
        ],
        out_specs=pl.BlockSpec((None, tm, d), lambda i, j: (i, j, 0)),
        out_shape=jax.ShapeDtypeStruct((b, s, d), BF16),
        compiler_params=_cp(("parallel", "parallel")),
        name="norm_modulate",
    )(x, g.reshape(1, d), sc.reshape(b, 1, d), sh.reshape(b, 1, d))


FOX_BIAS_PIECES = 3
FOX_SLOT = 2 * LANES
FOX_BIAS_STRIDE = SUBLANES


def _pack_pieces(x):
    hi, mid, lo = _split3(x)
    p = hi.astype(F32) + pltpu.roll(mid.astype(F32), FOX_HEADS, axis=1) + pltpu.roll(lo.astype(F32), 2 * FOX_HEADS, axis=1)
    return p.astype(BF16)


def _fox_proj_kernel(h_ref, wq_ref, wk_ref, wvt_ref, wf_ref, gq_ref, gk_ref, bf_ref, eq_ref, ek_ref, cq_ref, ck_ref,
                     q_out, k_out, vt_out, carry_ref):
    @pl.when(pl.program_id(1) == 0)
    def _():
        carry_ref[...] = jnp.zeros_like(carry_ref)

    h = h_ref[...]
    tm = h.shape[0]
    pr = lax.broadcasted_iota(I32, (LANES, LANES), 0)
    pc = lax.broadcasted_iota(I32, (LANES, LANES), 1)
    avg_pair = jnp.where((pr < FOX_HEAD_DIM) == (pc < FOX_HEAD_DIM), 1.0 / FOX_HEAD_DIM, 0.0).astype(BF16)

    def head_norm(x):
        outs = []
        for pair in range(FOX_HEADS // 2):
            xp = x[:, LANES * pair:LANES * (pair + 1)]
            ms = _dot((xp * xp).astype(BF16), avg_pair)
            outs.append(xp * lax.rsqrt(ms + EPS))
        return outs

    def slots(normed, g_ref, bias):
        parts = []
        for pair in range(FOX_HEADS // 2):
            lanes = slice(LANES * pair, LANES * (pair + 1))
            parts += [normed[pair] * g_ref[:, lanes], bias[:, lanes]]
        return jnp.concatenate(parts, axis=1).astype(BF16)

    lane = lax.broadcasted_iota(I32, (tm, LANES), 1)
    logf = jnp.where(lane < FOX_HEADS, _log_sigmoid(_dot(h, wf_ref[...]) + bf_ref[...]), 0.0)
    cs = _dot(_tri_incl(tm), _pack_pieces(logf))
    cum = cs + pltpu.roll(cs, LANES - FOX_HEADS, axis=1) + pltpu.roll(cs, LANES - 2 * FOX_HEADS, axis=1)
    cum = jnp.where(lane < FOX_HEADS, cum, 0.0) + carry_ref[...]
    carry_ref[...] = cum[tm - 1:tm, :]
    pieces = _pack_pieces(cum * LOG2E)

    q_out[...] = slots(head_norm(_dot(h, wq_ref[...])), gq_ref, _dot(pieces, eq_ref[...]) + cq_ref[...])
    k_out[...] = slots(head_norm(_dot(h, wk_ref[...])), gk_ref, _dot(pieces, ek_ref[...]) + ck_ref[...])
    vt = _dot_nt(wvt_ref[...], h)
    ones = jnp.ones((FOX_HEAD_DIM, tm), F32)
    slots = []
    for hd in range(FOX_HEADS):
        slots += [vt[FOX_HEAD_DIM * hd:FOX_HEAD_DIM * (hd + 1), :], ones]
    vt_out[...] = jnp.concatenate(slots, axis=0).astype(BF16)


def _pad_heads(w, heads, dim):
    lead = w.shape[:-1]
    w = w.reshape(*lead, heads, dim)
    w = jnp.pad(w, [(0, 0)] * len(lead) + [(0, 0), (0, LANES - dim)])
    return w.reshape(*lead, heads * LANES)


def _fox_constants():
    width = FOX_HEADS // 2 * LANES
    eq = np.zeros((LANES, width), np.float32)
    ek = np.zeros((LANES, width), np.float32)
    cq = np.zeros((1, width), np.float32)
    ck = np.zeros((1, width), np.float32)
    for hd in range(FOX_HEADS):
        base = LANES * (hd // 2) + FOX_BIAS_STRIDE * (hd % 2)
        for p in range(FOX_BIAS_PIECES):
            eq[p * FOX_HEADS + hd, base + p] = 1.0
            cq[0, base + FOX_BIAS_PIECES + p] = 1.0
            ck[0, base + p] = 1.0
            ek[p * FOX_HEADS + hd, base + FOX_BIAS_PIECES + p] = -1.0
    return jnp.asarray(eq, BF16), jnp.asarray(ek, BF16), jnp.asarray(cq), jnp.asarray(ck)


def _fox_head_masks():
    m = np.zeros((2, FOX_SLOT), np.float32)
    for a in range(2):
        m[a, FOX_HEAD_DIM * a:FOX_HEAD_DIM * (a + 1)] = 1.0
        m[a, LANES + FOX_BIAS_STRIDE * a:LANES + FOX_BIAS_STRIDE * a + 2 * FOX_BIAS_PIECES] = 1.0
    return jnp.asarray(m, BF16)


def fox_project(h, wq, wk, wv, wf, q_g, k_g, bf):
    b, s, d = h.shape
    tm = min(PROJ_TILE, s)
    hp = FOX_HEADS // 2 * FOX_SLOT
    hw = FOX_HEADS * FOX_HEAD_DIM
    wq_p = wq.astype(BF16)
    wk_p = wk.astype(BF16)
    wf_p = jnp.pad(wf, ((0, 0), (0, LANES - FOX_HEADS))).astype(BF16)
    gq = jnp.tile(q_g * (FOX_HEAD_DIM ** -0.5 * LOG2E), FOX_HEADS)[None, :]
    gk = jnp.tile(k_g, FOX_HEADS)[None, :]
    bf_p = jnp.pad(bf, (0, LANES - FOX_HEADS))[None, :]
    eq, ek, cq, ck = _fox_constants()
    const = lambda shape: pl.BlockSpec(shape, lambda i, j: (0,) * len(shape))
    row = lambda n: pl.BlockSpec((None, tm, n), lambda i, j: (i, j, 0))
    return pl.pallas_call(
        _fox_proj_kernel,
        grid=(b, s // tm),
        in_specs=[row(d), const((d, hw)), const((d, hw)), const((BRANCH_WIDTH, d)), const((d, LANES)),
                  const((1, hw)), const((1, hw)), const((1, LANES)), const((LANES, hw)), const((LANES, hw)),
                  const((1, hw)), const((1, hw))],
        out_specs=[row(hp), row(hp), pl.BlockSpec((None, hp, tm), lambda i, j: (i, 0, j))],
        out_shape=[jax.ShapeDtypeStruct((b, s, hp), BF16), jax.ShapeDtypeStruct((b, s, hp), BF16),
                   jax.ShapeDtypeStruct((b, hp, s), BF16)],
        scratch_shapes=[pltpu.VMEM((1, LANES), F32)],
        compiler_params=_cp(("parallel", "arbitrary")),
        name="fox_project",
    )(h, wq_p, wk_p, wv.T.astype(BF16), wf_p, gq, gk, bf_p, eq, ek, cq, ck)


ATTN_HEADS = 8


def _fox_attn_kernel(q_ref, k_ref, vt_ref, hm_ref, o_ref):
    i = pl.program_id(2)
    t = q_ref.shape[0]
    krow = lax.broadcasted_iota(I32, (t, t), 0)
    qcol = lax.broadcasted_iota(I32, (t, t), 1)
    slot = lambda a: slice(FOX_SLOT * (a // 2), FOX_SLOT * (a // 2 + 1))
    qs = [q_ref[:, slot(a)] * hm_ref[a % 2:a % 2 + 1, :] for a in range(ATTN_HEADS)]

    def scores(j):
        start = pl.multiple_of(j * t, t)
        return tuple(_dot_nt(k_ref[pl.ds(start, t), slot(a)], qs[a]) for a in range(ATTN_HEADS))

    def consume(j, state, ss, masked):
        start = pl.multiple_of(j * t, t)
        new = []
        for a in range(ATTN_HEADS):
            m, acc = state[a]
            s = jnp.where(krow <= qcol, ss[a], NEG) if masked else ss[a]
            m_new = jnp.maximum(m, jnp.max(s, axis=0, keepdims=True))
            p = jnp.exp2(s - m_new)
            alpha = jnp.exp2(m - m_new)
            vt = vt_ref[LANES * a:LANES * (a + 1), pl.ds(start, t)]
            acc = alpha * acc + _dot(vt, p.astype(BF16))
            new.append((m_new, acc))
        return tuple(new)

    def body(j, state):
        return consume(j, state, scores(j), False)

    init = tuple((jnp.full((1, t), NEG, F32), jnp.zeros((LANES, t), F32)) for _ in range(ATTN_HEADS))
    state = lax.fori_loop(0, i, body, init)
    state = consume(i, state, scores(i), True)
    for p in range(ATTN_HEADS // 2):
        halves = []
        for _, acc in (state[2 * p], state[2 * p + 1]):
            halves.append(acc[:FOX_HEAD_DIM, :] / acc[FOX_HEAD_DIM:FOX_HEAD_DIM + 1, :])
        o_ref[:, LANES * p:LANES * (p + 1)] = jnp.concatenate(halves, axis=0).T.astype(BF16)


def fox_attention(qp, kp, vt):
    b, s, hp = qp.shape
    t = ATTN_TILE
    groups = FOX_HEADS // ATTN_HEADS
    return pl.pallas_call(
        _fox_attn_kernel,
        grid=(b, groups, s // t),
        in_specs=[
            pl.BlockSpec((None, t, ATTN_HEADS * LANES), lambda bi, p, i: (bi, i, p)),
            pl.BlockSpec((None, s, ATTN_HEADS * LANES), lambda bi, p, i: (bi, 0, p)),
            pl.BlockSpec((None, ATTN_HEADS * LANES, s), lambda bi, p, i: (bi, p, 0)),
            pl.BlockSpec((2, FOX_SLOT), lambda bi, p, i: (0, 0)),
        ],
        out_specs=pl.BlockSpec((None, t, ATTN_HEADS // 2 * LANES), lambda bi, p, i: (bi, i, p)),
        out_shape=jax.ShapeDtypeStruct((b, s, BRANCH_WIDTH), BF16),
        compiler_params=_cp(("parallel", "parallel", "arbitrary")),
        name="fox_attention",
    )(qp, kp, vt, _fox_head_masks())


CONV_HALO = SUBLANES


def _ml_proj_kernel(h_ref, wqk_ref, wvt_ref, wo_ref, wg_ref, conv_ref, gb_ref, q_out, k_out, vt_out, og_out, g_out, buf_ref):
    tm = h_ref.shape[0]
    half = ML_HEADS * ML_DQK

    @pl.when(pl.program_id(1) == 0)
    def _():
        buf_ref[0:CONV_HALO, :] = jnp.zeros((CONV_HALO, 2 * half), F32)

    h = h_ref[...]
    buf_ref[CONV_HALO:CONV_HALO + tm, :] = _dot(h, wqk_ref[...])
    acc = jnp.zeros((tm, 2 * half), F32)
    for j in range(ML_CONV):
        off = CONV_HALO - (ML_CONV - 1) + j
        acc = acc + conv_ref[j:j + 1, :] * buf_ref[off:off + tm, :]
    buf_ref[0:CONV_HALO, :] = buf_ref[tm:tm + CONV_HALO, :]
    act = _silu(acc)
    q_out[...] = act[:, :half].astype(BF16)
    k_out[...] = (act[:, half:] * (ML_DQK ** -0.5)).astype(BF16)
    vt_out[...] = _dot_nt(wvt_ref[...], h).astype(BF16)
    og_out[...] = jax.nn.sigmoid(_dot(h, wo_ref[...])).astype(BF16)
    g = _dot(h, wg_ref[...]) + gb_ref[...]
    lane = lax.broadcasted_iota(I32, (tm, LANES), 1)
    g_out[...] = jnp.where(lane < ML_HEADS, g, _log_sigmoid(g))


def ml_project(h, wqk, wv, wi, wf, wo, conv, bi, bf):
    b, s, d = h.shape
    tm = min(PROJ_TILE, s)
    half = ML_HEADS * ML_DQK
    wqk_p = wqk.astype(BF16)
    conv_p = conv
    wg = jnp.pad(jnp.concatenate([wi, wf], axis=1), ((0, 0), (0, LANES - 2 * ML_HEADS))).astype(BF16)
    gb = jnp.pad(jnp.concatenate([bi, bf]), (0, LANES - 2 * ML_HEADS))[None, :]
    const = lambda shape: pl.BlockSpec(shape, lambda i, j: (0,) * len(shape))
    row = lambda n: pl.BlockSpec((None, tm, n), lambda i, j: (i, j, 0))
    return pl.pallas_call(
        _ml_proj_kernel,
        grid=(b, s // tm),
        in_specs=[row(d), const((d, 2 * half)), const((BRANCH_WIDTH, d)), const((d, BRANCH_WIDTH)), const((d, LANES)),
                  const((ML_CONV, 2 * half)), const((1, LANES))],
        out_specs=[row(half), row(half), pl.BlockSpec((None, BRANCH_WIDTH, tm), lambda i, j: (i, 0, j)), row(BRANCH_WIDTH), row(LANES)],
        out_shape=[jax.ShapeDtypeStruct((b, s, half), BF16), jax.ShapeDtypeStruct((b, s, half), BF16),
                   jax.ShapeDtypeStruct((b, BRANCH_WIDTH, s), BF16), jax.ShapeDtypeStruct((b, s, BRANCH_WIDTH), BF16),
                   jax.ShapeDtypeStruct((b, s, LANES), F32)],
        scratch_shapes=[pltpu.VMEM((tm + CONV_HALO, 2 * half), F32)],
        compiler_params=_cp(("parallel", "arbitrary")),
        name="ml_project",
    )(h, wqk_p, wv.T.astype(BF16), wo.astype(BF16), wg, conv_p, gb)


def _mlstm_kernel(q_ref, k_ref, vt_ref, og_ref, g_ref, gt_ref, ng_ref, o_ref, c_ref, n_ref, m_ref):
    nb, s = q_ref.shape[0], q_ref.shape[1]
    L = ML_CHUNK
    c_ref[...] = jnp.zeros_like(c_ref)
    n_ref[...] = jnp.zeros_like(n_ref)
    m_ref[...] = jnp.zeros_like(m_ref)
    tril = _tri_incl(L)
    triu = tril.T
    srow = lax.broadcasted_iota(I32, (L, L), 0)
    tcol = lax.broadcasted_iota(I32, (L, L), 1)
    causal = srow <= tcol
    lane = lax.broadcasted_iota(I32, (L, LANES), 1)
    head_lanes = (jnp.where(lane < ML_DQK, 1.0, 0.0).astype(BF16), jnp.where(lane >= ML_DQK, 1.0, 0.0).astype(BF16))

    def chunk(c, carry):
        for bi in range(nb):
            one_chunk(c, bi)
        return carry

    def one_chunk(c, bi):
        r0 = pl.multiple_of(c * L, L)
        g = g_ref[bi, pl.ds(r0, L), :]
        gt = gt_ref[bi, :, pl.ds(r0, L)]
        bc = _tri_dot(tril, g)
        br = _dot_tri(gt, triu)
        for hd in range(ML_HEADS):
            sl = slice(LANES * hd, LANES * (hd + 1))
            st = bi * ML_HEADS + hd
            ccol = g[:, hd:hd + 1] - bc[:, ML_HEADS + hd:ML_HEADS + hd + 1]
            brow = br[ML_HEADS + hd:ML_HEADS + hd + 1, :]
            irow = gt[hd:hd + 1, :]
            m_prev = m_ref[st][:, 0:1]
            log_d = jnp.where(causal, brow + ccol, -jnp.inf)
            log_inter = brow + m_prev
            m_t = jnp.maximum(jnp.max(log_d, axis=0, keepdims=True), log_inter)
            w_intra = jnp.exp(log_d - m_t)
            w_inter = jnp.exp(log_inter - m_t)
            pair = slice(LANES * (hd // 2), LANES * (hd // 2 + 1))
            qc = q_ref[bi, pl.ds(r0, L), pair] * head_lanes[hd % 2]
            kc = k_ref[bi, pl.ds(r0, L), pair]
            vt = vt_ref[bi, sl, pl.ds(r0, L)]
            sc = _dot_nt(kc, qc) * w_intra
            cst = c_ref[st]
            nst = n_ref[st]
            num = _dot(vt, sc.astype(BF16)) + w_inter * _dot_nt(cst.astype(BF16), qc)
            qn = _dot_nt(jnp.broadcast_to(nst, (SUBLANES, LANES)).astype(BF16), qc)[0:1, :]
            den = jnp.sum(sc, axis=0, keepdims=True) + w_inter * qn
            hout = num / jnp.maximum(jnp.abs(den), jnp.exp(-m_t))
            b_last = brow[:, L - 1:L]
            lw = b_last - brow + irow
            m_new = jnp.maximum(b_last + m_prev, jnp.max(lw, axis=1, keepdims=True))
            w_in = jnp.exp(lw - m_new)
            decay = jnp.exp(b_last + m_prev - m_new)
            c_ref[st] = decay * cst + _dot((vt.astype(F32) * w_in).astype(BF16), kc)
            n_ref[st] = decay * nst + _dot(jnp.broadcast_to(w_in, (SUBLANES, L)).astype(BF16), kc)[0:1, :]
            m_ref[st] = jnp.broadcast_to(m_new, (1, LANES))
            ms = jnp.mean(hout * hout, axis=0, keepdims=True)
            y = (hout * lax.rsqrt(ms + EPS)).T * (og_ref[bi, pl.ds(r0, L), sl].astype(F32) * ng_ref[:, sl])
            o_ref[bi, pl.ds(r0, L), sl] = y.astype(BF16)

    lax.fori_loop(0, s // L, chunk, 0)


ML_SEQS = 2


def mlstm(q, k, vt, og, gates, norm_g):
    b, w, s = vt.shape
    nb = ML_SEQS if b % ML_SEQS == 0 else 1
    gt = jnp.swapaxes(gates[:, :, :2 * ML_HEADS], 1, 2)
    seq = lambda n: pl.BlockSpec((nb, s, n), lambda i: (i, 0, 0))
    return pl.pallas_call(
        _mlstm_kernel,
        grid=(b // nb,),
        in_specs=[seq(ML_HEADS * ML_DQK), seq(ML_HEADS * ML_DQK), pl.BlockSpec((nb, w, s), lambda i: (i, 0, 0)), seq(w), seq(LANES),
                  pl.BlockSpec((nb, 2 * ML_HEADS, s), lambda i: (i, 0, 0)),
                  pl.BlockSpec((1, w), lambda i: (0, 0))],
        out_specs=seq(w),
        out_shape=jax.ShapeDtypeStruct((b, s, w), BF16),
        scratch_shapes=[pltpu.VMEM((nb * ML_HEADS, ML_DV, LANES), F32), pltpu.VMEM((nb * ML_HEADS, 1, LANES), F32),
                        pltpu.VMEM((nb * ML_HEADS, 1, LANES), F32)],
        compiler_params=_cp(("parallel",)),
        name="mlstm",
    )(q, k, vt, og, gates, gt, norm_g.reshape(1, w))


def _hg_proj_kernel(h_ref, wf_ref, wq_ref, wi_ref, wg_ref, lb_ref, q_out, k_out, v_out, og_out, lf_out):
    h = h_ref[...]
    fz = _dot(h, wf_ref[...])
    log_lb = lb_ref[0:1, :]
    log_1m = lb_ref[1:2, :]
    one_m = lb_ref[2:3, :]
    a = log_lb
    bb = log_1m + _log_sigmoid(fz)
    lf_out[...] = jnp.maximum(a, bb) + jnp.log1p(jnp.exp(-jnp.abs(a - bb)))
    k_out[...] = (one_m * jax.nn.sigmoid(-fz)).astype(BF16)
    q_out[...] = _silu(_dot(h, wq_ref[...])).astype(BF16)
    v_out[...] = _dot(h, wi_ref[...]).astype(BF16)
    og_out[...] = _silu(_dot(h, wg_ref[...])).astype(BF16)


def hg_project(h, wf, wq, wi, wg, lb):
    b, s, d = h.shape
    tm = min(PROJ_TILE, s)
    w = BRANCH_WIDTH
    lbp =jnp.stack([jnp.log(lb), jnp.log1p(-lb), 1.0 - lb], axis=0)
    const = lambda shape: pl.BlockSpec(shape, lambda i, j: (0,) * len(shape))
    row = lambda n: pl.BlockSpec((None, tm, n), lambda i, j: (i, j, 0))
    return pl.pallas_call(
        _hg_proj_kernel,
        grid=(b, s // tm),
        in_specs=[row(d), const((d, w)), const((d, w)), const((d, w)), const((d, w)), const((3, w))],
        out_specs=[row(w), row(w), row(w), row(w), row(w)],
        out_shape=[jax.ShapeDtypeStruct((b, s, w), BF16)] * 4 + [jax.ShapeDtypeStruct((b, s, w), F32)],
        compiler_params=_cp(("parallel", "parallel")),
        name="hg_project",
    )(h, wf.astype(BF16), wq.astype(BF16), wi.astype(BF16), wg.astype(BF16), lbp)


HG_LEVELS = tuple(HG_CHUNK >> (i + 1) for i in range(HG_CHUNK.bit_length() - 1))


def _hg_tables():
    L = HG_CHUNK
    t = np.arange(L)
    tri = (t[None, :] <= t[:, None]).astype(np.float32)
    mats = [tri]
    x = t[:, None] ^ t[None, :]
    lvl = np.full((L, L), -1, np.int32)
    lvl[t[:, None] == t[None, :]] = 0
    for i, m in enumerate(HG_LEVELS):
        if m < SUBLANES:
            mats.append(tri[(t // (2 * m)) * (2 * m) + m - 1])
        lvl[(t[:, None] > t[None, :]) & (x >= m) & (x < 2 * m)] = i + 1
    return jnp.asarray(np.concatenate(mats, axis=0), BF16), jnp.asarray(lvl)


def _hgrn_kernel(q_ref, k_ref, v_ref, og_ref, lf_ref, ng_ref, tall_ref, lvl_ref, o_ref, st_ref):
    s, wd = q_ref.shape
    L = HG_CHUNK
    st_ref[...] = jnp.zeros_like(st_ref)
    rowi = lax.broadcasted_iota(I32, (L, LANES), 0)

    lvl = lvl_ref[...]
    level_masks = [lvl == i for i in range(len(HG_LEVELS) + 1)]

    def chunk(c, carry):
        r0 = pl.multiple_of(c * L, L)
        tall = tall_ref[...]
        g = lf_ref[pl.ds(r0, L), :]
        hi = g.astype(BF16)
        mid = (g - hi.astype(F32)).astype(BF16)
        cums = (_dot(tall, hi) + _dot(tall, mid)) * LOG2E
        a = cums[0:L]
        qb = q_ref[pl.ds(r0, L), :]
        kb = k_ref[pl.ds(r0, L), :]
        qf = qb.astype(F32)
        kf = kb.astype(F32)
        ws = []
        fine = 0
        for m in HG_LEVELS:
            if m >= SUBLANES:
                ref = jnp.concatenate([jnp.broadcast_to(a[g0 + m - 1:g0 + m, :], (2 * m, wd)) for g0 in range(0, L, 2 * m)], axis=0)
            else:
                fine += 1
                ref = cums[L * fine:L * (fine + 1)]
            e = jnp.exp2(-jnp.abs(a - ref))
            upper = (rowi & m) != 0
            qk = jnp.concatenate([jnp.where(upper, qf[:, LANES * hd:LANES * (hd + 1)], kf[:, LANES * hd:LANES * (hd + 1)])
                                  for hd in range(HG_HEADS)], axis=1)
            ws.append((e * qk).astype(BF16))
        a_last = a[L - 1:L, :]
        qa = (qf * jnp.exp2(a)).astype(BF16)
        kt = (kf * jnp.exp2(a_last - a)).astype(BF16)
        decay = jnp.exp2(a_last)
        for hd in range(HG_HEADS):
            sl = slice(LANES * hd, LANES * (hd + 1))
            vb = v_ref[pl.ds(r0, L), sl]
            sc = jnp.where(level_masks[0], _dot_nt(qb[:, sl], kb[:, sl]), 0.0)
            for i in range(len(HG_LEVELS)):
                w = ws[i][:, sl]
                sc = jnp.where(level_masks[i + 1], _dot_nt(w, w), sc)
            st = st_ref[hd]
            out = _dot(sc.astype(BF16), vb) + _dot_nt(qa[:, sl], st.astype(BF16))
            st_ref[hd] = st * decay[:, sl] + _dot_tn(vb, kt[:, sl])
            ms = jnp.mean(out * out, axis=-1, keepdims=True)
            y = (out * lax.rsqrt(ms + EPS)) * ng_ref[:, sl] * og_ref[pl.ds(r0, L), sl].astype(F32)
            o_ref[pl.ds(r0, L), sl] = y.astype(BF16)
        return carry

    lax.fori_loop(0, s // L, chunk, 0)


def hgrn(q, k, v, og, logf, norm_g):
    b, s, w = v.shape
    tall, lvl = _hg_tables()
    seq = pl.BlockSpec((None, s, w), lambda i: (i, 0, 0))
    const = lambda shape: pl.BlockSpec(shape, lambda i: (0,) * len(shape))
    return pl.pallas_call(
        _hgrn_kernel,
        grid=(b,),
        in_specs=[seq, seq, seq, seq, seq, const((1, w)), const(tall.shape), const(lvl.shape)],
        out_specs=seq,
        out_shape=jax.ShapeDtypeStruct((b, s, w), BF16),
        scratch_shapes=[pltpu.VMEM((HG_HEADS, LANES, HG_DK), F32)],
        compiler_params=_cp(("parallel",)),
        name="hgrn",
    )(q, k, v, og, logf, norm_g.reshape(1, w), tall, lvl)


def _merge_kernel(x_ref, h_ref, yf_ref, ym_ref, yh_ref, wg_ref, wb_ref, wo_ref, g1_ref, n2_ref, sc2_ref, sh2_ref,
                  x_out, h2_out):
    d = x_ref.shape[1]
    h = h_ref[...]
    merged = None
    for br, y_ref in enumerate((yf_ref, ym_ref, yh_ref)):
        gate = jax.nn.sigmoid(_dot(h, wg_ref[:, d * br:d * (br + 1)]))
        term = gate * _dot(y_ref[...], wb_ref[br])
        merged = term if merged is None else merged + term
    mixed = _dot(merged.astype(BF16), wo_ref[...])
    x1 = x_ref[...] + g1_ref[...] * mixed
    x_out[...] = x1
    h2_out[...] = _norm_mod(x1, n2_ref[...], sc2_ref[...], sh2_ref[...]).astype(BF16)


def merge_branches(x, h, y_fox, y_ml, y_hg, w_gates, w_branch, w_out, g1, norm2_g, sc2, sh2):
    b, s, d = x.shape
    tm = ROW_TILE
    w = BRANCH_WIDTH
    const = lambda shape: pl.BlockSpec(shape, lambda i, j: (0,) * len(shape))
    row = lambda n: pl.BlockSpec((None, tm, n), lambda i, j: (i, j, 0))
    per_b = pl.BlockSpec((None, 1, d), lambda i, j: (i, 0, 0))
    return pl.pallas_call(
        _merge_kernel,
        grid=(b, s // tm),
        in_specs=[row(d), row(d), row(w), row(w), row(w), const((d, N_BRANCH * d)), const((N_BRANCH, w, d)), const((d, d)),
                  per_b, const((1, d)), per_b, per_b],
        out_specs=[row(d), row(d)],
        out_shape=[jax.ShapeDtypeStruct((b, s, d), F32), jax.ShapeDtypeStruct((b, s, d), BF16)],
        compiler_params=_cp(("parallel", "parallel"), 56),
        name="merge_branches",
    )(x, h, y_fox, y_ml, y_hg, w_gates.astype(BF16), w_branch.astype(BF16), w_out.astype(BF16),
      g1.reshape(b, 1, d), norm2_g.reshape(1, d), sc2.reshape(b, 1, d), sh2.reshape(b, 1, d))


def _first_max(x, iota, size):
    m = jnp.max(x, axis=0, keepdims=True)
    idx = jnp.min(jnp.where(x == m, iota, size), axis=0, keepdims=True)
    return m, idx


def _route_kernel(h_ref, wr_ref, rb_ref, dest_out, w_out, cnt_out):
    n = h_ref.shape[0]
    scores = jax.nn.sigmoid(_dot_nt(wr_ref[...], h_ref[...]))
    choice = scores + rb_ref[...]
    e_iota = lax.broadcasted_iota(I32, (N_EXPERTS, n), 0)
    c3 = choice.reshape(N_GROUPS, GROUP_SIZE, n)
    i3 = lax.broadcasted_iota(I32, (N_GROUPS, GROUP_SIZE, n), 1)
    m1 = jnp.max(c3, axis=1, keepdims=True)
    i1 = jnp.min(jnp.where(c3 == m1, i3, GROUP_SIZE), axis=1, keepdims=True)
    m2 = jnp.max(jnp.where(i3 == i1, -jnp.inf, c3), axis=1, keepdims=True)
    gs = (m1 + m2).reshape(N_GROUPS, n)
    g_iota = lax.broadcasted_iota(I32, (N_GROUPS, n), 0)
    gsel = jnp.zeros((N_GROUPS, n), F32)
    for _ in range(TOPK_GROUPS):
        _, gi = _first_max(gs, g_iota, N_GROUPS)
        hit = g_iota == gi
        gsel = jnp.where(hit, 1.0, gsel)
        gs = jnp.where(hit, -jnp.inf, gs)
    gmask = jnp.broadcast_to(gsel.reshape(N_GROUPS, 1, n), (N_GROUPS, GROUP_SIZE, n)).reshape(N_EXPERTS, n)
    masked = jnp.where(gmask > 0.0, choice, -jnp.inf)
    sel = jnp.zeros((N_EXPERTS, n), F32)
    for _ in range(TOP_K):
        _, ei = _first_max(masked, e_iota, N_EXPERTS)
        hit = e_iota == ei
        sel = jnp.where(hit, 1.0, sel)
        masked = jnp.where(hit, -jnp.inf, masked)
    tr = lax.broadcasted_iota(I32, (n, n), 0)
    tc = lax.broadcasted_iota(I32, (n, n), 1)
    before = jnp.where(tr < tc, 1.0, 0.0).astype(BF16)
    pos = _dot(sel.astype(BF16), before)
    cnt = jnp.sum(sel, axis=1, keepdims=True)
    units = jnp.floor((cnt + (RUN_ALIGN - 1)) * (1.0 / RUN_ALIGN))
    er = lax.broadcasted_iota(I32, (N_EXPERTS, N_EXPERTS), 0)
    ec = lax.broadcasted_iota(I32, (N_EXPERTS, N_EXPERTS), 1)
    lower = jnp.where(ec < er, 1.0, 0.0).astype(BF16)
    off = _dot(lower, jnp.broadcast_to(units, (N_EXPERTS, LANES)).astype(BF16))[:, 0:1] * RUN_ALIGN
    dest = jnp.where(sel > 0.0, off + pos, float(DEST_NONE))
    dhi = jnp.floor(dest * (1.0 / DEST_RADIX))
    dest_out[0:N_EXPERTS, :] = dhi.astype(BF16)
    dest_out[N_EXPERTS:, :] = (dest - dhi * DEST_RADIX).astype(BF16)
    wsum = jnp.sum(scores * sel, axis=0, keepdims=True)
    w_out[0:N_EXPERTS, :] = jnp.zeros((N_EXPERTS, n), BF16)
    w_out[N_EXPERTS:, :] = (scores * sel / wsum * ROUTE_SCALE).astype(BF16)
    cnt_out[...] = jnp.broadcast_to(cnt, (N_EXPERTS, LANES)).astype(I32)


def moe_route(h2, router_w, router_bias):
    t, d = h2.shape
    n = SUB_TOKENS
    nsub = t // n
    mat = pl.BlockSpec((None, 2 * N_EXPERTS, n), lambda i: (i, 0, 0))
    dest, wts, cnt = pl.pallas_call(
        _route_kernel,
        grid=(nsub,),
        in_specs=[pl.BlockSpec((n, d), lambda i: (i, 0)), pl.BlockSpec((N_EXPERTS, d), lambda i: (0, 0)),
                  pl.BlockSpec((N_EXPERTS, 1), lambda i: (0, 0))],
        out_specs=[mat, mat, pl.BlockSpec((None, N_EXPERTS, LANES), lambda i: (i, 0, 0))],
        out_shape=[jax.ShapeDtypeStruct((nsub, 2 * N_EXPERTS, n), BF16), jax.ShapeDtypeStruct((nsub, 2 * N_EXPERTS, n), BF16),
                   jax.ShapeDtypeStruct((nsub, N_EXPERTS, LANES), I32)],
        compiler_params=_cp(("parallel",)),
        name="moe_route",
    )(h2, router_w.T.astype(BF16), router_bias.reshape(N_EXPERTS, 1))
    return dest, wts, cnt[:, :, 0]


def _run_tables(cnt, n_blocks):
    units = (cnt + (RUN_ALIGN - 1)) // RUN_ALIGN
    src = jnp.cumsum(units, axis=1) - units
    per_block = MOE_BLOCK // RUN_ALIGN
    tot = jnp.sum(units, axis=0)
    tot_blocks = (tot + per_block - 1) // per_block
    blk_end = jnp.cumsum(tot_blocks)
    base = (blk_end - tot_blocks) * per_block
    dst = base[None, :] + jnp.cumsum(units, axis=0) - units
    n_used = blk_end[-1]
    tail_units = tot_blocks * per_block - tot
    tail_dst = base + tot
    blk = jnp.minimum(jnp.arange(n_blocks), n_used - 1)
    blk_exp = jnp.minimum(jnp.sum(blk[:, None] >= blk_end[None, :], axis=1), N_EXPERTS - 1)
    return (src.reshape(-1).astype(I32), dst.reshape(-1).astype(I32), units.reshape(-1).astype(I32),
            blk_exp.astype(I32), n_used.astype(I32).reshape(1), tail_dst.astype(I32), tail_units.astype(I32),
            jnp.sum(units, axis=1).astype(I32))


def _copy_lists(src, dst, units, nsub):
    s0 = src.reshape(nsub, 1, N_EXPERTS)
    d0 = dst.reshape(nsub, 1, N_EXPERTS)
    u = units.reshape(nsub, 1, N_EXPERTS)
    npair = u // 2
    poff = jnp.cumsum(npair, axis=2) - npair
    p = jnp.arange(SUB_UNITS // 2, dtype=I32).reshape(1, -1, 1)
    own = (p >= poff) & (p < poff + npair)
    psrc = jnp.sum(jnp.where(own, s0 - 2 * poff, 0), axis=2) + 2 * p[:, :, 0]
    pdst = jnp.sum(jnp.where(own, d0 - 2 * poff, 0), axis=2) + 2 * p[:, :, 0]
    odd = u & 1
    soff = jnp.cumsum(odd, axis=2) - odd
    q = jnp.arange(N_EXPERTS, dtype=I32).reshape(1, -1, 1)
    owns = (odd == 1) & (soff == q)
    ssrc = jnp.sum(jnp.where(owns, s0 + u - 1, 0), axis=2)
    sdst = jnp.sum(jnp.where(owns, d0 + u - 1, 0), axis=2)
    flat = lambda a: a.reshape(-1).astype(I32)
    return (flat(psrc), flat(pdst), flat(jnp.sum(npair, axis=2)), flat(ssrc), flat(sdst), flat(jnp.sum(odd, axis=2)))


def _run_bounds(src, units, nsub):
    lo = (src.reshape(nsub, N_EXPERTS) * RUN_ALIGN).astype(F32)
    hi = lo + (units.reshape(nsub, N_EXPERTS) * RUN_ALIGN).astype(F32)
    return jnp.stack([jnp.concatenate([lo, lo], axis=1), jnp.concatenate([hi, hi], axis=1)], axis=1)


def _max_blocks(t):
    nsub = t // SUB_TOKENS
    worst_units = t * TOP_K // RUN_ALIGN + nsub * N_EXPERTS
    per_block = MOE_BLOCK // RUN_ALIGN
    return -(-worst_units // per_block) + N_EXPERTS


SUB_UNITS = SUB_ROWS // RUN_ALIGN
COPY_UNROLL = 4


PAIR_SLOTS = SUB_UNITS // 2


def _piece_copies(lists, step, buf, hbm, sem, to_hbm):
    psrc_ref, pdst_ref, npair_ref, ssrc_ref, sdst_ref, nsingle_ref = lists

    def piece(src_ref, dst_ref, idx, units):
        rows = units * RUN_ALIGN
        v = buf.at[pl.ds(pl.multiple_of(src_ref[idx] * RUN_ALIGN, RUN_ALIGN), rows)]
        g = hbm.at[pl.ds(pl.multiple_of(dst_ref[idx] * RUN_ALIGN, RUN_ALIGN), rows)]
        cp = pltpu.make_async_copy(v, g, sem) if to_hbm else pltpu.make_async_copy(g, v, sem)
        cp.start()

    def issue(src_ref, dst_ref, base, count, units):
        def group(q, carry):
            for r in range(COPY_UNROLL):
                piece(src_ref, dst_ref, base + q * COPY_UNROLL + r, units)
            return carry

        groups = lax.shift_right_logical(count, COPY_UNROLL.bit_length() - 1)
        lax.fori_loop(0, groups, group, 0)
        lax.fori_loop(groups * COPY_UNROLL, count, lambda j, c: (piece(src_ref, dst_ref, base + j, units), c)[1], 0)

    issue(psrc_ref, pdst_ref, step * PAIR_SLOTS, npair_ref[step], 2)
    issue(ssrc_ref, sdst_ref, step * N_EXPERTS, nsingle_ref[step], 1)


TOTAL_BITS = tuple(1 << b for b in range(SUB_UNITS.bit_length()))


def _wait_runs(total_units, buf, hbm, sem, to_hbm):
    for bit in TOTAL_BITS:
        @pl.when((total_units & bit) != 0)
        def _():
            rows = bit * RUN_ALIGN
            v = buf.at[pl.ds(0, rows)]
            g = hbm.at[pl.ds(0, rows)]
            cp = pltpu.make_async_copy(v, g, sem) if to_hbm else pltpu.make_async_copy(g, v, sem)
            cp.wait()


def _dispatch_kernel(psrc_ref, pdst_ref, npair_ref, ssrc_ref, sdst_ref, nsingle_ref, tot_ref, tdst_ref, tunits_ref, nused_ref,
                     h_ref, dest_ref, lohi_ref, xs_out, buf_ref, zero_ref, sem):
    lists = (psrc_ref, pdst_ref, npair_ref, ssrc_ref, sdst_ref, nsingle_ref)
    i = pl.program_id(0)
    nsub = pl.num_programs(0)
    slot = i % 2
    n = h_ref.shape[0]
    chunk = SORT_CHUNK

    for sl in range(2):
        @pl.when((slot == sl) & (i >= 2))
        def _():
            _wait_runs(tot_ref[i - 2], buf_ref.at[sl], xs_out, sem.at[sl], True)

    h = h_ref[...]
    dest = dest_ref[...]
    lo = lohi_ref[0:1, :]
    hi = lohi_ref[1:2, :]
    radix = jnp.where(lax.broadcasted_iota(I32, (1, 2 * N_EXPERTS), 1) < N_EXPERTS, float(DEST_RADIX), 1.0)
    r_e = lax.broadcasted_iota(I32, (chunk, 2 * N_EXPERTS), 0).astype(F32)
    r_t = lax.broadcasted_iota(I32, (chunk, n), 0).astype(F32)
    for sl in range(2):
        @pl.when(slot == sl)
        def _():
            for c in range(SUB_ROWS // chunk):
                own = jnp.where(r_e + c * chunk >= lo, jnp.where(r_e + c * chunk < hi, radix, 0.0), 0.0)
                row_of = _dot(own.astype(BF16), dest)
                p = jnp.where(row_of == r_t + c * chunk, 1.0, 0.0)
                buf_ref[sl, c * chunk:(c + 1) * chunk, :] = _dot(p.astype(BF16), h).astype(BF16)
            _piece_copies(lists, i, buf_ref.at[sl], xs_out, sem.at[sl], True)

    @pl.when(i == nsub - 1)
    def _():
        zero_ref[...] = jnp.zeros_like(zero_ref)

        def tails(e, wait):
            u = tunits_ref[e]
            d0 = tdst_ref[e]
            for bit in RUN_BITS:
                low = u & (bit - 1)

                @pl.when((u & bit) != 0)
                def _():
                    rows = bit * RUN_ALIGN
                    cp = pltpu.make_async_copy(zero_ref.at[pl.ds(0, rows)],
                                               xs_out.at[pl.ds(pl.multiple_of((d0 + low) * RUN_ALIGN, RUN_ALIGN), rows)], sem.at[2])
                    if wait:
                        cp.wait()
                    else:
                        cp.start()
            return wait

        def unused(b, wait):
            cp = pltpu.make_async_copy(zero_ref.at[pl.ds(0, MOE_BLOCK)],
                                       xs_out.at[pl.ds(pl.multiple_of(b * MOE_BLOCK, MOE_BLOCK), MOE_BLOCK)], sem.at[2])
            if wait:
                cp.wait()
            else:
                cp.start()
            return wait

        n_blocks = xs_out.shape[0] // MOE_BLOCK
        lax.fori_loop(0, N_EXPERTS, lambda e, c: (tails(e, False), c)[1], 0)
        lax.fori_loop(nused_ref[0], n_blocks, lambda b, c: (unused(b, False), c)[1], 0)
        for sl in range(2):
            @pl.when((slot != sl) & (i >= 1))
            def _():
                _wait_runs(tot_ref[i - 1], buf_ref.at[sl], xs_out, sem.at[sl], True)

            @pl.when(slot == sl)
            def _():
                _wait_runs(tot_ref[i], buf_ref.at[sl], xs_out, sem.at[sl], True)
        lax.fori_loop(0, N_EXPERTS, lambda e, c: (tails(e, True), c)[1], 0)
        lax.fori_loop(nused_ref[0], n_blocks, lambda b, c: (unused(b, True), c)[1], 0)


def moe_dispatch(h2, dest, tables, n_rows):
    t, d = h2.shape
    n = SUB_TOKENS
    nsub = t // n
    src, dst, units, _, n_used, tail_dst, tail_units, tot = tables
    grid_spec = pltpu.PrefetchScalarGridSpec(
        num_scalar_prefetch=10,
        grid=(nsub,),
        in_specs=[pl.BlockSpec((n, d), lambda i, *_: (i, 0)), pl.BlockSpec((None, 2 * N_EXPERTS, n), lambda i, *_: (i, 0, 0)),
                  pl.BlockSpec((None, 2, 2 * N_EXPERTS), lambda i, *_: (i, 0, 0))],
        out_specs=pl.BlockSpec(memory_space=pl.ANY),
        scratch_shapes=[pltpu.VMEM((2, SUB_ROWS, d), BF16), pltpu.VMEM((max(SUB_TOKENS, MOE_BLOCK), d), BF16), pltpu.SemaphoreType.DMA((3,))],
    )
    return pl.pallas_call(
        _dispatch_kernel,
        grid_spec=grid_spec,
        out_shape=jax.ShapeDtypeStruct((n_rows, d), BF16),
        compiler_params=_cp(("arbitrary",)),
        name="moe_dispatch",
    )(*_copy_lists(src, dst, units, nsub), tot, tail_dst, tail_units, n_used, h2, dest, _run_bounds(src, units, nsub))


X_SLOTS = 3


def _expert_kernel(blk_exp_ref, n_used_ref, x_hbm, w1_ref, w3_ref, w2_ref, y_ref, xbuf, sem):
    b = pl.program_id(0)
    n_used = n_used_ref[0]

    def x_copy(blk, slot):
        rows = pl.ds(pl.multiple_of(blk * MOE_BLOCK, MOE_BLOCK), MOE_BLOCK)
        return pltpu.make_async_copy(x_hbm.at[rows], xbuf.at[slot], sem.at[slot])

    @pl.when(b == 0)
    def _():
        x_copy(0, 0).start()

        @pl.when(n_used > 1)
        def _():
            x_copy(1, 1).start()

    ahead = b + (X_SLOTS - 1)

    @pl.when(ahead < n_used)
    def _():
        x_copy(ahead, lax.rem(ahead, X_SLOTS)).start()

    @pl.when(b < n_used)
    def _():
        slot = lax.rem(b, X_SLOTS)
        x_copy(b, slot).wait()
        x = xbuf[slot]
        hid = _silu(_dot(x, w1_ref[...].astype(BF16))) * _dot(x, w3_ref[...].astype(BF16))
        y_ref[...] = _dot(hid.astype(BF16), w2_ref[...].astype(BF16)).astype(BF16)

    @pl.when(b >= n_used)
    def _():
        y_ref[...] = jnp.zeros_like(y_ref)


def moe_experts(xs, w1, w3, w2, layer, tables, n_blocks):
    n_rows, d = xs.shape
    blk_exp, n_used = tables[3], tables[4]
    f = w1.shape[-1]

    def w_map(b, be, nu):
        return (layer, be[b], 0, 0)

    grid_spec = pltpu.PrefetchScalarGridSpec(
        num_scalar_prefetch=2,
        grid=(n_blocks,),
        in_specs=[pl.BlockSpec(memory_space=pl.ANY), pl.BlockSpec((None, None, d, f), w_map),
                  pl.BlockSpec((None, None, d, f), w_map), pl.BlockSpec((None, None, f, d), w_map)],
        out_specs=pl.BlockSpec((MOE_BLOCK, d), lambda b, be, nu: (b, 0)),
        scratch_shapes=[pltpu.VMEM((X_SLOTS, MOE_BLOCK, d), BF16), pltpu.SemaphoreType.DMA((X_SLOTS,))],
    )
    return pl.pallas_call(
        _expert_kernel,
        grid_spec=grid_spec,
        out_shape=jax.ShapeDtypeStruct((n_rows, d), BF16),
        compiler_params=_cp(("arbitrary",)),
        name="moe_experts",
    )(blk_exp, n_used, xs, w1, w3, w2)


def _combine_kernel(psrc_ref, pdst_ref, npair_ref, ssrc_ref, sdst_ref, nsingle_ref, tot_ref, ys_ref, dcol_ref, wcol_ref, lohi_ref,
                    h_ref, x_ref, g2_ref, ws1_ref, ws3_ref, ws2_ref, o_ref, buf_ref, sem):
    lists = (psrc_ref, pdst_ref, npair_ref, ssrc_ref, sdst_ref, nsingle_ref)
    i = pl.program_id(0)
    nsub = pl.num_programs(0)
    slot = i % 2
    n = h_ref.shape[0]
    chunk = 512

    def fetch(step, sl):
        always = SUB_TOKENS * TOP_K
        buf_ref[sl, always:, :] = jnp.zeros((SUB_ROWS - always, buf_ref.shape[2]), BF16)
        _piece_copies(lists, step, buf_ref.at[sl], ys_ref, sem.at[sl], False)

    @pl.when(i == 0)
    def _():
        fetch(0, 0)

    for sl in range(2):
        @pl.when((slot != sl) & (i + 1 < nsub))
        def _():
            fetch(i + 1, sl)

    h = h_ref[...]
    shared = _dot((_silu(_dot(h, ws1_ref[...])) * _dot(h, ws3_ref[...])).astype(BF16), ws2_ref[...])
    dest_t = dcol_ref[...]
    w_t = wcol_ref[...]
    lo = lohi_ref[:, 0:1]
    hi = lohi_ref[:, 1:2]
    radix = jnp.where(lax.broadcasted_iota(I32, (2 * N_EXPERTS, 1), 0) < N_EXPERTS, float(DEST_RADIX), 1.0)
    r_e = lax.broadcasted_iota(I32, (2 * N_EXPERTS, chunk), 1).astype(F32)
    r_t = lax.broadcasted_iota(I32, (n, chunk), 1).astype(F32)
    for sl in range(2):
        @pl.when(slot == sl)
        def _():
            _wait_runs(tot_ref[i], buf_ref.at[sl], ys_ref, sem.at[sl], False)
            acc = shared
            for c in range(SUB_ROWS // chunk):
                own = jnp.where(r_e + c * chunk >= lo, jnp.where(r_e + c * chunk < hi, radix, 0.0), 0.0).astype(BF16)
                row_of = _dot(dest_t, own)
                pw = jnp.where(row_of == r_t + c * chunk, _dot(w_t, own), 0.0)
                acc = acc + _dot(pw.astype(BF16), buf_ref[sl, c * chunk:(c + 1) * chunk, :])
            o_ref[...] = x_ref[...] + g2_ref[...] * acc


def moe_combine(ys, dest, wts, h2, x1, g2, ws1, ws3, ws2, tables, seq):
    t, d = h2.shape
    n = SUB_TOKENS
    nsub = t // n
    src, dst, units, tot = tables[0], tables[1], tables[2], tables[7]
    dcol = jnp.swapaxes(dest, 1, 2)
    wcol = jnp.swapaxes(wts, 1, 2)
    bounds = jnp.swapaxes(_run_bounds(src, units, nsub), 1, 2)
    per_seq = seq // n
    f = ws1.shape[-1]
    pair = pl.BlockSpec((None, n, 2 * N_EXPERTS), lambda i, *_: (i, 0, 0))
    grid_spec = pltpu.PrefetchScalarGridSpec(
        num_scalar_prefetch=7,
        grid=(nsub,),
        in_specs=[pl.BlockSpec(memory_space=pl.ANY), pair, pair,
                  pl.BlockSpec((None, 2 * N_EXPERTS, 2), lambda i, *_: (i, 0, 0)),
                  pl.BlockSpec((n, d), lambda i, *_: (i, 0)), pl.BlockSpec((n, d), lambda i, *_: (i, 0)),
                  pl.BlockSpec((None, 1, d), lambda i, *_: (i // per_seq, 0, 0)),
                  pl.BlockSpec((d, f), lambda i, *_: (0, 0)), pl.BlockSpec((d, f), lambda i, *_: (0, 0)),
                  pl.BlockSpec((f, d), lambda i, *_: (0, 0))],
        out_specs=pl.BlockSpec((n, d), lambda i, *_: (i, 0)),
        scratch_shapes=[pltpu.VMEM((2, SUB_ROWS, d), BF16), pltpu.SemaphoreType.DMA((2,))],
    )
    return pl.pallas_call(
        _combine_kernel,
        grid_spec=grid_spec,
        out_shape=jax.ShapeDtypeStruct((t, d), F32),
        compiler_params=_cp(("arbitrary",)),
        name="moe_combine",
    )(*_copy_lists(src, dst, units, nsub), tot, ys, dcol, wcol, bounds, h2, x1, g2, ws1.astype(BF16), ws3.astype(BF16), ws2.astype(BF16))


def _split_w_in(w):
    fh = FOX_HEADS * FOX_HEAD_DIM
    sizes = (fh, fh, fh, FOX_HEADS,
             2 * ML_HEADS * ML_DQK, ML_HEADS * ML_DV, ML_HEADS, ML_HEADS, ML_HEADS * ML_DV,
             HG_HEADS * HG_DK, HG_HEADS * HG_DK, BRANCH_WIDTH, BRANCH_WIDTH,
             N_BRANCH * D_MODEL)
    outs, o = [], 0
    for sz in sizes:
        outs.append(w[:, o:o + sz])
        o += sz
    return outs


def moe_ffn(h2, x1, g2, router_w, router_bias, w1, w3, w2, layer, ws1, ws3, ws2, seq):
    t, d = h2.shape
    dest, wts, cnt = moe_route(h2, router_w, router_bias)
    n_blocks = _max_blocks(t)
    tables = _run_tables(cnt, n_blocks)
    xs = moe_dispatch(h2, dest, tables, n_blocks * MOE_BLOCK)
    ys = moe_experts(xs, w1, w3, w2, layer, tables, n_blocks)
    return moe_combine(ys, dest, wts, h2, x1, g2, ws1, ws3, ws2, tables, seq)


def kernel(x, c, ada_w, ada_b, norm1_g, norm2_g, w_in, fox_bf, fox_q_g, fox_k_g, mlstm_conv, mlstm_bi, mlstm_bf, mlstm_norm_g, hgrn_lower_bounds, hgrn_norm_g, w_branch, w_out, router_w, router_bias, exp_w1, exp_w3, exp_w2, sh_w1, sh_w3, sh_w2):
    b, s, d = x.shape
    depth = ada_w.shape[0]
    mod = adaln_mod(c, ada_w, ada_b)
    lb_all = jnp.cumsum(jax.nn.softmax(hgrn_lower_bounds.astype(F32), axis=0), axis=0)
    lb_all = lb_all - lb_all[0]
    for l in range(depth):
        sh1, sc1, g1, sh2, sc2, g2 = [mod[l][:, d * j:d * (j + 1)] for j in range(6)]
        (wfq, wfk, wfv, wff, wmqk, wmv, wmi, wmf, wmo, whf, whq, whi, whg, wgates) = _split_w_in(w_in[l])
        h = norm_modulate(x, norm1_g[l], sc1, sh1)
        qp, kp, fv = fox_project(h, wfq, wfk, wfv, wff, fox_q_g[l], fox_k_g[l], fox_bf[l])
        y_fox = fox_attention(qp, kp, fv)
        mq, mk, mv, mog, mgates = ml_project(h, wmqk, wmv, wmi, wmf, wmo, mlstm_conv[l], mlstm_bi[l], mlstm_bf[l])
        y_ml = mlstm(mq, mk, mv, mog, mgates, mlstm_norm_g[l])
        hq, hk, hv, hog, hlf = hg_project(h, whf, whq, whi, whg, lb_all[l])
        y_hg = hgrn(hq, hk, hv, hog, hlf, hgrn_norm_g[l])
        x1, h2 = merge_branches(x, h, y_fox, y_ml, y_hg, wgates, w_branch[l], w_out[l], g1, norm2_g[l], sc2, sh2)
        x = moe_ffn(h2.reshape(b * s, d), x1.reshape(b * s, d), g2.reshape(b, 1, d), router_w[l], router_bias[l],
                    exp_w1, exp_w3, exp_w2, l, sh_w1[l], sh_w3[l], sh_w2[l], s).reshape(b, s, d)
    return x
```

```python
import functools

import jax
import jax.numpy as jnp
import numpy as np
from jax import lax
from jax.experimental import pallas as pl
from jax.experimental.pallas import tpu as pltpu

F32 = jnp.float32
BF16 = jnp.bfloat16
I32 = jnp.int32

LANES = 128
SUBLANES = 8
BF16_ROWS = 16
VMEM_BYTES = 64 * 1024 * 1024

D_MODEL = 1024
BRANCH_WIDTH = D_MODEL // 2
N_BRANCH = 3
FOX_HEAD_DIM = 64
FOX_HEADS = BRANCH_WIDTH // FOX_HEAD_DIM
ML_HEADS = 4
ML_DV = BRANCH_WIDTH // ML_HEADS
ML_DQK = ML_DV // 2
ML_CONV = 4
ML_CHUNK = 128
HG_HEADS = 4
HG_DK = 128
HG_CHUNK = 128
N_EXPERTS = 64
N_GROUPS = 8
GROUP_SIZE = N_EXPERTS // N_GROUPS
TOPK_GROUPS = 4
TOP_K = 8
D_FF = D_MODEL // 4
ROUTE_SCALE = 2.5
MOE_BLOCK = 1024
SORT_CHUNK = 1024
EPS = 1e-6
NEG = -1e30
LOG2E = 1.4426950408889634

ROW_TILE = 512
PROJ_TILE = 1024
ATTN_TILE = 256
SUB_TOKENS = 512
RUN_ALIGN = BF16_ROWS
SUB_ROWS = ((SUB_TOKENS * TOP_K + N_EXPERTS * (RUN_ALIGN - 1)) + 255) // 256 * 256
RUN_BITS = tuple(1 << b for b in range((SUB_TOKENS // RUN_ALIGN).bit_length()))
DEST_RADIX = 64
DEST_NONE = DEST_RADIX * 127
WEIGHT_SHIFT = 0.25
assert ROUTE_SCALE * WEIGHT_SHIFT < 1.0
assert SUB_ROWS <= DEST_NONE


def _cp(sem, vmem_mb=48):
    return pltpu.CompilerParams(dimension_semantics=sem, vmem_limit_bytes=vmem_mb * 1024 * 1024)


def _dot(a, b):
    return jnp.dot(a, b, preferred_element_type=F32)


def _dot_nt(a, b):
    return lax.dot_general(a, b, (((1,), (1,)), ((), ())), preferred_element_type=F32)


def _dot_tn(a, b):
    return lax.dot_general(a, b, (((0,), (0,)), ((), ())), preferred_element_type=F32)


def _split3(x):
    hi = x.astype(BF16)
    r = x - hi.astype(F32)
    mid = r.astype(BF16)
    lo = (r - mid.astype(F32)).astype(BF16)
    return hi, mid, lo


def _tri_dot(tri, x):
    hi, mid, lo = _split3(x)
    return (_dot(tri, hi) + _dot(tri, mid)) + _dot(tri, lo)


def _dot_tri(x, tri):
    hi, mid, lo = _split3(x)
    return (_dot(hi, tri) + _dot(mid, tri)) + _dot(lo, tri)


def _log_sigmoid(x):
    return jnp.minimum(x, 0.0) - jnp.log1p(jnp.exp(-jnp.abs(x)))


def _silu(x):
    return x * jax.nn.sigmoid(x)


def _tri_incl(n, dtype=BF16):
    r = lax.broadcasted_iota(I32, (n, n), 0)
    c = lax.broadcasted_iota(I32, (n, n), 1)
    return jnp.where(c <= r, 1.0, 0.0).astype(dtype)


def _mod_kernel(c_ref, w_ref, b_ref, o_ref):
    cond = _silu(c_ref[...])
    hi, mid, lo = _split3(cond)
    w = w_ref[...]
    whi, wmid, wlo = _split3(w)
    acc = _dot(hi, whi) + (_dot(hi, wmid) + _dot(mid, whi))
    acc = acc + (_dot(mid, wmid) + _dot(hi, wlo) + _dot(lo, whi))
    o_ref[...] = acc + b_ref[...]


def adaln_mod(c, ada_w, ada_b):
    depth, d, n = ada_w.shape
    b = c.shape[0]
    tn = 1024
    return pl.pallas_call(
        _mod_kernel,
        grid=(depth, n // tn),
        in_specs=[
            pl.BlockSpec((b, d), lambda l, j: (0, 0)),
            pl.BlockSpec((None, d, tn), lambda l, j: (l, 0, j)),
            pl.BlockSpec((None, 1, tn), lambda l, j: (l, 0, j)),
        ],
        out_specs=pl.BlockSpec((None, b, tn), lambda l, j: (l, 0, j)),
        out_shape=jax.ShapeDtypeStruct((depth, b, n), F32),
        compiler_params=_cp(("parallel", "parallel")),
        name="adaln_mod",
    )(c, ada_w, ada_b.reshape(depth, 1, n))


def _norm_mod(x, g, sc, sh):
    ms = jnp.mean(x * x, axis=-1, keepdims=True)
    return x * lax.rsqrt(ms + EPS) * g * (1.0 + sc) + sh


FOX_BIAS_PIECES = 3
FOX_SLOT = 2 * LANES
FOX_BIAS_STRIDE = SUBLANES


def _pack_pieces(x):
    hi, mid, lo = _split3(x)
    p = hi.astype(F32) + pltpu.roll(mid.astype(F32), FOX_HEADS, axis=1) + pltpu.roll(lo.astype(F32), 2 * FOX_HEADS, axis=1)
    return p.astype(BF16)


def _fox_proj_kernel(x_ref, ng_ref, sc_ref, sh_ref, wq_ref, wk_ref, wvt_ref, wf_ref, gq_ref, gk_ref, bf_ref, eq_ref, ek_ref,
                     cq_ref, ck_ref, q_out, k_out, vt_out, h_out, carry_ref):
    @pl.when(pl.program_id(1) == 0)
    def _():
        carry_ref[...] = jnp.zeros_like(carry_ref)

    h = _norm_mod(x_ref[...], ng_ref[...], sc_ref[...], sh_ref[...]).astype(BF16)
    h_out[...] = h
    tm = h.shape[0]
    pr = lax.broadcasted_iota(I32, (LANES, LANES), 0)
    pc = lax.broadcasted_iota(I32, (LANES, LANES), 1)
    avg_pair = jnp.where((pr < FOX_HEAD_DIM) == (pc < FOX_HEAD_DIM), 1.0 / FOX_HEAD_DIM, 0.0).astype(BF16)

    def head_norm(x):
        outs = []
        for pair in range(FOX_HEADS // 2):
            xp = x[:, LANES * pair:LANES * (pair + 1)]
            ms = _dot((xp * xp).astype(BF16), avg_pair)
            outs.append(xp * lax.rsqrt(ms + EPS))
        return outs

    def slots(normed, g_ref, bias):
        parts = []
        for pair in range(FOX_HEADS // 2):
            lanes = slice(LANES * pair, LANES * (pair + 1))
            parts += [normed[pair] * g_ref[:, lanes], bias[:, lanes]]
        return jnp.concatenate(parts, axis=1).astype(BF16)

    lane = lax.broadcasted_iota(I32, (tm, LANES), 1)
    logf = jnp.where(lane < FOX_HEADS, _log_sigmoid(_dot(h, wf_ref[...]) + bf_ref[...]), 0.0)
    cs = _dot(_tri_incl(tm), _pack_pieces(logf))
    cum = cs + pltpu.roll(cs, LANES - FOX_HEADS, axis=1) + pltpu.roll(cs, LANES - 2 * FOX_HEADS, axis=1)
    cum = jnp.where(lane < FOX_HEADS, cum, 0.0) + carry_ref[...]
    carry_ref[...] = cum[tm - 1:tm, :]
    pieces = _pack_pieces(cum * LOG2E)

    q_out[...] = slots(head_norm(_dot(h, wq_ref[...])), gq_ref, _dot(pieces, eq_ref[...]) + cq_ref[...])
    k_out[...] = slots(head_norm(_dot(h, wk_ref[...])), gk_ref, _dot(pieces, ek_ref[...]) + ck_ref[...])
    vt = _dot_nt(wvt_ref[...], h)
    ones = jnp.ones((FOX_HEAD_DIM, tm), F32)
    slots = []
    for hd in range(FOX_HEADS):
        slots += [vt[FOX_HEAD_DIM * hd:FOX_HEAD_DIM * (hd + 1), :], ones]
    vt_out[...] = jnp.concatenate(slots, axis=0).astype(BF16)


def _pad_heads(w, heads, dim):
    lead = w.shape[:-1]
    w = w.reshape(*lead, heads, dim)
    w = jnp.pad(w, [(0, 0)] * len(lead) + [(0, 0), (0, LANES - dim)])
    return w.reshape(*lead, heads * LANES)


def _fox_constants():
    width = FOX_HEADS // 2 * LANES
    eq = np.zeros((LANES, width), np.float32)
    ek = np.zeros((LANES, width), np.float32)
    cq = np.zeros((1, width), np.float32)
    ck = np.zeros((1, width), np.float32)
    for hd in range(FOX_HEADS):
        base = LANES * (hd // 2) + FOX_BIAS_STRIDE * (hd % 2)
        for p in range(FOX_BIAS_PIECES):
            eq[p * FOX_HEADS + hd, base + p] = 1.0
            cq[0, base + FOX_BIAS_PIECES + p] = 1.0
            ck[0, base + p] = 1.0
            ek[p * FOX_HEADS + hd, base + FOX_BIAS_PIECES + p] = -1.0
    return jnp.asarray(eq, BF16), jnp.asarray(ek, BF16), jnp.asarray(cq), jnp.asarray(ck)


def _fox_head_masks():
    m = np.zeros((2, FOX_SLOT), np.float32)
    for a in range(2):
        m[a, FOX_HEAD_DIM * a:FOX_HEAD_DIM * (a + 1)] = 1.0
        m[a, LANES + FOX_BIAS_STRIDE * a:LANES + FOX_BIAS_STRIDE * a + 2 * FOX_BIAS_PIECES] = 1.0
    return jnp.asarray(m, BF16)


def fox_project(x, norm_g, sc, sh, wq, wk, wv, wf, q_g, k_g, bf):
    b, s, d = x.shape
    tm = min(PROJ_TILE, s)
    hp = FOX_HEADS // 2 * FOX_SLOT
    hw = FOX_HEADS * FOX_HEAD_DIM
    wq_p = wq.astype(BF16)
    wk_p = wk.astype(BF16)
    wf_p = jnp.pad(wf, ((0, 0), (0, LANES - FOX_HEADS))).astype(BF16)
    gq = jnp.tile(q_g * (FOX_HEAD_DIM ** -0.5 * LOG2E), FOX_HEADS)[None, :]
    gk = jnp.tile(k_g, FOX_HEADS)[None, :]
    bf_p = jnp.pad(bf, (0, LANES - FOX_HEADS))[None, :]
    eq, ek, cq, ck = _fox_constants()
    const = lambda shape: pl.BlockSpec(shape, lambda i, j: (0,) * len(shape))
    row = lambda n: pl.BlockSpec((None, tm, n), lambda i, j: (i, j, 0))
    per_b = pl.BlockSpec((None, 1, d), lambda i, j: (i, 0, 0))
    return pl.pallas_call(
        _fox_proj_kernel,
        grid=(b, s // tm),
        in_specs=[row(d), const((1, d)), per_b, per_b, const((d, hw)), const((d, hw)), const((BRANCH_WIDTH, d)), const((d, LANES)),
                  const((1, hw)), const((1, hw)), const((1, LANES)), const((LANES, hw)), const((LANES, hw)),
                  const((1, hw)), const((1, hw))],
        out_specs=[row(hp), row(hp), pl.BlockSpec((None, hp, tm), lambda i, j: (i, 0, j)), row(d)],
        out_shape=[jax.ShapeDtypeStruct((b, s, hp), BF16), jax.ShapeDtypeStruct((b, s, hp), BF16),
                   jax.ShapeDtypeStruct((b, hp, s), BF16), jax.ShapeDtypeStruct((b, s, d), BF16)],
        scratch_shapes=[pltpu.VMEM((1, LANES), F32)],
        compiler_params=_cp(("parallel", "arbitrary")),
        name="fox_project",
    )(x, norm_g.reshape(1, d), sc.reshape(b, 1, d), sh.reshape(b, 1, d), wq_p, wk_p, wv.T.astype(BF16), wf_p, gq, gk, bf_p,
      eq, ek, cq, ck)


ATTN_HEADS = 8


def _fox_attn_kernel(q_ref, k_ref, vt_ref, hm_ref, o_ref):
    i = pl.program_id(2)
    t = q_ref.shape[0]
    krow = lax.broadcasted_iota(I32, (t, t), 0)
    qcol = lax.broadcasted_iota(I32, (t, t), 1)
    slot = lambda a: slice(FOX_SLOT * (a // 2), FOX_SLOT * (a // 2 + 1))
    qs = [q_ref[:, slot(a)] * hm_ref[a % 2:a % 2 + 1, :] for a in range(ATTN_HEADS)]

    def scores(j):
        start = pl.multiple_of(j * t, t)
        return tuple(_dot_nt(k_ref[pl.ds(start, t), slot(a)], qs[a]) for a in range(ATTN_HEADS))

    def consume(j, state, ss, masked):
        start = pl.multiple_of(j * t, t)
        new = []
        for a in range(ATTN_HEADS):
            m, acc = state[a]
            s = jnp.where(krow <= qcol, ss[a], NEG) if masked else ss[a]
            m_new = jnp.maximum(m, jnp.max(s, axis=0, keepdims=True))
            p = jnp.exp2(s - m_new)
            alpha = jnp.exp2(m - m_new)
            vt = vt_ref[LANES * a:LANES * (a + 1), pl.ds(start, t)]
            acc = alpha * acc + _dot(vt, p.astype(BF16))
            new.append((m_new, acc))
        return tuple(new)

    def body(j, state):
        return consume(j, state, scores(j), False)

    init = tuple((jnp.full((1, t), NEG, F32), jnp.zeros((LANES, t), F32)) for _ in range(ATTN_HEADS))
    state = lax.fori_loop(0, i, body, init)
    state = consume(i, state, scores(i), True)
    for p in range(ATTN_HEADS // 2):
        halves = []
        for _, acc in (state[2 * p], state[2 * p + 1]):
            halves.append(acc[:FOX_HEAD_DIM, :] / acc[FOX_HEAD_DIM:FOX_HEAD_DIM + 1, :])
        o_ref[:, LANES * p:LANES * (p + 1)] = jnp.concatenate(halves, axis=0).T.astype(BF16)


def fox_attention(qp, kp, vt):
    b, s, hp = qp.shape
    t = ATTN_TILE
    groups = FOX_HEADS // ATTN_HEADS
    return pl.pallas_call(
        _fox_attn_kernel,
        grid=(b, groups, s // t),
        in_specs=[
            pl.BlockSpec((None, t, ATTN_HEADS * LANES), lambda bi, p, i: (bi, i, p)),
            pl.BlockSpec((None, s, ATTN_HEADS * LANES), lambda bi, p, i: (bi, 0, p)),
            pl.BlockSpec((None, ATTN_HEADS * LANES, s), lambda bi, p, i: (bi, p, 0)),
            pl.BlockSpec((2, FOX_SLOT), lambda bi, p, i: (0, 0)),
        ],
        out_specs=pl.BlockSpec((None, t, ATTN_HEADS // 2 * LANES), lambda bi, p, i: (bi, i, p)),
        out_shape=jax.ShapeDtypeStruct((b, s, BRANCH_WIDTH), BF16),
        compiler_params=_cp(("parallel", "parallel", "arbitrary")),
        name="fox_attention",
    )(qp, kp, vt, _fox_head_masks())


CONV_HALO = SUBLANES


def _ml_proj_kernel(h_ref, wqk_ref, wvt_ref, wo_ref, wg_ref, conv_ref, gb_ref, q_out, k_out, vt_out, og_out, g_out, buf_ref):
    tm = h_ref.shape[0]
    half = ML_HEADS * ML_DQK

    @pl.when(pl.program_id(1) == 0)
    def _():
        buf_ref[0:CONV_HALO, :] = jnp.zeros((CONV_HALO, 2 * half), F32)

    h = h_ref[...]
    buf_ref[CONV_HALO:CONV_HALO + tm, :] = _dot(h, wqk_ref[...])
    acc = jnp.zeros((tm, 2 * half), F32)
    for j in range(ML_CONV):
        off = CONV_HALO - (ML_CONV - 1) + j
        acc = acc + conv_ref[j:j + 1, :] * buf_ref[off:off + tm, :]
    buf_ref[0:CONV_HALO, :] = buf_ref[tm:tm + CONV_HALO, :]
    act = _silu(acc)
    q_out[...] = act[:, :half].astype(BF16)
    k_out[...] = (act[:, half:] * (ML_DQK ** -0.5)).astype(BF16)
    vt_out[...] = _dot_nt(wvt_ref[...], h).astype(BF16)
    og_out[...] = jax.nn.sigmoid(_dot(h, wo_ref[...])).astype(BF16)
    g = _dot(h, wg_ref[...]) + gb_ref[...]
    lane = lax.broadcasted_iota(I32, (tm, LANES), 1)
    g_out[...] = jnp.where(lane < ML_HEADS, g, _log_sigmoid(g))


def ml_project(h, wqk, wv, wi, wf, wo, conv, bi, bf):
    b, s, d = h.shape
    tm = min(PROJ_TILE, s)
    half = ML_HEADS * ML_DQK
    wqk_p = wqk.astype(BF16)
    conv_p = conv
    wg = jnp.pad(jnp.concatenate([wi, wf], axis=1), ((0, 0), (0, LANES - 2 * ML_HEADS))).astype(BF16)
    gb = jnp.pad(jnp.concatenate([bi, bf]), (0, LANES - 2 * ML_HEADS))[None, :]
    const = lambda shape: pl.BlockSpec(shape, lambda i, j: (0,) * len(shape))
    row = lambda n: pl.BlockSpec((None, tm, n), lambda i, j: (i, j, 0))
    return pl.pallas_call(
        _ml_proj_kernel,
        grid=(b, s // tm),
        in_specs=[row(d), const((d, 2 * half)), const((BRANCH_WIDTH, d)), const((d, BRANCH_WIDTH)), const((d, LANES)),
                  const((ML_CONV, 2 * half)), const((1, LANES))],
        out_specs=[row(half), row(half), pl.BlockSpec((None, BRANCH_WIDTH, tm), lambda i, j: (i, 0, j)), row(BRANCH_WIDTH), row(LANES)],
        out_shape=[jax.ShapeDtypeStruct((b, s, half), BF16), jax.ShapeDtypeStruct((b, s, half), BF16),
                   jax.ShapeDtypeStruct((b, BRANCH_WIDTH, s), BF16), jax.ShapeDtypeStruct((b, s, BRANCH_WIDTH), BF16),
                   jax.ShapeDtypeStruct((b, s, LANES), F32)],
        scratch_shapes=[pltpu.VMEM((tm + CONV_HALO, 2 * half), F32)],
        compiler_params=_cp(("parallel", "arbitrary")),
        name="ml_project",
    )(h, wqk_p, wv.T.astype(BF16), wo.astype(BF16), wg, conv_p, gb)


def _mlstm_kernel(q_ref, k_ref, vt_ref, og_ref, g_ref, gt_ref, ng_ref, o_ref, c_ref, n_ref, m_ref):
    nb, s = q_ref.shape[0], q_ref.shape[1]
    L = ML_CHUNK
    c_ref[...] = jnp.zeros_like(c_ref)
    n_ref[...] = jnp.zeros_like(n_ref)
    m_ref[...] = jnp.zeros_like(m_ref)
    tril = _tri_incl(L)
    triu = tril.T
    srow = lax.broadcasted_iota(I32, (L, L), 0)
    tcol = lax.broadcasted_iota(I32, (L, L), 1)
    causal = srow <= tcol
    lane = lax.broadcasted_iota(I32, (L, LANES), 1)
    head_lanes = (jnp.where(lane < ML_DQK, 1.0, 0.0).astype(BF16), jnp.where(lane >= ML_DQK, 1.0, 0.0).astype(BF16))

    def chunk(c, carry):
        for bi in range(nb):
            one_chunk(c, bi)
        return carry

    def one_chunk(c, bi):
        r0 = pl.multiple_of(c * L, L)
        g = g_ref[bi, pl.ds(r0, L), :]
        gt = gt_ref[bi, :, pl.ds(r0, L)]
        bc = _tri_dot(tril, g)
        br = _dot_tri(gt, triu)
        for hd in range(ML_HEADS):
            sl = slice(LANES * hd, LANES * (hd + 1))
            st = bi * ML_HEADS + hd
            ccol = g[:, hd:hd + 1] - bc[:, ML_HEADS + hd:ML_HEADS + hd + 1]
            brow = br[ML_HEADS + hd:ML_HEADS + hd + 1, :]
            irow = gt[hd:hd + 1, :]
            m_prev = m_ref[st][:, 0:1]
            log_d = jnp.where(causal, brow + ccol, -jnp.inf)
            log_inter = brow + m_prev
            m_t = jnp.maximum(jnp.max(log_d, axis=0, keepdims=True), log_inter)
            w_intra = jnp.exp(log_d - m_t)
            w_inter = jnp.exp(log_inter - m_t)
            pair = slice(LANES * (hd // 2), LANES * (hd // 2 + 1))
            qc = q_ref[bi, pl.ds(r0, L), pair] * head_lanes[hd % 2]
            kc = k_ref[bi, pl.ds(r0, L), pair]
            vt = vt_ref[bi, sl, pl.ds(r0, L)]
            sc = _dot_nt(kc, qc) * w_intra
            cst = c_ref[st]
            nst = n_ref[st]
            num = _dot(vt, sc.astype(BF16)) + w_inter * _dot_nt(cst.astype(BF16), qc)
            qn = _dot_nt(jnp.broadcast_to(nst, (SUBLANES, LANES)).astype(BF16), qc)[0:1, :]
            den = jnp.sum(sc, axis=0, keepdims=True) + w_inter * qn
            hout = num / jnp.maximum(jnp.abs(den), jnp.exp(-m_t))
            b_last = brow[:, L - 1:L]
            lw = b_last - brow + irow
            m_new = jnp.maximum(b_last + m_prev, jnp.max(lw, axis=1, keepdims=True))
            w_in = jnp.exp(lw - m_new)
            decay = jnp.exp(b_last + m_prev - m_new)
            c_ref[st] = decay * cst + _dot((vt.astype(F32) * w_in).astype(BF16), kc)
            n_ref[st] = decay * nst + _dot(jnp.broadcast_to(w_in, (SUBLANES, L)).astype(BF16), kc)[0:1, :]
            m_ref[st] = jnp.broadcast_to(m_new, (1, LANES))
            ms = jnp.mean(hout * hout, axis=0, keepdims=True)
            y = (hout * lax.rsqrt(ms + EPS)).T * (og_ref[bi, pl.ds(r0, L), sl].astype(F32) * ng_ref[:, sl])
            o_ref[bi, pl.ds(r0, L), sl] = y.astype(BF16)

    lax.fori_loop(0, s // L, chunk, 0)


ML_SEQS = 2


def mlstm(q, k, vt, og, gates, norm_g):
    b, w, s = vt.shape
    nb = ML_SEQS if b % ML_SEQS == 0 else 1
    gt = jnp.swapaxes(gates[:, :, :2 * ML_HEADS], 1, 2)
    seq = lambda n: pl.BlockSpec((nb, s, n), lambda i: (i, 0, 0))
    return pl.pallas_call(
        _mlstm_kernel,
        grid=(b // nb,),
        in_specs=[seq(ML_HEADS * ML_DQK), seq(ML_HEADS * ML_DQK), pl.BlockSpec((nb, w, s), lambda i: (i, 0, 0)), seq(w), seq(LANES),
                  pl.BlockSpec((nb, 2 * ML_HEADS, s), lambda i: (i, 0, 0)),
                  pl.BlockSpec((1, w), lambda i: (0, 0))],
        out_specs=seq(w),
        out_shape=jax.ShapeDtypeStruct((b, s, w), BF16),
        scratch_shapes=[pltpu.VMEM((nb * ML_HEADS, ML_DV, LANES), F32), pltpu.VMEM((nb * ML_HEADS, 1, LANES), F32),
                        pltpu.VMEM((nb * ML_HEADS, 1, LANES), F32)],
        compiler_params=_cp(("parallel",)),
        name="mlstm",
    )(q, k, vt, og, gates, gt, norm_g.reshape(1, w))


def _hg_proj_kernel(h_ref, wf_ref, wq_ref, wi_ref, wg_ref, lb_ref, q_out, k_out, v_out, og_out, lf_out):
    h = h_ref[...]
    fz = _dot(h, wf_ref[...])
    log_lb = lb_ref[0:1, :]
    log_1m = lb_ref[1:2, :]
    one_m = lb_ref[2:3, :]
    a = log_lb
    bb = log_1m + _log_sigmoid(fz)
    lf_out[...] = jnp.maximum(a, bb) + jnp.log1p(jnp.exp(-jnp.abs(a - bb)))
    k_out[...] = (one_m * jax.nn.sigmoid(-fz)).astype(BF16)
    q_out[...] = _silu(_dot(h, wq_ref[...])).astype(BF16)
    v_out[...] = _dot(h, wi_ref[...]).astype(BF16)
    og_out[...] = _silu(_dot(h, wg_ref[...])).astype(BF16)


def hg_project(h, wf, wq, wi, wg, lb):
    b, s, d = h.shape
    tm = min(PROJ_TILE, s)
    w = BRANCH_WIDTH
    lbp =jnp.stack([jnp.log(lb), jnp.log1p(-lb), 1.0 - lb], axis=0)
    const = lambda shape: pl.BlockSpec(shape, lambda i, j: (0,) * len(shape))
    row = lambda n: pl.BlockSpec((None, tm, n), lambda i, j: (i, j, 0))
    return pl.pallas_call(
        _hg_proj_kernel,
        grid=(b, s // tm),
        in_specs=[row(d), const((d, w)), const((d, w)), const((d, w)), const((d, w)), const((3, w))],
        out_specs=[row(w), row(w), row(w), row(w), row(w)],
        out_shape=[jax.ShapeDtypeStruct((b, s, w), BF16)] * 4 + [jax.ShapeDtypeStruct((b, s, w), F32)],
        compiler_params=_cp(("parallel", "parallel")),
        name="hg_project",
    )(h, wf.astype(BF16), wq.astype(BF16), wi.astype(BF16), wg.astype(BF16), lbp)


HG_LEVELS = tuple(HG_CHUNK >> (i + 1) for i in range(HG_CHUNK.bit_length() - 1))


def _hg_tables():
    L = HG_CHUNK
    t = np.arange(L)
    tri = (t[None, :] <= t[:, None]).astype(np.float32)
    mats = [tri]
    x = t[:, None] ^ t[None, :]
    lvl = np.full((L, L), -1, np.int32)
    lvl[t[:, None] == t[None, :]] = 0
    for i, m in enumerate(HG_LEVELS):
        if m < SUBLANES:
            mats.append(tri[(t // (2 * m)) * (2 * m) + m - 1])
        lvl[(t[:, None] > t[None, :]) & (x >= m) & (x < 2 * m)] = i + 1
    return jnp.asarray(np.concatenate(mats, axis=0), BF16), jnp.asarray(lvl)


def _hgrn_kernel(q_ref, k_ref, v_ref, og_ref, lf_ref, ng_ref, tall_ref, lvl_ref, o_ref, st_ref):
    s, wd = q_ref.shape
    L = HG_CHUNK
    st_ref[...] = jnp.zeros_like(st_ref)
    rowi = lax.broadcasted_iota(I32, (L, LANES), 0)

    lvl = lvl_ref[...]
    level_masks = [lvl == i for i in range(len(HG_LEVELS) + 1)]

    def chunk(c, carry):
        r0 = pl.multiple_of(c * L, L)
        tall = tall_ref[...]
        g = lf_ref[pl.ds(r0, L), :]
        hi = g.astype(BF16)
        mid = (g - hi.astype(F32)).astype(BF16)
        cums = (_dot(tall, hi) + _dot(tall, mid)) * LOG2E
        a = cums[0:L]
        qb = q_ref[pl.ds(r0, L), :]
        kb = k_ref[pl.ds(r0, L), :]
        qf = qb.astype(F32)
        kf = kb.astype(F32)
        ws = []
        fine = 0
        for m in HG_LEVELS:
            if m >= SUBLANES:
                ref = jnp.concatenate([jnp.broadcast_to(a[g0 + m - 1:g0 + m, :], (2 * m, wd)) for g0 in range(0, L, 2 * m)], axis=0)
            else:
                fine += 1
                ref = cums[L * fine:L * (fine + 1)]
            e = jnp.exp2(-jnp.abs(a - ref))
            upper = (rowi & m) != 0
            qk = jnp.concatenate([jnp.where(upper, qf[:, LANES * hd:LANES * (hd + 1)], kf[:, LANES * hd:LANES * (hd + 1)])
                                  for hd in range(HG_HEADS)], axis=1)
            ws.append((e * qk).astype(BF16))
        a_last = a[L - 1:L, :]
        qa = (qf * jnp.exp2(a)).astype(BF16)
        kt = (kf * jnp.exp2(a_last - a)).astype(BF16)
        decay = jnp.exp2(a_last)
        for hd in range(HG_HEADS):
            sl = slice(LANES * hd, LANES * (hd + 1))
            vb = v_ref[pl.ds(r0, L), sl]
            sc = jnp.where(level_masks[0], _dot_nt(qb[:, sl], kb[:, sl]), 0.0)
            for i in range(len(HG_LEVELS)):
                w = ws[i][:, sl]
                sc = jnp.where(level_masks[i + 1], _dot_nt(w, w), sc)
            st = st_ref[hd]
            out = _dot(sc.astype(BF16), vb) + _dot_nt(qa[:, sl], st.astype(BF16))
            st_ref[hd] = st * decay[:, sl] + _dot_tn(vb, kt[:, sl])
            ms = jnp.mean(out * out, axis=-1, keepdims=True)
            y = (out * lax.rsqrt(ms + EPS)) * ng_ref[:, sl] * og_ref[pl.ds(r0, L), sl].astype(F32)
            o_ref[pl.ds(r0, L), sl] = y.astype(BF16)
        return carry

    lax.fori_loop(0, s // L, chunk, 0)


def hgrn(q, k, v, og, logf, norm_g):
    b, s, w = v.shape
    tall, lvl = _hg_tables()
    seq = pl.BlockSpec((None, s, w), lambda i: (i, 0, 0))
    const = lambda shape: pl.BlockSpec(shape, lambda i: (0,) * len(shape))
    return pl.pallas_call(
        _hgrn_kernel,
        grid=(b,),
        in_specs=[seq, seq, seq, seq, seq, const((1, w)), const(tall.shape), const(lvl.shape)],
        out_specs=seq,
        out_shape=jax.ShapeDtypeStruct((b, s, w), BF16),
        scratch_shapes=[pltpu.VMEM((HG_HEADS, LANES, HG_DK), F32)],
        compiler_params=_cp(("parallel",)),
        name="hgrn",
    )(q, k, v, og, logf, norm_g.reshape(1, w), tall, lvl)


def _merge_kernel(x_ref, h_ref, yf_ref, ym_ref, yh_ref, wg_ref, wb_ref, wo_ref, g1_ref, n2_ref, sc2_ref, sh2_ref, wr_ref, rb_ref,
                  x_out, h2_out, dest_out, w_out, cnt_out):
    d = x_ref.shape[1]
    h = h_ref[...]
    merged = None
    for br, y_ref in enumerate((yf_ref, ym_ref, yh_ref)):
        gate = jax.nn.sigmoid(_dot(h, wg_ref[:, d * br:d * (br + 1)]))
        term = gate * _dot(y_ref[...], wb_ref[br])
        merged = term if merged is None else merged + term
    mixed = _dot(merged.astype(BF16), wo_ref[...])
    x1 = x_ref[...] + g1_ref[...] * mixed
    x_out[...] = x1
    h2 = _norm_mod(x1, n2_ref[...], sc2_ref[...], sh2_ref[...]).astype(BF16)
    h2_out[...] = h2
    _route(h2, wr_ref[...], rb_ref[...], dest_out, w_out, cnt_out)


def merge_branches(x, h, y_fox, y_ml, y_hg, w_gates, w_branch, w_out, g1, norm2_g, sc2, sh2, router_w, router_bias):
    b, s, d = x.shape
    tm = SUB_TOKENS
    per_seq = s // tm
    nsub = b * per_seq
    w = BRANCH_WIDTH
    mat = pl.BlockSpec((None, 2 * N_EXPERTS, tm), lambda i, j: (i * per_seq + j, 0, 0))
    const = lambda shape: pl.BlockSpec(shape, lambda i, j: (0,) * len(shape))
    row = lambda n: pl.BlockSpec((None, tm, n), lambda i, j: (i, j, 0))
    per_b = pl.BlockSpec((None, 1, d), lambda i, j: (i, 0, 0))
    x1, h2, dest, wts, cnt = pl.pallas_call(
        _merge_kernel,
        grid=(b, s // tm),
        in_specs=[row(d), row(d), row(w), row(w), row(w), const((d, N_BRANCH * d)), const((N_BRANCH, w, d)), const((d, d)),
                  per_b, const((1, d)), per_b, per_b, const((N_EXPERTS, d)), const((N_EXPERTS, 1))],
        out_specs=[row(d), row(d), mat, mat, pl.BlockSpec((None, N_EXPERTS, LANES), lambda i, j: (i * per_seq + j, 0, 0))],
        out_shape=[jax.ShapeDtypeStruct((b, s, d), F32), jax.ShapeDtypeStruct((b, s, d), BF16),
                   jax.ShapeDtypeStruct((nsub, 2 * N_EXPERTS, tm), BF16), jax.ShapeDtypeStruct((nsub, 2 * N_EXPERTS, tm), BF16),
                   jax.ShapeDtypeStruct((nsub, N_EXPERTS, LANES), I32)],
        compiler_params=_cp(("parallel", "parallel"), 56),
        name="merge_branches",
    )(x, h, y_fox, y_ml, y_hg, w_gates.astype(BF16), w_branch.astype(BF16), w_out.astype(BF16),
      g1.reshape(b, 1, d), norm2_g.reshape(1, d), sc2.reshape(b, 1, d), sh2.reshape(b, 1, d),
      router_w.T.astype(BF16), router_bias.reshape(N_EXPERTS, 1))
    return x1, h2, dest, wts, cnt[:, :, 0]


def _first_max(x, iota, size):
    m = jnp.max(x, axis=0, keepdims=True)
    idx = jnp.min(jnp.where(x == m, iota, size), axis=0, keepdims=True)
    return m, idx


def _route(h, wr, rb, dest_out, w_out, cnt_out):
    n = h.shape[0]
    scores = jax.nn.sigmoid(_dot_nt(wr, h))
    choice = scores + rb
    e_iota = lax.broadcasted_iota(I32, (N_EXPERTS, n), 0)
    c3 = choice.reshape(N_GROUPS, GROUP_SIZE, n)
    i3 = lax.broadcasted_iota(I32, (N_GROUPS, GROUP_SIZE, n), 1)
    m1 = jnp.max(c3, axis=1, keepdims=True)
    i1 = jnp.min(jnp.where(c3 == m1, i3, GROUP_SIZE), axis=1, keepdims=True)
    m2 = jnp.max(jnp.where(i3 == i1, -jnp.inf, c3), axis=1, keepdims=True)
    gs = (m1 + m2).reshape(N_GROUPS, n)
    g_iota = lax.broadcasted_iota(I32, (N_GROUPS, n), 0)
    gsel = jnp.zeros((N_GROUPS, n), F32)
    for _ in range(TOPK_GROUPS):
        _, gi = _first_max(gs, g_iota, N_GROUPS)
        hit = g_iota == gi
        gsel = jnp.where(hit, 1.0, gsel)
        gs = jnp.where(hit, -jnp.inf, gs)
    gmask = jnp.broadcast_to(gsel.reshape(N_GROUPS, 1, n), (N_GROUPS, GROUP_SIZE, n)).reshape(N_EXPERTS, n)
    masked = jnp.where(gmask > 0.0, choice, -jnp.inf)
    sel = jnp.zeros((N_EXPERTS, n), F32)
    for _ in range(TOP_K):
        _, ei = _first_max(masked, e_iota, N_EXPERTS)
        hit = e_iota == ei
        sel = jnp.where(hit, 1.0, sel)
        masked = jnp.where(hit, -jnp.inf, masked)
    tr = lax.broadcasted_iota(I32, (n, n), 0)
    tc = lax.broadcasted_iota(I32, (n, n), 1)
    before = jnp.where(tr < tc, 1.0, 0.0).astype(BF16)
    pos = _dot(sel.astype(BF16), before)
    cnt = jnp.sum(sel, axis=1, keepdims=True)
    units = jnp.floor((cnt + (RUN_ALIGN - 1)) * (1.0 / RUN_ALIGN))
    er = lax.broadcasted_iota(I32, (N_EXPERTS, N_EXPERTS), 0)
    ec = lax.broadcasted_iota(I32, (N_EXPERTS, N_EXPERTS), 1)
    lower = jnp.where(ec < er, 1.0, 0.0).astype(BF16)
    off = _dot(lower, jnp.broadcast_to(units, (N_EXPERTS, LANES)).astype(BF16))[:, 0:1] * RUN_ALIGN
    dest = jnp.where(sel > 0.0, off + pos, float(DEST_NONE))
    dhi = jnp.floor(dest * (1.0 / DEST_RADIX))
    dest_out[0:N_EXPERTS, :] = dhi.astype(BF16)
    dest_out[N_EXPERTS:, :] = (dest - dhi * DEST_RADIX).astype(BF16)
    wsum = jnp.sum(scores * sel, axis=0, keepdims=True)
    w_out[0:N_EXPERTS, :] = jnp.zeros((N_EXPERTS, n), BF16)
    w_out[N_EXPERTS:, :] = (scores * sel / wsum * ROUTE_SCALE).astype(BF16)
    cnt_out[...] = jnp.broadcast_to(cnt, (N_EXPERTS, LANES)).astype(I32)


def _run_tables(cnt, n_blocks):
    units = (cnt + (RUN_ALIGN - 1)) // RUN_ALIGN
    src = jnp.cumsum(units, axis=1) - units
    per_block = MOE_BLOCK // RUN_ALIGN
    tot = jnp.sum(units, axis=0)
    tot_blocks = (tot + per_block - 1) // per_block
    blk_end = jnp.cumsum(tot_blocks)
    base = (blk_end - tot_blocks) * per_block
    dst = base[None, :] + jnp.cumsum(units, axis=0) - units
    n_used = blk_end[-1]
    tail_units = tot_blocks * per_block - tot
    tail_dst = base + tot
    blk = jnp.minimum(jnp.arange(n_blocks), n_used - 1)
    blk_exp = jnp.minimum(jnp.sum(blk[:, None] >= blk_end[None, :], axis=1), N_EXPERTS - 1)
    return (src.reshape(-1).astype(I32), dst.reshape(-1).astype(I32), units.reshape(-1).astype(I32),
            blk_exp.astype(I32), n_used.astype(I32).reshape(1), tail_dst.astype(I32), tail_units.astype(I32),
            jnp.sum(units, axis=1).astype(I32))


def _copy_lists(src, dst, units, nsub):
    s0 = src.reshape(nsub, 1, N_EXPERTS)
    d0 = dst.reshape(nsub, 1, N_EXPERTS)
    u = units.reshape(nsub, 1, N_EXPERTS)
    npair = u // 2
    poff = jnp.cumsum(npair, axis=2) - npair
    p = jnp.arange(SUB_UNITS // 2, dtype=I32).reshape(1, -1, 1)
    own = (p >= poff) & (p < poff + npair)
    psrc = jnp.sum(jnp.where(own, s0 - 2 * poff, 0), axis=2) + 2 * p[:, :, 0]
    pdst = jnp.sum(jnp.where(own, d0 - 2 * poff, 0), axis=2) + 2 * p[:, :, 0]
    odd = u & 1
    soff = jnp.cumsum(odd, axis=2) - odd
    q = jnp.arange(N_EXPERTS, dtype=I32).reshape(1, -1, 1)
    owns = (odd == 1) & (soff == q)
    ssrc = jnp.sum(jnp.where(owns, s0 + u - 1, 0), axis=2)
    sdst = jnp.sum(jnp.where(owns, d0 + u - 1, 0), axis=2)
    flat = lambda a: a.reshape(-1).astype(I32)
    return (flat(psrc), flat(pdst), flat(jnp.sum(npair, axis=2)), flat(ssrc), flat(sdst), flat(jnp.sum(odd, axis=2)))


def _run_bounds(src, units, nsub):
    lo = (src.reshape(nsub, N_EXPERTS) * RUN_ALIGN).astype(F32)
    hi = lo + (units.reshape(nsub, N_EXPERTS) * RUN_ALIGN).astype(F32)
    return jnp.stack([jnp.concatenate([lo, lo], axis=1), jnp.concatenate([hi, hi], axis=1)], axis=1)


def _max_blocks(t):
    nsub = t // SUB_TOKENS
    worst_units = t * TOP_K // RUN_ALIGN + nsub * N_EXPERTS
    per_block = MOE_BLOCK // RUN_ALIGN
    return -(-worst_units // per_block) + N_EXPERTS


SUB_UNITS = SUB_ROWS // RUN_ALIGN
COPY_UNROLL = 4


PAIR_SLOTS = SUB_UNITS // 2


def _piece_copies(lists, step, buf, hbm, sem, to_hbm):
    psrc_ref, pdst_ref, npair_ref, ssrc_ref, sdst_ref, nsingle_ref = lists

    def piece(src_ref, dst_ref, idx, units):
        rows = units * RUN_ALIGN
        v = buf.at[pl.ds(pl.multiple_of(src_ref[idx] * RUN_ALIGN, RUN_ALIGN), rows)]
        g = hbm.at[pl.ds(pl.multiple_of(dst_ref[idx] * RUN_ALIGN, RUN_ALIGN), rows)]
        cp = pltpu.make_async_copy(v, g, sem) if to_hbm else pltpu.make_async_copy(g, v, sem)
        cp.start()

    def issue(src_ref, dst_ref, base, count, units):
        def group(q, carry):
            for r in range(COPY_UNROLL):
                piece(src_ref, dst_ref, base + q * COPY_UNROLL + r, units)
            return carry

        groups = lax.shift_right_logical(count, COPY_UNROLL.bit_length() - 1)
        lax.fori_loop(0, groups, group, 0)
        lax.fori_loop(groups * COPY_UNROLL, count, lambda j, c: (piece(src_ref, dst_ref, base + j, units), c)[1], 0)

    issue(psrc_ref, pdst_ref, step * PAIR_SLOTS, npair_ref[step], 2)
    issue(ssrc_ref, sdst_ref, step * N_EXPERTS, nsingle_ref[step], 1)


TOTAL_BITS = tuple(1 << b for b in range(SUB_UNITS.bit_length()))


def _wait_runs(total_units, buf, hbm, sem, to_hbm):
    for bit in TOTAL_BITS:
        @pl.when((total_units & bit) != 0)
        def _():
            rows = bit * RUN_ALIGN
            v = buf.at[pl.ds(0, rows)]
            g = hbm.at[pl.ds(0, rows)]
            cp = pltpu.make_async_copy(v, g, sem) if to_hbm else pltpu.make_async_copy(g, v, sem)
            cp.wait()


def _dispatch_kernel(psrc_ref, pdst_ref, npair_ref, ssrc_ref, sdst_ref, nsingle_ref, tot_ref, tdst_ref, tunits_ref, nused_ref,
                     h_ref, dest_ref, lohi_ref, xs_out, buf_ref, zero_ref, sem):
    lists = (psrc_ref, pdst_ref, npair_ref, ssrc_ref, sdst_ref, nsingle_ref)
    i = pl.program_id(0)
    nsub = pl.num_programs(0)
    slot = i % 2
    n = h_ref.shape[0]
    chunk = SORT_CHUNK

    for sl in range(2):
        @pl.when((slot == sl) & (i >= 2))
        def _():
            _wait_runs(tot_ref[i - 2], buf_ref.at[sl], xs_out, sem.at[sl], True)

    h = h_ref[...]
    dest = dest_ref[...]
    lo = lohi_ref[0:1, :]
    hi = lohi_ref[1:2, :]
    radix = jnp.where(lax.broadcasted_iota(I32, (1, 2 * N_EXPERTS), 1) < N_EXPERTS, float(DEST_RADIX), 1.0)
    r_e = lax.broadcasted_iota(I32, (chunk, 2 * N_EXPERTS), 0).astype(F32)
    r_t = lax.broadcasted_iota(I32, (chunk, n), 0).astype(F32)
    for sl in range(2):
        @pl.when(slot == sl)
        def _():
            for c in range(SUB_ROWS // chunk):
                own = jnp.where(r_e + c * chunk >= lo, jnp.where(r_e + c * chunk < hi, radix, 0.0), 0.0)
                row_of = _dot(own.astype(BF16), dest)
                p = jnp.where(row_of == r_t + c * chunk, 1.0, 0.0)
                buf_ref[sl, c * chunk:(c + 1) * chunk, :] = _dot(p.astype(BF16), h).astype(BF16)
            _piece_copies(lists, i, buf_ref.at[sl], xs_out, sem.at[sl], True)

    @pl.when(i == nsub - 1)
    def _():
        zero_ref[...] = jnp.zeros_like(zero_ref)

        def tails(e, wait):
            u = tunits_ref[e]
            d0 = tdst_ref[e]
            for bit in RUN_BITS:
                low = u & (bit - 1)

                @pl.when((u & bit) != 0)
                def _():
                    rows = bit * RUN_ALIGN
                    cp = pltpu.make_async_copy(zero_ref.at[pl.ds(0, rows)],
                                               xs_out.at[pl.ds(pl.multiple_of((d0 + low) * RUN_ALIGN, RUN_ALIGN), rows)], sem.at[2])
                    if wait:
                        cp.wait()
                    else:
                        cp.start()
            return wait

        def unused(b, wait):
            cp = pltpu.make_async_copy(zero_ref.at[pl.ds(0, MOE_BLOCK)],
                                       xs_out.at[pl.ds(pl.multiple_of(b * MOE_BLOCK, MOE_BLOCK), MOE_BLOCK)], sem.at[2])
            if wait:
                cp.wait()
            else:
                cp.start()
            return wait

        n_blocks = xs_out.shape[0] // MOE_BLOCK
        lax.fori_loop(0, N_EXPERTS, lambda e, c: (tails(e, False), c)[1], 0)
        lax.fori_loop(nused_ref[0], n_blocks, lambda b, c: (unused(b, False), c)[1], 0)
        for sl in range(2):
            @pl.when((slot != sl) & (i >= 1))
            def _():
                _wait_runs(tot_ref[i - 1], buf_ref.at[sl], xs_out, sem.at[sl], True)

            @pl.when(slot == sl)
            def _():
                _wait_runs(tot_ref[i], buf_ref.at[sl], xs_out, sem.at[sl], True)
        lax.fori_loop(0, N_EXPERTS, lambda e, c: (tails(e, True), c)[1], 0)
        lax.fori_loop(nused_ref[0], n_blocks, lambda b, c: (unused(b, True), c)[1], 0)


def moe_dispatch(h2, dest, tables, n_rows):
    t, d = h2.shape
    n = SUB_TOKENS
    nsub = t // n
    src, dst, units, _, n_used, tail_dst, tail_units, tot = tables
    grid_spec = pltpu.PrefetchScalarGridSpec(
        num_scalar_prefetch=10,
        grid=(nsub,),
        in_specs=[pl.BlockSpec((n, d), lambda i, *_: (i, 0)), pl.BlockSpec((None, 2 * N_EXPERTS, n), lambda i, *_: (i, 0, 0)),
                  pl.BlockSpec((None, 2, 2 * N_EXPERTS), lambda i, *_: (i, 0, 0))],
        out_specs=pl.BlockSpec(memory_space=pl.ANY),
        scratch_shapes=[pltpu.VMEM((2, SUB_ROWS, d), BF16), pltpu.VMEM((max(SUB_TOKENS, MOE_BLOCK), d), BF16), pltpu.SemaphoreType.DMA((3,))],
    )
    return pl.pallas_call(
        _dispatch_kernel,
        grid_spec=grid_spec,
        out_shape=jax.ShapeDtypeStruct((n_rows, d), BF16),
        compiler_params=_cp(("arbitrary",)),
        name="moe_dispatch",
    )(*_copy_lists(src, dst, units, nsub), tot, tail_dst, tail_units, n_used, h2, dest, _run_bounds(src, units, nsub))


X_SLOTS = 3


def _expert_kernel(blk_exp_ref, n_used_ref, x_hbm, w1_ref, w3_ref, w2_ref, y_ref, xbuf, sem):
    b = pl.program_id(0)
    n_used = n_used_ref[0]

    def x_copy(blk, slot):
        rows = pl.ds(pl.multiple_of(blk * MOE_BLOCK, MOE_BLOCK), MOE_BLOCK)
        return pltpu.make_async_copy(x_hbm.at[rows], xbuf.at[slot], sem.at[slot])

    @pl.when(b == 0)
    def _():
        x_copy(0, 0).start()

        @pl.when(n_used > 1)
        def _():
            x_copy(1, 1).start()

    ahead = b + (X_SLOTS - 1)

    @pl.when(ahead < n_used)
    def _():
        x_copy(ahead, lax.rem(ahead, X_SLOTS)).start()

    @pl.when(b < n_used)
    def _():
        slot = lax.rem(b, X_SLOTS)
        x_copy(b, slot).wait()
        x = xbuf[slot]
        hid = _silu(_dot(x, w1_ref[...].astype(BF16))) * _dot(x, w3_ref[...].astype(BF16))
        y_ref[...] = _dot(hid.astype(BF16), w2_ref[...].astype(BF16)).astype(BF16)

    @pl.when(b >= n_used)
    def _():
        y_ref[...] = jnp.zeros_like(y_ref)


def moe_experts(xs, w1, w3, w2, layer, tables, n_blocks):
    n_rows, d = xs.shape
    blk_exp, n_used = tables[3], tables[4]
    f = w1.shape[-1]

    def w_map(b, be, nu):
        return (layer, be[b], 0, 0)

    grid_spec = pltpu.PrefetchScalarGridSpec(
        num_scalar_prefetch=2,
        grid=(n_blocks,),
        in_specs=[pl.BlockSpec(memory_space=pl.ANY), pl.BlockSpec((None, None, d, f), w_map),
                  pl.BlockSpec((None, None, d, f), w_map), pl.BlockSpec((None, None, f, d), w_map)],
        out_specs=pl.BlockSpec((MOE_BLOCK, d), lambda b, be, nu: (b, 0)),
        scratch_shapes=[pltpu.VMEM((X_SLOTS, MOE_BLOCK, d), BF16), pltpu.SemaphoreType.DMA((X_SLOTS,))],
    )
    return pl.pallas_call(
        _expert_kernel,
        grid_spec=grid_spec,
        out_shape=jax.ShapeDtypeStruct((n_rows, d), BF16),
        compiler_params=_cp(("arbitrary",)),
        name="moe_experts",
    )(blk_exp, n_used, xs, w1, w3, w2)


def _combine_kernel(psrc_ref, pdst_ref, npair_ref, ssrc_ref, sdst_ref, nsingle_ref, tot_ref, ys_ref, dcol_ref, lohi_ref,
                    h_ref, x_ref, g2_ref, ws1_ref, ws3_ref, ws2_ref, o_ref, buf_ref, sem):
    lists = (psrc_ref, pdst_ref, npair_ref, ssrc_ref, sdst_ref, nsingle_ref)
    i = pl.program_id(0)
    nsub = pl.num_programs(0)
    slot = i % 2
    n = h_ref.shape[0]
    chunk = 512

    def fetch(step, sl):
        always = SUB_TOKENS * TOP_K
        buf_ref[sl, always:, :] = jnp.zeros((SUB_ROWS - always, buf_ref.shape[2]), BF16)
        _piece_copies(lists, step, buf_ref.at[sl], ys_ref, sem.at[sl], False)

    @pl.when(i == 0)
    def _():
        fetch(0, 0)

    for sl in range(2):
        @pl.when((slot != sl) & (i + 1 < nsub))
        def _():
            fetch(i + 1, sl)

    h = h_ref[...]
    shared = _dot((_silu(_dot(h, ws1_ref[...])) * _dot(h, ws3_ref[...])).astype(BF16), ws2_ref[...])
    dw_t = dcol_ref[...]
    lo = lohi_ref[:, 0:1]
    hi = lohi_ref[:, 1:2]
    part = lax.broadcasted_iota(I32, (4 * N_EXPERTS, 1), 0)
    radix = jnp.where(part < N_EXPERTS, float(DEST_RADIX),
                      jnp.where(part < 2 * N_EXPERTS, 1.0, jnp.where(part < 3 * N_EXPERTS, WEIGHT_SHIFT, 0.0)))
    r_e = lax.broadcasted_iota(I32, (4 * N_EXPERTS, chunk), 1).astype(F32)
    r_t = lax.broadcasted_iota(I32, (n, chunk), 1).astype(F32)
    for sl in range(2):
        @pl.when(slot == sl)
        def _():
            _wait_runs(tot_ref[i], buf_ref.at[sl], ys_ref, sem.at[sl], False)
            acc = shared
            for c in range(SUB_ROWS // chunk):
                own = jnp.where(r_e + c * chunk >= lo, jnp.where(r_e + c * chunk < hi, radix, 0.0), 0.0).astype(BF16)
                val = _dot(dw_t, own)
                row_of = jnp.floor(val)
                pw = jnp.where(row_of == r_t + c * chunk, (val - row_of) * (1.0 / WEIGHT_SHIFT), 0.0)
                acc = acc + _dot(pw.astype(BF16), buf_ref[sl, c * chunk:(c + 1) * chunk, :])
            o_ref[...] = x_ref[...] + g2_ref[...] * acc


def moe_combine(ys, dest, wts, h2, x1, g2, ws1, ws3, ws2, tables, seq):
    t, d = h2.shape
    n = SUB_TOKENS
    nsub = t // n
    src, dst, units, tot = tables[0], tables[1], tables[2], tables[7]
    dw = jnp.concatenate([dest, wts[:, N_EXPERTS:, :], jnp.zeros((nsub, N_EXPERTS, n), BF16)], axis=1)
    dcol = jnp.swapaxes(dw, 1, 2)
    b2 = _run_bounds(src, units, nsub)
    bounds = jnp.swapaxes(jnp.concatenate([b2, b2], axis=2), 1, 2)
    per_seq = seq // n
    f = ws1.shape[-1]
    pair = pl.BlockSpec((None, n, 4 * N_EXPERTS), lambda i, *_: (i, 0, 0))
    grid_spec = pltpu.PrefetchScalarGridSpec(
        num_scalar_prefetch=7,
        grid=(nsub,),
        in_specs=[pl.BlockSpec(memory_space=pl.ANY), pair,
                  pl.BlockSpec((None, 4 * N_EXPERTS, 2), lambda i, *_: (i, 0, 0)),
                  pl.BlockSpec((n, d), lambda i, *_: (i, 0)), pl.BlockSpec((n, d), lambda i, *_: (i, 0)),
                  pl.BlockSpec((None, 1, d), lambda i, *_: (i // per_seq, 0, 0)),
                  pl.BlockSpec((d, f), lambda i, *_: (0, 0)), pl.BlockSpec((d, f), lambda i, *_: (0, 0)),
                  pl.BlockSpec((f, d), lambda i, *_: (0, 0))],
        out_specs=pl.BlockSpec((n, d), lambda i, *_: (i, 0)),
        scratch_shapes=[pltpu.VMEM((2, SUB_ROWS, d), BF16), pltpu.SemaphoreType.DMA((2,))],
    )
    return pl.pallas_call(
        _combine_kernel,
        grid_spec=grid_spec,
        out_shape=jax.ShapeDtypeStruct((t, d), F32),
        compiler_params=_cp(("arbitrary",)),
        name="moe_combine",
    )(*_copy_lists(src, dst, units, nsub), tot, ys, dcol, bounds, h2, x1, g2, ws1.astype(BF16), ws3.astype(BF16), ws2.astype(BF16))


def _split_w_in(w):
    fh = FOX_HEADS * FOX_HEAD_DIM
    sizes = (fh, fh, fh, FOX_HEADS,
             2 * ML_HEADS * ML_DQK, ML_HEADS * ML_DV, ML_HEADS, ML_HEADS, ML_HEADS * ML_DV,
             HG_HEADS * HG_DK, HG_HEADS * HG_DK, BRANCH_WIDTH, BRANCH_WIDTH,
             N_BRANCH * D_MODEL)
    outs, o = [], 0
    for sz in sizes:
        outs.append(w[:, o:o + sz])
        o += sz
    return outs


def moe_ffn(h2, x1, g2, dest, wts, cnt, w1, w3, w2, layer, ws1, ws3, ws2, seq):
    t, d = h2.shape
    n_blocks = _max_blocks(t)
    tables = _run_tables(cnt, n_blocks)
    xs = moe_dispatch(h2, dest, tables, n_blocks * MOE_BLOCK)
    ys = moe_experts(xs, w1, w3, w2, layer, tables, n_blocks)
    return moe_combine(ys, dest, wts, h2, x1, g2, ws1, ws3, ws2, tables, seq)


def kernel(x, c, ada_w, ada_b, norm1_g, norm2_g, w_in, fox_bf, fox_q_g, fox_k_g, mlstm_conv, mlstm_bi, mlstm_bf, mlstm_norm_g, hgrn_lower_bounds, hgrn_norm_g, w_branch, w_out, router_w, router_bias, exp_w1, exp_w3, exp_w2, sh_w1, sh_w3, sh_w2):
    b, s, d = x.shape
    depth = ada_w.shape[0]
    mod = adaln_mod(c, ada_w, ada_b)
    lb_all = jnp.cumsum(jax.nn.softmax(hgrn_lower_bounds.astype(F32), axis=0), axis=0)
    lb_all = lb_all - lb_all[0]
    for l in range(depth):
        sh1, sc1, g1, sh2, sc2, g2 = [mod[l][:, d * j:d * (j + 1)] for j in range(6)]
        (wfq, wfk, wfv, wff, wmqk, wmv, wmi, wmf, wmo, whf, whq, whi, whg, wgates) = _split_w_in(w_in[l])
        qp, kp, fv, h = fox_project(x, norm1_g[l], sc1, sh1, wfq, wfk, wfv, wff, fox_q_g[l], fox_k_g[l], fox_bf[l])
        y_fox = fox_attention(qp, kp, fv)
        mq, mk, mv, mog, mgates = ml_project(h, wmqk, wmv, wmi, wmf, wmo, mlstm_conv[l], mlstm_bi[l], mlstm_bf[l])
        y_ml = mlstm(mq, mk, mv, mog, mgates, mlstm_norm_g[l])
        hq, hk, hv, hog, hlf = hg_project(h, whf, whq, whi, whg, lb_all[l])
        y_hg = hgrn(hq, hk, hv, hog, hlf, hgrn_norm_g[l])
        x1, h2, dest, wts, cnt = merge_branches(x, h, y_fox, y_ml, y_hg, wgates, w_branch[l], w_out[l], g1, norm2_g[l], sc2, sh2,
                                                router_w[l], router_bias[l])
        x = moe_ffn(h2.reshape(b * s, d), x1.reshape(b * s, d), g2.reshape(b, 1, d), dest, wts, cnt,
                    exp_w1, exp_w3, exp_w2, l, sh_w1[l], sh_w3[l], sh_w2[l], s).reshape(b, s, d)
    return x
```

```python
import jax
import jax.numpy as jnp
import numpy as np
from jax import lax
from jax.experimental import pallas as pl
from jax.experimental.pallas import tpu as pltpu

F32 = jnp.float32
BF16 = jnp.bfloat16
I32 = jnp.int32

LANES = 128
SUBLANES = 8
BF16_ROWS = 16

D_MODEL = 1024
BRANCH_WIDTH = D_MODEL // 2
N_BRANCH = 3
FOX_HEAD_DIM = 64
FOX_HEADS = BRANCH_WIDTH // FOX_HEAD_DIM
ML_HEADS = 4
ML_DV = BRANCH_WIDTH // ML_HEADS
ML_DQK = ML_DV // 2
ML_CONV = 4
ML_CHUNK = 128
HG_HEADS = 4
HG_DK = 128
HG_CHUNK = 128
N_EXPERTS = 64
N_GROUPS = 8
GROUP_SIZE = N_EXPERTS // N_GROUPS
TOPK_GROUPS = 4
TOP_K = 8
ROUTE_SCALE = 2.5
MOE_BLOCK = 1024
SORT_CHUNK = 1024
EPS = 1e-6
NEG = -1e30
LOG2E = 1.4426950408889634

PROJ_TILE = 1024
ATTN_TILE = 256
SUB_TOKENS = 256
MERGE_TILE = 512
RUN_ALIGN = BF16_ROWS
SUB_ROWS = ((SUB_TOKENS * TOP_K + N_EXPERTS * (RUN_ALIGN - 1)) + 255) // 256 * 256
RUN_BITS = tuple(1 << b for b in range((SUB_TOKENS // RUN_ALIGN).bit_length()))
DEST_RADIX = 64
DEST_NONE = DEST_RADIX * 127
WEIGHT_SHIFT = 0.25
assert ROUTE_SCALE * WEIGHT_SHIFT < 1.0
assert SUB_ROWS <= DEST_NONE


def _cp(sem, vmem_mb=48):
    return pltpu.CompilerParams(dimension_semantics=sem, vmem_limit_bytes=vmem_mb * 1024 * 1024)


def _dot(a, b):
    return jnp.dot(a, b, preferred_element_type=F32)


def _dot_nt(a, b):
    return lax.dot_general(a, b, (((1,), (1,)), ((), ())), preferred_element_type=F32)


def _dot_tn(a, b):
    return lax.dot_general(a, b, (((0,), (0,)), ((), ())), preferred_element_type=F32)


def _split3(x):
    hi = x.astype(BF16)
    r = x - hi.astype(F32)
    mid = r.astype(BF16)
    lo = (r - mid.astype(F32)).astype(BF16)
    return hi, mid, lo


def _tri_dot(tri, x):
    hi, mid, lo = _split3(x)
    return (_dot(tri, hi) + _dot(tri, mid)) + _dot(tri, lo)


def _dot_tri(x, tri):
    hi, mid, lo = _split3(x)
    return (_dot(hi, tri) + _dot(mid, tri)) + _dot(lo, tri)


def _log_sigmoid(x):
    return jnp.minimum(x, 0.0) - jnp.log1p(jnp.exp(-jnp.abs(x)))


def _silu(x):
    return x * jax.nn.sigmoid(x)


def _tri_incl(n, dtype=BF16):
    r = lax.broadcasted_iota(I32, (n, n), 0)
    c = lax.broadcasted_iota(I32, (n, n), 1)
    return jnp.where(c <= r, 1.0, 0.0).astype(dtype)


def _mod_kernel(c_ref, w_ref, b_ref, o_ref):
    cond = _silu(c_ref[...])
    hi, mid, lo = _split3(cond)
    w = w_ref[...]
    whi, wmid, wlo = _split3(w)
    acc = _dot(hi, whi) + (_dot(hi, wmid) + _dot(mid, whi))
    acc = acc + (_dot(mid, wmid) + _dot(hi, wlo) + _dot(lo, whi))
    o_ref[...] = acc + b_ref[...]


def adaln_mod(c, ada_w, ada_b):
    depth, d, n = ada_w.shape
    b = c.shape[0]
    tn = 1024
    return pl.pallas_call(
        _mod_kernel,
        grid=(depth, n // tn),
        in_specs=[
            pl.BlockSpec((b, d), lambda l, j: (0, 0)),
            pl.BlockSpec((None, d, tn), lambda l, j: (l, 0, j)),
            pl.BlockSpec((None, 1, tn), lambda l, j: (l, 0, j)),
        ],
        out_specs=pl.BlockSpec((None, b, tn), lambda l, j: (l, 0, j)),
        out_shape=jax.ShapeDtypeStruct((depth, b, n), F32),
        compiler_params=_cp(("parallel", "parallel")),
        name="adaln_mod",
    )(c, ada_w, ada_b.reshape(depth, 1, n))


def _norm_mod(x, g, sc, sh):
    ms = jnp.mean(x * x, axis=-1, keepdims=True)
    return x * lax.rsqrt(ms + EPS) * g * (1.0 + sc) + sh


FOX_BIAS_PIECES = 3
FOX_SLOT = 2 * LANES
FOX_BIAS_STRIDE = SUBLANES


def _pack_pieces(x):
    hi, mid, lo = _split3(x)
    p = hi.astype(F32) + pltpu.roll(mid.astype(F32), FOX_HEADS, axis=1) + pltpu.roll(lo.astype(F32), 2 * FOX_HEADS, axis=1)
    return p.astype(BF16)


def _fox_proj_kernel(x_ref, ng_ref, sc_ref, sh_ref, wq_ref, wk_ref, wvt_ref, wf_ref, gq_ref, gk_ref, bf_ref, eq_ref, ek_ref,
                     cq_ref, ck_ref, q_out, k_out, vt_out, h_out, carry_ref):
    @pl.when(pl.program_id(1) == 0)
    def _():
        carry_ref[...] = jnp.zeros_like(carry_ref)

    h = _norm_mod(x_ref[...], ng_ref[...], sc_ref[...], sh_ref[...]).astype(BF16)
    h_out[...] = h
    tm = h.shape[0]
    pr = lax.broadcasted_iota(I32, (LANES, LANES), 0)
    pc = lax.broadcasted_iota(I32, (LANES, LANES), 1)
    avg_pair = jnp.where((pr < FOX_HEAD_DIM) == (pc < FOX_HEAD_DIM), 1.0 / FOX_HEAD_DIM, 0.0).astype(BF16)

    def head_norm(x):
        outs = []
        for pair in range(FOX_HEADS // 2):
            xp = x[:, LANES * pair:LANES * (pair + 1)]
            ms = _dot((xp * xp).astype(BF16), avg_pair)
            outs.append(xp * lax.rsqrt(ms + EPS))
        return outs

    def slots(normed, g_ref, bias):
        parts = []
        for pair in range(FOX_HEADS // 2):
            lanes = slice(LANES * pair, LANES * (pair + 1))
            parts += [normed[pair] * g_ref[:, lanes], bias[:, lanes]]
        return jnp.concatenate(parts, axis=1).astype(BF16)

    lane = lax.broadcasted_iota(I32, (tm, LANES), 1)
    logf = jnp.where(lane < FOX_HEADS, _log_sigmoid(_dot(h, wf_ref[...]) + bf_ref[...]), 0.0)
    cs = _dot(_tri_incl(tm), _pack_pieces(logf))
    cum = cs + pltpu.roll(cs, LANES - FOX_HEADS, axis=1) + pltpu.roll(cs, LANES - 2 * FOX_HEADS, axis=1)
    cum = jnp.where(lane < FOX_HEADS, cum, 0.0) + carry_ref[...]
    carry_ref[...] = cum[tm - 1:tm, :]
    pieces = _pack_pieces(cum * LOG2E)

    q_out[...] = slots(head_norm(_dot(h, wq_ref[...])), gq_ref, _dot(pieces, eq_ref[...]) + cq_ref[...])
    k_out[...] = slots(head_norm(_dot(h, wk_ref[...])), gk_ref, _dot(pieces, ek_ref[...]) + ck_ref[...])
    vt = _dot_nt(wvt_ref[...], h)
    ones = jnp.ones((FOX_HEAD_DIM, tm), F32)
    slots = []
    for hd in range(FOX_HEADS):
        slots += [vt[FOX_HEAD_DIM * hd:FOX_HEAD_DIM * (hd + 1), :], ones]
    vt_out[...] = jnp.concatenate(slots, axis=0).astype(BF16)


def _fox_constants():
    width = FOX_HEADS // 2 * LANES
    eq = np.zeros((LANES, width), np.float32)
    ek = np.zeros((LANES, width), np.float32)
    cq = np.zeros((1, width), np.float32)
    ck = np.zeros((1, width), np.float32)
    for hd in range(FOX_HEADS):
        base = LANES * (hd // 2) + FOX_BIAS_STRIDE * (hd % 2)
        for p in range(FOX_BIAS_PIECES):
            eq[p * FOX_HEADS + hd, base + p] = 1.0
            cq[0, base + FOX_BIAS_PIECES + p] = 1.0
            ck[0, base + p] = 1.0
            ek[p * FOX_HEADS + hd, base + FOX_BIAS_PIECES + p] = -1.0
    return jnp.asarray(eq, BF16), jnp.asarray(ek, BF16), jnp.asarray(cq), jnp.asarray(ck)


def _fox_head_masks():
    m = np.zeros((2, FOX_SLOT), np.float32)
    for a in range(2):
        m[a, FOX_HEAD_DIM * a:FOX_HEAD_DIM * (a + 1)] = 1.0
        m[a, LANES + FOX_BIAS_STRIDE * a:LANES + FOX_BIAS_STRIDE * a + 2 * FOX_BIAS_PIECES] = 1.0
    return jnp.asarray(m, BF16)


def fox_project(x, norm_g, sc, sh, wq, wk, wv, wf, q_g, k_g, bf):
    b, s, d = x.shape
    tm = min(PROJ_TILE, s)
    hp = FOX_HEADS // 2 * FOX_SLOT
    hw = FOX_HEADS * FOX_HEAD_DIM
    wq_p = wq.astype(BF16)
    wk_p = wk.astype(BF16)
    wf_p = jnp.pad(wf, ((0, 0), (0, LANES - FOX_HEADS))).astype(BF16)
    gq = jnp.tile(q_g * (FOX_HEAD_DIM ** -0.5 * LOG2E), FOX_HEADS)[None, :]
    gk = jnp.tile(k_g, FOX_HEADS)[None, :]
    bf_p = jnp.pad(bf, (0, LANES - FOX_HEADS))[None, :]
    eq, ek, cq, ck = _fox_constants()
    const = lambda shape: pl.BlockSpec(shape, lambda i, j: (0,) * len(shape))
    row = lambda n: pl.BlockSpec((None, tm, n), lambda i, j: (i, j, 0))
    per_b = pl.BlockSpec((None, 1, d), lambda i, j: (i, 0, 0))
    return pl.pallas_call(
        _fox_proj_kernel,
        grid=(b, s // tm),
        in_specs=[row(d), const((1, d)), per_b, per_b, const((d, hw)), const((d, hw)), const((BRANCH_WIDTH, d)), const((d, LANES)),
                  const((1, hw)), const((1, hw)), const((1, LANES)), const((LANES, hw)), const((LANES, hw)),
                  const((1, hw)), const((1, hw))],
        out_specs=[row(hp), row(hp), pl.BlockSpec((None, hp, tm), lambda i, j: (i, 0, j)), row(d)],
        out_shape=[jax.ShapeDtypeStruct((b, s, hp), BF16), jax.ShapeDtypeStruct((b, s, hp), BF16),
                   jax.ShapeDtypeStruct((b, hp, s), BF16), jax.ShapeDtypeStruct((b, s, d), BF16)],
        scratch_shapes=[pltpu.VMEM((1, LANES), F32)],
        compiler_params=_cp(("parallel", "arbitrary")),
        name="fox_project",
    )(x, norm_g.reshape(1, d), sc.reshape(b, 1, d), sh.reshape(b, 1, d), wq_p, wk_p, wv.T.astype(BF16), wf_p, gq, gk, bf_p,
      eq, ek, cq, ck)


ATTN_HEADS = 8


def _fox_attn_kernel(q_ref, k_ref, vt_ref, hm_ref, o_ref):
    i = pl.program_id(2)
    t = q_ref.shape[0]
    krow = lax.broadcasted_iota(I32, (t, t), 0)
    qcol = lax.broadcasted_iota(I32, (t, t), 1)
    slot = lambda a: slice(FOX_SLOT * (a // 2), FOX_SLOT * (a // 2 + 1))
    qs = [q_ref[:, slot(a)] * hm_ref[a % 2:a % 2 + 1, :] for a in range(ATTN_HEADS)]

    def scores(j):
        start = pl.multiple_of(j * t, t)
        return tuple(_dot_nt(k_ref[pl.ds(start, t), slot(a)], qs[a]) for a in range(ATTN_HEADS))

    def consume(j, state, ss, masked):
        start = pl.multiple_of(j * t, t)
        new = []
        for a in range(ATTN_HEADS):
            m, acc = state[a]
            s = jnp.where(krow <= qcol, ss[a], NEG) if masked else ss[a]
            m_new = jnp.maximum(m, jnp.max(s, axis=0, keepdims=True))
            p = jnp.exp2(s - m_new)
            alpha = jnp.exp2(m - m_new)
            vt = vt_ref[LANES * a:LANES * (a + 1), pl.ds(start, t)]
            acc = alpha * acc + _dot(vt, p.astype(BF16))
            new.append((m_new, acc))
        return tuple(new)

    def body(j, state):
        return consume(j, state, scores(j), False)

    init = tuple((jnp.full((1, t), NEG, F32), jnp.zeros((LANES, t), F32)) for _ in range(ATTN_HEADS))
    state = lax.fori_loop(0, i, body, init)
    state = consume(i, state, scores(i), True)
    for p in range(ATTN_HEADS // 2):
        halves = []
        for _, acc in (state[2 * p], state[2 * p + 1]):
            halves.append(acc[:FOX_HEAD_DIM, :] / acc[FOX_HEAD_DIM:FOX_HEAD_DIM + 1, :])
        o_ref[:, LANES * p:LANES * (p + 1)] = jnp.concatenate(halves, axis=0).T.astype(BF16)


def fox_attention(qp, kp, vt):
    b, s, hp = qp.shape
    t = ATTN_TILE
    groups = FOX_HEADS // ATTN_HEADS
    return pl.pallas_call(
        _fox_attn_kernel,
        grid=(b, groups, s // t),
        in_specs=[
            pl.BlockSpec((None, t, ATTN_HEADS * LANES), lambda bi, p, i: (bi, i, p)),
            pl.BlockSpec((None, s, ATTN_HEADS * LANES), lambda bi, p, i: (bi, 0, p)),
            pl.BlockSpec((None, ATTN_HEADS * LANES, s), lambda bi, p, i: (bi, p, 0)),
            pl.BlockSpec((2, FOX_SLOT), lambda bi, p, i: (0, 0)),
        ],
        out_specs=pl.BlockSpec((None, t, ATTN_HEADS // 2 * LANES), lambda bi, p, i: (bi, i, p)),
        out_shape=jax.ShapeDtypeStruct((b, s, BRANCH_WIDTH), BF16),
        compiler_params=_cp(("parallel", "parallel", "arbitrary")),
        name="fox_attention",
    )(qp, kp, vt, _fox_head_masks())


CONV_HALO = SUBLANES


def _ml_proj_kernel(h_ref, wqk_ref, wvt_ref, wo_ref, wg_ref, conv_ref, gb_ref, q_out, k_out, vt_out, og_out, g_out, buf_ref):
    tm = h_ref.shape[0]
    half = ML_HEADS * ML_DQK

    @pl.when(pl.program_id(1) == 0)
    def _():
        buf_ref[0:CONV_HALO, :] = jnp.zeros((CONV_HALO, 2 * half), F32)

    h = h_ref[...]
    buf_ref[CONV_HALO:CONV_HALO + tm, :] = _dot(h, wqk_ref[...])
    acc = jnp.zeros((tm, 2 * half), F32)
    for j in range(ML_CONV):
        off = CONV_HALO - (ML_CONV - 1) + j
        acc = acc + conv_ref[j:j + 1, :] * buf_ref[off:off + tm, :]
    buf_ref[0:CONV_HALO, :] = buf_ref[tm:tm + CONV_HALO, :]
    act = _silu(acc)
    q_out[...] = act[:, :half].astype(BF16)
    k_out[...] = (act[:, half:] * (ML_DQK ** -0.5)).astype(BF16)
    vt_out[...] = _dot_nt(wvt_ref[...], h).astype(BF16)
    og_out[...] = jax.nn.sigmoid(_dot(h, wo_ref[...])).astype(BF16)
    g = _dot(h, wg_ref[...]) + gb_ref[...]
    lane = lax.broadcasted_iota(I32, (tm, LANES), 1)
    g_out[...] = jnp.where(lane < ML_HEADS, g, _log_sigmoid(g))


def ml_project(h, wqk, wv, wi, wf, wo, conv, bi, bf):
    b, s, d = h.shape
    tm = min(PROJ_TILE, s)
    half = ML_HEADS * ML_DQK
    wqk_p = wqk.astype(BF16)
    conv_p = conv
    wg = jnp.pad(jnp.concatenate([wi, wf], axis=1), ((0, 0), (0, LANES - 2 * ML_HEADS))).astype(BF16)
    gb = jnp.pad(jnp.concatenate([bi, bf]), (0, LANES - 2 * ML_HEADS))[None, :]
    const = lambda shape: pl.BlockSpec(shape, lambda i, j: (0,) * len(shape))
    row = lambda n: pl.BlockSpec((None, tm, n), lambda i, j: (i, j, 0))
    return pl.pallas_call(
        _ml_proj_kernel,
        grid=(b, s // tm),
        in_specs=[row(d), const((d, 2 * half)), const((BRANCH_WIDTH, d)), const((d, BRANCH_WIDTH)), const((d, LANES)),
                  const((ML_CONV, 2 * half)), const((1, LANES))],
        out_specs=[row(half), row(half), pl.BlockSpec((None, BRANCH_WIDTH, tm), lambda i, j: (i, 0, j)), row(BRANCH_WIDTH), row(LANES)],
        out_shape=[jax.ShapeDtypeStruct((b, s, half), BF16), jax.ShapeDtypeStruct((b, s, half), BF16),
                   jax.ShapeDtypeStruct((b, BRANCH_WIDTH, s), BF16), jax.ShapeDtypeStruct((b, s, BRANCH_WIDTH), BF16),
                   jax.ShapeDtypeStruct((b, s, LANES), F32)],
        scratch_shapes=[pltpu.VMEM((tm + CONV_HALO, 2 * half), F32)],
        compiler_params=_cp(("parallel", "arbitrary")),
        name="ml_project",
    )(h, wqk_p, wv.T.astype(BF16), wo.astype(BF16), wg, conv_p, gb)


def _mlstm_kernel(q_ref, k_ref, vt_ref, og_ref, g_ref, gt_ref, ng_ref, o_ref):
    nb, s = q_ref.shape[0], q_ref.shape[1]
    L = ML_CHUNK
    tril = _tri_incl(L)
    triu = tril.T
    srow = lax.broadcasted_iota(I32, (L, L), 0)
    tcol = lax.broadcasted_iota(I32, (L, L), 1)
    causal = srow <= tcol
    lane = lax.broadcasted_iota(I32, (L, LANES), 1)
    head_lanes = (jnp.where(lane < ML_DQK, 1.0, 0.0).astype(BF16), jnp.where(lane >= ML_DQK, 1.0, 0.0).astype(BF16))

    def chunk(c2, states):
        states = list(states)
        for u in range(ML_UNROLL):
            for bi in range(nb):
                states[bi] = one_chunk(c2 * ML_UNROLL + u, bi, states[bi])
        return tuple(states)

    def one_chunk(c, bi, state):
        new_state = []
        r0 = pl.multiple_of(c * L, L)
        g = g_ref[bi, pl.ds(r0, L), :]
        gt = gt_ref[bi, :, pl.ds(r0, L)]
        bc = _tri_dot(tril, g)
        br = _dot_tri(gt, triu)
        for hd in range(ML_HEADS):
            sl = slice(LANES * hd, LANES * (hd + 1))
            cst, nst, m_prev = state[hd]
            ccol = g[:, hd:hd + 1] - bc[:, ML_HEADS + hd:ML_HEADS + hd + 1]
            brow = br[ML_HEADS + hd:ML_HEADS + hd + 1, :]
            irow = gt[hd:hd + 1, :]
            log_d = jnp.where(causal, brow + ccol, -jnp.inf)
            log_inter = brow + m_prev
            m_t = jnp.maximum(jnp.max(log_d, axis=0, keepdims=True), log_inter)
            w_intra = jnp.exp(log_d - m_t)
            w_inter = jnp.exp(log_inter - m_t)
            pair = slice(LANES * (hd // 2), LANES * (hd // 2 + 1))
            qc = q_ref[bi, pl.ds(r0, L), pair] * head_lanes[hd % 2]
            kc = k_ref[bi, pl.ds(r0, L), pair]
            vt = vt_ref[bi, sl, pl.ds(r0, L)]
            sc = _dot_nt(kc, qc) * w_intra
            num =_dot(vt, sc.astype(BF16)) + w_inter * _dot_nt(cst.astype(BF16), qc)
            qn = _dot_nt(jnp.broadcast_to(nst, (SUBLANES, LANES)).astype(BF16), qc)[0:1, :]
            den = jnp.sum(sc, axis=0, keepdims=True) + w_inter * qn
            hout = num / jnp.maximum(jnp.abs(den), jnp.exp(-m_t))
            b_last = brow[:, L - 1:L]
            lw = b_last - brow + irow
            m_new = jnp.maximum(b_last + m_prev, jnp.max(lw, axis=1, keepdims=True))
            w_in = jnp.exp(lw - m_new)
            decay = jnp.exp(b_last + m_prev - m_new)
            new_state.append((decay * cst + _dot((vt.astype(F32) * w_in).astype(BF16), kc),
                              decay * nst + _dot(jnp.broadcast_to(w_in, (SUBLANES, L)).astype(BF16), kc)[0:1, :],
                              m_new))
            ms = jnp.mean(hout * hout, axis=0, keepdims=True)
            y = (hout * lax.rsqrt(ms + EPS)).T * (og_ref[bi, pl.ds(r0, L), sl].astype(F32) * ng_ref[:, sl])
            o_ref[bi, pl.ds(r0, L), sl] = y.astype(BF16)
        return tuple(new_state)

    zero = (jnp.zeros((ML_DV, LANES), F32), jnp.zeros((1, LANES), F32), jnp.zeros((1, 1), F32))
    lax.fori_loop(0, s // (L * ML_UNROLL), chunk, tuple(tuple(zero for _ in range(ML_HEADS)) for _ in range(nb)))


ML_SEQS = 2
ML_UNROLL = 2


def mlstm(q, k, vt, og, gates, norm_g):
    b, w, s = vt.shape
    nb = ML_SEQS if b % ML_SEQS == 0 else 1
    gt = jnp.swapaxes(gates[:, :, :2 * ML_HEADS], 1, 2)
    seq = lambda n: pl.BlockSpec((nb, s, n), lambda i: (i, 0, 0))
    return pl.pallas_call(
        _mlstm_kernel,
        grid=(b // nb,),
        in_specs=[seq(ML_HEADS * ML_DQK), seq(ML_HEADS * ML_DQK), pl.BlockSpec((nb, w, s), lambda i: (i, 0, 0)), seq(w), seq(LANES),
                  pl.BlockSpec((nb, 2 * ML_HEADS, s), lambda i: (i, 0, 0)),
                  pl.BlockSpec((1, w), lambda i: (0, 0))],
        out_specs=seq(w),
        out_shape=jax.ShapeDtypeStruct((b, s, w), BF16),
        compiler_params=_cp(("parallel",)),
        name="mlstm",
    )(q, k, vt, og, gates, gt, norm_g.reshape(1, w))


def _hg_proj_kernel(h_ref, wf_ref, wq_ref, wi_ref, wg_ref, lb_ref, q_out, k_out, v_out, og_out, lf_out):
    h = h_ref[...]
    fz = _dot(h, wf_ref[...])
    log_lb = lb_ref[0:1, :]
    log_1m = lb_ref[1:2, :]
    one_m = lb_ref[2:3, :]
    u = jnp.exp(-jnp.abs(fz))
    a = log_lb
    bb = log_1m + (jnp.minimum(fz, 0.0) - jnp.log1p(u))
    lf_out[...] = jnp.maximum(a, bb) + jnp.log1p(jnp.exp(-jnp.abs(a - bb)))
    k_out[...] = (one_m * (jnp.where(fz >= 0.0, u, 1.0) / (1.0 + u))).astype(BF16)
    q_out[...] = _silu(_dot(h, wq_ref[...])).astype(BF16)
    v_out[...] = _dot(h, wi_ref[...]).astype(BF16)
    og_out[...] = _silu(_dot(h, wg_ref[...])).astype(BF16)


def hg_project(h, wf, wq, wi, wg, lb):
    b, s, d = h.shape
    tm = min(PROJ_TILE, s)
    w = BRANCH_WIDTH
    lbp = jnp.stack([jnp.log(lb), jnp.log1p(-lb), 1.0 - lb], axis=0)
    const = lambda shape: pl.BlockSpec(shape, lambda i, j: (0,) * len(shape))
    row = lambda n: pl.BlockSpec((None, tm, n), lambda i, j: (i, j, 0))
    return pl.pallas_call(
        _hg_proj_kernel,
        grid=(b, s // tm),
        in_specs=[row(d), const((d, w)), const((d, w)), const((d, w)), const((d, w)), const((3, w))],
        out_specs=[row(w), row(w), row(w), row(w), row(w)],
        out_shape=[jax.ShapeDtypeStruct((b, s, w), BF16)] * 4 + [jax.ShapeDtypeStruct((b, s, w), F32)],
        compiler_params=_cp(("parallel", "parallel")),
        name="hg_project",
    )(h, wf.astype(BF16), wq.astype(BF16), wi.astype(BF16), wg.astype(BF16), lbp)


HG_UNROLL = 4
HG_LEVELS = tuple(HG_CHUNK >> (i + 1) for i in range(HG_CHUNK.bit_length() - 1))


def _hg_tables():
    L = HG_CHUNK
    t = np.arange(L)
    tri = (t[None, :] <= t[:, None]).astype(np.float32)
    mats = [tri]
    x = t[:, None] ^ t[None, :]
    lvl = np.full((L, L), -1, np.int32)
    lvl[t[:, None] == t[None, :]] = 0
    for i, m in enumerate(HG_LEVELS):
        if m < SUBLANES:
            mats.append(tri[(t // (2 * m)) * (2 * m) + m - 1])
        lvl[(t[:, None] > t[None, :]) & (x >= m) & (x < 2 * m)] = i + 1
    return jnp.asarray(np.concatenate(mats, axis=0), BF16), jnp.asarray(lvl)


def _hgrn_kernel(q_ref, k_ref, v_ref, og_ref, lf_ref, ng_ref, tall_ref, lvl_ref, o_ref, st_ref):
    s, wd = q_ref.shape
    L = HG_CHUNK
    st_ref[...] = jnp.zeros_like(st_ref)
    rowi = lax.broadcasted_iota(I32, (L, LANES), 0)

    lvl = lvl_ref[...]
    level_masks = [lvl == i for i in range(len(HG_LEVELS) + 1)]

    def chunk(c, carry):
        r0 = pl.multiple_of(c * L, L)
        tall = tall_ref[...]
        g = lf_ref[pl.ds(r0, L), :]
        hi = g.astype(BF16)
        mid = (g - hi.astype(F32)).astype(BF16)
        cums = (_dot(tall, hi) + _dot(tall, mid)) * LOG2E
        a = cums[0:L]
        qb = q_ref[pl.ds(r0, L), :]
        kb = k_ref[pl.ds(r0, L), :]
        qf = qb.astype(F32)
        kf = kb.astype(F32)
        ws = []
        fine = 0
        for m in HG_LEVELS:
            if m >= SUBLANES:
                ref = jnp.concatenate([jnp.broadcast_to(a[g0 + m - 1:g0 + m, :], (2 * m, wd)) for g0 in range(0, L, 2 * m)], axis=0)
            else:
                fine += 1
                ref = cums[L * fine:L * (fine + 1)]
            e = jnp.exp2(-jnp.abs(a - ref))
            upper = (rowi & m) != 0
            qk = jnp.concatenate([jnp.where(upper, qf[:, LANES * hd:LANES * (hd + 1)], kf[:, LANES * hd:LANES * (hd + 1)])
                                  for hd in range(HG_HEADS)], axis=1)
            ws.append((e * qk).astype(BF16))
        a_last = a[L - 1:L, :]
        qa = (qf * jnp.exp2(a)).astype(BF16)
        kt = (kf * jnp.exp2(a_last - a)).astype(BF16)
        decay = jnp.exp2(a_last)
        for hd in range(HG_HEADS):
            sl = slice(LANES * hd, LANES * (hd + 1))
            vb = v_ref[pl.ds(r0, L), sl]
            sc = jnp.where(level_masks[0], _dot_nt(qb[:, sl], kb[:, sl]), 0.0)
            for i in range(len(HG_LEVELS)):
                w = ws[i][:, sl]
                sc = jnp.where(level_masks[i + 1], _dot_nt(w, w), sc)
            st = st_ref[hd]
            out = _dot(sc.astype(BF16), vb) + _dot_nt(qa[:, sl], st.astype(BF16))
            st_ref[hd] = st * decay[:, sl] + _dot_tn(vb, kt[:, sl])
            ms = jnp.mean(out * out, axis=-1, keepdims=True)
            y = (out * lax.rsqrt(ms + EPS)) * ng_ref[:, sl] * og_ref[pl.ds(r0, L), sl].astype(F32)
            o_ref[pl.ds(r0, L), sl] = y.astype(BF16)
        return carry

    def chunks(c2, carry):
        for u in range(HG_UNROLL):
            chunk(c2 * HG_UNROLL + u, carry)
        return carry

    lax.fori_loop(0, s // (L * HG_UNROLL), chunks, 0)


def hgrn(q, k, v, og, logf, norm_g):
    b, s, w = v.shape
    tall, lvl = _hg_tables()
    seq = pl.BlockSpec((None, s, w), lambda i: (i, 0, 0))
    const = lambda shape: pl.BlockSpec(shape, lambda i: (0,) * len(shape))
    return pl.pallas_call(
        _hgrn_kernel,
        grid=(b,),
        in_specs=[seq, seq, seq, seq, seq, const((1, w)), const(tall.shape), const(lvl.shape)],
        out_specs=seq,
        out_shape=jax.ShapeDtypeStruct((b, s, w), BF16),
        scratch_shapes=[pltpu.VMEM((HG_HEADS, LANES, HG_DK), F32)],
        compiler_params=_cp(("parallel",)),
        name="hgrn",
    )(q, k, v, og, logf, norm_g.reshape(1, w), tall, lvl)


def _merge_kernel(x_ref, h_ref, yf_ref, ym_ref, yh_ref, wg_ref, wb_ref, wo_ref, g1_ref, n2_ref, sc2_ref, sh2_ref, wr_ref, rb_ref,
                  x_out, h2_out, dest_out, w_out, cnt_out):
    d = x_ref.shape[1]
    h = h_ref[...]
    merged = None
    for br, y_ref in enumerate((yf_ref, ym_ref, yh_ref)):
        gate = jax.nn.sigmoid(_dot(h, wg_ref[:, d * br:d * (br + 1)]))
        term = gate * _dot(y_ref[...], wb_ref[br])
        merged = term if merged is None else merged + term
    mixed = _dot(merged.astype(BF16), wo_ref[...])
    x1 = x_ref[...] + g1_ref[...] * mixed
    x_out[...] = x1
    h2 = _norm_mod(x1, n2_ref[...], sc2_ref[...], sh2_ref[...]).astype(BF16)
    h2_out[...] = h2
    for u in range(h2.shape[0] // SUB_TOKENS):
        _route(h2[SUB_TOKENS * u:SUB_TOKENS * (u + 1), :], wr_ref[...], rb_ref[...], dest_out.at[u], w_out.at[u], cnt_out.at[u])


def merge_branches(x, h, y_fox, y_ml, y_hg, w_gates, w_branch, w_out, g1, norm2_g, sc2, sh2, router_w, router_bias):
    b, s, d = x.shape
    tm = MERGE_TILE
    per_seq = s // tm
    per_tile = tm // SUB_TOKENS
    nsub = b * s // SUB_TOKENS
    w = BRANCH_WIDTH
    mat = pl.BlockSpec((per_tile, 2 * N_EXPERTS, SUB_TOKENS), lambda i, j: (i * per_seq + j, 0, 0))
    const = lambda shape: pl.BlockSpec(shape, lambda i, j: (0,) * len(shape))
    row = lambda n: pl.BlockSpec((None, tm, n), lambda i, j: (i, j, 0))
    per_b = pl.BlockSpec((None, 1, d), lambda i, j: (i, 0, 0))
    x1, h2, dest, wts, cnt = pl.pallas_call(
        _merge_kernel,
        grid=(b, s // tm),
        in_specs=[row(d), row(d), row(w), row(w), row(w), const((d, N_BRANCH * d)), const((N_BRANCH, w, d)), const((d, d)),
                  per_b, const((1, d)), per_b, per_b, const((N_EXPERTS, d)), const((N_EXPERTS, 1))],
        out_specs=[row(d), row(d), mat, mat, pl.BlockSpec((per_tile, N_EXPERTS, LANES), lambda i, j: (i * per_seq + j, 0, 0))],
        out_shape=[jax.ShapeDtypeStruct((b, s, d), F32), jax.ShapeDtypeStruct((b, s, d), BF16),
                   jax.ShapeDtypeStruct((nsub, 2 * N_EXPERTS, SUB_TOKENS), BF16),
                   jax.ShapeDtypeStruct((nsub, 2 * N_EXPERTS, SUB_TOKENS), BF16),
                   jax.ShapeDtypeStruct((nsub, N_EXPERTS, LANES), I32)],
        compiler_params=_cp(("parallel", "parallel"), 56),
        name="merge_branches",
    )(x, h, y_fox, y_ml, y_hg, w_gates.astype(BF16), w_branch.astype(BF16), w_out.astype(BF16),
      g1.reshape(b, 1, d), norm2_g.reshape(1, d), sc2.reshape(b, 1, d), sh2.reshape(b, 1, d),
      router_w.T.astype(BF16), router_bias.reshape(N_EXPERTS, 1))
    return x1, h2, dest, wts, cnt[:, :, 0]


def _first_max(x, iota, size):
    m = jnp.max(x, axis=0, keepdims=True)
    idx = jnp.min(jnp.where(x == m, iota, size), axis=0, keepdims=True)
    return m, idx


def _route(h, wr, rb, dest_out, w_out, cnt_out):
    n = h.shape[0]
    scores = jax.nn.sigmoid(_dot_nt(wr, h))
    choice = scores + rb
    e_iota = lax.broadcasted_iota(I32, (N_EXPERTS, n), 0)
    c3 = choice.reshape(N_GROUPS, GROUP_SIZE, n)
    i3 = lax.broadcasted_iota(I32, (N_GROUPS, GROUP_SIZE, n), 1)
    m1 = jnp.max(c3, axis=1, keepdims=True)
    i1 = jnp.min(jnp.where(c3 == m1, i3, GROUP_SIZE), axis=1, keepdims=True)
    m2 = jnp.max(jnp.where(i3 == i1, -jnp.inf, c3), axis=1, keepdims=True)
    gs = (m1 + m2).reshape(N_GROUPS, n)
    g_iota = lax.broadcasted_iota(I32, (N_GROUPS, n), 0)
    gsel = jnp.zeros((N_GROUPS, n), F32)
    for _ in range(TOPK_GROUPS):
        _, gi = _first_max(gs, g_iota, N_GROUPS)
        hit = g_iota == gi
        gsel = jnp.where(hit, 1.0, gsel)
        gs = jnp.where(hit, -jnp.inf, gs)
    gmask = jnp.broadcast_to(gsel.reshape(N_GROUPS, 1, n), (N_GROUPS, GROUP_SIZE, n)).reshape(N_EXPERTS, n)
    masked = jnp.where(gmask > 0.0, choice, -jnp.inf)
    sel = jnp.zeros((N_EXPERTS, n), F32)
    for _ in range(TOP_K):
        _, ei = _first_max(masked, e_iota, N_EXPERTS)
        hit = e_iota == ei
        sel = jnp.where(hit, 1.0, sel)
        masked = jnp.where(hit, -jnp.inf, masked)
    tr = lax.broadcasted_iota(I32, (n, n), 0)
    tc = lax.broadcasted_iota(I32, (n, n), 1)
    before = jnp.where(tr < tc, 1.0, 0.0).astype(BF16)
    pos = _dot(sel.astype(BF16), before)
    cnt = jnp.sum(sel, axis=1, keepdims=True)
    units = jnp.floor((cnt + (RUN_ALIGN - 1)) * (1.0 / RUN_ALIGN))
    er = lax.broadcasted_iota(I32, (N_EXPERTS, N_EXPERTS), 0)
    ec = lax.broadcasted_iota(I32, (N_EXPERTS, N_EXPERTS), 1)
    lower = jnp.where(ec < er, 1.0, 0.0).astype(BF16)
    off = _dot(lower, jnp.broadcast_to(units, (N_EXPERTS, LANES)).astype(BF16))[:, 0:1] * RUN_ALIGN
    dest = jnp.where(sel > 0.0, off + pos, float(DEST_NONE))
    dhi = jnp.floor(dest * (1.0 / DEST_RADIX))
    dest_out[0:N_EXPERTS, :] = dhi.astype(BF16)
    dest_out[N_EXPERTS:, :] = (dest - dhi * DEST_RADIX).astype(BF16)
    wsum = jnp.sum(scores * sel, axis=0, keepdims=True)
    w_out[0:N_EXPERTS, :] = jnp.zeros((N_EXPERTS, n), BF16)
    w_out[N_EXPERTS:, :] = (scores * sel / wsum * ROUTE_SCALE).astype(BF16)
    cnt_out[...] = jnp.broadcast_to(cnt, (N_EXPERTS, LANES)).astype(I32)


def _run_tables(cnt, n_blocks):
    units = (cnt + (RUN_ALIGN - 1)) // RUN_ALIGN
    src = jnp.cumsum(units, axis=1) - units
    per_block = MOE_BLOCK // RUN_ALIGN
    tot = jnp.sum(units, axis=0)
    tot_blocks = (tot + per_block - 1) // per_block
    blk_end = jnp.cumsum(tot_blocks)
    base = (blk_end - tot_blocks) * per_block
    dst = base[None, :] + jnp.cumsum(units, axis=0) - units
    n_used = blk_end[-1]
    tail_units = tot_blocks * per_block - tot
    tail_dst = base + tot
    blk = jnp.minimum(jnp.arange(n_blocks), n_used - 1)
    blk_exp = jnp.minimum(jnp.sum(blk[:, None] >= blk_end[None, :], axis=1), N_EXPERTS - 1)
    return (src.reshape(-1).astype(I32), dst.reshape(-1).astype(I32), units.reshape(-1).astype(I32),
            blk_exp.astype(I32), n_used.astype(I32).reshape(1), tail_dst.astype(I32), tail_units.astype(I32),
            jnp.sum(units, axis=1).astype(I32))


def _copy_lists(src, dst, units, nsub):
    s0 = src.reshape(nsub, 1, N_EXPERTS)
    d0 = dst.reshape(nsub, 1, N_EXPERTS)
    u = units.reshape(nsub, 1, N_EXPERTS)
    npair = u // 2
    poff = jnp.cumsum(npair, axis=2) - npair
    p = jnp.arange(SUB_UNITS // 2, dtype=I32).reshape(1, -1, 1)
    own = (p >= poff) & (p < poff + npair)
    psrc = jnp.sum(jnp.where(own, s0 - 2 * poff, 0), axis=2) + 2 * p[:, :, 0]
    pdst = jnp.sum(jnp.where(own, d0 - 2 * poff, 0), axis=2) + 2 * p[:, :, 0]
    odd = u & 1
    soff = jnp.cumsum(odd, axis=2) - odd
    q = jnp.arange(N_EXPERTS, dtype=I32).reshape(1, -1, 1)
    owns = (odd == 1) & (soff == q)
    ssrc = jnp.sum(jnp.where(owns, s0 + u - 1, 0), axis=2)
    sdst = jnp.sum(jnp.where(owns, d0 + u - 1, 0), axis=2)
    flat = lambda a: a.reshape(-1).astype(I32)
    return (flat(psrc), flat(pdst), flat(jnp.sum(npair, axis=2)), flat(ssrc), flat(sdst), flat(jnp.sum(odd, axis=2)))


def _run_bounds(src, units, nsub):
    lo = (src.reshape(nsub, N_EXPERTS) * RUN_ALIGN).astype(F32)
    hi = lo + (units.reshape(nsub, N_EXPERTS) * RUN_ALIGN).astype(F32)
    return jnp.stack([jnp.concatenate([lo, lo], axis=1), jnp.concatenate([hi, hi], axis=1)], axis=1)


def _max_blocks(t):
    nsub = t // SUB_TOKENS
    worst_units = t * TOP_K // RUN_ALIGN + nsub * N_EXPERTS
    per_block = MOE_BLOCK // RUN_ALIGN
    return -(-worst_units // per_block) + N_EXPERTS


SUB_UNITS = SUB_ROWS // RUN_ALIGN
COPY_UNROLL = 4


PAIR_SLOTS = SUB_UNITS // 2


def _piece_copies(lists, step, buf, hbm, sem, to_hbm):
    psrc_ref, pdst_ref, npair_ref, ssrc_ref, sdst_ref, nsingle_ref = lists

    def piece(src_ref, dst_ref, idx, units):
        rows = units * RUN_ALIGN
        v = buf.at[pl.ds(pl.multiple_of(src_ref[idx] * RUN_ALIGN, RUN_ALIGN), rows)]
        g = hbm.at[pl.ds(pl.multiple_of(dst_ref[idx] * RUN_ALIGN, RUN_ALIGN), rows)]
        cp = pltpu.make_async_copy(v, g, sem) if to_hbm else pltpu.make_async_copy(g, v, sem)
        cp.start()

    def issue(src_ref, dst_ref, base, count, units):
        def group(q, carry):
            for r in range(COPY_UNROLL):
                piece(src_ref, dst_ref, base + q * COPY_UNROLL + r, units)
            return carry

        groups = lax.shift_right_logical(count, COPY_UNROLL.bit_length() - 1)
        lax.fori_loop(0, groups, group, 0)
        lax.fori_loop(groups * COPY_UNROLL, count, lambda j, c: (piece(src_ref, dst_ref, base + j, units), c)[1], 0)

    issue(psrc_ref, pdst_ref, step * PAIR_SLOTS, npair_ref[step], 2)
    issue(ssrc_ref, sdst_ref, step * N_EXPERTS, nsingle_ref[step], 1)


TOTAL_BITS = tuple(1 << b for b in range(SUB_UNITS.bit_length()))


def _wait_runs(total_units, buf, hbm, sem, to_hbm):
    for bit in TOTAL_BITS:
        @pl.when((total_units & bit) != 0)
        def _():
            rows = bit * RUN_ALIGN
            v = buf.at[pl.ds(0, rows)]
            g = hbm.at[pl.ds(0, rows)]
            cp = pltpu.make_async_copy(v, g, sem) if to_hbm else pltpu.make_async_copy(g, v, sem)
            cp.wait()


def _dispatch_kernel(psrc_ref, pdst_ref, npair_ref, ssrc_ref, sdst_ref, nsingle_ref, tot_ref, tdst_ref, tunits_ref, nused_ref,
                     h_ref, dest_ref, lohi_ref, xs_out, buf_ref, zero_ref, sem):
    lists = (psrc_ref, pdst_ref, npair_ref, ssrc_ref, sdst_ref, nsingle_ref)
    i = pl.program_id(0)
    nsub = pl.num_programs(0)
    slot = i % 2
    n = h_ref.shape[0]
    chunk = SORT_CHUNK

    for sl in range(2):
        @pl.when((slot == sl) & (i >= 2))
        def _():
            _wait_runs(tot_ref[i - 2], buf_ref.at[sl], xs_out, sem.at[sl], True)

    h = h_ref[...]
    dest = dest_ref[...]
    lo = lohi_ref[0:1, :]
    hi = lohi_ref[1:2, :]
    radix = jnp.where(lax.broadcasted_iota(I32, (1, 2 * N_EXPERTS), 1) < N_EXPERTS, float(DEST_RADIX), 1.0)
    r_e = lax.broadcasted_iota(I32, (chunk, 2 * N_EXPERTS), 0).astype(F32)
    r_t = lax.broadcasted_iota(I32, (chunk, n), 0).astype(F32)
    for sl in range(2):
        @pl.when(slot == sl)
        def _():
            for c in range(SUB_ROWS // chunk):
                own = jnp.where(r_e + c * chunk >= lo, jnp.where(r_e + c * chunk < hi, radix, 0.0), 0.0)
                row_of = _dot(own.astype(BF16), dest)
                p = jnp.where(row_of == r_t + c * chunk, 1.0, 0.0)
                buf_ref[sl, c * chunk:(c + 1) * chunk, :] = _dot(p.astype(BF16), h).astype(BF16)
            _piece_copies(lists, i, buf_ref.at[sl], xs_out, sem.at[sl], True)

    @pl.when(i == nsub - 1)
    def _():
        zero_ref[...] = jnp.zeros_like(zero_ref)

        def tails(e, wait):
            u = tunits_ref[e]
            d0 = tdst_ref[e]
            for bit in RUN_BITS:
                low = u & (bit - 1)

                @pl.when((u & bit) != 0)
                def _():
                    rows = bit * RUN_ALIGN
                    cp = pltpu.make_async_copy(zero_ref.at[pl.ds(0, rows)],
                                               xs_out.at[pl.ds(pl.multiple_of((d0 + low) * RUN_ALIGN, RUN_ALIGN), rows)], sem.at[2])
                    if wait:
                        cp.wait()
                    else:
                        cp.start()
            return wait

        def unused(b, wait):
            cp = pltpu.make_async_copy(zero_ref.at[pl.ds(0, MOE_BLOCK)],
                                       xs_out.at[pl.ds(pl.multiple_of(b * MOE_BLOCK, MOE_BLOCK), MOE_BLOCK)], sem.at[2])
            if wait:
                cp.wait()
            else:
                cp.start()
            return wait

        n_blocks = xs_out.shape[0] // MOE_BLOCK
        lax.fori_loop(0, N_EXPERTS, lambda e, c: (tails(e, False), c)[1], 0)
        lax.fori_loop(nused_ref[0], n_blocks, lambda b, c: (unused(b, False), c)[1], 0)
        for sl in range(2):
            @pl.when((slot != sl) & (i >= 1))
            def _():
                _wait_runs(tot_ref[i - 1], buf_ref.at[sl], xs_out, sem.at[sl], True)

            @pl.when(slot == sl)
            def _():
                _wait_runs(tot_ref[i], buf_ref.at[sl], xs_out, sem.at[sl], True)
        lax.fori_loop(0, N_EXPERTS, lambda e, c: (tails(e, True), c)[1], 0)
        lax.fori_loop(nused_ref[0], n_blocks, lambda b, c: (unused(b, True), c)[1], 0)


def moe_dispatch(h2, dest, tables, n_rows):
    t, d = h2.shape
    n = SUB_TOKENS
    nsub = t // n
    src, dst, units, _, n_used, tail_dst, tail_units, tot = tables
    grid_spec = pltpu.PrefetchScalarGridSpec(
        num_scalar_prefetch=10,
        grid=(nsub,),
        in_specs=[pl.BlockSpec((n, d), lambda i, *_: (i, 0)), pl.BlockSpec((None, 2 * N_EXPERTS, n), lambda i, *_: (i, 0, 0)),
                  pl.BlockSpec((None, 2, 2 * N_EXPERTS), lambda i, *_: (i, 0, 0))],
        out_specs=pl.BlockSpec(memory_space=pl.ANY),
        scratch_shapes=[pltpu.VMEM((2, SUB_ROWS, d), BF16), pltpu.VMEM((max(SUB_TOKENS, MOE_BLOCK), d), BF16), pltpu.SemaphoreType.DMA((3,))],
    )
    return pl.pallas_call(
        _dispatch_kernel,
        grid_spec=grid_spec,
        out_shape=jax.ShapeDtypeStruct((n_rows, d), BF16),
        compiler_params=_cp(("arbitrary",)),
        name="moe_dispatch",
    )(*_copy_lists(src, dst, units, nsub), tot, tail_dst, tail_units, n_used, h2, dest, _run_bounds(src, units, nsub))


X_SLOTS = 3


def _expert_kernel(blk_exp_ref, n_used_ref, x_hbm, w1_ref, w3_ref, w2_ref, y_ref, xbuf, sem):
    b = pl.program_id(0)
    n_used = n_used_ref[0]

    def x_copy(blk, slot):
        rows = pl.ds(pl.multiple_of(blk * MOE_BLOCK, MOE_BLOCK), MOE_BLOCK)
        return pltpu.make_async_copy(x_hbm.at[rows], xbuf.at[slot], sem.at[slot])

    @pl.when(b == 0)
    def _():
        x_copy(0, 0).start()

        @pl.when(n_used > 1)
        def _():
            x_copy(1, 1).start()

    ahead = b + (X_SLOTS - 1)

    @pl.when(ahead < n_used)
    def _():
        x_copy(ahead, lax.rem(ahead, X_SLOTS)).start()

    @pl.when(b < n_used)
    def _():
        slot = lax.rem(b, X_SLOTS)
        x_copy(b, slot).wait()
        x = xbuf[slot]
        hid = _silu(_dot(x, w1_ref[...].astype(BF16))) * _dot(x, w3_ref[...].astype(BF16))
        y_ref[...] = _dot(hid.astype(BF16), w2_ref[...].astype(BF16)).astype(BF16)

    @pl.when(b >= n_used)
    def _():
        y_ref[...] = jnp.zeros_like(y_ref)


def moe_experts(xs, w1, w3, w2, layer, tables, n_blocks):
    n_rows, d = xs.shape
    blk_exp, n_used = tables[3], tables[4]
    f = w1.shape[-1]

    def w_map(b, be, nu):
        return (layer, be[b], 0, 0)

    grid_spec = pltpu.PrefetchScalarGridSpec(
        num_scalar_prefetch=2,
        grid=(n_blocks,),
        in_specs=[pl.BlockSpec(memory_space=pl.ANY), pl.BlockSpec((None, None, d, f), w_map),
                  pl.BlockSpec((None, None, d, f), w_map), pl.BlockSpec((None, None, f, d), w_map)],
        out_specs=pl.BlockSpec((MOE_BLOCK, d), lambda b, be, nu: (b, 0)),
        scratch_shapes=[pltpu.VMEM((X_SLOTS, MOE_BLOCK, d), BF16), pltpu.SemaphoreType.DMA((X_SLOTS,))],
    )
    return pl.pallas_call(
        _expert_kernel,
        grid_spec=grid_spec,
        out_shape=jax.ShapeDtypeStruct((n_rows, d), BF16),
        compiler_params=_cp(("arbitrary",)),
        name="moe_experts",
    )(blk_exp, n_used, xs, w1, w3, w2)


def _combine_kernel(psrc_ref, pdst_ref, npair_ref, ssrc_ref, sdst_ref, nsingle_ref, tot_ref, ys_ref, dcol_ref, lohi_ref,
                    h_ref, x_ref, g2_ref, ws1_ref, ws3_ref, ws2_ref, o_ref, buf_ref, sem):
    lists = (psrc_ref, pdst_ref, npair_ref, ssrc_ref, sdst_ref, nsingle_ref)
    i = pl.program_id(0)
    nsub = pl.num_programs(0)
    slot = i % 2
    n = h_ref.shape[0]
    chunk = 512

    def fetch(step, sl):
        always = SUB_TOKENS * TOP_K
        buf_ref[sl, always:, :] = jnp.zeros((SUB_ROWS - always, buf_ref.shape[2]), BF16)
        _piece_copies(lists, step, buf_ref.at[sl], ys_ref, sem.at[sl], False)

    @pl.when(i == 0)
    def _():
        fetch(0, 0)

    for sl in range(2):
        @pl.when((slot != sl) & (i + 1 < nsub))
        def _():
            fetch(i + 1, sl)

    h = h_ref[...]
    shared = _dot((_silu(_dot(h, ws1_ref[...])) * _dot(h, ws3_ref[...])).astype(BF16), ws2_ref[...])
    dw_t = dcol_ref[...]
    lo = lohi_ref[:, 0:1]
    hi = lohi_ref[:, 1:2]
    part = lax.broadcasted_iota(I32, (4 * N_EXPERTS, 1), 0)
    radix = jnp.where(part < N_EXPERTS, float(DEST_RADIX),
                      jnp.where(part < 2 * N_EXPERTS, 1.0, jnp.where(part < 3 * N_EXPERTS, WEIGHT_SHIFT, 0.0)))
    r_e = lax.broadcasted_iota(I32, (4 * N_EXPERTS, chunk), 1).astype(F32)
    r_t = lax.broadcasted_iota(I32, (n, chunk), 1).astype(F32)
    for sl in range(2):
        @pl.when(slot == sl)
        def _():
            _wait_runs(tot_ref[i], buf_ref.at[sl], ys_ref, sem.at[sl], False)
            acc = shared
            for c in range(SUB_ROWS // chunk):
                own = jnp.where(r_e + c * chunk >= lo, jnp.where(r_e + c * chunk < hi, radix, 0.0), 0.0).astype(BF16)
                val = _dot(dw_t, own)
                row_of = jnp.floor(val)
                pw = jnp.where(row_of == r_t + c * chunk, (val - row_of) * (1.0 / WEIGHT_SHIFT), 0.0)
                acc = acc + _dot(pw.astype(BF16), buf_ref[sl, c * chunk:(c + 1) * chunk, :])
            o_ref[...] = x_ref[...] + g2_ref[...] * acc


def moe_combine(ys, dest, wts, h2, x1, g2, ws1, ws3, ws2, tables, seq):
    t, d = h2.shape
    n = SUB_TOKENS
    nsub = t // n
    src, dst, units, tot = tables[0], tables[1], tables[2], tables[7]
    dw = jnp.concatenate([dest, wts[:, N_EXPERTS:, :], jnp.zeros((nsub, N_EXPERTS, n), BF16)], axis=1)
    dcol = jnp.swapaxes(dw, 1, 2)
    b2 = _run_bounds(src, units, nsub)
    bounds = jnp.swapaxes(jnp.concatenate([b2, b2], axis=2), 1, 2)
    per_seq = seq // n
    f = ws1.shape[-1]
    pair = pl.BlockSpec((None, n, 4 * N_EXPERTS), lambda i, *_: (i, 0, 0))
    grid_spec = pltpu.PrefetchScalarGridSpec(
        num_scalar_prefetch=7,
        grid=(nsub,),
        in_specs=[pl.BlockSpec(memory_space=pl.ANY), pair,
                  pl.BlockSpec((None, 4 * N_EXPERTS, 2), lambda i, *_: (i, 0, 0)),
                  pl.BlockSpec((n, d), lambda i, *_: (i, 0)), pl.BlockSpec((n, d), lambda i, *_: (i, 0)),
                  pl.BlockSpec((None, 1, d), lambda i, *_: (i // per_seq, 0, 0)),
                  pl.BlockSpec((d, f), lambda i, *_: (0, 0)), pl.BlockSpec((d, f), lambda i, *_: (0, 0)),
                  pl.BlockSpec((f, d), lambda i, *_: (0, 0))],
        out_specs=pl.BlockSpec((n, d), lambda i, *_: (i, 0)),
        scratch_shapes=[pltpu.VMEM((2, SUB_ROWS, d), BF16), pltpu.SemaphoreType.DMA((2,))],
    )
    return pl.pallas_call(
        _combine_kernel,
        grid_spec=grid_spec,
        out_shape=jax.ShapeDtypeStruct((t, d), F32),
        compiler_params=_cp(("arbitrary",)),
        name="moe_combine",
    )(*_copy_lists(src, dst, units, nsub), tot, ys, dcol, bounds, h2, x1, g2, ws1.astype(BF16), ws3.astype(BF16), ws2.astype(BF16))


def _split_w_in(w):
    fh = FOX_HEADS * FOX_HEAD_DIM
    sizes = (fh, fh, fh, FOX_HEADS,
             2 * ML_HEADS * ML_DQK, ML_HEADS * ML_DV, ML_HEADS, ML_HEADS, ML_HEADS * ML_DV,
             HG_HEADS * HG_DK, HG_HEADS * HG_DK, BRANCH_WIDTH, BRANCH_WIDTH,
             N_BRANCH * D_MODEL)
    outs, o = [], 0
    for sz in sizes:
        outs.append(w[:, o:o + sz])
        o += sz
    return outs


def moe_ffn(h2, x1, g2, dest, wts, cnt, w1, w3, w2, layer, ws1, ws3, ws2, seq):
    t, d = h2.shape
    n_blocks = _max_blocks(t)
    tables = _run_tables(cnt, n_blocks)
    xs = moe_dispatch(h2, dest, tables, n_blocks * MOE_BLOCK)
    ys = moe_experts(xs, w1, w3, w2, layer, tables, n_blocks)
    return moe_combine(ys, dest, wts, h2, x1, g2, ws1, ws3, ws2, tables, seq)


def kernel(x, c, ada_w, ada_b, norm1_g, norm2_g, w_in, fox_bf, fox_q_g, fox_k_g, mlstm_conv, mlstm_bi, mlstm_bf, mlstm_norm_g, hgrn_lower_bounds, hgrn_norm_g, w_branch, w_out, router_w, router_bias, exp_w1, exp_w3, exp_w2, sh_w1, sh_w3, sh_w2):
    b, s, d = x.shape
    depth = ada_w.shape[0]
    mod = adaln_mod(c, ada_w, ada_b)
    lb_all = jnp.cumsum(jax.nn.softmax(hgrn_lower_bounds.astype(F32), axis=0), axis=0)
    lb_all = lb_all - lb_all[0]
    for l in range(depth):
        sh1, sc1, g1, sh2, sc2, g2 = [mod[l][:, d * j:d * (j + 1)] for j in range(6)]
        (wfq, wfk, wfv, wff, wmqk, wmv, wmi, wmf, wmo, whf, whq, whi, whg, wgates) = _split_w_in(w_in[l])
        qp, kp, fv, h = fox_project(x, norm1_g[l], sc1, sh1, wfq, wfk, wfv, wff, fox_q_g[l], fox_k_g[l], fox_bf[l])
        y_fox = fox_attention(qp, kp, fv)
        mq, mk, mv, mog, mgates = ml_project(h, wmqk, wmv, wmi, wmf, wmo, mlstm_conv[l], mlstm_bi[l], mlstm_bf[l])
        y_ml = mlstm(mq, mk, mv, mog, mgates, mlstm_norm_g[l])
        hq, hk, hv, hog, hlf = hg_project(h, whf, whq, whi, whg, lb_all[l])
        y_hg = hgrn(hq, hk, hv, hog, hlf, hgrn_norm_g[l])
        x1, h2, dest, wts, cnt = merge_branches(x, h, y_fox, y_ml, y_hg, wgates, w_branch[l], w_out[l], g1, norm2_g[l], sc2, sh2,
                                                router_w[l], router_bias[l])
        x = moe_ffn(h2.reshape(b * s, d), x1.reshape(b * s, d), g2.reshape(b, 1, d), dest, wts, cnt,
                    exp_w1, exp_w3, exp_w2, l, sh_w1[l], sh_w3[l], sh_w2[l], s).reshape(b, s, d)
    return x
```

```python
import jax
import jax.numpy as jnp
import numpy as np
from jax import lax
from jax.experimental import pallas as pl
from jax.experimental.pallas import tpu as pltpu

F32 = jnp.float32
BF16 = jnp.bfloat16
I32 = jnp.int32

LANES = 128
SUBLANES = 8
BF16_ROWS = 16

D_MODEL = 1024
BRANCH_WIDTH = D_MODEL // 2
N_BRANCH = 3
FOX_HEAD_DIM = 64
FOX_HEADS = BRANCH_WIDTH // FOX_HEAD_DIM
ML_HEADS = 4
ML_DV = BRANCH_WIDTH // ML_HEADS
ML_DQK = ML_DV // 2
ML_CONV = 4
ML_CHUNK = 128
HG_HEADS = 4
HG_DK = 128
HG_CHUNK = 128
N_EXPERTS = 64
N_GROUPS = 8
GROUP_SIZE = N_EXPERTS // N_GROUPS
TOPK_GROUPS = 4
TOP_K = 8
ROUTE_SCALE = 2.5
MOE_BLOCK = 1024
SORT_CHUNK = 1024
EPS = 1e-6
NEG = -1e30
LOG2E = 1.4426950408889634

PROJ_TILE = 1024
ATTN_TILE = 256
SUB_TOKENS = 256
MERGE_TILE = 512
RUN_ALIGN = BF16_ROWS
SUB_ROWS = ((SUB_TOKENS * TOP_K + N_EXPERTS * (RUN_ALIGN - 1)) + 255) // 256 * 256
TAIL_BITS = tuple(1 << b for b in range((MOE_BLOCK // RUN_ALIGN - 1).bit_length()))
DEST_RADIX = 64
DEST_NONE = DEST_RADIX * 127
WEIGHT_SHIFT = 0.25
assert ROUTE_SCALE * WEIGHT_SHIFT < 1.0
assert SUB_ROWS <= DEST_NONE


def _cp(sem, vmem_mb=48):
    return pltpu.CompilerParams(dimension_semantics=sem, vmem_limit_bytes=vmem_mb * 1024 * 1024)


def _dot(a, b):
    return jnp.dot(a, b, preferred_element_type=F32)


def _dot_nt(a, b):
    return lax.dot_general(a, b, (((1,), (1,)), ((), ())), preferred_element_type=F32)


def _dot_tn(a, b):
    return lax.dot_general(a, b, (((0,), (0,)), ((), ())), preferred_element_type=F32)


def _split3(x):
    hi = x.astype(BF16)
    r = x - hi.astype(F32)
    mid = r.astype(BF16)
    lo = (r - mid.astype(F32)).astype(BF16)
    return hi, mid, lo


def _tri_dot(tri, x):
    hi, mid, lo = _split3(x)
    return (_dot(tri, hi) + _dot(tri, mid)) + _dot(tri, lo)


def _dot_tri(x, tri):
    hi, mid, lo = _split3(x)
    return (_dot(hi, tri) + _dot(mid, tri)) + _dot(lo, tri)


def _log_sigmoid(x):
    return jnp.minimum(x, 0.0) - jnp.log1p(jnp.exp(-jnp.abs(x)))


def _silu(x):
    return x * jax.nn.sigmoid(x)


def _tri_incl(n, dtype=BF16):
    r = lax.broadcasted_iota(I32, (n, n), 0)
    c = lax.broadcasted_iota(I32, (n, n), 1)
    return jnp.where(c <= r, 1.0, 0.0).astype(dtype)


def _mod_kernel(c_ref, w_ref, b_ref, o_ref):
    cond = _silu(c_ref[...])
    hi, mid, lo = _split3(cond)
    w = w_ref[...]
    whi, wmid, wlo = _split3(w)
    acc = _dot(hi, whi) + (_dot(hi, wmid) + _dot(mid, whi))
    acc = acc + (_dot(mid, wmid) + _dot(hi, wlo) + _dot(lo, whi))
    o_ref[...] = acc + b_ref[...]


def adaln_mod(c, ada_w, ada_b):
    depth, d, n = ada_w.shape
    b = c.shape[0]
    tn = 1024
    return pl.pallas_call(
        _mod_kernel,
        grid=(depth, n // tn),
        in_specs=[
            pl.BlockSpec((b, d), lambda l, j: (0, 0)),
            pl.BlockSpec((None, d, tn), lambda l, j: (l, 0, j)),
            pl.BlockSpec((None, 1, tn), lambda l, j: (l, 0, j)),
        ],
        out_specs=pl.BlockSpec((None, b, tn), lambda l, j: (l, 0, j)),
        out_shape=jax.ShapeDtypeStruct((depth, b, n), F32),
        compiler_params=_cp(("parallel", "parallel")),
        name="adaln_mod",
    )(c, ada_w, ada_b.reshape(depth, 1, n))


def _norm_mod(x, g, sc, sh):
    ms = jnp.mean(x * x, axis=-1, keepdims=True)
    return x * lax.rsqrt(ms + EPS) * g * (1.0 + sc) + sh


FOX_BIAS_PIECES = 3
FOX_SLOT = 2 * LANES
FOX_BIAS_STRIDE = SUBLANES


def _pack_pieces(x):
    hi, mid, lo = _split3(x)
    p = hi.astype(F32) + pltpu.roll(mid.astype(F32), FOX_HEADS, axis=1) + pltpu.roll(lo.astype(F32), 2 * FOX_HEADS, axis=1)
    return p.astype(BF16)


def _fox_proj_kernel(x_ref, ng_ref, sc_ref, sh_ref, wq_ref, wk_ref, wvt_ref, wf_ref, gq_ref, gk_ref, bf_ref, eq_ref, ek_ref,
                     cq_ref, ck_ref, q_out, k_out, vt_out, h_out, carry_ref):
    @pl.when(pl.program_id(1) == 0)
    def _():
        carry_ref[...] = jnp.zeros_like(carry_ref)

    h = _norm_mod(x_ref[...], ng_ref[...], sc_ref[...], sh_ref[...]).astype(BF16)
    h_out[...] = h
    tm = h.shape[0]
    pr = lax.broadcasted_iota(I32, (LANES, LANES), 0)
    pc = lax.broadcasted_iota(I32, (LANES, LANES), 1)
    avg_pair = jnp.where((pr < FOX_HEAD_DIM) == (pc < FOX_HEAD_DIM), 1.0 / FOX_HEAD_DIM, 0.0).astype(BF16)

    def head_norm(x):
        outs = []
        for pair in range(FOX_HEADS // 2):
            xp = x[:, LANES * pair:LANES * (pair + 1)]
            ms = _dot((xp * xp).astype(BF16), avg_pair)
            outs.append(xp * lax.rsqrt(ms + EPS))
        return outs

    def slots(normed, g_ref, bias):
        parts = []
        for pair in range(FOX_HEADS // 2):
            lanes = slice(LANES * pair, LANES * (pair + 1))
            parts += [normed[pair] * g_ref[:, lanes], bias[:, lanes]]
        return jnp.concatenate(parts, axis=1).astype(BF16)

    lane = lax.broadcasted_iota(I32, (tm, LANES), 1)
    logf = jnp.where(lane < FOX_HEADS, _log_sigmoid(_dot(h, wf_ref[...]) + bf_ref[...]), 0.0)
    cs = _dot(_tri_incl(tm), _pack_pieces(logf))
    cum = cs + pltpu.roll(cs, LANES - FOX_HEADS, axis=1) + pltpu.roll(cs, LANES - 2 * FOX_HEADS, axis=1)
    cum = jnp.where(lane < FOX_HEADS, cum, 0.0) + carry_ref[...]
    carry_ref[...] = cum[tm - 1:tm, :]
    pieces = _pack_pieces(cum * LOG2E)

    q_out[...] = slots(head_norm(_dot(h, wq_ref[...])), gq_ref, _dot(pieces, eq_ref[...]) + cq_ref[...])
    k_out[...] = slots(head_norm(_dot(h, wk_ref[...])), gk_ref, _dot(pieces, ek_ref[...]) + ck_ref[...])
    vt = _dot_nt(wvt_ref[...], h)
    ones = jnp.ones((FOX_HEAD_DIM, tm), F32)
    slots = []
    for hd in range(FOX_HEADS):
        slots += [vt[FOX_HEAD_DIM * hd:FOX_HEAD_DIM * (hd + 1), :], ones]
    vt_out[...] = jnp.concatenate(slots, axis=0).astype(BF16)


def _fox_constants():
    width = FOX_HEADS // 2 * LANES
    eq = np.zeros((LANES, width), np.float32)
    ek = np.zeros((LANES, width), np.float32)
    cq = np.zeros((1, width), np.float32)
    ck = np.zeros((1, width), np.float32)
    for hd in range(FOX_HEADS):
        base = LANES * (hd // 2) + FOX_BIAS_STRIDE * (hd % 2)
        for p in range(FOX_BIAS_PIECES):
            eq[p * FOX_HEADS + hd, base + p] = 1.0
            cq[0, base + FOX_BIAS_PIECES + p] = 1.0
            ck[0, base + p] = 1.0
            ek[p * FOX_HEADS + hd, base + FOX_BIAS_PIECES + p] = -1.0
    return jnp.asarray(eq, BF16), jnp.asarray(ek, BF16), jnp.asarray(cq), jnp.asarray(ck)


def _fox_head_masks():
    m = np.zeros((2, FOX_SLOT), np.float32)
    for a in range(2):
        m[a, FOX_HEAD_DIM * a:FOX_HEAD_DIM * (a + 1)] = 1.0
        m[a, LANES + FOX_BIAS_STRIDE * a:LANES + FOX_BIAS_STRIDE * a + 2 * FOX_BIAS_PIECES] = 1.0
    return jnp.asarray(m, BF16)


def fox_project(x, norm_g, sc, sh, wq, wk, wv, wf, q_g, k_g, bf):
    b, s, d = x.shape
    tm = min(PROJ_TILE, s)
    hp = FOX_HEADS // 2 * FOX_SLOT
    hw = FOX_HEADS * FOX_HEAD_DIM
    wq_p = wq.astype(BF16)
    wk_p = wk.astype(BF16)
    wf_p = jnp.pad(wf, ((0, 0), (0, LANES - FOX_HEADS))).astype(BF16)
    gq = jnp.tile(q_g * (FOX_HEAD_DIM ** -0.5 * LOG2E), FOX_HEADS)[None, :]
    gk = jnp.tile(k_g, FOX_HEADS)[None, :]
    bf_p = jnp.pad(bf, (0, LANES - FOX_HEADS))[None, :]
    eq, ek, cq, ck = _fox_constants()
    const = lambda shape: pl.BlockSpec(shape, lambda i, j: (0,) * len(shape))
    row = lambda n: pl.BlockSpec((None, tm, n), lambda i, j: (i, j, 0))
    per_b = pl.BlockSpec((None, 1, d), lambda i, j: (i, 0, 0))
    return pl.pallas_call(
        _fox_proj_kernel,
        grid=(b, s // tm),
        in_specs=[row(d), const((1, d)), per_b, per_b, const((d, hw)), const((d, hw)), const((BRANCH_WIDTH, d)), const((d, LANES)),
                  const((1, hw)), const((1, hw)), const((1, LANES)), const((LANES, hw)), const((LANES, hw)),
                  const((1, hw)), const((1, hw))],
        out_specs=[row(hp), row(hp), pl.BlockSpec((None, hp, tm), lambda i, j: (i, 0, j)), row(d)],
        out_shape=[jax.ShapeDtypeStruct((b, s, hp), BF16), jax.ShapeDtypeStruct((b, s, hp), BF16),
                   jax.ShapeDtypeStruct((b, hp, s), BF16), jax.ShapeDtypeStruct((b, s, d), BF16)],
        scratch_shapes=[pltpu.VMEM((1, LANES), F32)],
        compiler_params=_cp(("parallel", "arbitrary")),
        name="fox_project",
    )(x, norm_g.reshape(1, d), sc.reshape(b, 1, d), sh.reshape(b, 1, d), wq_p, wk_p, wv.T.astype(BF16), wf_p, gq, gk, bf_p,
      eq, ek, cq, ck)


ATTN_HEADS = 8


def _fox_attn_kernel(q_ref, k_ref, vt_ref, hm_ref, o_ref):
    i = pl.program_id(2)
    t = q_ref.shape[0]
    krow = lax.broadcasted_iota(I32, (t, t), 0)
    qcol = lax.broadcasted_iota(I32, (t, t), 1)
    slot = lambda a: slice(FOX_SLOT * (a // 2), FOX_SLOT * (a // 2 + 1))
    qs = [q_ref[:, slot(a)] * hm_ref[a % 2:a % 2 + 1, :] for a in range(ATTN_HEADS)]

    def scores(j):
        start = pl.multiple_of(j * t, t)
        return tuple(_dot_nt(k_ref[pl.ds(start, t), slot(a)], qs[a]) for a in range(ATTN_HEADS))

    def consume(j, state, ss, masked):
        start = pl.multiple_of(j * t, t)
        new = []
        for a in range(ATTN_HEADS):
            m, acc = state[a]
            s = jnp.where(krow <= qcol, ss[a], NEG) if masked else ss[a]
            m_new = jnp.maximum(m, jnp.max(s, axis=0, keepdims=True))
            p = jnp.exp2(s - m_new)
            alpha = jnp.exp2(m - m_new)
            vt = vt_ref[LANES * a:LANES * (a + 1), pl.ds(start, t)]
            acc = alpha * acc + _dot(vt, p.astype(BF16))
            new.append((m_new, acc))
        return tuple(new)

    def body(j, state):
        return consume(j, state, scores(j), False)

    init = tuple((jnp.full((1, t), NEG, F32), jnp.zeros((LANES, t), F32)) for _ in range(ATTN_HEADS))
    state = lax.fori_loop(0, i, body, init)
    state = consume(i, state, scores(i), True)
    for p in range(ATTN_HEADS // 2):
        halves = []
        for _, acc in (state[2 * p], state[2 * p + 1]):
            halves.append(acc[:FOX_HEAD_DIM, :] / acc[FOX_HEAD_DIM:FOX_HEAD_DIM + 1, :])
        o_ref[:, LANES * p:LANES * (p + 1)] = jnp.concatenate(halves, axis=0).T.astype(BF16)


def fox_attention(qp, kp, vt):
    b, s, hp = qp.shape
    t = ATTN_TILE
    groups = FOX_HEADS // ATTN_HEADS
    return pl.pallas_call(
        _fox_attn_kernel,
        grid=(b, groups, s // t),
        in_specs=[
            pl.BlockSpec((None, t, ATTN_HEADS * LANES), lambda bi, p, i: (bi, i, p)),
            pl.BlockSpec((None, s, ATTN_HEADS * LANES), lambda bi, p, i: (bi, 0, p)),
            pl.BlockSpec((None, ATTN_HEADS * LANES, s), lambda bi, p, i: (bi, p, 0)),
            pl.BlockSpec((2, FOX_SLOT), lambda bi, p, i: (0, 0)),
        ],
        out_specs=pl.BlockSpec((None, t, ATTN_HEADS // 2 * LANES), lambda bi, p, i: (bi, i, p)),
        out_shape=jax.ShapeDtypeStruct((b, s, BRANCH_WIDTH), BF16),
        compiler_params=_cp(("parallel", "parallel", "arbitrary")),
        name="fox_attention",
    )(qp, kp, vt, _fox_head_masks())


CONV_HALO = SUBLANES


def _ml_proj_kernel(h_ref, wqk_ref, wvt_ref, wo_ref, wg_ref, conv_ref, gb_ref, q_out, k_out, vt_out, og_out, g_out, buf_ref):
    tm = h_ref.shape[0]
    half = ML_HEADS * ML_DQK

    @pl.when(pl.program_id(1) == 0)
    def _():
        buf_ref[0:CONV_HALO, :] = jnp.zeros((CONV_HALO, 2 * half), F32)

    h = h_ref[...]
    buf_ref[CONV_HALO:CONV_HALO + tm, :] = _dot(h, wqk_ref[...])
    acc = jnp.zeros((tm, 2 * half), F32)
    for j in range(ML_CONV):
        off = CONV_HALO - (ML_CONV - 1) + j
        acc = acc + conv_ref[j:j + 1, :] * buf_ref[off:off + tm, :]
    buf_ref[0:CONV_HALO, :] = buf_ref[tm:tm + CONV_HALO, :]
    act = _silu(acc)
    q_out[...] = act[:, :half].astype(BF16)
    k_out[...] = (act[:, half:] * (ML_DQK ** -0.5)).astype(BF16)
    vt_out[...] = _dot_nt(wvt_ref[...], h).astype(BF16)
    og_out[...] = jax.nn.sigmoid(_dot(h, wo_ref[...])).astype(BF16)
    g = _dot(h, wg_ref[...]) + gb_ref[...]
    lane = lax.broadcasted_iota(I32, (tm, LANES), 1)
    g_out[...] = jnp.where(lane < ML_HEADS, g, _log_sigmoid(g))


def ml_project(h, wqk, wv, wi, wf, wo, conv, bi, bf):
    b, s, d = h.shape
    tm = min(PROJ_TILE, s)
    half = ML_HEADS * ML_DQK
    wqk_p = wqk.astype(BF16)
    conv_p = conv
    wg = jnp.pad(jnp.concatenate([wi, wf], axis=1), ((0, 0), (0, LANES - 2 * ML_HEADS))).astype(BF16)
    gb = jnp.pad(jnp.concatenate([bi, bf]), (0, LANES - 2 * ML_HEADS))[None, :]
    const = lambda shape: pl.BlockSpec(shape, lambda i, j: (0,) * len(shape))
    row = lambda n: pl.BlockSpec((None, tm, n), lambda i, j: (i, j, 0))
    return pl.pallas_call(
        _ml_proj_kernel,
        grid=(b, s // tm),
        in_specs=[row(d), const((d, 2 * half)), const((BRANCH_WIDTH, d)), const((d, BRANCH_WIDTH)), const((d, LANES)),
                  const((ML_CONV, 2 * half)), const((1, LANES))],
        out_specs=[row(half), row(half), pl.BlockSpec((None, BRANCH_WIDTH, tm), lambda i, j: (i, 0, j)), row(BRANCH_WIDTH), row(LANES)],
        out_shape=[jax.ShapeDtypeStruct((b, s, half), BF16), jax.ShapeDtypeStruct((b, s, half), BF16),
                   jax.ShapeDtypeStruct((b, BRANCH_WIDTH, s), BF16), jax.ShapeDtypeStruct((b, s, BRANCH_WIDTH), BF16),
                   jax.ShapeDtypeStruct((b, s, LANES), F32)],
        scratch_shapes=[pltpu.VMEM((tm + CONV_HALO, 2 * half), F32)],
        compiler_params=_cp(("parallel", "arbitrary")),
        name="ml_project",
    )(h, wqk_p, wv.T.astype(BF16), wo.astype(BF16), wg, conv_p, gb)


def _mlstm_kernel(q_ref, k_ref, vt_ref, og_ref, g_ref, gt_ref, ng_ref, o_ref):
    nb, s = q_ref.shape[0], q_ref.shape[1]
    L = ML_CHUNK
    tril = _tri_incl(L)
    triu = tril.T
    srow = lax.broadcasted_iota(I32, (L, L), 0)
    tcol = lax.broadcasted_iota(I32, (L, L), 1)
    causal = srow <= tcol
    lane = lax.broadcasted_iota(I32, (L, LANES), 1)
    head_lanes = (jnp.where(lane < ML_DQK, 1.0, 0.0).astype(BF16), jnp.where(lane >= ML_DQK, 1.0, 0.0).astype(BF16))

    def chunk(c2, states):
        states = list(states)
        for u in range(ML_UNROLL):
            for bi in range(nb):
                states[bi] = one_chunk(c2 * ML_UNROLL + u, bi, states[bi])
        return tuple(states)

    def one_chunk(c, bi, state):
        new_state = []
        r0 = pl.multiple_of(c * L, L)
        g = g_ref[bi, pl.ds(r0, L), :]
        gt = gt_ref[bi, :, pl.ds(r0, L)]
        bc = _tri_dot(tril, g)
        br = _dot_tri(gt, triu)
        for hd in range(ML_HEADS):
            sl = slice(LANES * hd, LANES * (hd + 1))
            cst, nst, m_prev = state[hd]
            ccol = g[:, hd:hd + 1] - bc[:, ML_HEADS + hd:ML_HEADS + hd + 1]
            brow = br[ML_HEADS + hd:ML_HEADS + hd + 1, :]
            irow = gt[hd:hd + 1, :]
            log_d = jnp.where(causal, brow + ccol, -jnp.inf)
            log_inter = brow + m_prev
            m_t = jnp.maximum(jnp.max(log_d, axis=0, keepdims=True), log_inter)
            w_intra = jnp.exp(log_d - m_t)
            w_inter = jnp.exp(log_inter - m_t)
            pair = slice(LANES * (hd // 2), LANES * (hd // 2 + 1))
            qc = q_ref[bi, pl.ds(r0, L), pair] * head_lanes[hd % 2]
            kc = k_ref[bi, pl.ds(r0, L), pair]
            vt = vt_ref[bi, sl, pl.ds(r0, L)]
            sc = _dot_nt(kc, qc) * w_intra
            num =_dot(vt, sc.astype(BF16)) + w_inter * _dot_nt(cst.astype(BF16), qc)
            qn = _dot_nt(jnp.broadcast_to(nst, (SUBLANES, LANES)).astype(BF16), qc)[0:1, :]
            den = jnp.sum(sc, axis=0, keepdims=True) + w_inter * qn
            hout = num / jnp.maximum(jnp.abs(den), jnp.exp(-m_t))
            b_last = brow[:, L - 1:L]
            lw = b_last - brow + irow
            m_new = jnp.maximum(b_last + m_prev, jnp.max(lw, axis=1, keepdims=True))
            w_in = jnp.exp(lw - m_new)
            decay = jnp.exp(b_last + m_prev - m_new)
            new_state.append((decay * cst + _dot((vt.astype(F32) * w_in).astype(BF16), kc),
                              decay * nst + _dot(jnp.broadcast_to(w_in, (SUBLANES, L)).astype(BF16), kc)[0:1, :],
                              m_new))
            ms = jnp.mean(hout * hout, axis=0, keepdims=True)
            y = (hout * lax.rsqrt(ms + EPS)).T * (og_ref[bi, pl.ds(r0, L), sl].astype(F32) * ng_ref[:, sl])
            o_ref[bi, pl.ds(r0, L), sl] = y.astype(BF16)
        return tuple(new_state)

    zero = (jnp.zeros((ML_DV, LANES), F32), jnp.zeros((1, LANES), F32), jnp.zeros((1, 1), F32))
    lax.fori_loop(0, s // (L * ML_UNROLL), chunk, tuple(tuple(zero for _ in range(ML_HEADS)) for _ in range(nb)))


ML_SEQS = 2
ML_UNROLL = 2


def mlstm(q, k, vt, og, gates, norm_g):
    b, w, s = vt.shape
    nb = ML_SEQS if b % ML_SEQS == 0 else 1
    gt = jnp.swapaxes(gates[:, :, :2 * ML_HEADS], 1, 2)
    seq = lambda n: pl.BlockSpec((nb, s, n), lambda i: (i, 0, 0))
    return pl.pallas_call(
        _mlstm_kernel,
        grid=(b // nb,),
        in_specs=[seq(ML_HEADS * ML_DQK), seq(ML_HEADS * ML_DQK), pl.BlockSpec((nb, w, s), lambda i: (i, 0, 0)), seq(w), seq(LANES),
                  pl.BlockSpec((nb, 2 * ML_HEADS, s), lambda i: (i, 0, 0)),
                  pl.BlockSpec((1, w), lambda i: (0, 0))],
        out_specs=seq(w),
        out_shape=jax.ShapeDtypeStruct((b, s, w), BF16),
        compiler_params=_cp(("parallel",)),
        name="mlstm",
    )(q, k, vt, og, gates, gt, norm_g.reshape(1, w))


def _hg_proj_kernel(h_ref, wf_ref, wq_ref, wi_ref, wg_ref, lb_ref, q_out, k_out, v_out, og_out, lf_out):
    h = h_ref[...]
    fz = _dot(h, wf_ref[...])
    log_lb = lb_ref[0:1, :]
    log_1m = lb_ref[1:2, :]
    one_m = lb_ref[2:3, :]
    u = jnp.exp(-jnp.abs(fz))
    a = log_lb
    bb = log_1m + (jnp.minimum(fz, 0.0) - jnp.log1p(u))
    lf_out[...] = jnp.maximum(a, bb) + jnp.log1p(jnp.exp(-jnp.abs(a - bb)))
    k_out[...] = (one_m * (jnp.where(fz >= 0.0, u, 1.0) / (1.0 + u))).astype(BF16)
    q_out[...] = _silu(_dot(h, wq_ref[...])).astype(BF16)
    v_out[...] = _dot(h, wi_ref[...]).astype(BF16)
    og_out[...] = _silu(_dot(h, wg_ref[...])).astype(BF16)


def hg_project(h, wf, wq, wi, wg, lb):
    b, s, d = h.shape
    tm = min(PROJ_TILE, s)
    w = BRANCH_WIDTH
    lbp = jnp.stack([jnp.log(lb), jnp.log1p(-lb), 1.0 - lb], axis=0)
    const = lambda shape: pl.BlockSpec(shape, lambda i, j: (0,) * len(shape))
    row = lambda n: pl.BlockSpec((None, tm, n), lambda i, j: (i, j, 0))
    return pl.pallas_call(
        _hg_proj_kernel,
        grid=(b, s // tm),
        in_specs=[row(d), const((d, w)), const((d, w)), const((d, w)), const((d, w)), const((3, w))],
        out_specs=[row(w), row(w), row(w), row(w), row(w)],
        out_shape=[jax.ShapeDtypeStruct((b, s, w), BF16)] * 4 + [jax.ShapeDtypeStruct((b, s, w), F32)],
        compiler_params=_cp(("parallel", "parallel")),
        name="hg_project",
    )(h, wf.astype(BF16), wq.astype(BF16), wi.astype(BF16), wg.astype(BF16), lbp)


HG_UNROLL = 4
HG_LEVELS = tuple(HG_CHUNK >> (i + 1) for i in range(HG_CHUNK.bit_length() - 1))


def _hg_tables():
    L = HG_CHUNK
    t = np.arange(L)
    tri = (t[None, :] <= t[:, None]).astype(np.float32)
    mats = [tri]
    x = t[:, None] ^ t[None, :]
    lvl = np.full((L, L), -1, np.int32)
    lvl[t[:, None] == t[None, :]] = 0
    for i, m in enumerate(HG_LEVELS):
        if m < SUBLANES:
            mats.append(tri[(t // (2 * m)) * (2 * m) + m - 1])
        lvl[(t[:, None] > t[None, :]) & (x >= m) & (x < 2 * m)] = i + 1
    return jnp.asarray(np.concatenate(mats, axis=0), BF16), jnp.asarray(lvl)


def _hgrn_kernel(q_ref, k_ref, v_ref, og_ref, lf_ref, ng_ref, tall_ref, lvl_ref, o_ref, st_ref):
    s, wd = q_ref.shape
    L = HG_CHUNK
    st_ref[...] = jnp.zeros_like(st_ref)
    rowi = lax.broadcasted_iota(I32, (L, LANES), 0)

    lvl = lvl_ref[...]
    level_masks = [lvl == i for i in range(len(HG_LEVELS) + 1)]

    def chunk(c, carry):
        r0 = pl.multiple_of(c * L, L)
        tall = tall_ref[...]
        g = lf_ref[pl.ds(r0, L), :]
        hi = g.astype(BF16)
        mid = (g - hi.astype(F32)).astype(BF16)
        cums = (_dot(tall, hi) + _dot(tall, mid)) * LOG2E
        a = cums[0:L]
        qb = q_ref[pl.ds(r0, L), :]
        kb = k_ref[pl.ds(r0, L), :]
        qf = qb.astype(F32)
        kf = kb.astype(F32)
        ws = []
        fine = 0
        for m in HG_LEVELS:
            if m >= SUBLANES:
                ref = jnp.concatenate([jnp.broadcast_to(a[g0 + m - 1:g0 + m, :], (2 * m, wd)) for g0 in range(0, L, 2 * m)], axis=0)
            else:
                fine += 1
                ref = cums[L * fine:L * (fine + 1)]
            e = jnp.exp2(-jnp.abs(a - ref))
            upper = (rowi & m) != 0
            qk = jnp.concatenate([jnp.where(upper, qf[:, LANES * hd:LANES * (hd + 1)], kf[:, LANES * hd:LANES * (hd + 1)])
                                  for hd in range(HG_HEADS)], axis=1)
            ws.append((e * qk).astype(BF16))
        a_last = a[L - 1:L, :]
        qa = (qf * jnp.exp2(a)).astype(BF16)
        kt = (kf * jnp.exp2(a_last - a)).astype(BF16)
        decay = jnp.exp2(a_last)
        for hd in range(HG_HEADS):
            sl = slice(LANES * hd, LANES * (hd + 1))
            vb = v_ref[pl.ds(r0, L), sl]
            sc = jnp.where(level_masks[0], _dot_nt(qb[:, sl], kb[:, sl]), 0.0)
            for i in range(len(HG_LEVELS)):
                w = ws[i][:, sl]
                sc = jnp.where(level_masks[i + 1], _dot_nt(w, w), sc)
            st = st_ref[hd]
            out = _dot(sc.astype(BF16), vb) + _dot_nt(qa[:, sl], st.astype(BF16))
            st_ref[hd] = st * decay[:, sl] + _dot_tn(vb, kt[:, sl])
            ms = jnp.mean(out * out, axis=-1, keepdims=True)
            y = (out * lax.rsqrt(ms + EPS)) * ng_ref[:, sl] * og_ref[pl.ds(r0, L), sl].astype(F32)
            o_ref[pl.ds(r0, L), sl] = y.astype(BF16)
        return carry

    def chunks(c2, carry):
        for u in range(HG_UNROLL):
            chunk(c2 * HG_UNROLL + u, carry)
        return carry

    lax.fori_loop(0, s // (L * HG_UNROLL), chunks, 0)


def hgrn(q, k, v, og, logf, norm_g):
    b, s, w = v.shape
    tall, lvl = _hg_tables()
    seq = pl.BlockSpec((None, s, w), lambda i: (i, 0, 0))
    const = lambda shape: pl.BlockSpec(shape, lambda i: (0,) * len(shape))
    return pl.pallas_call(
        _hgrn_kernel,
        grid=(b,),
        in_specs=[seq, seq, seq, seq, seq, const((1, w)), const(tall.shape), const(lvl.shape)],
        out_specs=seq,
        out_shape=jax.ShapeDtypeStruct((b, s, w), BF16),
        scratch_shapes=[pltpu.VMEM((HG_HEADS, LANES, HG_DK), F32)],
        compiler_params=_cp(("parallel",)),
        name="hgrn",
    )(q, k, v, og, logf, norm_g.reshape(1, w), tall, lvl)


def _merge_kernel(x_ref, h_ref, yf_ref, ym_ref, yh_ref, wg_ref, wb_ref, wo_ref, g1_ref, n2_ref, sc2_ref, sh2_ref, wr_ref, rb_ref,
                  x_out, h2_out, dest_out, w_out, cnt_out):
    d = x_ref.shape[1]
    h = h_ref[...]
    merged = None
    for br, y_ref in enumerate((yf_ref, ym_ref, yh_ref)):
        gate = jax.nn.sigmoid(_dot(h, wg_ref[:, d * br:d * (br + 1)]))
        term = gate * _dot(y_ref[...], wb_ref[br])
        merged = term if merged is None else merged + term
    mixed = _dot(merged.astype(BF16), wo_ref[...])
    x1 = x_ref[...] + g1_ref[...] * mixed
    x_out[...] = x1
    h2 = _norm_mod(x1, n2_ref[...], sc2_ref[...], sh2_ref[...]).astype(BF16)
    h2_out[...] = h2
    for u in range(h2.shape[0] // SUB_TOKENS):
        _route(h2[SUB_TOKENS * u:SUB_TOKENS * (u + 1), :], wr_ref[...], rb_ref[...], dest_out.at[u], w_out.at[u], cnt_out.at[u])


def merge_branches(x, h, y_fox, y_ml, y_hg, w_gates, w_branch, w_out, g1, norm2_g, sc2, sh2, router_w, router_bias):
    b, s, d = x.shape
    tm = MERGE_TILE
    per_seq = s // tm
    per_tile = tm // SUB_TOKENS
    nsub = b * s // SUB_TOKENS
    w = BRANCH_WIDTH
    mat = pl.BlockSpec((per_tile, 2 * N_EXPERTS, SUB_TOKENS), lambda i, j: (i * per_seq + j, 0, 0))
    const = lambda shape: pl.BlockSpec(shape, lambda i, j: (0,) * len(shape))
    row = lambda n: pl.BlockSpec((None, tm, n), lambda i, j: (i, j, 0))
    per_b = pl.BlockSpec((None, 1, d), lambda i, j: (i, 0, 0))
    x1, h2, dest, wts, cnt = pl.pallas_call(
        _merge_kernel,
        grid=(b, s // tm),
        in_specs=[row(d), row(d), row(w), row(w), row(w), const((d, N_BRANCH * d)), const((N_BRANCH, w, d)), const((d, d)),
                  per_b, const((1, d)), per_b, per_b, const((N_EXPERTS, d)), const((N_EXPERTS, 1))],
        out_specs=[row(d), row(d), mat, mat, pl.BlockSpec((per_tile, N_EXPERTS, LANES), lambda i, j: (i * per_seq + j, 0, 0))],
        out_shape=[jax.ShapeDtypeStruct((b, s, d), F32), jax.ShapeDtypeStruct((b, s, d), BF16),
                   jax.ShapeDtypeStruct((nsub, 2 * N_EXPERTS, SUB_TOKENS), BF16),
                   jax.ShapeDtypeStruct((nsub, 2 * N_EXPERTS, SUB_TOKENS), BF16),
                   jax.ShapeDtypeStruct((nsub, N_EXPERTS, LANES), I32)],
        compiler_params=_cp(("parallel", "parallel"), 56),
        name="merge_branches",
    )(x, h, y_fox, y_ml, y_hg, w_gates.astype(BF16), w_branch.astype(BF16), w_out.astype(BF16),
      g1.reshape(b, 1, d), norm2_g.reshape(1, d), sc2.reshape(b, 1, d), sh2.reshape(b, 1, d),
      router_w.T.astype(BF16), router_bias.reshape(N_EXPERTS, 1))
    return x1, h2, dest, wts, cnt[:, :, 0]


def _first_max(x, iota, size):
    m = jnp.max(x, axis=0, keepdims=True)
    idx = jnp.min(jnp.where(x == m, iota, size), axis=0, keepdims=True)
    return m, idx


def _route(h, wr, rb, dest_out, w_out, cnt_out):
    n = h.shape[0]
    scores = jax.nn.sigmoid(_dot_nt(wr, h))
    choice = scores + rb
    e_iota = lax.broadcasted_iota(I32, (N_EXPERTS, n), 0)
    c3 = choice.reshape(N_GROUPS, GROUP_SIZE, n)
    i3 = lax.broadcasted_iota(I32, (N_GROUPS, GROUP_SIZE, n), 1)
    m1 = jnp.max(c3, axis=1, keepdims=True)
    i1 = jnp.min(jnp.where(c3 == m1, i3, GROUP_SIZE), axis=1, keepdims=True)
    m2 = jnp.max(jnp.where(i3 == i1, -jnp.inf, c3), axis=1, keepdims=True)
    gs = (m1 + m2).reshape(N_GROUPS, n)
    g_iota = lax.broadcasted_iota(I32, (N_GROUPS, n), 0)
    gsel = jnp.zeros((N_GROUPS, n), F32)
    for _ in range(TOPK_GROUPS):
        _, gi = _first_max(gs, g_iota, N_GROUPS)
        hit = g_iota == gi
        gsel = jnp.where(hit, 1.0, gsel)
        gs = jnp.where(hit, -jnp.inf, gs)
    gmask = jnp.broadcast_to(gsel.reshape(N_GROUPS, 1, n), (N_GROUPS, GROUP_SIZE, n)).reshape(N_EXPERTS, n)
    masked = jnp.where(gmask > 0.0, choice, -jnp.inf)
    sel = jnp.zeros((N_EXPERTS, n), F32)
    for _ in range(TOP_K):
        _, ei = _first_max(masked, e_iota, N_EXPERTS)
        hit = e_iota == ei
        sel = jnp.where(hit, 1.0, sel)
        masked = jnp.where(hit, -jnp.inf, masked)
    tr = lax.broadcasted_iota(I32, (n, n), 0)
    tc = lax.broadcasted_iota(I32, (n, n), 1)
    before = jnp.where(tr < tc, 1.0, 0.0).astype(BF16)
    pos = _dot(sel.astype(BF16), before)
    cnt = jnp.sum(sel, axis=1, keepdims=True)
    units = jnp.floor((cnt + (RUN_ALIGN - 1)) * (1.0 / RUN_ALIGN))
    er = lax.broadcasted_iota(I32, (N_EXPERTS, N_EXPERTS), 0)
    ec = lax.broadcasted_iota(I32, (N_EXPERTS, N_EXPERTS), 1)
    lower = jnp.where(ec < er, 1.0, 0.0).astype(BF16)
    off = _dot(lower, jnp.broadcast_to(units, (N_EXPERTS, LANES)).astype(BF16))[:, 0:1] * RUN_ALIGN
    dest = jnp.where(sel > 0.0, off + pos, float(DEST_NONE))
    dhi = jnp.floor(dest * (1.0 / DEST_RADIX))
    dest_out[0:N_EXPERTS, :] = dhi.astype(BF16)
    dest_out[N_EXPERTS:, :] = (dest - dhi * DEST_RADIX).astype(BF16)
    wsum = jnp.sum(scores * sel, axis=0, keepdims=True)
    w_out[0:N_EXPERTS, :] = jnp.zeros((N_EXPERTS, n), BF16)
    w_out[N_EXPERTS:, :] = (scores * sel / wsum * ROUTE_SCALE).astype(BF16)
    cnt_out[...] = jnp.broadcast_to(cnt, (N_EXPERTS, LANES)).astype(I32)


def _run_tables(cnt, n_blocks):
    units = (cnt + (RUN_ALIGN - 1)) // RUN_ALIGN
    src = jnp.cumsum(units, axis=1) - units
    per_block = MOE_BLOCK // RUN_ALIGN
    tot = jnp.sum(units, axis=0)
    tot_blocks = (tot + per_block - 1) // per_block
    blk_end = jnp.cumsum(tot_blocks)
    base = (blk_end - tot_blocks) * per_block
    dst = base[None, :] + jnp.cumsum(units, axis=0) - units
    n_used = blk_end[-1]
    tail_units = tot_blocks * per_block - tot
    tail_dst = base + tot
    blk = jnp.minimum(jnp.arange(n_blocks), n_used - 1)
    blk_exp = jnp.minimum(jnp.sum(blk[:, None] >= blk_end[None, :], axis=1), N_EXPERTS - 1)
    return (src.reshape(-1).astype(I32), dst.reshape(-1).astype(I32), units.reshape(-1).astype(I32),
            blk_exp.astype(I32), n_used.astype(I32).reshape(1), tail_dst.astype(I32), tail_units.astype(I32),
            jnp.sum(units, axis=1).astype(I32))


def _copy_lists(src, dst, units, nsub):
    s0 = src.reshape(nsub, 1, N_EXPERTS)
    d0 = dst.reshape(nsub, 1, N_EXPERTS)
    u = units.reshape(nsub, 1, N_EXPERTS)
    npair = u // 2
    poff = jnp.cumsum(npair, axis=2) - npair
    p = jnp.arange(SUB_UNITS // 2, dtype=I32).reshape(1, -1, 1)
    own = (p >= poff) & (p < poff + npair)
    psrc = jnp.sum(jnp.where(own, s0 - 2 * poff, 0), axis=2) + 2 * p[:, :, 0]
    pdst = jnp.sum(jnp.where(own, d0 - 2 * poff, 0), axis=2) + 2 * p[:, :, 0]
    odd = u & 1
    soff = jnp.cumsum(odd, axis=2) - odd
    q = jnp.arange(N_EXPERTS, dtype=I32).reshape(1, -1, 1)
    owns = (odd == 1) & (soff == q)
    ssrc = jnp.sum(jnp.where(owns, s0 + u - 1, 0), axis=2)
    sdst = jnp.sum(jnp.where(owns, d0 + u - 1, 0), axis=2)
    flat = lambda a: a.reshape(-1).astype(I32)
    return (flat(psrc), flat(pdst), flat(jnp.sum(npair, axis=2)), flat(ssrc), flat(sdst), flat(jnp.sum(odd, axis=2)))


def _run_bounds(src, units, nsub):
    lo = (src.reshape(nsub, N_EXPERTS) * RUN_ALIGN).astype(F32)
    hi = lo + (units.reshape(nsub, N_EXPERTS) * RUN_ALIGN).astype(F32)
    return jnp.stack([jnp.concatenate([lo, lo], axis=1), jnp.concatenate([hi, hi], axis=1)], axis=1)


def _max_blocks(t):
    nsub = t // SUB_TOKENS
    worst_units = t * TOP_K // RUN_ALIGN + nsub * N_EXPERTS
    per_block = MOE_BLOCK // RUN_ALIGN
    return -(-worst_units // per_block) + N_EXPERTS


SUB_UNITS = SUB_ROWS // RUN_ALIGN
COPY_UNROLL = 4


PAIR_SLOTS = SUB_UNITS // 2


def _piece_copies(lists, step, buf, hbm, sem, to_hbm):
    psrc_ref, pdst_ref, npair_ref, ssrc_ref, sdst_ref, nsingle_ref = lists

    def piece(src_ref, dst_ref, idx, units):
        rows = units * RUN_ALIGN
        v = buf.at[pl.ds(pl.multiple_of(src_ref[idx] * RUN_ALIGN, RUN_ALIGN), rows)]
        g = hbm.at[pl.ds(pl.multiple_of(dst_ref[idx] * RUN_ALIGN, RUN_ALIGN), rows)]
        cp = pltpu.make_async_copy(v, g, sem) if to_hbm else pltpu.make_async_copy(g, v, sem)
        cp.start()

    def issue(src_ref, dst_ref, base, count, units):
        def group(q, carry):
            for r in range(COPY_UNROLL):
                piece(src_ref, dst_ref, base + q * COPY_UNROLL + r, units)
            return carry

        groups = lax.shift_right_logical(count, COPY_UNROLL.bit_length() - 1)
        lax.fori_loop(0, groups, group, 0)
        lax.fori_loop(groups * COPY_UNROLL, count, lambda j, c: (piece(src_ref, dst_ref, base + j, units), c)[1], 0)

    issue(psrc_ref, pdst_ref, step * PAIR_SLOTS, npair_ref[step], 2)
    issue(ssrc_ref, sdst_ref, step * N_EXPERTS, nsingle_ref[step], 1)


TOTAL_BITS = tuple(1 << b for b in range(SUB_UNITS.bit_length()))


def _wait_runs(total_units, buf, hbm, sem, to_hbm):
    for bit in TOTAL_BITS:
        @pl.when((total_units & bit) != 0)
        def _():
            rows = bit * RUN_ALIGN
            v = buf.at[pl.ds(0, rows)]
            g = hbm.at[pl.ds(0, rows)]
            cp = pltpu.make_async_copy(v, g, sem) if to_hbm else pltpu.make_async_copy(g, v, sem)
            cp.wait()


def _dispatch_kernel(psrc_ref, pdst_ref, npair_ref, ssrc_ref, sdst_ref, nsingle_ref, tot_ref, tdst_ref, tunits_ref, nused_ref,
                     h_ref, dest_ref, lohi_ref, xs_out, buf_ref, zero_ref, sem):
    lists = (psrc_ref, pdst_ref, npair_ref, ssrc_ref, sdst_ref, nsingle_ref)
    i = pl.program_id(0)
    nsub = pl.num_programs(0)
    slot = i % 2
    n = h_ref.shape[0]
    chunk = SORT_CHUNK

    for sl in range(2):
        @pl.when((slot == sl) & (i >= 2))
        def _():
            _wait_runs(tot_ref[i - 2], buf_ref.at[sl], xs_out, sem.at[sl], True)

    h = h_ref[...]
    dest = dest_ref[...]
    lo = lohi_ref[0:1, :]
    hi = lohi_ref[1:2, :]
    radix = jnp.where(lax.broadcasted_iota(I32, (1, 2 * N_EXPERTS), 1) < N_EXPERTS, float(DEST_RADIX), 1.0)
    r_e = lax.broadcasted_iota(I32, (chunk, 2 * N_EXPERTS), 0).astype(F32)
    r_t = lax.broadcasted_iota(I32, (chunk, n), 0).astype(F32)
    for sl in range(2):
        @pl.when(slot == sl)
        def _():
            for c in range(SUB_ROWS // chunk):
                own = jnp.where(r_e + c * chunk >= lo, jnp.where(r_e + c * chunk < hi, radix, 0.0), 0.0)
                row_of = _dot(own.astype(BF16), dest)
                p = jnp.where(row_of == r_t + c * chunk, 1.0, 0.0)
                buf_ref[sl, c * chunk:(c + 1) * chunk, :] = _dot(p.astype(BF16), h).astype(BF16)
            _piece_copies(lists, i, buf_ref.at[sl], xs_out, sem.at[sl], True)

    @pl.when(i == nsub - 1)
    def _():
        zero_ref[...] = jnp.zeros_like(zero_ref)

        def tails(e, wait):
            u = tunits_ref[e]
            d0 = tdst_ref[e]
            for bit in TAIL_BITS:
                low = u & (bit - 1)

                @pl.when((u & bit) != 0)
                def _():
                    rows = bit * RUN_ALIGN
                    cp = pltpu.make_async_copy(zero_ref.at[pl.ds(0, rows)],
                                               xs_out.at[pl.ds(pl.multiple_of((d0 + low) * RUN_ALIGN, RUN_ALIGN), rows)], sem.at[2])
                    if wait:
                        cp.wait()
                    else:
                        cp.start()
            return wait

        def unused(b, wait):
            cp = pltpu.make_async_copy(zero_ref.at[pl.ds(0, MOE_BLOCK)],
                                       xs_out.at[pl.ds(pl.multiple_of(b * MOE_BLOCK, MOE_BLOCK), MOE_BLOCK)], sem.at[2])
            if wait:
                cp.wait()
            else:
                cp.start()
            return wait

        n_blocks = xs_out.shape[0] // MOE_BLOCK
        lax.fori_loop(0, N_EXPERTS, lambda e, c: (tails(e, False), c)[1], 0)
        lax.fori_loop(nused_ref[0], n_blocks, lambda b, c: (unused(b, False), c)[1], 0)
        for sl in range(2):
            @pl.when((slot != sl) & (i >= 1))
            def _():
                _wait_runs(tot_ref[i - 1], buf_ref.at[sl], xs_out, sem.at[sl], True)

            @pl.when(slot == sl)
            def _():
                _wait_runs(tot_ref[i], buf_ref.at[sl], xs_out, sem.at[sl], True)
        lax.fori_loop(0, N_EXPERTS, lambda e, c: (tails(e, True), c)[1], 0)
        lax.fori_loop(nused_ref[0], n_blocks, lambda b, c: (unused(b, True), c)[1], 0)


def moe_dispatch(h2, dest, tables, n_rows):
    t, d = h2.shape
    n = SUB_TOKENS
    nsub = t // n
    src, dst, units, _, n_used, tail_dst, tail_units, tot = tables
    grid_spec = pltpu.PrefetchScalarGridSpec(
        num_scalar_prefetch=10,
        grid=(nsub,),
        in_specs=[pl.BlockSpec((n, d), lambda i, *_: (i, 0)), pl.BlockSpec((None, 2 * N_EXPERTS, n), lambda i, *_: (i, 0, 0)),
                  pl.BlockSpec((None, 2, 2 * N_EXPERTS), lambda i, *_: (i, 0, 0))],
        out_specs=pl.BlockSpec(memory_space=pl.ANY),
        scratch_shapes=[pltpu.VMEM((2, SUB_ROWS, d), BF16), pltpu.VMEM((max(SUB_TOKENS, MOE_BLOCK), d), BF16), pltpu.SemaphoreType.DMA((3,))],
    )
    return pl.pallas_call(
        _dispatch_kernel,
        grid_spec=grid_spec,
        out_shape=jax.ShapeDtypeStruct((n_rows, d), BF16),
        compiler_params=_cp(("arbitrary",)),
        name="moe_dispatch",
    )(*_copy_lists(src, dst, units, nsub), tot, tail_dst, tail_units, n_used, h2, dest, _run_bounds(src, units, nsub))


X_SLOTS = 3


def _expert_kernel(blk_exp_ref, n_used_ref, x_hbm, w1_ref, w3_ref, w2_ref, y_ref, xbuf, sem):
    b = pl.program_id(0)
    n_used = n_used_ref[0]

    def x_copy(blk, slot):
        rows = pl.ds(pl.multiple_of(blk * MOE_BLOCK, MOE_BLOCK), MOE_BLOCK)
        return pltpu.make_async_copy(x_hbm.at[rows], xbuf.at[slot], sem.at[slot])

    @pl.when(b == 0)
    def _():
        x_copy(0, 0).start()

        @pl.when(n_used > 1)
        def _():
            x_copy(1, 1).start()

    ahead = b + (X_SLOTS - 1)

    @pl.when(ahead < n_used)
    def _():
        x_copy(ahead, lax.rem(ahead, X_SLOTS)).start()

    @pl.when(b < n_used)
    def _():
        slot = lax.rem(b, X_SLOTS)
        x_copy(b, slot).wait()
        x = xbuf[slot]
        hid = _silu(_dot(x, w1_ref[...].astype(BF16))) * _dot(x, w3_ref[...].astype(BF16))
        y_ref[...] = _dot(hid.astype(BF16), w2_ref[...].astype(BF16)).astype(BF16)

    @pl.when(b >= n_used)
    def _():
        y_ref[...] = jnp.zeros_like(y_ref)


def moe_experts(xs, w1, w3, w2, layer, tables, n_blocks):
    n_rows, d = xs.shape
    blk_exp, n_used = tables[3], tables[4]
    f = w1.shape[-1]

    def w_map(b, be, nu):
        return (layer, be[b], 0, 0)

    grid_spec = pltpu.PrefetchScalarGridSpec(
        num_scalar_prefetch=2,
        grid=(n_blocks,),
        in_specs=[pl.BlockSpec(memory_space=pl.ANY), pl.BlockSpec((None, None, d, f), w_map),
                  pl.BlockSpec((None, None, d, f), w_map), pl.BlockSpec((None, None, f, d), w_map)],
        out_specs=pl.BlockSpec((MOE_BLOCK, d), lambda b, be, nu: (b, 0)),
        scratch_shapes=[pltpu.VMEM((X_SLOTS, MOE_BLOCK, d), BF16), pltpu.SemaphoreType.DMA((X_SLOTS,))],
    )
    return pl.pallas_call(
        _expert_kernel,
        grid_spec=grid_spec,
        out_shape=jax.ShapeDtypeStruct((n_rows, d), BF16),
        compiler_params=_cp(("arbitrary",)),
        name="moe_experts",
    )(blk_exp, n_used, xs, w1, w3, w2)


def _combine_kernel(psrc_ref, pdst_ref, npair_ref, ssrc_ref, sdst_ref, nsingle_ref, tot_ref, ys_ref, dcol_ref, lohi_ref,
                    h_ref, x_ref, g2_ref, ws1_ref, ws3_ref, ws2_ref, o_ref, buf_ref, sem):
    lists = (psrc_ref, pdst_ref, npair_ref, ssrc_ref, sdst_ref, nsingle_ref)
    i = pl.program_id(0)
    nsub = pl.num_programs(0)
    slot = i % 2
    n = h_ref.shape[0]
    chunk = 512

    def fetch(step, sl):
        always = SUB_TOKENS * TOP_K
        buf_ref[sl, always:, :] = jnp.zeros((SUB_ROWS - always, buf_ref.shape[2]), BF16)
        _piece_copies(lists, step, buf_ref.at[sl], ys_ref, sem.at[sl], False)

    @pl.when(i == 0)
    def _():
        fetch(0, 0)

    for sl in range(2):
        @pl.when((slot != sl) & (i + 1 < nsub))
        def _():
            fetch(i + 1, sl)

    h = h_ref[...]
    shared = _dot((_silu(_dot(h, ws1_ref[...])) * _dot(h, ws3_ref[...])).astype(BF16), ws2_ref[...])
    dw_t = dcol_ref[...]
    lo = lohi_ref[:, 0:1]
    hi = lohi_ref[:, 1:2]
    part = lax.broadcasted_iota(I32, (4 * N_EXPERTS, 1), 0)
    radix = jnp.where(part < N_EXPERTS, float(DEST_RADIX),
                      jnp.where(part < 2 * N_EXPERTS, 1.0, jnp.where(part < 3 * N_EXPERTS, WEIGHT_SHIFT, 0.0)))
    r_e = lax.broadcasted_iota(I32, (4 * N_EXPERTS, chunk), 1).astype(F32)
    r_t = lax.broadcasted_iota(I32, (n, chunk), 1).astype(F32)
    for sl in range(2):
        @pl.when(slot == sl)
        def _():
            _wait_runs(tot_ref[i], buf_ref.at[sl], ys_ref, sem.at[sl], False)
            acc = shared
            for c in range(SUB_ROWS // chunk):
                own = jnp.where(r_e + c * chunk >= lo, jnp.where(r_e + c * chunk < hi, radix, 0.0), 0.0).astype(BF16)
                val = _dot(dw_t, own)
                row_of = jnp.floor(val)
                pw = jnp.where(row_of == r_t + c * chunk, (val - row_of) * (1.0 / WEIGHT_SHIFT), 0.0)
                acc = acc + _dot(pw.astype(BF16), buf_ref[sl, c * chunk:(c + 1) * chunk, :])
            o_ref[...] = x_ref[...] + g2_ref[...] * acc


def moe_combine(ys, dest, wts, h2, x1, g2, ws1, ws3, ws2, tables, seq):
    t, d = h2.shape
    n = SUB_TOKENS
    nsub = t // n
    src, dst, units, tot = tables[0], tables[1], tables[2], tables[7]
    dw = jnp.concatenate([dest, wts[:, N_EXPERTS:, :], jnp.zeros((nsub, N_EXPERTS, n), BF16)], axis=1)
    dcol = jnp.swapaxes(dw, 1, 2)
    b2 = _run_bounds(src, units, nsub)
    bounds = jnp.swapaxes(jnp.concatenate([b2, b2], axis=2), 1, 2)
    per_seq = seq // n
    f = ws1.shape[-1]
    pair = pl.BlockSpec((None, n, 4 * N_EXPERTS), lambda i, *_: (i, 0, 0))
    grid_spec = pltpu.PrefetchScalarGridSpec(
        num_scalar_prefetch=7,
        grid=(nsub,),
        in_specs=[pl.BlockSpec(memory_space=pl.ANY), pair,
                  pl.BlockSpec((None, 4 * N_EXPERTS, 2), lambda i, *_: (i, 0, 0)),
                  pl.BlockSpec((n, d), lambda i, *_: (i, 0)), pl.BlockSpec((n, d), lambda i, *_: (i, 0)),
                  pl.BlockSpec((None, 1, d), lambda i, *_: (i // per_seq, 0, 0)),
                  pl.BlockSpec((d, f), lambda i, *_: (0, 0)), pl.BlockSpec((d, f), lambda i, *_: (0, 0)),
                  pl.BlockSpec((f, d), lambda i, *_: (0, 0))],
        out_specs=pl.BlockSpec((n, d), lambda i, *_: (i, 0)),
        scratch_shapes=[pltpu.VMEM((2, SUB_ROWS, d), BF16), pltpu.SemaphoreType.DMA((2,))],
    )
    return pl.pallas_call(
        _combine_kernel,
        grid_spec=grid_spec,
        out_shape=jax.ShapeDtypeStruct((t, d), F32),
        compiler_params=_cp(("arbitrary",)),
        name="moe_combine",
    )(*_copy_lists(src, dst, units, nsub), tot, ys, dcol, bounds, h2, x1, g2, ws1.astype(BF16), ws3.astype(BF16), ws2.astype(BF16))


def _split_w_in(w):
    fh = FOX_HEADS * FOX_HEAD_DIM
    sizes = (fh, fh, fh, FOX_HEADS,
             2 * ML_HEADS * ML_DQK, ML_HEADS * ML_DV, ML_HEADS, ML_HEADS, ML_HEADS * ML_DV,
             HG_HEADS * HG_DK, HG_HEADS * HG_DK, BRANCH_WIDTH, BRANCH_WIDTH,
             N_BRANCH * D_MODEL)
    outs, o = [], 0
    for sz in sizes:
        outs.append(w[:, o:o + sz])
        o += sz
    return outs


def moe_ffn(h2, x1, g2, dest, wts, cnt, w1, w3, w2, layer, ws1, ws3, ws2, seq):
    t, d = h2.shape
    n_blocks = _max_blocks(t)
    tables = _run_tables(cnt, n_blocks)
    xs = moe_dispatch(h2, dest, tables, n_blocks * MOE_BLOCK)
    ys = moe_experts(xs, w1, w3, w2, layer, tables, n_blocks)
    return moe_combine(ys, dest, wts, h2, x1, g2, ws1, ws3, ws2, tables, seq)


def kernel(x, c, ada_w, ada_b, norm1_g, norm2_g, w_in, fox_bf, fox_q_g, fox_k_g, mlstm_conv, mlstm_bi, mlstm_bf, mlstm_norm_g, hgrn_lower_bounds, hgrn_norm_g, w_branch, w_out, router_w, router_bias, exp_w1, exp_w3, exp_w2, sh_w1, sh_w3, sh_w2):
    b, s, d = x.shape
    depth = ada_w.shape[0]
    mod = adaln_mod(c, ada_w, ada_b)
    lb_all = jnp.cumsum(jax.nn.softmax(hgrn_lower_bounds.astype(F32), axis=0), axis=0)
    lb_all = lb_all - lb_all[0]
    for l in range(depth):
        sh1, sc1, g1, sh2, sc2, g2 = [mod[l][:, d * j:d * (j + 1)] for j in range(6)]
        (wfq, wfk, wfv, wff, wmqk, wmv, wmi, wmf, wmo, whf, whq, whi, whg, wgates) = _split_w_in(w_in[l])
        qp, kp, fv, h = fox_project(x, norm1_g[l], sc1, sh1, wfq, wfk, wfv, wff, fox_q_g[l], fox_k_g[l], fox_bf[l])
        y_fox = fox_attention(qp, kp, fv)
        mq, mk, mv, mog, mgates = ml_project(h, wmqk, wmv, wmi, wmf, wmo, mlstm_conv[l], mlstm_bi[l], mlstm_bf[l])
        y_ml = mlstm(mq, mk, mv, mog, mgates, mlstm_norm_g[l])
        hq, hk, hv, hog, hlf = hg_project(h, whf, whq, whi, whg, lb_all[l])
        y_hg = hgrn(hq, hk, hv, hog, hlf, hgrn_norm_g[l])
        x1, h2, dest, wts, cnt = merge_branches(x, h, y_fox, y_ml, y_hg, wgates, w_branch[l], w_out[l], g1, norm2_g[l], sc2, sh2,
                                                router_w[l], router_bias[l])
        x = moe_ffn(h2.reshape(b * s, d), x1.reshape(b * s, d), g2.reshape(b, 1, d), dest, wts, cnt,
                    exp_w1, exp_w3, exp_w2, l, sh_w1[l], sh_w3[l], sh_w2[l], s).reshape(b, s, d)
    return x
```

```python
import jax
import jax.numpy as jnp
import numpy as np
from jax import lax
from jax.experimental import pallas as pl
from jax.experimental.pallas import tpu as pltpu

F32 = jnp.float32
BF16 = jnp.bfloat16
I32 = jnp.int32

LANES = 128
SUBLANES = 8
BF16_ROWS = 16

D_MODEL = 1024
BRANCH_WIDTH = D_MODEL // 2
N_BRANCH = 3
FOX_HEAD_DIM = 64
FOX_HEADS = BRANCH_WIDTH // FOX_HEAD_DIM
ML_HEADS = 4
ML_DV = BRANCH_WIDTH // ML_HEADS
ML_DQK = ML_DV // 2
ML_CONV = 4
ML_CHUNK = 128
HG_HEADS = 4
HG_DK = 128
HG_CHUNK = 128
N_EXPERTS = 64
N_GROUPS = 8
GROUP_SIZE = N_EXPERTS // N_GROUPS
TOPK_GROUPS = 4
TOP_K = 8
ROUTE_SCALE = 2.5
MOE_BLOCK = 1024
SORT_CHUNK = 1024
EPS = 1e-6
NEG = -1e30
LOG2E = 1.4426950408889634

PROJ_TILE = 1024
ATTN_TILE = 256
SUB_TOKENS = 256
MERGE_TILE = 512
RUN_ALIGN = BF16_ROWS
COMBINE_CHUNK = 512
SUB_ROWS = -(-(SUB_TOKENS * TOP_K + N_EXPERTS * (RUN_ALIGN - 1)) // COMBINE_CHUNK) * COMBINE_CHUNK
assert SUB_ROWS % SORT_CHUNK == 0
TAIL_BITS = tuple(1 << b for b in range((MOE_BLOCK // RUN_ALIGN - 1).bit_length()))
DEST_RADIX = 64
DEST_NONE = DEST_RADIX * 127
WEIGHT_SHIFT = 0.25
assert ROUTE_SCALE * WEIGHT_SHIFT < 1.0
assert SUB_ROWS <= DEST_NONE


V7X_VMEM_MIB = 64
VMEM_LIMIT_MIB = 48
MERGE_VMEM_LIMIT_MIB = 56
assert MERGE_VMEM_LIMIT_MIB < V7X_VMEM_MIB


def _cp(sem, vmem_mib=VMEM_LIMIT_MIB):
    return pltpu.CompilerParams(dimension_semantics=sem, vmem_limit_bytes=vmem_mib * 1024 * 1024)


def _dot(a, b):
    return jnp.dot(a, b, preferred_element_type=F32)


def _dot_nt(a, b):
    return lax.dot_general(a, b, (((1,), (1,)), ((), ())), preferred_element_type=F32)


def _dot_tn(a, b):
    return lax.dot_general(a, b, (((0,), (0,)), ((), ())), preferred_element_type=F32)


def _split3(x):
    hi = x.astype(BF16)
    r = x - hi.astype(F32)
    mid = r.astype(BF16)
    lo = (r - mid.astype(F32)).astype(BF16)
    return hi, mid, lo


def _tri_dot(tri, x):
    hi, mid, lo = _split3(x)
    return (_dot(tri, hi) + _dot(tri, mid)) + _dot(tri, lo)


def _dot_tri(x, tri):
    hi, mid, lo = _split3(x)
    return (_dot(hi, tri) + _dot(mid, tri)) + _dot(lo, tri)


def _log_sigmoid(x):
    return jnp.minimum(x, 0.0) - jnp.log1p(jnp.exp(-jnp.abs(x)))


def _silu(x):
    return x * jax.nn.sigmoid(x)


def _tri_incl(n, dtype=BF16):
    r = lax.broadcasted_iota(I32, (n, n), 0)
    c = lax.broadcasted_iota(I32, (n, n), 1)
    return jnp.where(c <= r, 1.0, 0.0).astype(dtype)


def _mod_kernel(c_ref, w_ref, b_ref, o_ref):
    cond = _silu(c_ref[...])
    hi, mid, lo = _split3(cond)
    w = w_ref[...]
    whi, wmid, wlo = _split3(w)
    acc = _dot(hi, whi) + (_dot(hi, wmid) + _dot(mid, whi))
    acc = acc + (_dot(mid, wmid) + _dot(hi, wlo) + _dot(lo, whi))
    o_ref[...] = acc + b_ref[...]


def adaln_mod(c, ada_w, ada_b):
    depth, d, n = ada_w.shape
    b = c.shape[0]
    tn = 1024
    return pl.pallas_call(
        _mod_kernel,
        grid=(depth, n // tn),
        in_specs=[
            pl.BlockSpec((b, d), lambda l, j: (0, 0)),
            pl.BlockSpec((None, d, tn), lambda l, j: (l, 0, j)),
            pl.BlockSpec((None, 1, tn), lambda l, j: (l, 0, j)),
        ],
        out_specs=pl.BlockSpec((None, b, tn), lambda l, j: (l, 0, j)),
        out_shape=jax.ShapeDtypeStruct((depth, b, n), F32),
        compiler_params=_cp(("parallel", "parallel")),
        name="adaln_mod",
    )(c, ada_w, ada_b.reshape(depth, 1, n))


def _norm_mod(x, g, sc, sh):
    ms = jnp.mean(x * x, axis=-1, keepdims=True)
    return x * lax.rsqrt(ms + EPS) * g * (1.0 + sc) + sh


FOX_BIAS_PIECES = 3
FOX_SLOT = 2 * LANES
FOX_BIAS_STRIDE = SUBLANES


def _pack_pieces(x):
    hi, mid, lo = _split3(x)
    p = hi.astype(F32) + pltpu.roll(mid.astype(F32), FOX_HEADS, axis=1) + pltpu.roll(lo.astype(F32), 2 * FOX_HEADS, axis=1)
    return p.astype(BF16)


def _fox_proj_kernel(x_ref, ng_ref, sc_ref, sh_ref, wq_ref, wk_ref, wvt_ref, wf_ref, gq_ref, gk_ref, bf_ref, eq_ref, ek_ref,
                     cq_ref, ck_ref, q_out, k_out, vt_out, h_out, carry_ref):
    @pl.when(pl.program_id(1) == 0)
    def _():
        carry_ref[...] = jnp.zeros_like(carry_ref)

    h = _norm_mod(x_ref[...], ng_ref[...], sc_ref[...], sh_ref[...]).astype(BF16)
    h_out[...] = h
    tm = h.shape[0]
    pr = lax.broadcasted_iota(I32, (LANES, LANES), 0)
    pc = lax.broadcasted_iota(I32, (LANES, LANES), 1)
    avg_pair = jnp.where((pr < FOX_HEAD_DIM) == (pc < FOX_HEAD_DIM), 1.0 / FOX_HEAD_DIM, 0.0).astype(BF16)

    def head_norm(x):
        outs = []
        for pair in range(FOX_HEADS // 2):
            xp = x[:, LANES * pair:LANES * (pair + 1)]
            ms = _dot((xp * xp).astype(BF16), avg_pair)
            outs.append(xp * lax.rsqrt(ms + EPS))
        return outs

    def slots(normed, g_ref, bias):
        parts = []
        for pair in range(FOX_HEADS // 2):
            lanes = slice(LANES * pair, LANES * (pair + 1))
            parts += [normed[pair] * g_ref[:, lanes], bias[:, lanes]]
        return jnp.concatenate(parts, axis=1).astype(BF16)

    lane = lax.broadcasted_iota(I32, (tm, LANES), 1)
    logf = jnp.where(lane < FOX_HEADS, _log_sigmoid(_dot(h, wf_ref[...]) + bf_ref[...]), 0.0)
    cs = _dot(_tri_incl(tm), _pack_pieces(logf))
    cum = cs + pltpu.roll(cs, LANES - FOX_HEADS, axis=1) + pltpu.roll(cs, LANES - 2 * FOX_HEADS, axis=1)
    cum = jnp.where(lane < FOX_HEADS, cum, 0.0) + carry_ref[...]
    carry_ref[...] = cum[tm - 1:tm, :]
    pieces = _pack_pieces(cum * LOG2E)

    q_out[...] = slots(head_norm(_dot(h, wq_ref[...])), gq_ref, _dot(pieces, eq_ref[...]) + cq_ref[...])
    k_out[...] = slots(head_norm(_dot(h, wk_ref[...])), gk_ref, _dot(pieces, ek_ref[...]) + ck_ref[...])
    vt = _dot_nt(wvt_ref[...], h)
    ones = jnp.ones((FOX_HEAD_DIM, tm), F32)
    slots = []
    for hd in range(FOX_HEADS):
        slots += [vt[FOX_HEAD_DIM * hd:FOX_HEAD_DIM * (hd + 1), :], ones]
    vt_out[...] = jnp.concatenate(slots, axis=0).astype(BF16)


def _fox_constants():
    width = FOX_HEADS // 2 * LANES
    eq = np.zeros((LANES, width), np.float32)
    ek = np.zeros((LANES, width), np.float32)
    cq = np.zeros((1, width), np.float32)
    ck = np.zeros((1, width), np.float32)
    for hd in range(FOX_HEADS):
        base = LANES * (hd // 2) + FOX_BIAS_STRIDE * (hd % 2)
        for p in range(FOX_BIAS_PIECES):
            eq[p * FOX_HEADS + hd, base + p] = 1.0
            cq[0, base + FOX_BIAS_PIECES + p] = 1.0
            ck[0, base + p] = 1.0
            ek[p * FOX_HEADS + hd, base + FOX_BIAS_PIECES + p] = -1.0
    return jnp.asarray(eq, BF16), jnp.asarray(ek, BF16), jnp.asarray(cq), jnp.asarray(ck)


def _fox_head_masks():
    m = np.zeros((2, FOX_SLOT), np.float32)
    for a in range(2):
        m[a, FOX_HEAD_DIM * a:FOX_HEAD_DIM * (a + 1)] = 1.0
        m[a, LANES + FOX_BIAS_STRIDE * a:LANES + FOX_BIAS_STRIDE * a + 2 * FOX_BIAS_PIECES] = 1.0
    return jnp.asarray(m, BF16)


def fox_project(x, norm_g, sc, sh, wq, wk, wv, wf, q_g, k_g, bf):
    b, s, d = x.shape
    tm = min(PROJ_TILE, s)
    hp = FOX_HEADS // 2 * FOX_SLOT
    hw = FOX_HEADS * FOX_HEAD_DIM
    wq_p = wq.astype(BF16)
    wk_p = wk.astype(BF16)
    wf_p = jnp.pad(wf, ((0, 0), (0, LANES - FOX_HEADS))).astype(BF16)
    gq = jnp.tile(q_g * (FOX_HEAD_DIM ** -0.5 * LOG2E), FOX_HEADS)[None, :]
    gk = jnp.tile(k_g, FOX_HEADS)[None, :]
    bf_p = jnp.pad(bf, (0, LANES - FOX_HEADS))[None, :]
    eq, ek, cq, ck = _fox_constants()
    const = lambda shape: pl.BlockSpec(shape, lambda i, j: (0,) * len(shape))
    row = lambda n: pl.BlockSpec((None, tm, n), lambda i, j: (i, j, 0))
    per_b = pl.BlockSpec((None, 1, d), lambda i, j: (i, 0, 0))
    return pl.pallas_call(
        _fox_proj_kernel,
        grid=(b, s // tm),
        in_specs=[row(d), const((1, d)), per_b, per_b, const((d, hw)), const((d, hw)), const((BRANCH_WIDTH, d)), const((d, LANES)),
                  const((1, hw)), const((1, hw)), const((1, LANES)), const((LANES, hw)), const((LANES, hw)),
                  const((1, hw)), const((1, hw))],
        out_specs=[row(hp), row(hp), pl.BlockSpec((None, hp, tm), lambda i, j: (i, 0, j)), row(d)],
        out_shape=[jax.ShapeDtypeStruct((b, s, hp), BF16), jax.ShapeDtypeStruct((b, s, hp), BF16),
                   jax.ShapeDtypeStruct((b, hp, s), BF16), jax.ShapeDtypeStruct((b, s, d), BF16)],
        scratch_shapes=[pltpu.VMEM((1, LANES), F32)],
        compiler_params=_cp(("parallel", "arbitrary")),
        name="fox_project",
    )(x, norm_g.reshape(1, d), sc.reshape(b, 1, d), sh.reshape(b, 1, d), wq_p, wk_p, wv.T.astype(BF16), wf_p, gq, gk, bf_p,
      eq, ek, cq, ck)


ATTN_HEADS = 8


def _fox_attn_kernel(q_ref, k_ref, vt_ref, hm_ref, o_ref):
    i = pl.program_id(2)
    t = q_ref.shape[0]
    krow = lax.broadcasted_iota(I32, (t, t), 0)
    qcol = lax.broadcasted_iota(I32, (t, t), 1)
    slot = lambda a: slice(FOX_SLOT * (a // 2), FOX_SLOT * (a // 2 + 1))
    qs = [q_ref[:, slot(a)] * hm_ref[a % 2:a % 2 + 1, :] for a in range(ATTN_HEADS)]

    def scores(j):
        start = pl.multiple_of(j * t, t)
        return tuple(_dot_nt(k_ref[pl.ds(start, t), slot(a)], qs[a]) for a in range(ATTN_HEADS))

    def consume(j, state, ss, masked):
        start = pl.multiple_of(j * t, t)
        new = []
        for a in range(ATTN_HEADS):
            m, acc = state[a]
            s = jnp.where(krow <= qcol, ss[a], NEG) if masked else ss[a]
            m_new = jnp.maximum(m, jnp.max(s, axis=0, keepdims=True))
            p = jnp.exp2(s - m_new)
            alpha = jnp.exp2(m - m_new)
            vt = vt_ref[LANES * a:LANES * (a + 1), pl.ds(start, t)]
            acc = alpha * acc + _dot(vt, p.astype(BF16))
            new.append((m_new, acc))
        return tuple(new)

    def body(j, state):
        return consume(j, state, scores(j), False)

    init = tuple((jnp.full((1, t), NEG, F32), jnp.zeros((LANES, t), F32)) for _ in range(ATTN_HEADS))
    state = lax.fori_loop(0, i, body, init)
    state = consume(i, state, scores(i), True)
    for p in range(ATTN_HEADS // 2):
        halves = []
        for _, acc in (state[2 * p], state[2 * p + 1]):
            halves.append(acc[:FOX_HEAD_DIM, :] / acc[FOX_HEAD_DIM:FOX_HEAD_DIM + 1, :])
        o_ref[:, LANES * p:LANES * (p + 1)] = jnp.concatenate(halves, axis=0).T.astype(BF16)


def fox_attention(qp, kp, vt):
    b, s, hp = qp.shape
    t = ATTN_TILE
    groups = FOX_HEADS // ATTN_HEADS
    return pl.pallas_call(
        _fox_attn_kernel,
        grid=(b, groups, s // t),
        in_specs=[
            pl.BlockSpec((None, t, ATTN_HEADS * LANES), lambda bi, p, i: (bi, i, p)),
            pl.BlockSpec((None, s, ATTN_HEADS * LANES), lambda bi, p, i: (bi, 0, p)),
            pl.BlockSpec((None, ATTN_HEADS * LANES, s), lambda bi, p, i: (bi, p, 0)),
            pl.BlockSpec((2, FOX_SLOT), lambda bi, p, i: (0, 0)),
        ],
        out_specs=pl.BlockSpec((None, t, ATTN_HEADS // 2 * LANES), lambda bi, p, i: (bi, i, p)),
        out_shape=jax.ShapeDtypeStruct((b, s, BRANCH_WIDTH), BF16),
        compiler_params=_cp(("parallel", "parallel", "arbitrary")),
        name="fox_attention",
    )(qp, kp, vt, _fox_head_masks())


CONV_HALO = SUBLANES


def _ml_proj_kernel(h_ref, wqk_ref, wvt_ref, wo_ref, wg_ref, conv_ref, gb_ref, q_out, k_out, vt_out, og_out, g_out, buf_ref):
    tm = h_ref.shape[0]
    half = ML_HEADS * ML_DQK

    @pl.when(pl.program_id(1) == 0)
    def _():
        buf_ref[0:CONV_HALO, :] = jnp.zeros((CONV_HALO, 2 * half), F32)

    h = h_ref[...]
    buf_ref[CONV_HALO:CONV_HALO + tm, :] = _dot(h, wqk_ref[...])
    acc = jnp.zeros((tm, 2 * half), F32)
    for j in range(ML_CONV):
        off = CONV_HALO - (ML_CONV - 1) + j
        acc = acc + conv_ref[j:j + 1, :] * buf_ref[off:off + tm, :]
    buf_ref[0:CONV_HALO, :] = buf_ref[tm:tm + CONV_HALO, :]
    act = _silu(acc)
    q_out[...] = act[:, :half].astype(BF16)
    k_out[...] = (act[:, half:] * (ML_DQK ** -0.5)).astype(BF16)
    vt_out[...] = _dot_nt(wvt_ref[...], h).astype(BF16)
    og_out[...] = jax.nn.sigmoid(_dot(h, wo_ref[...])).astype(BF16)
    g = _dot(h, wg_ref[...]) + gb_ref[...]
    lane = lax.broadcasted_iota(I32, (tm, LANES), 1)
    g_out[...] = jnp.where(lane < ML_HEADS, g, _log_sigmoid(g))


def ml_project(h, wqk, wv, wi, wf, wo, conv, bi, bf):
    b, s, d = h.shape
    tm = min(PROJ_TILE, s)
    half = ML_HEADS * ML_DQK
    wqk_p = wqk.astype(BF16)
    conv_p = conv
    wg = jnp.pad(jnp.concatenate([wi, wf], axis=1), ((0, 0), (0, LANES - 2 * ML_HEADS))).astype(BF16)
    gb = jnp.pad(jnp.concatenate([bi, bf]), (0, LANES - 2 * ML_HEADS))[None, :]
    const = lambda shape: pl.BlockSpec(shape, lambda i, j: (0,) * len(shape))
    row = lambda n: pl.BlockSpec((None, tm, n), lambda i, j: (i, j, 0))
    return pl.pallas_call(
        _ml_proj_kernel,
        grid=(b, s // tm),
        in_specs=[row(d), const((d, 2 * half)), const((BRANCH_WIDTH, d)), const((d, BRANCH_WIDTH)), const((d, LANES)),
                  const((ML_CONV, 2 * half)), const((1, LANES))],
        out_specs=[row(half), row(half), pl.BlockSpec((None, BRANCH_WIDTH, tm), lambda i, j: (i, 0, j)), row(BRANCH_WIDTH), row(LANES)],
        out_shape=[jax.ShapeDtypeStruct((b, s, half), BF16), jax.ShapeDtypeStruct((b, s, half), BF16),
                   jax.ShapeDtypeStruct((b, BRANCH_WIDTH, s), BF16), jax.ShapeDtypeStruct((b, s, BRANCH_WIDTH), BF16),
                   jax.ShapeDtypeStruct((b, s, LANES), F32)],
        scratch_shapes=[pltpu.VMEM((tm + CONV_HALO, 2 * half), F32)],
        compiler_params=_cp(("parallel", "arbitrary")),
        name="ml_project",
    )(h, wqk_p, wv.T.astype(BF16), wo.astype(BF16), wg, conv_p, gb)


def _mlstm_kernel(q_ref, k_ref, vt_ref, og_ref, g_ref, gt_ref, ng_ref, o_ref):
    nb, s = q_ref.shape[0], q_ref.shape[1]
    L = ML_CHUNK
    tril = _tri_incl(L)
    triu = tril.T
    srow = lax.broadcasted_iota(I32, (L, L), 0)
    tcol = lax.broadcasted_iota(I32, (L, L), 1)
    causal = srow <= tcol
    lane = lax.broadcasted_iota(I32, (L, LANES), 1)
    head_lanes = (jnp.where(lane < ML_DQK, 1.0, 0.0).astype(BF16), jnp.where(lane >= ML_DQK, 1.0, 0.0).astype(BF16))

    def chunk(c2, states):
        states = list(states)
        for u in range(ML_UNROLL):
            for bi in range(nb):
                states[bi] = one_chunk(c2 * ML_UNROLL + u, bi, states[bi])
        return tuple(states)

    def one_chunk(c, bi, state):
        new_state = []
        r0 = pl.multiple_of(c * L, L)
        g = g_ref[bi, pl.ds(r0, L), :]
        gt = gt_ref[bi, :, pl.ds(r0, L)]
        bc = _tri_dot(tril, g)
        br = _dot_tri(gt, triu)
        for hd in range(ML_HEADS):
            sl = slice(LANES * hd, LANES * (hd + 1))
            cst, nst, m_prev = state[hd]
            ccol = g[:, hd:hd + 1] - bc[:, ML_HEADS + hd:ML_HEADS + hd + 1]
            brow = br[ML_HEADS + hd:ML_HEADS + hd + 1, :]
            irow = gt[hd:hd + 1, :]
            log_d = jnp.where(causal, brow + ccol, -jnp.inf)
            log_inter = brow + m_prev
            m_t = jnp.maximum(jnp.max(log_d, axis=0, keepdims=True), log_inter)
            w_intra = jnp.exp(log_d - m_t)
            w_inter = jnp.exp(log_inter - m_t)
            pair = slice(LANES * (hd // 2), LANES * (hd // 2 + 1))
            qc = q_ref[bi, pl.ds(r0, L), pair] * head_lanes[hd % 2]
            kc = k_ref[bi, pl.ds(r0, L), pair]
            vt = vt_ref[bi, sl, pl.ds(r0, L)]
            sc = _dot_nt(kc, qc) * w_intra
            num =_dot(vt, sc.astype(BF16)) + w_inter * _dot_nt(cst.astype(BF16), qc)
            qn = _dot_nt(jnp.broadcast_to(nst, (SUBLANES, LANES)).astype(BF16), qc)[0:1, :]
            den = jnp.sum(sc, axis=0, keepdims=True) + w_inter * qn
            hout = num / jnp.maximum(jnp.abs(den), jnp.exp(-m_t))
            b_last = brow[:, L - 1:L]
            lw = b_last - brow + irow
            m_new = jnp.maximum(b_last + m_prev, jnp.max(lw, axis=1, keepdims=True))
            w_in = jnp.exp(lw - m_new)
            decay = jnp.exp(b_last + m_prev - m_new)
            new_state.append((decay * cst + _dot((vt.astype(F32) * w_in).astype(BF16), kc),
                              decay * nst + _dot(jnp.broadcast_to(w_in, (SUBLANES, L)).astype(BF16), kc)[0:1, :],
                              m_new))
            ms = jnp.mean(hout * hout, axis=0, keepdims=True)
            y = (hout * lax.rsqrt(ms + EPS)).T * (og_ref[bi, pl.ds(r0, L), sl].astype(F32) * ng_ref[:, sl])
            o_ref[bi, pl.ds(r0, L), sl] = y.astype(BF16)
        return tuple(new_state)

    zero = (jnp.zeros((ML_DV, LANES), F32), jnp.zeros((1, LANES), F32), jnp.zeros((1, 1), F32))
    lax.fori_loop(0, s // (L * ML_UNROLL), chunk, tuple(tuple(zero for _ in range(ML_HEADS)) for _ in range(nb)))


ML_SEQS = 2
ML_UNROLL = 2


def mlstm(q, k, vt, og, gates, norm_g):
    b, w, s = vt.shape
    nb = ML_SEQS if b % ML_SEQS == 0 else 1
    gt = jnp.swapaxes(gates[:, :, :2 * ML_HEADS], 1, 2)
    seq = lambda n: pl.BlockSpec((nb, s, n), lambda i: (i, 0, 0))
    return pl.pallas_call(
        _mlstm_kernel,
        grid=(b // nb,),
        in_specs=[seq(ML_HEADS * ML_DQK), seq(ML_HEADS * ML_DQK), pl.BlockSpec((nb, w, s), lambda i: (i, 0, 0)), seq(w), seq(LANES),
                  pl.BlockSpec((nb, 2 * ML_HEADS, s), lambda i: (i, 0, 0)),
                  pl.BlockSpec((1, w), lambda i: (0, 0))],
        out_specs=seq(w),
        out_shape=jax.ShapeDtypeStruct((b, s, w), BF16),
        compiler_params=_cp(("parallel",)),
        name="mlstm",
    )(q, k, vt, og, gates, gt, norm_g.reshape(1, w))


def _hg_proj_kernel(h_ref, wf_ref, wq_ref, wi_ref, wg_ref, lb_ref, q_out, k_out, v_out, og_out, lf_out):
    h = h_ref[...]
    fz = _dot(h, wf_ref[...])
    log_lb = lb_ref[0:1, :]
    log_1m = lb_ref[1:2, :]
    one_m = lb_ref[2:3, :]
    u = jnp.exp(-jnp.abs(fz))
    a = log_lb
    bb = log_1m + (jnp.minimum(fz, 0.0) - jnp.log1p(u))
    lf_out[...] = jnp.maximum(a, bb) + jnp.log1p(jnp.exp(-jnp.abs(a - bb)))
    k_out[...] = (one_m * (jnp.where(fz >= 0.0, u, 1.0) / (1.0 + u))).astype(BF16)
    q_out[...] = _silu(_dot(h, wq_ref[...])).astype(BF16)
    v_out[...] = _dot(h, wi_ref[...]).astype(BF16)
    og_out[...] = _silu(_dot(h, wg_ref[...])).astype(BF16)


def hg_project(h, wf, wq, wi, wg, lb):
    b, s, d = h.shape
    tm = min(PROJ_TILE, s)
    w = BRANCH_WIDTH
    lbp = jnp.stack([jnp.log(lb), jnp.log1p(-lb), 1.0 - lb], axis=0)
    const = lambda shape: pl.BlockSpec(shape, lambda i, j: (0,) * len(shape))
    row = lambda n: pl.BlockSpec((None, tm, n), lambda i, j: (i, j, 0))
    return pl.pallas_call(
        _hg_proj_kernel,
        grid=(b, s // tm),
        in_specs=[row(d), const((d, w)), const((d, w)), const((d, w)), const((d, w)), const((3, w))],
        out_specs=[row(w), row(w), row(w), row(w), row(w)],
        out_shape=[jax.ShapeDtypeStruct((b, s, w), BF16)] * 4 + [jax.ShapeDtypeStruct((b, s, w), F32)],
        compiler_params=_cp(("parallel", "parallel")),
        name="hg_project",
    )(h, wf.astype(BF16), wq.astype(BF16), wi.astype(BF16), wg.astype(BF16), lbp)


HG_UNROLL = 4
HG_LEVELS = tuple(HG_CHUNK >> (i + 1) for i in range(HG_CHUNK.bit_length() - 1))


def _hg_tables():
    L = HG_CHUNK
    t = np.arange(L)
    tri = (t[None, :] <= t[:, None]).astype(np.float32)
    mats = [tri]
    x = t[:, None] ^ t[None, :]
    lvl = np.full((L, L), -1, np.int32)
    lvl[t[:, None] == t[None, :]] = 0
    for i, m in enumerate(HG_LEVELS):
        if m < SUBLANES:
            mats.append(tri[(t // (2 * m)) * (2 * m) + m - 1])
        lvl[(t[:, None] > t[None, :]) & (x >= m) & (x < 2 * m)] = i + 1
    return jnp.asarray(np.concatenate(mats, axis=0), BF16), jnp.asarray(lvl)


def _hgrn_kernel(q_ref, k_ref, v_ref, og_ref, lf_ref, ng_ref, tall_ref, lvl_ref, o_ref, st_ref):
    s, wd = q_ref.shape
    L = HG_CHUNK
    st_ref[...] = jnp.zeros_like(st_ref)
    rowi = lax.broadcasted_iota(I32, (L, LANES), 0)

    lvl = lvl_ref[...]
    level_masks = [lvl == i for i in range(len(HG_LEVELS) + 1)]

    def chunk(c, carry):
        r0 = pl.multiple_of(c * L, L)
        tall = tall_ref[...]
        g = lf_ref[pl.ds(r0, L), :]
        hi = g.astype(BF16)
        mid = (g - hi.astype(F32)).astype(BF16)
        cums = (_dot(tall, hi) + _dot(tall, mid)) * LOG2E
        a = cums[0:L]
        qb = q_ref[pl.ds(r0, L), :]
        kb = k_ref[pl.ds(r0, L), :]
        qf = qb.astype(F32)
        kf = kb.astype(F32)
        ws = []
        fine = 0
        for m in HG_LEVELS:
            if m >= SUBLANES:
                ref = jnp.concatenate([jnp.broadcast_to(a[g0 + m - 1:g0 + m, :], (2 * m, wd)) for g0 in range(0, L, 2 * m)], axis=0)
            else:
                fine += 1
                ref = cums[L * fine:L * (fine + 1)]
            e = jnp.exp2(-jnp.abs(a - ref))
            upper = (rowi & m) != 0
            qk = jnp.concatenate([jnp.where(upper, qf[:, LANES * hd:LANES * (hd + 1)], kf[:, LANES * hd:LANES * (hd + 1)])
                                  for hd in range(HG_HEADS)], axis=1)
            ws.append((e * qk).astype(BF16))
        a_last = a[L - 1:L, :]
        qa = (qf * jnp.exp2(a)).astype(BF16)
        kt = (kf * jnp.exp2(a_last - a)).astype(BF16)
        decay = jnp.exp2(a_last)
        for hd in range(HG_HEADS):
            sl = slice(LANES * hd, LANES * (hd + 1))
            vb = v_ref[pl.ds(r0, L), sl]
            sc = jnp.where(level_masks[0], _dot_nt(qb[:, sl], kb[:, sl]), 0.0)
            for i in range(len(HG_LEVELS)):
                w = ws[i][:, sl]
                sc = jnp.where(level_masks[i + 1], _dot_nt(w, w), sc)
            st = st_ref[hd]
            out = _dot(sc.astype(BF16), vb) + _dot_nt(qa[:, sl], st.astype(BF16))
            st_ref[hd] = st * decay[:, sl] + _dot_tn(vb, kt[:, sl])
            ms = jnp.mean(out * out, axis=-1, keepdims=True)
            y = (out * lax.rsqrt(ms + EPS)) * ng_ref[:, sl] * og_ref[pl.ds(r0, L), sl].astype(F32)
            o_ref[pl.ds(r0, L), sl] = y.astype(BF16)
        return carry

    def chunks(c2, carry):
        for u in range(HG_UNROLL):
            chunk(c2 * HG_UNROLL + u, carry)
        return carry

    lax.fori_loop(0, s // (L * HG_UNROLL), chunks, 0)


def hgrn(q, k, v, og, logf, norm_g):
    b, s, w = v.shape
    tall, lvl = _hg_tables()
    seq = pl.BlockSpec((None, s, w), lambda i: (i, 0, 0))
    const = lambda shape: pl.BlockSpec(shape, lambda i: (0,) * len(shape))
    return pl.pallas_call(
        _hgrn_kernel,
        grid=(b,),
        in_specs=[seq, seq, seq, seq, seq, const((1, w)), const(tall.shape), const(lvl.shape)],
        out_specs=seq,
        out_shape=jax.ShapeDtypeStruct((b, s, w), BF16),
        scratch_shapes=[pltpu.VMEM((HG_HEADS, LANES, HG_DK), F32)],
        compiler_params=_cp(("parallel",)),
        name="hgrn",
    )(q, k, v, og, logf, norm_g.reshape(1, w), tall, lvl)


def _merge_kernel(x_ref, h_ref, yf_ref, ym_ref, yh_ref, wg_ref, wb_ref, wo_ref, g1_ref, n2_ref, sc2_ref, sh2_ref, wr_ref, rb_ref,
                  x_out, h2_out, dest_out, w_out, cnt_out):
    d = x_ref.shape[1]
    h = h_ref[...]
    merged = None
    for br, y_ref in enumerate((yf_ref, ym_ref, yh_ref)):
        gate = jax.nn.sigmoid(_dot(h, wg_ref[:, d * br:d * (br + 1)]))
        term = gate * _dot(y_ref[...], wb_ref[br])
        merged = term if merged is None else merged + term
    mixed = _dot(merged.astype(BF16), wo_ref[...])
    x1 = x_ref[...] + g1_ref[...] * mixed
    x_out[...] = x1
    h2 = _norm_mod(x1, n2_ref[...], sc2_ref[...], sh2_ref[...]).astype(BF16)
    h2_out[...] = h2
    for u in range(h2.shape[0] // SUB_TOKENS):
        _route(h2[SUB_TOKENS * u:SUB_TOKENS * (u + 1), :], wr_ref[...], rb_ref[...], dest_out.at[u], w_out.at[u], cnt_out.at[u])


def merge_branches(x, h, y_fox, y_ml, y_hg, w_gates, w_branch, w_out, g1, norm2_g, sc2, sh2, router_w, router_bias):
    b, s, d = x.shape
    tm = MERGE_TILE
    per_seq = s // tm
    per_tile = tm // SUB_TOKENS
    nsub = b * s // SUB_TOKENS
    w = BRANCH_WIDTH
    mat = pl.BlockSpec((per_tile, 2 * N_EXPERTS, SUB_TOKENS), lambda i, j: (i * per_seq + j, 0, 0))
    const = lambda shape: pl.BlockSpec(shape, lambda i, j: (0,) * len(shape))
    row = lambda n: pl.BlockSpec((None, tm, n), lambda i, j: (i, j, 0))
    per_b = pl.BlockSpec((None, 1, d), lambda i, j: (i, 0, 0))
    x1, h2, dest, wts, cnt = pl.pallas_call(
        _merge_kernel,
        grid=(b, s // tm),
        in_specs=[row(d), row(d), row(w), row(w), row(w), const((d, N_BRANCH * d)), const((N_BRANCH, w, d)), const((d, d)),
                  per_b, const((1, d)), per_b, per_b, const((N_EXPERTS, d)), const((N_EXPERTS, 1))],
        out_specs=[row(d), row(d), mat, mat, pl.BlockSpec((per_tile, N_EXPERTS, LANES), lambda i, j: (i * per_seq + j, 0, 0))],
        out_shape=[jax.ShapeDtypeStruct((b, s, d), F32), jax.ShapeDtypeStruct((b, s, d), BF16),
                   jax.ShapeDtypeStruct((nsub, 2 * N_EXPERTS, SUB_TOKENS), BF16),
                   jax.ShapeDtypeStruct((nsub, 2 * N_EXPERTS, SUB_TOKENS), BF16),
                   jax.ShapeDtypeStruct((nsub, N_EXPERTS, LANES), I32)],
        compiler_params=_cp(("parallel", "parallel"), MERGE_VMEM_LIMIT_MIB),
        name="merge_branches",
    )(x, h, y_fox, y_ml, y_hg, w_gates.astype(BF16), w_branch.astype(BF16), w_out.astype(BF16),
      g1.reshape(b, 1, d), norm2_g.reshape(1, d), sc2.reshape(b, 1, d), sh2.reshape(b, 1, d),
      router_w.T.astype(BF16), router_bias.reshape(N_EXPERTS, 1))
    return x1, h2, dest, wts, cnt[:, :, 0]


def _first_max(x, iota, size):
    m = jnp.max(x, axis=0, keepdims=True)
    idx = jnp.min(jnp.where(x == m, iota, size), axis=0, keepdims=True)
    return m, idx


def _route(h, wr, rb, dest_out, w_out, cnt_out):
    n = h.shape[0]
    scores = jax.nn.sigmoid(_dot_nt(wr, h))
    choice = scores + rb
    e_iota = lax.broadcasted_iota(I32, (N_EXPERTS, n), 0)
    c3 = choice.reshape(N_GROUPS, GROUP_SIZE, n)
    i3 = lax.broadcasted_iota(I32, (N_GROUPS, GROUP_SIZE, n), 1)
    m1 = jnp.max(c3, axis=1, keepdims=True)
    i1 = jnp.min(jnp.where(c3 == m1, i3, GROUP_SIZE), axis=1, keepdims=True)
    m2 = jnp.max(jnp.where(i3 == i1, -jnp.inf, c3), axis=1, keepdims=True)
    gs = (m1 + m2).reshape(N_GROUPS, n)
    g_iota = lax.broadcasted_iota(I32, (N_GROUPS, n), 0)
    gsel = jnp.zeros((N_GROUPS, n), F32)
    for _ in range(TOPK_GROUPS):
        _, gi = _first_max(gs, g_iota, N_GROUPS)
        hit = g_iota == gi
        gsel = jnp.where(hit, 1.0, gsel)
        gs = jnp.where(hit, -jnp.inf, gs)
    gmask = jnp.broadcast_to(gsel.reshape(N_GROUPS, 1, n), (N_GROUPS, GROUP_SIZE, n)).reshape(N_EXPERTS, n)
    masked = jnp.where(gmask > 0.0, choice, -jnp.inf)
    sel = jnp.zeros((N_EXPERTS, n), F32)
    for _ in range(TOP_K):
        _, ei = _first_max(masked, e_iota, N_EXPERTS)
        hit = e_iota == ei
        sel = jnp.where(hit, 1.0, sel)
        masked = jnp.where(hit, -jnp.inf, masked)
    tr = lax.broadcasted_iota(I32, (n, n), 0)
    tc = lax.broadcasted_iota(I32, (n, n), 1)
    before = jnp.where(tr < tc, 1.0, 0.0).astype(BF16)
    pos = _dot(sel.astype(BF16), before)
    cnt = jnp.sum(sel, axis=1, keepdims=True)
    units = jnp.floor((cnt + (RUN_ALIGN - 1)) * (1.0 / RUN_ALIGN))
    er = lax.broadcasted_iota(I32, (N_EXPERTS, N_EXPERTS), 0)
    ec = lax.broadcasted_iota(I32, (N_EXPERTS, N_EXPERTS), 1)
    lower = jnp.where(ec < er, 1.0, 0.0).astype(BF16)
    off = _dot(lower, jnp.broadcast_to(units, (N_EXPERTS, LANES)).astype(BF16))[:, 0:1] * RUN_ALIGN
    dest = jnp.where(sel > 0.0, off + pos, float(DEST_NONE))
    dhi = jnp.floor(dest * (1.0 / DEST_RADIX))
    dest_out[0:N_EXPERTS, :] = dhi.astype(BF16)
    dest_out[N_EXPERTS:, :] = (dest - dhi * DEST_RADIX).astype(BF16)
    wsum = jnp.sum(scores * sel, axis=0, keepdims=True)
    w_out[0:N_EXPERTS, :] = jnp.zeros((N_EXPERTS, n), BF16)
    w_out[N_EXPERTS:, :] = (scores * sel / wsum * ROUTE_SCALE).astype(BF16)
    cnt_out[...] = jnp.broadcast_to(cnt, (N_EXPERTS, LANES)).astype(I32)


def _run_tables(cnt, n_blocks):
    units = (cnt + (RUN_ALIGN - 1)) // RUN_ALIGN
    src = jnp.cumsum(units, axis=1) - units
    per_block = MOE_BLOCK // RUN_ALIGN
    tot = jnp.sum(units, axis=0)
    tot_blocks = (tot + per_block - 1) // per_block
    blk_end = jnp.cumsum(tot_blocks)
    base = (blk_end - tot_blocks) * per_block
    dst = base[None, :] + jnp.cumsum(units, axis=0) - units
    n_used = blk_end[-1]
    tail_units = tot_blocks * per_block - tot
    tail_dst = base + tot
    blk = jnp.minimum(jnp.arange(n_blocks), n_used - 1)
    blk_exp = jnp.minimum(jnp.sum(blk[:, None] >= blk_end[None, :], axis=1), N_EXPERTS - 1)
    return (src.reshape(-1).astype(I32), dst.reshape(-1).astype(I32), units.reshape(-1).astype(I32),
            blk_exp.astype(I32), n_used.astype(I32).reshape(1), tail_dst.astype(I32), tail_units.astype(I32),
            jnp.sum(units, axis=1).astype(I32))


def _copy_lists(src, dst, units, nsub):
    s0 = src.reshape(nsub, 1, N_EXPERTS)
    d0 = dst.reshape(nsub, 1, N_EXPERTS)
    u = units.reshape(nsub, 1, N_EXPERTS)
    npair = u // 2
    poff = jnp.cumsum(npair, axis=2) - npair
    p = jnp.arange(SUB_UNITS // 2, dtype=I32).reshape(1, -1, 1)
    own = (p >= poff) & (p < poff + npair)
    psrc = jnp.sum(jnp.where(own, s0 - 2 * poff, 0), axis=2) + 2 * p[:, :, 0]
    pdst = jnp.sum(jnp.where(own, d0 - 2 * poff, 0), axis=2) + 2 * p[:, :, 0]
    odd = u & 1
    soff = jnp.cumsum(odd, axis=2) - odd
    q = jnp.arange(N_EXPERTS, dtype=I32).reshape(1, -1, 1)
    owns = (odd == 1) & (soff == q)
    ssrc = jnp.sum(jnp.where(owns, s0 + u - 1, 0), axis=2)
    sdst = jnp.sum(jnp.where(owns, d0 + u - 1, 0), axis=2)
    flat = lambda a: a.reshape(-1).astype(I32)
    return (flat(psrc), flat(pdst), flat(jnp.sum(npair, axis=2)), flat(ssrc), flat(sdst), flat(jnp.sum(odd, axis=2)))


def _run_bounds(src, units, nsub):
    lo = (src.reshape(nsub, N_EXPERTS) * RUN_ALIGN).astype(F32)
    hi = lo + (units.reshape(nsub, N_EXPERTS) * RUN_ALIGN).astype(F32)
    return jnp.stack([jnp.concatenate([lo, lo], axis=1), jnp.concatenate([hi, hi], axis=1)], axis=1)


def _max_blocks(t):
    nsub = t // SUB_TOKENS
    worst_units = t * TOP_K // RUN_ALIGN + nsub * N_EXPERTS
    per_block = MOE_BLOCK // RUN_ALIGN
    return -(-worst_units // per_block) + N_EXPERTS


SUB_UNITS = SUB_ROWS // RUN_ALIGN
COPY_UNROLL = 4


PAIR_SLOTS = SUB_UNITS // 2


def _piece_copies(lists, step, buf, hbm, sem, to_hbm):
    psrc_ref, pdst_ref, npair_ref, ssrc_ref, sdst_ref, nsingle_ref = lists

    def piece(src_ref, dst_ref, idx, units):
        rows = units * RUN_ALIGN
        v = buf.at[pl.ds(pl.multiple_of(src_ref[idx] * RUN_ALIGN, RUN_ALIGN), rows)]
        g = hbm.at[pl.ds(pl.multiple_of(dst_ref[idx] * RUN_ALIGN, RUN_ALIGN), rows)]
        cp = pltpu.make_async_copy(v, g, sem) if to_hbm else pltpu.make_async_copy(g, v, sem)
        cp.start()

    def issue(src_ref, dst_ref, base, count, units):
        def group(q, carry):
            for r in range(COPY_UNROLL):
                piece(src_ref, dst_ref, base + q * COPY_UNROLL + r, units)
            return carry

        groups = lax.shift_right_logical(count, COPY_UNROLL.bit_length() - 1)
        lax.fori_loop(0, groups, group, 0)
        lax.fori_loop(groups * COPY_UNROLL, count, lambda j, c: (piece(src_ref, dst_ref, base + j, units), c)[1], 0)

    issue(psrc_ref, pdst_ref, step * PAIR_SLOTS, npair_ref[step], 2)
    issue(ssrc_ref, sdst_ref, step * N_EXPERTS, nsingle_ref[step], 1)


TOTAL_BITS = tuple(1 << b for b in range(SUB_UNITS.bit_length()))


def _wait_runs(total_units, buf, hbm, sem, to_hbm):
    for bit in TOTAL_BITS:
        @pl.when((total_units & bit) != 0)
        def _():
            rows = bit * RUN_ALIGN
            v = buf.at[pl.ds(0, rows)]
            g = hbm.at[pl.ds(0, rows)]
            cp = pltpu.make_async_copy(v, g, sem) if to_hbm else pltpu.make_async_copy(g, v, sem)
            cp.wait()


def _dispatch_kernel(psrc_ref, pdst_ref, npair_ref, ssrc_ref, sdst_ref, nsingle_ref, tot_ref, tdst_ref, tunits_ref, nused_ref,
                     h_ref, dest_ref, lohi_ref, xs_out, buf_ref, zero_ref, sem):
    lists = (psrc_ref, pdst_ref, npair_ref, ssrc_ref, sdst_ref, nsingle_ref)
    i = pl.program_id(0)
    nsub = pl.num_programs(0)
    slot = i % 2
    n = h_ref.shape[0]
    chunk = SORT_CHUNK

    for sl in range(2):
        @pl.when((slot == sl) & (i >= 2))
        def _():
            _wait_runs(tot_ref[i - 2], buf_ref.at[sl], xs_out, sem.at[sl], True)

    h = h_ref[...]
    dest = dest_ref[...]
    lo = lohi_ref[0:1, :]
    hi = lohi_ref[1:2, :]
    radix = jnp.where(lax.broadcasted_iota(I32, (1, 2 * N_EXPERTS), 1) < N_EXPERTS, float(DEST_RADIX), 1.0)
    r_e = lax.broadcasted_iota(I32, (chunk, 2 * N_EXPERTS), 0).astype(F32)
    r_t = lax.broadcasted_iota(I32, (chunk, n), 0).astype(F32)
    for sl in range(2):
        @pl.when(slot == sl)
        def _():
            for c in range(SUB_ROWS // chunk):
                own = jnp.where(r_e + c * chunk >= lo, jnp.where(r_e + c * chunk < hi, radix, 0.0), 0.0)
                row_of = _dot(own.astype(BF16), dest)
                p = jnp.where(row_of == r_t + c * chunk, 1.0, 0.0)
                buf_ref[sl, c * chunk:(c + 1) * chunk, :] = _dot(p.astype(BF16), h).astype(BF16)
            _piece_copies(lists, i, buf_ref.at[sl], xs_out, sem.at[sl], True)

    @pl.when(i == nsub - 1)
    def _():
        zero_ref[...] = jnp.zeros_like(zero_ref)

        def tails(e, wait):
            u = tunits_ref[e]
            d0 = tdst_ref[e]
            for bit in TAIL_BITS:
                low = u & (bit - 1)

                @pl.when((u & bit) != 0)
                def _():
                    rows = bit * RUN_ALIGN
                    cp = pltpu.make_async_copy(zero_ref.at[pl.ds(0, rows)],
                                               xs_out.at[pl.ds(pl.multiple_of((d0 + low) * RUN_ALIGN, RUN_ALIGN), rows)], sem.at[2])
                    if wait:
                        cp.wait()
                    else:
                        cp.start()
            return wait

        def unused(b, wait):
            cp = pltpu.make_async_copy(zero_ref.at[pl.ds(0, MOE_BLOCK)],
                                       xs_out.at[pl.ds(pl.multiple_of(b * MOE_BLOCK, MOE_BLOCK), MOE_BLOCK)], sem.at[2])
            if wait:
                cp.wait()
            else:
                cp.start()
            return wait

        n_blocks = xs_out.shape[0] // MOE_BLOCK
        lax.fori_loop(0, N_EXPERTS, lambda e, c: (tails(e, False), c)[1], 0)
        lax.fori_loop(nused_ref[0], n_blocks, lambda b, c: (unused(b, False), c)[1], 0)
        for sl in range(2):
            @pl.when((slot != sl) & (i >= 1))
            def _():
                _wait_runs(tot_ref[i - 1], buf_ref.at[sl], xs_out, sem.at[sl], True)

            @pl.when(slot == sl)
            def _():
                _wait_runs(tot_ref[i], buf_ref.at[sl], xs_out, sem.at[sl], True)
        lax.fori_loop(0, N_EXPERTS, lambda e, c: (tails(e, True), c)[1], 0)
        lax.fori_loop(nused_ref[0], n_blocks, lambda b, c: (unused(b, True), c)[1], 0)


def moe_dispatch(h2, dest, tables, n_rows):
    t, d = h2.shape
    n = SUB_TOKENS
    nsub = t // n
    src, dst, units, _, n_used, tail_dst, tail_units, tot = tables
    grid_spec = pltpu.PrefetchScalarGridSpec(
        num_scalar_prefetch=10,
        grid=(nsub,),
        in_specs=[pl.BlockSpec((n, d), lambda i, *_: (i, 0)), pl.BlockSpec((None, 2 * N_EXPERTS, n), lambda i, *_: (i, 0, 0)),
                  pl.BlockSpec((None, 2, 2 * N_EXPERTS), lambda i, *_: (i, 0, 0))],
        out_specs=pl.BlockSpec(memory_space=pl.ANY),
        scratch_shapes=[pltpu.VMEM((2, SUB_ROWS, d), BF16), pltpu.VMEM((max(SUB_TOKENS, MOE_BLOCK), d), BF16), pltpu.SemaphoreType.DMA((3,))],
    )
    return pl.pallas_call(
        _dispatch_kernel,
        grid_spec=grid_spec,
        out_shape=jax.ShapeDtypeStruct((n_rows, d), BF16),
        compiler_params=_cp(("arbitrary",)),
        name="moe_dispatch",
    )(*_copy_lists(src, dst, units, nsub), tot, tail_dst, tail_units, n_used, h2, dest, _run_bounds(src, units, nsub))


X_SLOTS = 3


def _expert_kernel(blk_exp_ref, n_used_ref, x_hbm, w1_ref, w3_ref, w2_ref, y_ref, xbuf, sem):
    b = pl.program_id(0)
    n_used = n_used_ref[0]

    def x_copy(blk, slot):
        rows = pl.ds(pl.multiple_of(blk * MOE_BLOCK, MOE_BLOCK), MOE_BLOCK)
        return pltpu.make_async_copy(x_hbm.at[rows], xbuf.at[slot], sem.at[slot])

    @pl.when(b == 0)
    def _():
        x_copy(0, 0).start()

        @pl.when(n_used > 1)
        def _():
            x_copy(1, 1).start()

    ahead = b + (X_SLOTS - 1)

    @pl.when(ahead < n_used)
    def _():
        x_copy(ahead, lax.rem(ahead, X_SLOTS)).start()

    @pl.when(b < n_used)
    def _():
        slot = lax.rem(b, X_SLOTS)
        x_copy(b, slot).wait()
        x = xbuf[slot]
        hid = _silu(_dot(x, w1_ref[...].astype(BF16))) * _dot(x, w3_ref[...].astype(BF16))
        y_ref[...] = _dot(hid.astype(BF16), w2_ref[...].astype(BF16)).astype(BF16)

    @pl.when(b >= n_used)
    def _():
        y_ref[...] = jnp.zeros_like(y_ref)


def moe_experts(xs, w1, w3, w2, layer, tables, n_blocks):
    n_rows, d = xs.shape
    blk_exp, n_used = tables[3], tables[4]
    f = w1.shape[-1]

    def w_map(b, be, nu):
        return (layer, be[b], 0, 0)

    grid_spec = pltpu.PrefetchScalarGridSpec(
        num_scalar_prefetch=2,
        grid=(n_blocks,),
        in_specs=[pl.BlockSpec(memory_space=pl.ANY), pl.BlockSpec((None, None, d, f), w_map),
                  pl.BlockSpec((None, None, d, f), w_map), pl.BlockSpec((None, None, f, d), w_map)],
        out_specs=pl.BlockSpec((MOE_BLOCK, d), lambda b, be, nu: (b, 0)),
        scratch_shapes=[pltpu.VMEM((X_SLOTS, MOE_BLOCK, d), BF16), pltpu.SemaphoreType.DMA((X_SLOTS,))],
    )
    return pl.pallas_call(
        _expert_kernel,
        grid_spec=grid_spec,
        out_shape=jax.ShapeDtypeStruct((n_rows, d), BF16),
        compiler_params=_cp(("arbitrary",)),
        name="moe_experts",
    )(blk_exp, n_used, xs, w1, w3, w2)


def _combine_kernel(psrc_ref, pdst_ref, npair_ref, ssrc_ref, sdst_ref, nsingle_ref, tot_ref, ys_ref, dcol_ref, lohi_ref,
                    h_ref, x_ref, g2_ref, ws1_ref, ws3_ref, ws2_ref, o_ref, buf_ref, sem):
    lists = (psrc_ref, pdst_ref, npair_ref, ssrc_ref, sdst_ref, nsingle_ref)
    i = pl.program_id(0)
    nsub = pl.num_programs(0)
    slot = i % 2
    n = h_ref.shape[0]
    chunk = COMBINE_CHUNK

    def fetch(step, sl):
        always = SUB_TOKENS * TOP_K
        buf_ref[sl, always:, :] = jnp.zeros((SUB_ROWS - always, buf_ref.shape[2]), BF16)
        _piece_copies(lists, step, buf_ref.at[sl], ys_ref, sem.at[sl], False)

    @pl.when(i == 0)
    def _():
        fetch(0, 0)

    for sl in range(2):
        @pl.when((slot != sl) & (i + 1 < nsub))
        def _():
            fetch(i + 1, sl)

    h = h_ref[...]
    shared = _dot((_silu(_dot(h, ws1_ref[...])) * _dot(h, ws3_ref[...])).astype(BF16), ws2_ref[...])
    dw_t = dcol_ref[...]
    lo = lohi_ref[:, 0:1]
    hi = lohi_ref[:, 1:2]
    part = lax.broadcasted_iota(I32, (4 * N_EXPERTS, 1), 0)
    radix = jnp.where(part < N_EXPERTS, float(DEST_RADIX),
                      jnp.where(part < 2 * N_EXPERTS, 1.0, jnp.where(part < 3 * N_EXPERTS, WEIGHT_SHIFT, 0.0)))
    r_e = lax.broadcasted_iota(I32, (4 * N_EXPERTS, chunk), 1).astype(F32)
    r_t = lax.broadcasted_iota(I32, (n, chunk), 1).astype(F32)
    for sl in range(2):
        @pl.when(slot == sl)
        def _():
            _wait_runs(tot_ref[i], buf_ref.at[sl], ys_ref, sem.at[sl], False)
            acc = shared
            for c in range(SUB_ROWS // chunk):
                own = jnp.where(r_e + c * chunk >= lo, jnp.where(r_e + c * chunk < hi, radix, 0.0), 0.0).astype(BF16)
                val = _dot(dw_t, own)
                row_of = jnp.floor(val)
                pw = jnp.where(row_of == r_t + c * chunk, (val - row_of) * (1.0 / WEIGHT_SHIFT), 0.0)
                acc = acc + _dot(pw.astype(BF16), buf_ref[sl, c * chunk:(c + 1) * chunk, :])
            o_ref[...] = x_ref[...] + g2_ref[...] * acc


def moe_combine(ys, dest, wts, h2, x1, g2, ws1, ws3, ws2, tables, seq):
    t, d = h2.shape
    n = SUB_TOKENS
    nsub = t // n
    src, dst, units, tot = tables[0], tables[1], tables[2], tables[7]
    dw = jnp.concatenate([dest, wts[:, N_EXPERTS:, :], jnp.zeros((nsub, N_EXPERTS, n), BF16)], axis=1)
    dcol = jnp.swapaxes(dw, 1, 2)
    b2 = _run_bounds(src, units, nsub)
    bounds = jnp.swapaxes(jnp.concatenate([b2, b2], axis=2), 1, 2)
    per_seq = seq // n
    f = ws1.shape[-1]
    pair = pl.BlockSpec((None, n, 4 * N_EXPERTS), lambda i, *_: (i, 0, 0))
    grid_spec = pltpu.PrefetchScalarGridSpec(
        num_scalar_prefetch=7,
        grid=(nsub,),
        in_specs=[pl.BlockSpec(memory_space=pl.ANY), pair,
                  pl.BlockSpec((None, 4 * N_EXPERTS, 2), lambda i, *_: (i, 0, 0)),
                  pl.BlockSpec((n, d), lambda i, *_: (i, 0)), pl.BlockSpec((n, d), lambda i, *_: (i, 0)),
                  pl.BlockSpec((None, 1, d), lambda i, *_: (i // per_seq, 0, 0)),
                  pl.BlockSpec((d, f), lambda i, *_: (0, 0)), pl.BlockSpec((d, f), lambda i, *_: (0, 0)),
                  pl.BlockSpec((f, d), lambda i, *_: (0, 0))],
        out_specs=pl.BlockSpec((n, d), lambda i, *_: (i, 0)),
        scratch_shapes=[pltpu.VMEM((2, SUB_ROWS, d), BF16), pltpu.SemaphoreType.DMA((2,))],
    )
    return pl.pallas_call(
        _combine_kernel,
        grid_spec=grid_spec,
        out_shape=jax.ShapeDtypeStruct((t, d), F32),
        compiler_params=_cp(("arbitrary",)),
        name="moe_combine",
    )(*_copy_lists(src, dst, units, nsub), tot, ys, dcol, bounds, h2, x1, g2, ws1.astype(BF16), ws3.astype(BF16), ws2.astype(BF16))


def _split_w_in(w):
    fh = FOX_HEADS * FOX_HEAD_DIM
    sizes = (fh, fh, fh, FOX_HEADS,
             2 * ML_HEADS * ML_DQK, ML_HEADS * ML_DV, ML_HEADS, ML_HEADS, ML_HEADS * ML_DV,
             HG_HEADS * HG_DK, HG_HEADS * HG_DK, BRANCH_WIDTH, BRANCH_WIDTH,
             N_BRANCH * D_MODEL)
    outs, o = [], 0
    for sz in sizes:
        outs.append(w[:, o:o + sz])
        o += sz
    return outs


def moe_ffn(h2, x1, g2, dest, wts, cnt, w1, w3, w2, layer, ws1, ws3, ws2, seq):
    t, d = h2.shape
    n_blocks = _max_blocks(t)
    tables = _run_tables(cnt, n_blocks)
    xs = moe_dispatch(h2, dest, tables, n_blocks * MOE_BLOCK)
    ys = moe_experts(xs, w1, w3, w2, layer, tables, n_blocks)
    return moe_combine(ys, dest, wts, h2, x1, g2, ws1, ws3, ws2, tables, seq)


def kernel(x, c, ada_w, ada_b, norm1_g, norm2_g, w_in, fox_bf, fox_q_g, fox_k_g, mlstm_conv, mlstm_bi, mlstm_bf, mlstm_norm_g, hgrn_lower_bounds, hgrn_norm_g, w_branch, w_out, router_w, router_bias, exp_w1, exp_w3, exp_w2, sh_w1, sh_w3, sh_w2):
    b, s, d = x.shape
    depth = ada_w.shape[0]
    mod = adaln_mod(c, ada_w, ada_b)
    lb_all = jnp.cumsum(jax.nn.softmax(hgrn_lower_bounds.astype(F32), axis=0), axis=0)
    lb_all = lb_all - lb_all[0]
    for l in range(depth):
        sh1, sc1, g1, sh2, sc2, g2 = [mod[l][:, d * j:d * (j + 1)] for j in range(6)]
        (wfq, wfk, wfv, wff, wmqk, wmv, wmi, wmf, wmo, whf, whq, whi, whg, wgates) = _split_w_in(w_in[l])
        qp, kp, fv, h = fox_project(x, norm1_g[l], sc1, sh1, wfq, wfk, wfv, wff, fox_q_g[l], fox_k_g[l], fox_bf[l])
        y_fox = fox_attention(qp, kp, fv)
        mq, mk, mv, mog, mgates = ml_project(h, wmqk, wmv, wmi, wmf, wmo, mlstm_conv[l], mlstm_bi[l], mlstm_bf[l])
        y_ml = mlstm(mq, mk, mv, mog, mgates, mlstm_norm_g[l])
        hq, hk, hv, hog, hlf = hg_project(h, whf, whq, whi, whg, lb_all[l])
        y_hg = hgrn(hq, hk, hv, hog, hlf, hgrn_norm_g[l])
        x1, h2, dest, wts, cnt = merge_branches(x, h, y_fox, y_ml, y_hg, wgates, w_branch[l], w_out[l], g1, norm2_g[l], sc2, sh2,
                                                router_w[l], router_bias[l])
        x = moe_ffn(h2.reshape(b * s, d), x1.reshape(b * s, d), g2.reshape(b, 1, d), dest, wts, cnt,
                    exp_w1, exp_w3, exp_w2, l, sh_w1[l], sh_w3[l], sh_w2[l], s).reshape(b, s, d)
    return x
```

```python
import jax
import jax.numpy as jnp
import numpy as np
from jax import lax
from jax.experimental import pallas as pl
from jax.experimental.pallas import tpu as pltpu

F32 = jnp.float32
BF16 = jnp.bfloat16
I32 = jnp.int32

LANES = 128
SUBLANES = 8
BF16_ROWS = 16

D_MODEL = 1024
BRANCH_WIDTH = D_MODEL // 2
N_BRANCH = 3
FOX_HEAD_DIM = 64
FOX_HEADS = BRANCH_WIDTH // FOX_HEAD_DIM
ML_HEADS = 4
ML_DV = BRANCH_WIDTH // ML_HEADS
ML_DQK = ML_DV // 2
ML_CONV = 4
ML_CHUNK = 128
HG_HEADS = 4
HG_DK = 128
HG_CHUNK = 128
N_EXPERTS = 64
N_GROUPS = 8
GROUP_SIZE = N_EXPERTS // N_GROUPS
TOPK_GROUPS = 4
TOP_K = 8
ROUTE_SCALE = 2.5
MOE_BLOCK = 1024
SORT_CHUNK = 1024
EPS = 1e-6
NEG = -1e30
LOG2E = 1.4426950408889634

PROJ_TILE = 1024
ATTN_TILE = 256
SUB_TOKENS = 256
MERGE_TILE = 512
RUN_ALIGN = BF16_ROWS
COMBINE_CHUNK = 512
SUB_ROWS = -(-(SUB_TOKENS * TOP_K + N_EXPERTS * (RUN_ALIGN - 1)) // COMBINE_CHUNK) * COMBINE_CHUNK
assert SUB_ROWS % SORT_CHUNK == 0
TAIL_BITS = tuple(1 << b for b in range((MOE_BLOCK // RUN_ALIGN - 1).bit_length()))
DEST_RADIX = 64
DEST_NONE = DEST_RADIX * 127
WEIGHT_SHIFT = 0.25
assert ROUTE_SCALE * WEIGHT_SHIFT < 1.0
assert SUB_ROWS <= DEST_NONE


V7X_VMEM_MIB = 64
VMEM_LIMIT_MIB = 48
MERGE_VMEM_LIMIT_MIB = 56
assert MERGE_VMEM_LIMIT_MIB < V7X_VMEM_MIB


def _cp(sem, vmem_mib=VMEM_LIMIT_MIB):
    return pltpu.CompilerParams(dimension_semantics=sem, vmem_limit_bytes=vmem_mib * 1024 * 1024)


def _dot(a, b):
    return jnp.dot(a, b, preferred_element_type=F32)


def _dot_nt(a, b):
    return lax.dot_general(a, b, (((1,), (1,)), ((), ())), preferred_element_type=F32)


def _dot_tn(a, b):
    return lax.dot_general(a, b, (((0,), (0,)), ((), ())), preferred_element_type=F32)


def _split3(x):
    hi = x.astype(BF16)
    r = x - hi.astype(F32)
    mid = r.astype(BF16)
    lo = (r - mid.astype(F32)).astype(BF16)
    return hi, mid, lo


def _tri_dot(tri, x):
    hi, mid, lo = _split3(x)
    return (_dot(tri, hi) + _dot(tri, mid)) + _dot(tri, lo)


def _dot_tri(x, tri):
    hi, mid, lo = _split3(x)
    return (_dot(hi, tri) + _dot(mid, tri)) + _dot(lo, tri)


def _log_sigmoid(x):
    return jnp.minimum(x, 0.0) - jnp.log1p(jnp.exp(-jnp.abs(x)))


def _silu(x):
    return x * jax.nn.sigmoid(x)


def _tri_incl(n, dtype=BF16):
    r = lax.broadcasted_iota(I32, (n, n), 0)
    c = lax.broadcasted_iota(I32, (n, n), 1)
    return jnp.where(c <= r, 1.0, 0.0).astype(dtype)


def _mod_kernel(c_ref, w_ref, b_ref, o_ref):
    cond = _silu(c_ref[...])
    hi, mid, lo = _split3(cond)
    w = w_ref[...]
    whi, wmid, wlo = _split3(w)
    acc = _dot(hi, whi) + (_dot(hi, wmid) + _dot(mid, whi))
    acc = acc + (_dot(mid, wmid) + _dot(hi, wlo) + _dot(lo, whi))
    o_ref[...] = acc + b_ref[...]


def adaln_mod(c, ada_w, ada_b):
    depth, d, n = ada_w.shape
    b = c.shape[0]
    tn = 1024
    return pl.pallas_call(
        _mod_kernel,
        grid=(depth, n // tn),
        in_specs=[
            pl.BlockSpec((b, d), lambda l, j: (0, 0)),
            pl.BlockSpec((None, d, tn), lambda l, j: (l, 0, j)),
            pl.BlockSpec((None, 1, tn), lambda l, j: (l, 0, j)),
        ],
        out_specs=pl.BlockSpec((None, b, tn), lambda l, j: (l, 0, j)),
        out_shape=jax.ShapeDtypeStruct((depth, b, n), F32),
        compiler_params=_cp(("parallel", "parallel")),
        name="adaln_mod",
    )(c, ada_w, ada_b.reshape(depth, 1, n))


def _norm_mod(x, g, sc, sh):
    ms = jnp.mean(x * x, axis=-1, keepdims=True)
    return x * lax.rsqrt(ms + EPS) * g * (1.0 + sc) + sh


FOX_BIAS_PIECES = 3
FOX_SLOT = 2 * LANES
FOX_BIAS_STRIDE = SUBLANES


def _pack_pieces(x):
    hi, mid, lo = _split3(x)
    p = hi.astype(F32) + pltpu.roll(mid.astype(F32), FOX_HEADS, axis=1) + pltpu.roll(lo.astype(F32), 2 * FOX_HEADS, axis=1)
    return p.astype(BF16)


def _fox_proj_kernel(x_ref, ng_ref, sc_ref, sh_ref, wq_ref, wk_ref, wvt_ref, wf_ref, gq_ref, gk_ref, bf_ref, eq_ref, ek_ref,
                     cq_ref, ck_ref, q_out, k_out, vt_out, h_out, carry_ref):
    @pl.when(pl.program_id(1) == 0)
    def _():
        carry_ref[...] = jnp.zeros_like(carry_ref)

    h = _norm_mod(x_ref[...], ng_ref[...], sc_ref[...], sh_ref[...]).astype(BF16)
    h_out[...] = h
    tm = h.shape[0]
    pr = lax.broadcasted_iota(I32, (LANES, LANES), 0)
    pc = lax.broadcasted_iota(I32, (LANES, LANES), 1)
    avg_pair = jnp.where((pr < FOX_HEAD_DIM) == (pc < FOX_HEAD_DIM), 1.0 / FOX_HEAD_DIM, 0.0).astype(BF16)

    def head_norm(x):
        outs = []
        for pair in range(FOX_HEADS // 2):
            xp = x[:, LANES * pair:LANES * (pair + 1)]
            ms = _dot((xp * xp).astype(BF16), avg_pair)
            outs.append(xp * lax.rsqrt(ms + EPS))
        return outs

    def slots(normed, g_ref, bias):
        parts = []
        for pair in range(FOX_HEADS // 2):
            lanes = slice(LANES * pair, LANES * (pair + 1))
            parts += [normed[pair] * g_ref[:, lanes], bias[:, lanes]]
        return jnp.concatenate(parts, axis=1).astype(BF16)

    lane = lax.broadcasted_iota(I32, (tm, LANES), 1)
    logf = jnp.where(lane < FOX_HEADS, _log_sigmoid(_dot(h, wf_ref[...]) + bf_ref[...]), 0.0)
    cs = _dot(_tri_incl(tm), _pack_pieces(logf))
    cum = cs + pltpu.roll(cs, LANES - FOX_HEADS, axis=1) + pltpu.roll(cs, LANES - 2 * FOX_HEADS, axis=1)
    cum = jnp.where(lane < FOX_HEADS, cum, 0.0) + carry_ref[...]
    carry_ref[...] = cum[tm - 1:tm, :]
    pieces = _pack_pieces(cum * LOG2E)

    q_out[...] = slots(head_norm(_dot(h, wq_ref[...])), gq_ref, _dot(pieces, eq_ref[...]) + cq_ref[...])
    k_out[...] = slots(head_norm(_dot(h, wk_ref[...])), gk_ref, _dot(pieces, ek_ref[...]) + ck_ref[...])
    vt = _dot_nt(wvt_ref[...], h)
    ones = jnp.ones((FOX_HEAD_DIM, tm), F32)
    slots = []
    for hd in range(FOX_HEADS):
        slots += [vt[FOX_HEAD_DIM * hd:FOX_HEAD_DIM * (hd + 1), :], ones]
    vt_out[...] = jnp.concatenate(slots, axis=0).astype(BF16)


def _fox_constants():
    width = FOX_HEADS // 2 * LANES
    eq = np.zeros((LANES, width), np.float32)
    ek = np.zeros((LANES, width), np.float32)
    cq = np.zeros((1, width), np.float32)
    ck = np.zeros((1, width), np.float32)
    for hd in range(FOX_HEADS):
        base = LANES * (hd // 2) + FOX_BIAS_STRIDE * (hd % 2)
        for p in range(FOX_BIAS_PIECES):
            eq[p * FOX_HEADS + hd, base + p] = 1.0
            cq[0, base + FOX_BIAS_PIECES + p] = 1.0
            ck[0, base + p] = 1.0
            ek[p * FOX_HEADS + hd, base + FOX_BIAS_PIECES + p] = -1.0
    return jnp.asarray(eq, BF16), jnp.asarray(ek, BF16), jnp.asarray(cq), jnp.asarray(ck)


def _fox_head_masks():
    m = np.zeros((2, FOX_SLOT), np.float32)
    for a in range(2):
        m[a, FOX_HEAD_DIM * a:FOX_HEAD_DIM * (a + 1)] = 1.0
        m[a, LANES + FOX_BIAS_STRIDE * a:LANES + FOX_BIAS_STRIDE * a + 2 * FOX_BIAS_PIECES] = 1.0
    return jnp.asarray(m, BF16)


def fox_project(x, norm_g, sc, sh, wq, wk, wv, wf, q_g, k_g, bf):
    b, s, d = x.shape
    tm = min(PROJ_TILE, s)
    hp = FOX_HEADS // 2 * FOX_SLOT
    hw = FOX_HEADS * FOX_HEAD_DIM
    wq_p = wq.astype(BF16)
    wk_p = wk.astype(BF16)
    wf_p = jnp.pad(wf, ((0, 0), (0, LANES - FOX_HEADS))).astype(BF16)
    gq = jnp.tile(q_g * (FOX_HEAD_DIM ** -0.5 * LOG2E), FOX_HEADS)[None, :]
    gk = jnp.tile(k_g, FOX_HEADS)[None, :]
    bf_p = jnp.pad(bf, (0, LANES - FOX_HEADS))[None, :]
    eq, ek, cq, ck = _fox_constants()
    const = lambda shape: pl.BlockSpec(shape, lambda i, j: (0,) * len(shape))
    row = lambda n: pl.BlockSpec((None, tm, n), lambda i, j: (i, j, 0))
    per_b = pl.BlockSpec((None, 1, d), lambda i, j: (i, 0, 0))
    return pl.pallas_call(
        _fox_proj_kernel,
        grid=(b, s // tm),
        in_specs=[row(d), const((1, d)), per_b, per_b, const((d, hw)), const((d, hw)), const((BRANCH_WIDTH, d)), const((d, LANES)),
                  const((1, hw)), const((1, hw)), const((1, LANES)), const((LANES, hw)), const((LANES, hw)),
                  const((1, hw)), const((1, hw))],
        out_specs=[row(hp), row(hp), pl.BlockSpec((None, hp, tm), lambda i, j: (i, 0, j)), row(d)],
        out_shape=[jax.ShapeDtypeStruct((b, s, hp), BF16), jax.ShapeDtypeStruct((b, s, hp), BF16),
                   jax.ShapeDtypeStruct((b, hp, s), BF16), jax.ShapeDtypeStruct((b, s, d), BF16)],
        scratch_shapes=[pltpu.VMEM((1, LANES), F32)],
        compiler_params=_cp(("parallel", "arbitrary")),
        name="fox_project",
    )(x, norm_g.reshape(1, d), sc.reshape(b, 1, d), sh.reshape(b, 1, d), wq_p, wk_p, wv.T.astype(BF16), wf_p, gq, gk, bf_p,
      eq, ek, cq, ck)


ATTN_HEADS = 8


def _fox_attn_kernel(q_ref, k_ref, vt_ref, hm_ref, o_ref):
    i = pl.program_id(2)
    t = q_ref.shape[0]
    krow = lax.broadcasted_iota(I32, (t, t), 0)
    qcol = lax.broadcasted_iota(I32, (t, t), 1)
    slot = lambda a: slice(FOX_SLOT * (a // 2), FOX_SLOT * (a // 2 + 1))
    qs = [q_ref[:, slot(a)] * hm_ref[a % 2:a % 2 + 1, :] for a in range(ATTN_HEADS)]

    def scores(j):
        start = pl.multiple_of(j * t, t)
        return tuple(_dot_nt(k_ref[pl.ds(start, t), slot(a)], qs[a]) for a in range(ATTN_HEADS))

    def consume(j, state, ss, masked):
        start = pl.multiple_of(j * t, t)
        new = []
        for a in range(ATTN_HEADS):
            m, acc = state[a]
            s = jnp.where(krow <= qcol, ss[a], NEG) if masked else ss[a]
            m_new = jnp.maximum(m, jnp.max(s, axis=0, keepdims=True))
            p = jnp.exp2(s - m_new)
            alpha = jnp.exp2(m - m_new)
            vt = vt_ref[LANES * a:LANES * (a + 1), pl.ds(start, t)]
            acc = alpha * acc + _dot(vt, p.astype(BF16))
            new.append((m_new, acc))
        return tuple(new)

    def body(j, state):
        return consume(j, state, scores(j), False)

    init = tuple((jnp.full((1, t), NEG, F32), jnp.zeros((LANES, t), F32)) for _ in range(ATTN_HEADS))
    state = lax.fori_loop(0, i, body, init)
    state = consume(i, state, scores(i), True)
    for p in range(ATTN_HEADS // 2):
        halves = []
        for _, acc in (state[2 * p], state[2 * p + 1]):
            halves.append(acc[:FOX_HEAD_DIM, :] / acc[FOX_HEAD_DIM:FOX_HEAD_DIM + 1, :])
        o_ref[:, LANES * p:LANES * (p + 1)] = jnp.concatenate(halves, axis=0).T.astype(BF16)


def fox_attention(qp, kp, vt):
    b, s, hp = qp.shape
    t = ATTN_TILE
    groups = FOX_HEADS // ATTN_HEADS
    return pl.pallas_call(
        _fox_attn_kernel,
        grid=(b, groups, s // t),
        in_specs=[
            pl.BlockSpec((None, t, ATTN_HEADS * LANES), lambda bi, p, i: (bi, i, p)),
            pl.BlockSpec((None, s, ATTN_HEADS * LANES), lambda bi, p, i: (bi, 0, p)),
            pl.BlockSpec((None, ATTN_HEADS * LANES, s), lambda bi, p, i: (bi, p, 0)),
            pl.BlockSpec((2, FOX_SLOT), lambda bi, p, i: (0, 0)),
        ],
        out_specs=pl.BlockSpec((None, t, ATTN_HEADS // 2 * LANES), lambda bi, p, i: (bi, i, p)),
        out_shape=jax.ShapeDtypeStruct((b, s, BRANCH_WIDTH), BF16),
        compiler_params=_cp(("parallel", "parallel", "arbitrary")),
        name="fox_attention",
    )(qp, kp, vt, _fox_head_masks())


CONV_HALO = SUBLANES


def _ml_proj_kernel(h_ref, wqk_ref, wvt_ref, wo_ref, wg_ref, conv_ref, gb_ref, q_out, k_out, vt_out, og_out, g_out, buf_ref):
    tm = h_ref.shape[0]
    half = ML_HEADS * ML_DQK

    @pl.when(pl.program_id(1) == 0)
    def _():
        buf_ref[0:CONV_HALO, :] = jnp.zeros((CONV_HALO, 2 * half), F32)

    h = h_ref[...]
    buf_ref[CONV_HALO:CONV_HALO + tm, :] = _dot(h, wqk_ref[...])
    acc = jnp.zeros((tm, 2 * half), F32)
    for j in range(ML_CONV):
        off = CONV_HALO - (ML_CONV - 1) + j
        acc = acc + conv_ref[j:j + 1, :] * buf_ref[off:off + tm, :]
    buf_ref[0:CONV_HALO, :] = buf_ref[tm:tm + CONV_HALO, :]
    act = _silu(acc)
    q_out[...] = act[:, :half].astype(BF16)
    k_out[...] = (act[:, half:] * (ML_DQK ** -0.5)).astype(BF16)
    vt_out[...] = _dot_nt(wvt_ref[...], h).astype(BF16)
    og_out[...] = jax.nn.sigmoid(_dot(h, wo_ref[...])).astype(BF16)
    g = _dot(h, wg_ref[...]) + gb_ref[...]
    lane = lax.broadcasted_iota(I32, (tm, LANES), 1)
    g_out[...] = jnp.where(lane < ML_HEADS, g, _log_sigmoid(g))


def ml_project(h, wqk, wv, wi, wf, wo, conv, bi, bf):
    b, s, d = h.shape
    tm = min(PROJ_TILE, s)
    half = ML_HEADS * ML_DQK
    wqk_p = wqk.astype(BF16)
    conv_p = conv
    wg = jnp.pad(jnp.concatenate([wi, wf], axis=1), ((0, 0), (0, LANES - 2 * ML_HEADS))).astype(BF16)
    gb = jnp.pad(jnp.concatenate([bi, bf]), (0, LANES - 2 * ML_HEADS))[None, :]
    const = lambda shape: pl.BlockSpec(shape, lambda i, j: (0,) * len(shape))
    row = lambda n: pl.BlockSpec((None, tm, n), lambda i, j: (i, j, 0))
    return pl.pallas_call(
        _ml_proj_kernel,
        grid=(b, s // tm),
        in_specs=[row(d), const((d, 2 * half)), const((BRANCH_WIDTH, d)), const((d, BRANCH_WIDTH)), const((d, LANES)),
                  const((ML_CONV, 2 * half)), const((1, LANES))],
        out_specs=[row(half), row(half), pl.BlockSpec((None, BRANCH_WIDTH, tm), lambda i, j: (i, 0, j)), row(BRANCH_WIDTH), row(LANES)],
        out_shape=[jax.ShapeDtypeStruct((b, s, half), BF16), jax.ShapeDtypeStruct((b, s, half), BF16),
                   jax.ShapeDtypeStruct((b, BRANCH_WIDTH, s), BF16), jax.ShapeDtypeStruct((b, s, BRANCH_WIDTH), BF16),
                   jax.ShapeDtypeStruct((b, s, LANES), F32)],
        scratch_shapes=[pltpu.VMEM((tm + CONV_HALO, 2 * half), F32)],
        compiler_params=_cp(("parallel", "arbitrary")),
        name="ml_project",
    )(h, wqk_p, wv.T.astype(BF16), wo.astype(BF16), wg, conv_p, gb)


def _mlstm_kernel(q_ref, k_ref, vt_ref, og_ref, g_ref, gt_ref, ng_ref, o_ref):
    nb, s = q_ref.shape[0], q_ref.shape[1]
    L = ML_CHUNK
    tril = _tri_incl(L)
    triu = tril.T
    srow = lax.broadcasted_iota(I32, (L, L), 0)
    tcol = lax.broadcasted_iota(I32, (L, L), 1)
    causal = srow <= tcol
    lane = lax.broadcasted_iota(I32, (L, LANES), 1)
    head_lanes = (jnp.where(lane < ML_DQK, 1.0, 0.0).astype(BF16), jnp.where(lane >= ML_DQK, 1.0, 0.0).astype(BF16))

    def chunk(c2, states):
        states = list(states)
        for u in range(ML_UNROLL):
            for bi in range(nb):
                states[bi] = one_chunk(c2 * ML_UNROLL + u, bi, states[bi])
        return tuple(states)

    def one_chunk(c, bi, state):
        new_state = []
        r0 = pl.multiple_of(c * L, L)
        g = g_ref[bi, pl.ds(r0, L), :]
        gt = gt_ref[bi, :, pl.ds(r0, L)]
        bc = _tri_dot(tril, g)
        br = _dot_tri(gt, triu)
        for hd in range(ML_HEADS):
            sl = slice(LANES * hd, LANES * (hd + 1))
            cst, nst, m_prev = state[hd]
            ccol = g[:, hd:hd + 1] - bc[:, ML_HEADS + hd:ML_HEADS + hd + 1]
            brow = br[ML_HEADS + hd:ML_HEADS + hd + 1, :]
            irow = gt[hd:hd + 1, :]
            log_d = jnp.where(causal, brow + ccol, -jnp.inf)
            log_inter = brow + m_prev
            m_t = jnp.maximum(jnp.max(log_d, axis=0, keepdims=True), log_inter)
            w_intra = jnp.exp(log_d - m_t)
            w_inter = jnp.exp(log_inter - m_t)
            pair = slice(LANES * (hd // 2), LANES * (hd // 2 + 1))
            qc = q_ref[bi, pl.ds(r0, L), pair] * head_lanes[hd % 2]
            kc = k_ref[bi, pl.ds(r0, L), pair]
            vt = vt_ref[bi, sl, pl.ds(r0, L)]
            sc = _dot_nt(kc, qc) * w_intra
            num =_dot(vt, sc.astype(BF16)) + w_inter * _dot_nt(cst.astype(BF16), qc)
            qn = _dot_nt(jnp.broadcast_to(nst, (SUBLANES, LANES)).astype(BF16), qc)[0:1, :]
            den = jnp.sum(sc, axis=0, keepdims=True) + w_inter * qn
            hout = num / jnp.maximum(jnp.abs(den), jnp.exp(-m_t))
            b_last = brow[:, L - 1:L]
            lw = b_last - brow + irow
            m_new = jnp.maximum(b_last + m_prev, jnp.max(lw, axis=1, keepdims=True))
            w_in = jnp.exp(lw - m_new)
            decay = jnp.exp(b_last + m_prev - m_new)
            new_state.append((decay * cst + _dot((vt.astype(F32) * w_in).astype(BF16), kc),
                              decay * nst + _dot(jnp.broadcast_to(w_in, (SUBLANES, L)).astype(BF16), kc)[0:1, :],
                              m_new))
            ms = jnp.mean(hout * hout, axis=0, keepdims=True)
            y = (hout * lax.rsqrt(ms + EPS)).T * (og_ref[bi, pl.ds(r0, L), sl].astype(F32) * ng_ref[:, sl])
            o_ref[bi, pl.ds(r0, L), sl] = y.astype(BF16)
        return tuple(new_state)

    zero = (jnp.zeros((ML_DV, LANES), F32), jnp.zeros((1, LANES), F32), jnp.zeros((1, 1), F32))
    lax.fori_loop(0, s // (L * ML_UNROLL), chunk, tuple(tuple(zero for _ in range(ML_HEADS)) for _ in range(nb)))


ML_SEQS = 2
ML_UNROLL = 2


def mlstm(q, k, vt, og, gates, norm_g):
    b, w, s = vt.shape
    nb = ML_SEQS if b % ML_SEQS == 0 else 1
    gt = jnp.swapaxes(gates[:, :, :2 * ML_HEADS], 1, 2)
    seq = lambda n: pl.BlockSpec((nb, s, n), lambda i: (i, 0, 0))
    return pl.pallas_call(
        _mlstm_kernel,
        grid=(b // nb,),
        in_specs=[seq(ML_HEADS * ML_DQK), seq(ML_HEADS * ML_DQK), pl.BlockSpec((nb, w, s), lambda i: (i, 0, 0)), seq(w), seq(LANES),
                  pl.BlockSpec((nb, 2 * ML_HEADS, s), lambda i: (i, 0, 0)),
                  pl.BlockSpec((1, w), lambda i: (0, 0))],
        out_specs=seq(w),
        out_shape=jax.ShapeDtypeStruct((b, s, w), BF16),
        compiler_params=_cp(("parallel",)),
        name="mlstm",
    )(q, k, vt, og, gates, gt, norm_g.reshape(1, w))


def _hg_proj_kernel(h_ref, wf_ref, wq_ref, wi_ref, wg_ref, lb_ref, q_out, k_out, v_out, og_out, lf_out):
    h = h_ref[...]
    fz = _dot(h, wf_ref[...])
    log_lb = lb_ref[0:1, :]
    log_1m = lb_ref[1:2, :]
    one_m = lb_ref[2:3, :]
    u = jnp.exp(-jnp.abs(fz))
    a = log_lb
    bb = log_1m + (jnp.minimum(fz, 0.0) - jnp.log1p(u))
    lf_out[...] = jnp.maximum(a, bb) + jnp.log1p(jnp.exp(-jnp.abs(a - bb)))
    k_out[...] = (one_m * (jnp.where(fz >= 0.0, u, 1.0) / (1.0 + u))).astype(BF16)
    q_out[...] = _silu(_dot(h, wq_ref[...])).astype(BF16)
    v_out[...] = _dot(h, wi_ref[...]).astype(BF16)
    og_out[...] = _silu(_dot(h, wg_ref[...])).astype(BF16)


def hg_project(h, wf, wq, wi, wg, lb):
    b, s, d = h.shape
    tm = min(PROJ_TILE, s)
    w = BRANCH_WIDTH
    lbp = jnp.stack([jnp.log(lb), jnp.log1p(-lb), 1.0 - lb], axis=0)
    const = lambda shape: pl.BlockSpec(shape, lambda i, j: (0,) * len(shape))
    row = lambda n: pl.BlockSpec((None, tm, n), lambda i, j: (i, j, 0))
    return pl.pallas_call(
        _hg_proj_kernel,
        grid=(b, s // tm),
        in_specs=[row(d), const((d, w)), const((d, w)), const((d, w)), const((d, w)), const((3, w))],
        out_specs=[row(w), row(w), row(w), row(w), row(w)],
        out_shape=[jax.ShapeDtypeStruct((b, s, w), BF16)] * 4 + [jax.ShapeDtypeStruct((b, s, w), F32)],
        compiler_params=_cp(("parallel", "parallel")),
        name="hg_project",
    )(h, wf.astype(BF16), wq.astype(BF16), wi.astype(BF16), wg.astype(BF16), lbp)


HG_UNROLL = 4
HG_LEVELS = tuple(HG_CHUNK >> (i + 1) for i in range(HG_CHUNK.bit_length() - 1))


def _hg_tables():
    L = HG_CHUNK
    t = np.arange(L)
    tri = (t[None, :] <= t[:, None]).astype(np.float32)
    mats = [tri]
    x = t[:, None] ^ t[None, :]
    lvl = np.full((L, L), -1, np.int32)
    lvl[t[:, None] == t[None, :]] = 0
    for i, m in enumerate(HG_LEVELS):
        if m < SUBLANES:
            mats.append(tri[(t // (2 * m)) * (2 * m) + m - 1])
        lvl[(t[:, None] > t[None, :]) & (x >= m) & (x < 2 * m)] = i + 1
    return jnp.asarray(np.concatenate(mats, axis=0), BF16), jnp.asarray(lvl)


def _hgrn_kernel(q_ref, k_ref, v_ref, og_ref, lf_ref, ng_ref, tall_ref, lvl_ref, o_ref, st_ref):
    s, wd = q_ref.shape
    L = HG_CHUNK
    st_ref[...] = jnp.zeros_like(st_ref)
    rowi = lax.broadcasted_iota(I32, (L, LANES), 0)

    lvl = lvl_ref[...]
    level_masks = [lvl == i for i in range(len(HG_LEVELS) + 1)]

    def chunk(c, carry):
        r0 = pl.multiple_of(c * L, L)
        tall = tall_ref[...]
        g = lf_ref[pl.ds(r0, L), :]
        hi = g.astype(BF16)
        mid = (g - hi.astype(F32)).astype(BF16)
        cums = (_dot(tall, hi) + _dot(tall, mid)) * LOG2E
        a = cums[0:L]
        qb = q_ref[pl.ds(r0, L), :]
        kb = k_ref[pl.ds(r0, L), :]
        qf = qb.astype(F32)
        kf = kb.astype(F32)
        ws = []
        fine = 0
        for m in HG_LEVELS:
            if m >= SUBLANES:
                ref = jnp.concatenate([jnp.broadcast_to(a[g0 + m - 1:g0 + m, :], (2 * m, wd)) for g0 in range(0, L, 2 * m)], axis=0)
            else:
                fine += 1
                ref = cums[L * fine:L * (fine + 1)]
            e = jnp.exp2(-jnp.abs(a - ref))
            upper = (rowi & m) != 0
            qk = jnp.concatenate([jnp.where(upper, qf[:, LANES * hd:LANES * (hd + 1)], kf[:, LANES * hd:LANES * (hd + 1)])
                                  for hd in range(HG_HEADS)], axis=1)
            ws.append((e * qk).astype(BF16))
        a_last = a[L - 1:L, :]
        qa = (qf * jnp.exp2(a)).astype(BF16)
        kt = (kf * jnp.exp2(a_last - a)).astype(BF16)
        decay = jnp.exp2(a_last)
        for hd in range(HG_HEADS):
            sl = slice(LANES * hd, LANES * (hd + 1))
            vb = v_ref[pl.ds(r0, L), sl]
            sc = jnp.where(level_masks[0], _dot_nt(qb[:, sl], kb[:, sl]), 0.0)
            for i in range(len(HG_LEVELS)):
                w = ws[i][:, sl]
                sc = jnp.where(level_masks[i + 1], _dot_nt(w, w), sc)
            st = st_ref[hd]
            out = _dot(sc.astype(BF16), vb) + _dot_nt(qa[:, sl], st.astype(BF16))
            st_ref[hd] = st * decay[:, sl] + _dot_tn(vb, kt[:, sl])
            ms = jnp.mean(out * out, axis=-1, keepdims=True)
            y = (out * lax.rsqrt(ms + EPS)) * ng_ref[:, sl] * og_ref[pl.ds(r0, L), sl].astype(F32)
            o_ref[pl.ds(r0, L), sl] = y.astype(BF16)
        return carry

    def chunks(c2, carry):
        for u in range(HG_UNROLL):
            chunk(c2 * HG_UNROLL + u, carry)
        return carry

    lax.fori_loop(0, s // (L * HG_UNROLL), chunks, 0)


def hgrn(q, k, v, og, logf, norm_g):
    b, s, w = v.shape
    tall, lvl = _hg_tables()
    seq = pl.BlockSpec((None, s, w), lambda i: (i, 0, 0))
    const = lambda shape: pl.BlockSpec(shape, lambda i: (0,) * len(shape))
    return pl.pallas_call(
        _hgrn_kernel,
        grid=(b,),
        in_specs=[seq, seq, seq, seq, seq, const((1, w)), const(tall.shape), const(lvl.shape)],
        out_specs=seq,
        out_shape=jax.ShapeDtypeStruct((b, s, w), BF16),
        scratch_shapes=[pltpu.VMEM((HG_HEADS, LANES, HG_DK), F32)],
        compiler_params=_cp(("parallel",)),
        name="hgrn",
    )(q, k, v, og, logf, norm_g.reshape(1, w), tall, lvl)


def _merge_kernel(x_ref, h_ref, yf_ref, ym_ref, yh_ref, wg_ref, wb_ref, wo_ref, g1_ref, n2_ref, sc2_ref, sh2_ref, wr_ref, rb_ref,
                  x_out, h2_out, dest_out, w_out, cnt_out, prev_ref):
    d = x_ref.shape[1]

    @pl.when(pl.program_id(0) == 0)
    def _():
        prev_ref[...] = jnp.zeros_like(prev_ref)

    prev = prev_ref[...]
    picks = [_route_select(prev[SUB_TOKENS * u:SUB_TOKENS * (u + 1), :], wr_ref[...], rb_ref[...])
             for u in range(prev.shape[0] // SUB_TOKENS)]
    h = h_ref[...]
    merged = None
    for br, y_ref in enumerate((yf_ref, ym_ref, yh_ref)):
        gate = jax.nn.sigmoid(_dot(h, wg_ref[:, d * br:d * (br + 1)]))
        term = gate * _dot(y_ref[...], wb_ref[br])
        merged = term if merged is None else merged + term
    mixed = _dot(merged.astype(BF16), wo_ref[...])
    x1 = x_ref[...] + g1_ref[...] * mixed
    x_out[...] = x1
    h2 = _norm_mod(x1, n2_ref[...], sc2_ref[...], sh2_ref[...]).astype(BF16)
    h2_out[...] = h2
    prev_ref[...] = h2
    for u, (scores, sel) in enumerate(picks):
        _route_place(scores, sel, dest_out.at[u], w_out.at[u], cnt_out.at[u])


def merge_branches(x, h, y_fox, y_ml, y_hg, w_gates, w_branch, w_out, g1, norm2_g, sc2, sh2, router_w, router_bias):
    b, s, d = x.shape
    tm = MERGE_TILE
    per_seq = s // tm
    per_tile = tm // SUB_TOKENS
    nsub = b * s // SUB_TOKENS
    w = BRANCH_WIDTH
    n_tiles = b * per_seq
    tile = lambda g: jnp.minimum(g, n_tiles - 1)
    routed = lambda g: jnp.maximum(g - 1, 0)
    mat = pl.BlockSpec((per_tile, 2 * N_EXPERTS, SUB_TOKENS), lambda g: (routed(g), 0, 0))
    const = lambda shape: pl.BlockSpec(shape, lambda g: (0,) * len(shape))
    row = lambda n: pl.BlockSpec((None, tm, n), lambda g: (tile(g) // per_seq, tile(g) % per_seq, 0))
    per_b = pl.BlockSpec((None, 1, d), lambda g: (tile(g) // per_seq, 0, 0))
    x1, h2, dest, wts, cnt = pl.pallas_call(
        _merge_kernel,
        grid=(n_tiles + 1,),
        in_specs=[row(d), row(d), row(w), row(w), row(w), const((d, N_BRANCH * d)), const((N_BRANCH, w, d)), const((d, d)),
                  per_b, const((1, d)), per_b, per_b, const((N_EXPERTS, d)), const((N_EXPERTS, 1))],
        out_specs=[row(d), row(d), mat, mat, pl.BlockSpec((per_tile, N_EXPERTS, LANES), lambda g: (routed(g), 0, 0))],
        out_shape=[jax.ShapeDtypeStruct((b, s, d), F32), jax.ShapeDtypeStruct((b, s, d), BF16),
                   jax.ShapeDtypeStruct((nsub, 2 * N_EXPERTS, SUB_TOKENS), BF16),
                   jax.ShapeDtypeStruct((nsub, 2 * N_EXPERTS, SUB_TOKENS), BF16),
                   jax.ShapeDtypeStruct((nsub, N_EXPERTS, LANES), I32)],
        scratch_shapes=[pltpu.VMEM((tm, d), BF16)],
        compiler_params=_cp(("arbitrary",), MERGE_VMEM_LIMIT_MIB),
        name="merge_branches",
    )(x, h, y_fox, y_ml, y_hg, w_gates.astype(BF16), w_branch.astype(BF16), w_out.astype(BF16),
      g1.reshape(b, 1, d), norm2_g.reshape(1, d), sc2.reshape(b, 1, d), sh2.reshape(b, 1, d),
      router_w.T.astype(BF16), router_bias.reshape(N_EXPERTS, 1))
    return x1, h2, dest, wts, cnt[:, :, 0]


def _first_max(x, iota, size):
    m = jnp.max(x, axis=0, keepdims=True)
    idx = jnp.min(jnp.where(x == m, iota, size), axis=0, keepdims=True)
    return m, idx


def _route_select(h, wr, rb):
    n = h.shape[0]
    scores = jax.nn.sigmoid(_dot_nt(wr, h))
    choice = scores + rb
    e_iota = lax.broadcasted_iota(I32, (N_EXPERTS, n), 0)
    c3 = choice.reshape(N_GROUPS, GROUP_SIZE, n)
    i3 = lax.broadcasted_iota(I32, (N_GROUPS, GROUP_SIZE, n), 1)
    m1 = jnp.max(c3, axis=1, keepdims=True)
    i1 = jnp.min(jnp.where(c3 == m1, i3, GROUP_SIZE), axis=1, keepdims=True)
    m2 = jnp.max(jnp.where(i3 == i1, -jnp.inf, c3), axis=1, keepdims=True)
    gs = (m1 + m2).reshape(N_GROUPS, n)
    g_iota = lax.broadcasted_iota(I32, (N_GROUPS, n), 0)
    gsel = jnp.zeros((N_GROUPS, n), F32)
    for _ in range(TOPK_GROUPS):
        _, gi = _first_max(gs, g_iota, N_GROUPS)
        hit = g_iota == gi
        gsel = jnp.where(hit, 1.0, gsel)
        gs = jnp.where(hit, -jnp.inf, gs)
    gmask = jnp.broadcast_to(gsel.reshape(N_GROUPS, 1, n), (N_GROUPS, GROUP_SIZE, n)).reshape(N_EXPERTS, n)
    masked = jnp.where(gmask > 0.0, choice, -jnp.inf)
    sel = jnp.zeros((N_EXPERTS, n), F32)
    for _ in range(TOP_K):
        _, ei = _first_max(masked, e_iota, N_EXPERTS)
        hit = e_iota == ei
        sel = jnp.where(hit, 1.0, sel)
        masked = jnp.where(hit, -jnp.inf, masked)
    return scores, sel


def _route_place(scores, sel, dest_out, w_out, cnt_out):
    n = scores.shape[1]
    tr = lax.broadcasted_iota(I32, (n, n), 0)
    tc = lax.broadcasted_iota(I32, (n, n), 1)
    before = jnp.where(tr < tc, 1.0, 0.0).astype(BF16)
    pos = _dot(sel.astype(BF16), before)
    cnt = jnp.sum(sel, axis=1, keepdims=True)
    units = jnp.floor((cnt + (RUN_ALIGN - 1)) * (1.0 / RUN_ALIGN))
    er = lax.broadcasted_iota(I32, (N_EXPERTS, N_EXPERTS), 0)
    ec = lax.broadcasted_iota(I32, (N_EXPERTS, N_EXPERTS), 1)
    lower = jnp.where(ec < er, 1.0, 0.0).astype(BF16)
    off = _dot(lower, jnp.broadcast_to(units, (N_EXPERTS, LANES)).astype(BF16))[:, 0:1] * RUN_ALIGN
    dest = jnp.where(sel > 0.0, off + pos, float(DEST_NONE))
    dhi = jnp.floor(dest * (1.0 / DEST_RADIX))
    dest_out[0:N_EXPERTS, :] = dhi.astype(BF16)
    dest_out[N_EXPERTS:, :] = (dest - dhi * DEST_RADIX).astype(BF16)
    wsum = jnp.sum(scores * sel, axis=0, keepdims=True)
    w_out[0:N_EXPERTS, :] = jnp.zeros((N_EXPERTS, n), BF16)
    w_out[N_EXPERTS:, :] = (scores * sel / wsum * ROUTE_SCALE).astype(BF16)
    cnt_out[...] = jnp.broadcast_to(cnt, (N_EXPERTS, LANES)).astype(I32)


def _run_tables(cnt, n_blocks):
    units = (cnt + (RUN_ALIGN - 1)) // RUN_ALIGN
    src = jnp.cumsum(units, axis=1) - units
    per_block = MOE_BLOCK // RUN_ALIGN
    tot = jnp.sum(units, axis=0)
    tot_blocks = (tot + per_block - 1) // per_block
    blk_end = jnp.cumsum(tot_blocks)
    base = (blk_end - tot_blocks) * per_block
    dst = base[None, :] + jnp.cumsum(units, axis=0) - units
    n_used = blk_end[-1]
    tail_units = tot_blocks * per_block - tot
    tail_dst = base + tot
    blk = jnp.minimum(jnp.arange(n_blocks), n_used - 1)
    blk_exp = jnp.minimum(jnp.sum(blk[:, None] >= blk_end[None, :], axis=1), N_EXPERTS - 1)
    return (src.reshape(-1).astype(I32), dst.reshape(-1).astype(I32), units.reshape(-1).astype(I32),
            blk_exp.astype(I32), n_used.astype(I32).reshape(1), tail_dst.astype(I32), tail_units.astype(I32),
            jnp.sum(units, axis=1).astype(I32))


def _copy_lists(src, dst, units, nsub):
    s0 = src.reshape(nsub, 1, N_EXPERTS)
    d0 = dst.reshape(nsub, 1, N_EXPERTS)
    u = units.reshape(nsub, 1, N_EXPERTS)
    npair = u // 2
    poff = jnp.cumsum(npair, axis=2) - npair
    p = jnp.arange(SUB_UNITS // 2, dtype=I32).reshape(1, -1, 1)
    own = (p >= poff) & (p < poff + npair)
    psrc = jnp.sum(jnp.where(own, s0 - 2 * poff, 0), axis=2) + 2 * p[:, :, 0]
    pdst = jnp.sum(jnp.where(own, d0 - 2 * poff, 0), axis=2) + 2 * p[:, :, 0]
    odd = u & 1
    soff = jnp.cumsum(odd, axis=2) - odd
    q = jnp.arange(N_EXPERTS, dtype=I32).reshape(1, -1, 1)
    owns = (odd == 1) & (soff == q)
    ssrc = jnp.sum(jnp.where(owns, s0 + u - 1, 0), axis=2)
    sdst = jnp.sum(jnp.where(owns, d0 + u - 1, 0), axis=2)
    flat = lambda a: a.reshape(-1).astype(I32)
    return (flat(psrc), flat(pdst), flat(jnp.sum(npair, axis=2)), flat(ssrc), flat(sdst), flat(jnp.sum(odd, axis=2)))


def _run_bounds(src, units, nsub):
    lo = (src.reshape(nsub, N_EXPERTS) * RUN_ALIGN).astype(F32)
    hi = lo + (units.reshape(nsub, N_EXPERTS) * RUN_ALIGN).astype(F32)
    return jnp.stack([jnp.concatenate([lo, lo], axis=1), jnp.concatenate([hi, hi], axis=1)], axis=1)


def _max_blocks(t):
    nsub = t // SUB_TOKENS
    worst_units = t * TOP_K // RUN_ALIGN + nsub * N_EXPERTS
    per_block = MOE_BLOCK // RUN_ALIGN
    return -(-worst_units // per_block) + N_EXPERTS


SUB_UNITS = SUB_ROWS // RUN_ALIGN
COPY_UNROLL = 4


PAIR_SLOTS = SUB_UNITS // 2


def _piece_copies(lists, step, buf, hbm, sem, to_hbm):
    psrc_ref, pdst_ref, npair_ref, ssrc_ref, sdst_ref, nsingle_ref = lists

    def piece(src_ref, dst_ref, idx, units):
        rows = units * RUN_ALIGN
        v = buf.at[pl.ds(pl.multiple_of(src_ref[idx] * RUN_ALIGN, RUN_ALIGN), rows)]
        g = hbm.at[pl.ds(pl.multiple_of(dst_ref[idx] * RUN_ALIGN, RUN_ALIGN), rows)]
        cp = pltpu.make_async_copy(v, g, sem) if to_hbm else pltpu.make_async_copy(g, v, sem)
        cp.start()

    def issue(src_ref, dst_ref, base, count, units):
        def group(q, carry):
            for r in range(COPY_UNROLL):
                piece(src_ref, dst_ref, base + q * COPY_UNROLL + r, units)
            return carry

        groups = lax.shift_right_logical(count, COPY_UNROLL.bit_length() - 1)
        lax.fori_loop(0, groups, group, 0)
        lax.fori_loop(groups * COPY_UNROLL, count, lambda j, c: (piece(src_ref, dst_ref, base + j, units), c)[1], 0)

    issue(psrc_ref, pdst_ref, step * PAIR_SLOTS, npair_ref[step], 2)
    issue(ssrc_ref, sdst_ref, step * N_EXPERTS, nsingle_ref[step], 1)


TOTAL_BITS = tuple(1 << b for b in range(SUB_UNITS.bit_length()))


def _wait_runs(total_units, buf, hbm, sem, to_hbm):
    for bit in TOTAL_BITS:
        @pl.when((total_units & bit) != 0)
        def _():
            rows = bit * RUN_ALIGN
            v = buf.at[pl.ds(0, rows)]
            g = hbm.at[pl.ds(0, rows)]
            cp = pltpu.make_async_copy(v, g, sem) if to_hbm else pltpu.make_async_copy(g, v, sem)
            cp.wait()


def _dispatch_kernel(psrc_ref, pdst_ref, npair_ref, ssrc_ref, sdst_ref, nsingle_ref, tot_ref, tdst_ref, tunits_ref, nused_ref,
                     h_ref, dest_ref, lohi_ref, xs_out, buf_ref, zero_ref, sem):
    lists = (psrc_ref, pdst_ref, npair_ref, ssrc_ref, sdst_ref, nsingle_ref)
    i = pl.program_id(0)
    nsub = pl.num_programs(0)
    slot = i % 2
    n = h_ref.shape[0]
    chunk = SORT_CHUNK

    for sl in range(2):
        @pl.when((slot == sl) & (i >= 2))
        def _():
            _wait_runs(tot_ref[i - 2], buf_ref.at[sl], xs_out, sem.at[sl], True)

    h = h_ref[...]
    dest = dest_ref[...]
    lo = lohi_ref[0:1, :]
    hi = lohi_ref[1:2, :]
    radix = jnp.where(lax.broadcasted_iota(I32, (1, 2 * N_EXPERTS), 1) < N_EXPERTS, float(DEST_RADIX), 1.0)
    r_e = lax.broadcasted_iota(I32, (chunk, 2 * N_EXPERTS), 0).astype(F32)
    r_t = lax.broadcasted_iota(I32, (chunk, n), 0).astype(F32)
    for sl in range(2):
        @pl.when(slot == sl)
        def _():
            for c in range(SUB_ROWS // chunk):
                own = jnp.where(r_e + c * chunk >= lo, jnp.where(r_e + c * chunk < hi, radix, 0.0), 0.0)
                row_of = _dot(own.astype(BF16), dest)
                p = jnp.where(row_of == r_t + c * chunk, 1.0, 0.0)
                buf_ref[sl, c * chunk:(c + 1) * chunk, :] = _dot(p.astype(BF16), h).astype(BF16)
            _piece_copies(lists, i, buf_ref.at[sl], xs_out, sem.at[sl], True)

    @pl.when(i == nsub - 1)
    def _():
        zero_ref[...] = jnp.zeros_like(zero_ref)

        def tails(e, wait):
            u = tunits_ref[e]
            d0 = tdst_ref[e]
            for bit in TAIL_BITS:
                low = u & (bit - 1)

                @pl.when((u & bit) != 0)
                def _():
                    rows = bit * RUN_ALIGN
                    cp = pltpu.make_async_copy(zero_ref.at[pl.ds(0, rows)],
                                               xs_out.at[pl.ds(pl.multiple_of((d0 + low) * RUN_ALIGN, RUN_ALIGN), rows)], sem.at[2])
                    if wait:
                        cp.wait()
                    else:
                        cp.start()
            return wait

        def unused(b, wait):
            cp = pltpu.make_async_copy(zero_ref.at[pl.ds(0, MOE_BLOCK)],
                                       xs_out.at[pl.ds(pl.multiple_of(b * MOE_BLOCK, MOE_BLOCK), MOE_BLOCK)], sem.at[2])
            if wait:
                cp.wait()
            else:
                cp.start()
            return wait

        n_blocks = xs_out.shape[0] // MOE_BLOCK
        lax.fori_loop(0, N_EXPERTS, lambda e, c: (tails(e, False), c)[1], 0)
        lax.fori_loop(nused_ref[0], n_blocks, lambda b, c: (unused(b, False), c)[1], 0)
        for sl in range(2):
            @pl.when((slot != sl) & (i >= 1))
            def _():
                _wait_runs(tot_ref[i - 1], buf_ref.at[sl], xs_out, sem.at[sl], True)

            @pl.when(slot == sl)
            def _():
                _wait_runs(tot_ref[i], buf_ref.at[sl], xs_out, sem.at[sl], True)
        lax.fori_loop(0, N_EXPERTS, lambda e, c: (tails(e, True), c)[1], 0)
        lax.fori_loop(nused_ref[0], n_blocks, lambda b, c: (unused(b, True), c)[1], 0)


def moe_dispatch(h2, dest, tables, n_rows):
    t, d = h2.shape
    n = SUB_TOKENS
    nsub = t // n
    src, dst, units, _, n_used, tail_dst, tail_units, tot = tables
    grid_spec = pltpu.PrefetchScalarGridSpec(
        num_scalar_prefetch=10,
        grid=(nsub,),
        in_specs=[pl.BlockSpec((n, d), lambda i, *_: (i, 0)), pl.BlockSpec((None, 2 * N_EXPERTS, n), lambda i, *_: (i, 0, 0)),
                  pl.BlockSpec((None, 2, 2 * N_EXPERTS), lambda i, *_: (i, 0, 0))],
        out_specs=pl.BlockSpec(memory_space=pl.ANY),
        scratch_shapes=[pltpu.VMEM((2, SUB_ROWS, d), BF16), pltpu.VMEM((max(SUB_TOKENS, MOE_BLOCK), d), BF16), pltpu.SemaphoreType.DMA((3,))],
    )
    return pl.pallas_call(
        _dispatch_kernel,
        grid_spec=grid_spec,
        out_shape=jax.ShapeDtypeStruct((n_rows, d), BF16),
        compiler_params=_cp(("arbitrary",)),
        name="moe_dispatch",
    )(*_copy_lists(src, dst, units, nsub), tot, tail_dst, tail_units, n_used, h2, dest, _run_bounds(src, units, nsub))


X_SLOTS = 3


def _expert_kernel(blk_exp_ref, n_used_ref, x_hbm, w1_ref, w3_ref, w2_ref, y_ref, xbuf, sem):
    b = pl.program_id(0)
    n_used = n_used_ref[0]

    def x_copy(blk, slot):
        rows = pl.ds(pl.multiple_of(blk * MOE_BLOCK, MOE_BLOCK), MOE_BLOCK)
        return pltpu.make_async_copy(x_hbm.at[rows], xbuf.at[slot], sem.at[slot])

    @pl.when(b == 0)
    def _():
        x_copy(0, 0).start()

        @pl.when(n_used > 1)
        def _():
            x_copy(1, 1).start()

    ahead = b + (X_SLOTS - 1)

    @pl.when(ahead < n_used)
    def _():
        x_copy(ahead, lax.rem(ahead, X_SLOTS)).start()

    @pl.when(b < n_used)
    def _():
        slot = lax.rem(b, X_SLOTS)
        x_copy(b, slot).wait()
        x = xbuf[slot]
        hid = _silu(_dot(x, w1_ref[...].astype(BF16))) * _dot(x, w3_ref[...].astype(BF16))
        y_ref[...] = _dot(hid.astype(BF16), w2_ref[...].astype(BF16)).astype(BF16)

    @pl.when(b >= n_used)
    def _():
        y_ref[...] = jnp.zeros_like(y_ref)


def moe_experts(xs, w1, w3, w2, layer, tables, n_blocks):
    n_rows, d = xs.shape
    blk_exp, n_used = tables[3], tables[4]
    f = w1.shape[-1]

    def w_map(b, be, nu):
        return (layer, be[b], 0, 0)

    grid_spec = pltpu.PrefetchScalarGridSpec(
        num_scalar_prefetch=2,
        grid=(n_blocks,),
        in_specs=[pl.BlockSpec(memory_space=pl.ANY), pl.BlockSpec((None, None, d, f), w_map),
                  pl.BlockSpec((None, None, d, f), w_map), pl.BlockSpec((None, None, f, d), w_map)],
        out_specs=pl.BlockSpec((MOE_BLOCK, d), lambda b, be, nu: (b, 0)),
        scratch_shapes=[pltpu.VMEM((X_SLOTS, MOE_BLOCK, d), BF16), pltpu.SemaphoreType.DMA((X_SLOTS,))],
    )
    return pl.pallas_call(
        _expert_kernel,
        grid_spec=grid_spec,
        out_shape=jax.ShapeDtypeStruct((n_rows, d), BF16),
        compiler_params=_cp(("arbitrary",)),
        name="moe_experts",
    )(blk_exp, n_used, xs, w1, w3, w2)


def _combine_kernel(psrc_ref, pdst_ref, npair_ref, ssrc_ref, sdst_ref, nsingle_ref, tot_ref, ys_ref, dcol_ref, lohi_ref,
                    h_ref, x_ref, g2_ref, ws1_ref, ws3_ref, ws2_ref, o_ref, buf_ref, sem):
    lists = (psrc_ref, pdst_ref, npair_ref, ssrc_ref, sdst_ref, nsingle_ref)
    i = pl.program_id(0)
    nsub = pl.num_programs(0)
    slot = i % 2
    n = h_ref.shape[0]
    chunk = COMBINE_CHUNK

    def fetch(step, sl):
        always = SUB_TOKENS * TOP_K
        buf_ref[sl, always:, :] = jnp.zeros((SUB_ROWS - always, buf_ref.shape[2]), BF16)
        _piece_copies(lists, step, buf_ref.at[sl], ys_ref, sem.at[sl], False)

    @pl.when(i == 0)
    def _():
        fetch(0, 0)

    for sl in range(2):
        @pl.when((slot != sl) & (i + 1 < nsub))
        def _():
            fetch(i + 1, sl)

    h = h_ref[...]
    shared = _dot((_silu(_dot(h, ws1_ref[...])) * _dot(h, ws3_ref[...])).astype(BF16), ws2_ref[...])
    dw_t = dcol_ref[...]
    lo = lohi_ref[:, 0:1]
    hi = lohi_ref[:, 1:2]
    part = lax.broadcasted_iota(I32, (4 * N_EXPERTS, 1), 0)
    radix = jnp.where(part < N_EXPERTS, float(DEST_RADIX),
                      jnp.where(part < 2 * N_EXPERTS, 1.0, jnp.where(part < 3 * N_EXPERTS, WEIGHT_SHIFT, 0.0)))
    r_e = lax.broadcasted_iota(I32, (4 * N_EXPERTS, chunk), 1).astype(F32)
    r_t = lax.broadcasted_iota(I32, (n, chunk), 1).astype(F32)
    for sl in range(2):
        @pl.when(slot == sl)
        def _():
            _wait_runs(tot_ref[i], buf_ref.at[sl], ys_ref, sem.at[sl], False)
            acc = shared
            for c in range(SUB_ROWS // chunk):
                own = jnp.where(r_e + c * chunk >= lo, jnp.where(r_e + c * chunk < hi, radix, 0.0), 0.0).astype(BF16)
                val = _dot(dw_t, own)
                row_of = jnp.floor(val)
                pw = jnp.where(row_of == r_t + c * chunk, (val - row_of) * (1.0 / WEIGHT_SHIFT), 0.0)
                acc = acc + _dot(pw.astype(BF16), buf_ref[sl, c * chunk:(c + 1) * chunk, :])
            o_ref[...] = x_ref[...] + g2_ref[...] * acc


def moe_combine(ys, dest, wts, h2, x1, g2, ws1, ws3, ws2, tables, seq):
    t, d = h2.shape
    n = SUB_TOKENS
    nsub = t // n
    src, dst, units, tot = tables[0], tables[1], tables[2], tables[7]
    dw = jnp.concatenate([dest, wts[:, N_EXPERTS:, :], jnp.zeros((nsub, N_EXPERTS, n), BF16)], axis=1)
    dcol = jnp.swapaxes(dw, 1, 2)
    b2 = _run_bounds(src, units, nsub)
    bounds = jnp.swapaxes(jnp.concatenate([b2, b2], axis=2), 1, 2)
    per_seq = seq // n
    f = ws1.shape[-1]
    pair = pl.BlockSpec((None, n, 4 * N_EXPERTS), lambda i, *_: (i, 0, 0))
    grid_spec = pltpu.PrefetchScalarGridSpec(
        num_scalar_prefetch=7,
        grid=(nsub,),
        in_specs=[pl.BlockSpec(memory_space=pl.ANY), pair,
                  pl.BlockSpec((None, 4 * N_EXPERTS, 2), lambda i, *_: (i, 0, 0)),
                  pl.BlockSpec((n, d), lambda i, *_: (i, 0)), pl.BlockSpec((n, d), lambda i, *_: (i, 0)),
                  pl.BlockSpec((None, 1, d), lambda i, *_: (i // per_seq, 0, 0)),
                  pl.BlockSpec((d, f), lambda i, *_: (0, 0)), pl.BlockSpec((d, f), lambda i, *_: (0, 0)),
                  pl.BlockSpec((f, d), lambda i, *_: (0, 0))],
        out_specs=pl.BlockSpec((n, d), lambda i, *_: (i, 0)),
        scratch_shapes=[pltpu.VMEM((2, SUB_ROWS, d), BF16), pltpu.SemaphoreType.DMA((2,))],
    )
    return pl.pallas_call(
        _combine_kernel,
        grid_spec=grid_spec,
        out_shape=jax.ShapeDtypeStruct((t, d), F32),
        compiler_params=_cp(("arbitrary",)),
        name="moe_combine",
    )(*_copy_lists(src, dst, units, nsub), tot, ys, dcol, bounds, h2, x1, g2, ws1.astype(BF16), ws3.astype(BF16), ws2.astype(BF16))


def _split_w_in(w):
    fh = FOX_HEADS * FOX_HEAD_DIM
    sizes = (fh, fh, fh, FOX_HEADS,
             2 * ML_HEADS * ML_DQK, ML_HEADS * ML_DV, ML_HEADS, ML_HEADS, ML_HEADS * ML_DV,
             HG_HEADS * HG_DK, HG_HEADS * HG_DK, BRANCH_WIDTH, BRANCH_WIDTH,
             N_BRANCH * D_MODEL)
    outs, o = [], 0
    for sz in sizes:
        outs.append(w[:, o:o + sz])
        o += sz
    return outs


def moe_ffn(h2, x1, g2, dest, wts, cnt, w1, w3, w2, layer, ws1, ws3, ws2, seq):
    t, d = h2.shape
    n_blocks = _max_blocks(t)
    tables = _run_tables(cnt, n_blocks)
    xs = moe_dispatch(h2, dest, tables, n_blocks * MOE_BLOCK)
    ys = moe_experts(xs, w1, w3, w2, layer, tables, n_blocks)
    return moe_combine(ys, dest, wts, h2, x1, g2, ws1, ws3, ws2, tables, seq)


def kernel(x, c, ada_w, ada_b, norm1_g, norm2_g, w_in, fox_bf, fox_q_g, fox_k_g, mlstm_conv, mlstm_bi, mlstm_bf, mlstm_norm_g, hgrn_lower_bounds, hgrn_norm_g, w_branch, w_out, router_w, router_bias, exp_w1, exp_w3, exp_w2, sh_w1, sh_w3, sh_w2):
    b, s, d = x.shape
    depth = ada_w.shape[0]
    mod = adaln_mod(c, ada_w, ada_b)
    lb_all = jnp.cumsum(jax.nn.softmax(hgrn_lower_bounds.astype(F32), axis=0), axis=0)
    lb_all = lb_all - lb_all[0]
    for l in range(depth):
        sh1, sc1, g1, sh2, sc2, g2 = [mod[l][:, d * j:d * (j + 1)] for j in range(6)]
        (wfq, wfk, wfv, wff, wmqk, wmv, wmi, wmf, wmo, whf, whq, whi, whg, wgates) = _split_w_in(w_in[l])
        qp, kp, fv, h = fox_project(x, norm1_g[l], sc1, sh1, wfq, wfk, wfv, wff, fox_q_g[l], fox_k_g[l], fox_bf[l])
        y_fox = fox_attention(qp, kp, fv)
        mq, mk, mv, mog, mgates = ml_project(h, wmqk, wmv, wmi, wmf, wmo, mlstm_conv[l], mlstm_bi[l], mlstm_bf[l])
        y_ml = mlstm(mq, mk, mv, mog, mgates, mlstm_norm_g[l])
        hq, hk, hv, hog, hlf = hg_project(h, whf, whq, whi, whg, lb_all[l])
        y_hg = hgrn(hq, hk, hv, hog, hlf, hgrn_norm_g[l])
        x1, h2, dest, wts, cnt = merge_branches(x, h, y_fox, y_ml, y_hg, wgates, w_branch[l], w_out[l], g1, norm2_g[l], sc2, sh2,
                                                router_w[l], router_bias[l])
        x = moe_ffn(h2.reshape(b * s, d), x1.reshape(b * s, d), g2.reshape(b, 1, d), dest, wts, cnt,
                    exp_w1, exp_w3, exp_w2, l, sh_w1[l], sh_w3[l], sh_w2[l], s).reshape(b, s, d)
    return x
```

```python
import jax
import jax.numpy as jnp
import numpy as np
from jax import lax
from jax.experimental import pallas as pl
from jax.experimental.pallas import tpu as pltpu

F32 = jnp.float32
BF16 = jnp.bfloat16
I32 = jnp.int32

LANES = 128
SUBLANES = 8
BF16_ROWS = 16

D_MODEL = 1024
BRANCH_WIDTH = D_MODEL // 2
N_BRANCH = 3
FOX_HEAD_DIM = 64
FOX_HEADS = BRANCH_WIDTH // FOX_HEAD_DIM
ML_HEADS = 4
ML_DV = BRANCH_WIDTH // ML_HEADS
ML_DQK = ML_DV // 2
ML_CONV = 4
ML_CHUNK = 128
HG_HEADS = 4
HG_DK = 128
HG_CHUNK = 128
N_EXPERTS = 64
N_GROUPS = 8
GROUP_SIZE = N_EXPERTS // N_GROUPS
TOPK_GROUPS = 4
TOP_K = 8
ROUTE_SCALE = 2.5
MOE_BLOCK = 1024
SORT_CHUNK = 1024
EPS = 1e-6
NEG = -1e30
LOG2E = 1.4426950408889634

PROJ_TILE = 1024
ATTN_TILE = 256
SUB_TOKENS = 256
MERGE_TILE = 512
RUN_ALIGN = BF16_ROWS
COMBINE_CHUNK = 1024
SUB_ROWS = -(-(SUB_TOKENS * TOP_K + N_EXPERTS * (RUN_ALIGN - 1)) // COMBINE_CHUNK) * COMBINE_CHUNK
assert SUB_ROWS % SORT_CHUNK == 0
TAIL_BITS = tuple(1 << b for b in range((MOE_BLOCK // RUN_ALIGN - 1).bit_length()))
DEST_RADIX = 64
DEST_NONE = DEST_RADIX * 127
WEIGHT_SHIFT = 0.25
assert ROUTE_SCALE * WEIGHT_SHIFT < 1.0
assert SUB_ROWS <= DEST_NONE


V7X_VMEM_MIB = 64
VMEM_LIMIT_MIB = 48
MERGE_VMEM_LIMIT_MIB = 56
assert MERGE_VMEM_LIMIT_MIB < V7X_VMEM_MIB


def _cp(sem, vmem_mib=VMEM_LIMIT_MIB):
    return pltpu.CompilerParams(dimension_semantics=sem, vmem_limit_bytes=vmem_mib * 1024 * 1024)


def _dot(a, b):
    return jnp.dot(a, b, preferred_element_type=F32)


def _dot_nt(a, b):
    return lax.dot_general(a, b, (((1,), (1,)), ((), ())), preferred_element_type=F32)


def _dot_tn(a, b):
    return lax.dot_general(a, b, (((0,), (0,)), ((), ())), preferred_element_type=F32)


def _split3(x):
    hi = x.astype(BF16)
    r = x - hi.astype(F32)
    mid = r.astype(BF16)
    lo = (r - mid.astype(F32)).astype(BF16)
    return hi, mid, lo


def _tri_dot(tri, x):
    hi, mid, lo = _split3(x)
    return (_dot(tri, hi) + _dot(tri, mid)) + _dot(tri, lo)


def _dot_tri(x, tri):
    hi, mid, lo = _split3(x)
    return (_dot(hi, tri) + _dot(mid, tri)) + _dot(lo, tri)


def _log_sigmoid(x):
    return jnp.minimum(x, 0.0) - jnp.log1p(jnp.exp(-jnp.abs(x)))


def _silu(x):
    return x * jax.nn.sigmoid(x)


def _tri_incl(n, dtype=BF16):
    r = lax.broadcasted_iota(I32, (n, n), 0)
    c = lax.broadcasted_iota(I32, (n, n), 1)
    return jnp.where(c <= r, 1.0, 0.0).astype(dtype)


def _mod_kernel(c_ref, w_ref, b_ref, o_ref):
    cond = _silu(c_ref[...])
    hi, mid, lo = _split3(cond)
    w = w_ref[...]
    whi, wmid, wlo = _split3(w)
    acc = _dot(hi, whi) + (_dot(hi, wmid) + _dot(mid, whi))
    acc = acc + (_dot(mid, wmid) + _dot(hi, wlo) + _dot(lo, whi))
    o_ref[...] = acc + b_ref[...]


def adaln_mod(c, ada_w, ada_b):
    depth, d, n = ada_w.shape
    b = c.shape[0]
    tn = 1024
    return pl.pallas_call(
        _mod_kernel,
        grid=(depth, n // tn),
        in_specs=[
            pl.BlockSpec((b, d), lambda l, j: (0, 0)),
            pl.BlockSpec((None, d, tn), lambda l, j: (l, 0, j)),
            pl.BlockSpec((None, 1, tn), lambda l, j: (l, 0, j)),
        ],
        out_specs=pl.BlockSpec((None, b, tn), lambda l, j: (l, 0, j)),
        out_shape=jax.ShapeDtypeStruct((depth, b, n), F32),
        compiler_params=_cp(("parallel", "parallel")),
        name="adaln_mod",
    )(c, ada_w, ada_b.reshape(depth, 1, n))


def _norm_mod(x, g, sc, sh):
    ms = jnp.mean(x * x, axis=-1, keepdims=True)
    return x * lax.rsqrt(ms + EPS) * g * (1.0 + sc) + sh


FOX_BIAS_PIECES = 3
FOX_SLOT = 2 * LANES
FOX_BIAS_STRIDE = SUBLANES


def _pack_pieces(x):
    hi, mid, lo = _split3(x)
    p = hi.astype(F32) + pltpu.roll(mid.astype(F32), FOX_HEADS, axis=1) + pltpu.roll(lo.astype(F32), 2 * FOX_HEADS, axis=1)
    return p.astype(BF16)


def _fox_proj_kernel(x_ref, ng_ref, sc_ref, sh_ref, wq_ref, wk_ref, wvt_ref, wf_ref, gq_ref, gk_ref, bf_ref, eq_ref, ek_ref,
                     cq_ref, ck_ref, q_out, k_out, vt_out, h_out, carry_ref):
    @pl.when(pl.program_id(1) == 0)
    def _():
        carry_ref[...] = jnp.zeros_like(carry_ref)

    h = _norm_mod(x_ref[...], ng_ref[...], sc_ref[...], sh_ref[...]).astype(BF16)
    h_out[...] = h
    tm = h.shape[0]
    pr = lax.broadcasted_iota(I32, (LANES, LANES), 0)
    pc = lax.broadcasted_iota(I32, (LANES, LANES), 1)
    avg_pair = jnp.where((pr < FOX_HEAD_DIM) == (pc < FOX_HEAD_DIM), 1.0 / FOX_HEAD_DIM, 0.0).astype(BF16)

    def head_norm(x):
        outs = []
        for pair in range(FOX_HEADS // 2):
            xp = x[:, LANES * pair:LANES * (pair + 1)]
            ms = _dot((xp * xp).astype(BF16), avg_pair)
            outs.append(xp * lax.rsqrt(ms + EPS))
        return outs

    def slots(normed, g_ref, bias):
        parts = []
        for pair in range(FOX_HEADS // 2):
            lanes = slice(LANES * pair, LANES * (pair + 1))
            parts += [normed[pair] * g_ref[:, lanes], bias[:, lanes]]
        return jnp.concatenate(parts, axis=1).astype(BF16)

    lane = lax.broadcasted_iota(I32, (tm, LANES), 1)
    logf = jnp.where(lane < FOX_HEADS, _log_sigmoid(_dot(h, wf_ref[...]) + bf_ref[...]), 0.0)
    cs = _dot(_tri_incl(tm), _pack_pieces(logf))
    cum = cs + pltpu.roll(cs, LANES - FOX_HEADS, axis=1) + pltpu.roll(cs, LANES - 2 * FOX_HEADS, axis=1)
    cum = jnp.where(lane < FOX_HEADS, cum, 0.0) + carry_ref[...]
    carry_ref[...] = cum[tm - 1:tm, :]
    pieces = _pack_pieces(cum * LOG2E)

    q_out[...] = slots(head_norm(_dot(h, wq_ref[...])), gq_ref, _dot(pieces, eq_ref[...]) + cq_ref[...])
    k_out[...] = slots(head_norm(_dot(h, wk_ref[...])), gk_ref, _dot(pieces, ek_ref[...]) + ck_ref[...])
    vt = _dot_nt(wvt_ref[...], h)
    ones = jnp.ones((FOX_HEAD_DIM, tm), F32)
    slots = []
    for hd in range(FOX_HEADS):
        slots += [vt[FOX_HEAD_DIM * hd:FOX_HEAD_DIM * (hd + 1), :], ones]
    vt_out[...] = jnp.concatenate(slots, axis=0).astype(BF16)


def _fox_constants():
    width = FOX_HEADS // 2 * LANES
    eq = np.zeros((LANES, width), np.float32)
    ek = np.zeros((LANES, width), np.float32)
    cq = np.zeros((1, width), np.float32)
    ck = np.zeros((1, width), np.float32)
    for hd in range(FOX_HEADS):
        base = LANES * (hd // 2) + FOX_BIAS_STRIDE * (hd % 2)
        for p in range(FOX_BIAS_PIECES):
            eq[p * FOX_HEADS + hd, base + p] = 1.0
            cq[0, base + FOX_BIAS_PIECES + p] = 1.0
            ck[0, base + p] = 1.0
            ek[p * FOX_HEADS + hd, base + FOX_BIAS_PIECES + p] = -1.0
    return jnp.asarray(eq, BF16), jnp.asarray(ek, BF16), jnp.asarray(cq), jnp.asarray(ck)


def _fox_head_masks():
    m = np.zeros((2, FOX_SLOT), np.float32)
    for a in range(2):
        m[a, FOX_HEAD_DIM * a:FOX_HEAD_DIM * (a + 1)] = 1.0
        m[a, LANES + FOX_BIAS_STRIDE * a:LANES + FOX_BIAS_STRIDE * a + 2 * FOX_BIAS_PIECES] = 1.0
    return jnp.asarray(m, BF16)


def fox_project(x, norm_g, sc, sh, wq, wk, wv, wf, q_g, k_g, bf):
    b, s, d = x.shape
    tm = min(PROJ_TILE, s)
    hp = FOX_HEADS // 2 * FOX_SLOT
    hw = FOX_HEADS * FOX_HEAD_DIM
    wq_p = wq.astype(BF16)
    wk_p = wk.astype(BF16)
    wf_p = jnp.pad(wf, ((0, 0), (0, LANES - FOX_HEADS))).astype(BF16)
    gq = jnp.tile(q_g * (FOX_HEAD_DIM ** -0.5 * LOG2E), FOX_HEADS)[None, :]
    gk = jnp.tile(k_g, FOX_HEADS)[None, :]
    bf_p = jnp.pad(bf, (0, LANES - FOX_HEADS))[None, :]
    eq, ek, cq, ck = _fox_constants()
    const = lambda shape: pl.BlockSpec(shape, lambda i, j: (0,) * len(shape))
    row = lambda n: pl.BlockSpec((None, tm, n), lambda i, j: (i, j, 0))
    per_b = pl.BlockSpec((None, 1, d), lambda i, j: (i, 0, 0))
    return pl.pallas_call(
        _fox_proj_kernel,
        grid=(b, s // tm),
        in_specs=[row(d), const((1, d)), per_b, per_b, const((d, hw)), const((d, hw)), const((BRANCH_WIDTH, d)), const((d, LANES)),
                  const((1, hw)), const((1, hw)), const((1, LANES)), const((LANES, hw)), const((LANES, hw)),
                  const((1, hw)), const((1, hw))],
        out_specs=[row(hp), row(hp), pl.BlockSpec((None, hp, tm), lambda i, j: (i, 0, j)), row(d)],
        out_shape=[jax.ShapeDtypeStruct((b, s, hp), BF16), jax.ShapeDtypeStruct((b, s, hp), BF16),
                   jax.ShapeDtypeStruct((b, hp, s), BF16), jax.ShapeDtypeStruct((b, s, d), BF16)],
        scratch_shapes=[pltpu.VMEM((1, LANES), F32)],
        compiler_params=_cp(("parallel", "arbitrary")),
        name="fox_project",
    )(x, norm_g.reshape(1, d), sc.reshape(b, 1, d), sh.reshape(b, 1, d), wq_p, wk_p, wv.T.astype(BF16), wf_p, gq, gk, bf_p,
      eq, ek, cq, ck)


ATTN_HEADS = 8


def _fox_attn_kernel(q_ref, k_ref, vt_ref, hm_ref, o_ref):
    i = pl.program_id(2)
    t = q_ref.shape[0]
    krow = lax.broadcasted_iota(I32, (t, t), 0)
    qcol = lax.broadcasted_iota(I32, (t, t), 1)
    slot = lambda a: slice(FOX_SLOT * (a // 2), FOX_SLOT * (a // 2 + 1))
    qs = [q_ref[:, slot(a)] * hm_ref[a % 2:a % 2 + 1, :] for a in range(ATTN_HEADS)]

    def scores(j):
        start = pl.multiple_of(j * t, t)
        return tuple(_dot_nt(k_ref[pl.ds(start, t), slot(a)], qs[a]) for a in range(ATTN_HEADS))

    def consume(j, state, ss, masked):
        start = pl.multiple_of(j * t, t)
        new = []
        for a in range(ATTN_HEADS):
            m, acc = state[a]
            s = jnp.where(krow <= qcol, ss[a], NEG) if masked else ss[a]
            m_new = jnp.maximum(m, jnp.max(s, axis=0, keepdims=True))
            p = jnp.exp2(s - m_new)
            alpha = jnp.exp2(m - m_new)
            vt = vt_ref[LANES * a:LANES * (a + 1), pl.ds(start, t)]
            acc = alpha * acc + _dot(vt, p.astype(BF16))
            new.append((m_new, acc))
        return tuple(new)

    def body(j, state):
        return consume(j, state, scores(j), False)

    init = tuple((jnp.full((1, t), NEG, F32), jnp.zeros((LANES, t), F32)) for _ in range(ATTN_HEADS))
    state = lax.fori_loop(0, i, body, init)
    state = consume(i, state, scores(i), True)
    for p in range(ATTN_HEADS // 2):
        halves = []
        for _, acc in (state[2 * p], state[2 * p + 1]):
            halves.append(acc[:FOX_HEAD_DIM, :] / acc[FOX_HEAD_DIM:FOX_HEAD_DIM + 1, :])
        o_ref[:, LANES * p:LANES * (p + 1)] = jnp.concatenate(halves, axis=0).T.astype(BF16)


def fox_attention(qp, kp, vt):
    b, s, hp = qp.shape
    t = ATTN_TILE
    groups = FOX_HEADS // ATTN_HEADS
    return pl.pallas_call(
        _fox_attn_kernel,
        grid=(b, groups, s // t),
        in_specs=[
            pl.BlockSpec((None, t, ATTN_HEADS * LANES), lambda bi, p, i: (bi, i, p)),
            pl.BlockSpec((None, s, ATTN_HEADS * LANES), lambda bi, p, i: (bi, 0, p)),
            pl.BlockSpec((None, ATTN_HEADS * LANES, s), lambda bi, p, i: (bi, p, 0)),
            pl.BlockSpec((2, FOX_SLOT), lambda bi, p, i: (0, 0)),
        ],
        out_specs=pl.BlockSpec((None, t, ATTN_HEADS // 2 * LANES), lambda bi, p, i: (bi, i, p)),
        out_shape=jax.ShapeDtypeStruct((b, s, BRANCH_WIDTH), BF16),
        compiler_params=_cp(("parallel", "parallel", "arbitrary")),
        name="fox_attention",
    )(qp, kp, vt, _fox_head_masks())


CONV_HALO = SUBLANES


def _ml_proj_kernel(h_ref, wqk_ref, wvt_ref, wo_ref, wg_ref, conv_ref, gb_ref, q_out, k_out, vt_out, og_out, g_out, buf_ref):
    tm = h_ref.shape[0]
    half = ML_HEADS * ML_DQK

    @pl.when(pl.program_id(1) == 0)
    def _():
        buf_ref[0:CONV_HALO, :] = jnp.zeros((CONV_HALO, 2 * half), F32)

    h = h_ref[...]
    buf_ref[CONV_HALO:CONV_HALO + tm, :] = _dot(h, wqk_ref[...])
    acc = jnp.zeros((tm, 2 * half), F32)
    for j in range(ML_CONV):
        off = CONV_HALO - (ML_CONV - 1) + j
        acc = acc + conv_ref[j:j + 1, :] * buf_ref[off:off + tm, :]
    buf_ref[0:CONV_HALO, :] = buf_ref[tm:tm + CONV_HALO, :]
    act = _silu(acc)
    q_out[...] = act[:, :half].astype(BF16)
    k_out[...] = (act[:, half:] * (ML_DQK ** -0.5)).astype(BF16)
    vt_out[...] = _dot_nt(wvt_ref[...], h).astype(BF16)
    og_out[...] = jax.nn.sigmoid(_dot(h, wo_ref[...])).astype(BF16)
    g = _dot(h, wg_ref[...]) + gb_ref[...]
    lane = lax.broadcasted_iota(I32, (tm, LANES), 1)
    g_out[...] = jnp.where(lane < ML_HEADS, g, _log_sigmoid(g))


def ml_project(h, wqk, wv, wi, wf, wo, conv, bi, bf):
    b, s, d = h.shape
    tm = min(PROJ_TILE, s)
    half = ML_HEADS * ML_DQK
    wqk_p = wqk.astype(BF16)
    conv_p = conv
    wg = jnp.pad(jnp.concatenate([wi, wf], axis=1), ((0, 0), (0, LANES - 2 * ML_HEADS))).astype(BF16)
    gb = jnp.pad(jnp.concatenate([bi, bf]), (0, LANES - 2 * ML_HEADS))[None, :]
    const = lambda shape: pl.BlockSpec(shape, lambda i, j: (0,) * len(shape))
    row = lambda n: pl.BlockSpec((None, tm, n), lambda i, j: (i, j, 0))
    return pl.pallas_call(
        _ml_proj_kernel,
        grid=(b, s // tm),
        in_specs=[row(d), const((d, 2 * half)), const((BRANCH_WIDTH, d)), const((d, BRANCH_WIDTH)), const((d, LANES)),
                  const((ML_CONV, 2 * half)), const((1, LANES))],
        out_specs=[row(half), row(half), pl.BlockSpec((None, BRANCH_WIDTH, tm), lambda i, j: (i, 0, j)), row(BRANCH_WIDTH), row(LANES)],
        out_shape=[jax.ShapeDtypeStruct((b, s, half), BF16), jax.ShapeDtypeStruct((b, s, half), BF16),
                   jax.ShapeDtypeStruct((b, BRANCH_WIDTH, s), BF16), jax.ShapeDtypeStruct((b, s, BRANCH_WIDTH), BF16),
                   jax.ShapeDtypeStruct((b, s, LANES), F32)],
        scratch_shapes=[pltpu.VMEM((tm + CONV_HALO, 2 * half), F32)],
        compiler_params=_cp(("parallel", "arbitrary")),
        name="ml_project",
    )(h, wqk_p, wv.T.astype(BF16), wo.astype(BF16), wg, conv_p, gb)


def _mlstm_kernel(q_ref, k_ref, vt_ref, og_ref, g_ref, gt_ref, ng_ref, o_ref):
    nb, s = q_ref.shape[0], q_ref.shape[1]
    L = ML_CHUNK
    tril = _tri_incl(L)
    triu = tril.T
    srow = lax.broadcasted_iota(I32, (L, L), 0)
    tcol = lax.broadcasted_iota(I32, (L, L), 1)
    causal = srow <= tcol
    lane = lax.broadcasted_iota(I32, (L, LANES), 1)
    head_lanes = (jnp.where(lane < ML_DQK, 1.0, 0.0).astype(BF16), jnp.where(lane >= ML_DQK, 1.0, 0.0).astype(BF16))

    def chunk(c2, states):
        states = list(states)
        for u in range(ML_UNROLL):
            for bi in range(nb):
                states[bi] = one_chunk(c2 * ML_UNROLL + u, bi, states[bi])
        return tuple(states)

    def one_chunk(c, bi, state):
        new_state = []
        r0 = pl.multiple_of(c * L, L)
        g = g_ref[bi, pl.ds(r0, L), :]
        gt = gt_ref[bi, :, pl.ds(r0, L)]
        bc = _tri_dot(tril, g)
        br = _dot_tri(gt, triu)
        for hd in range(ML_HEADS):
            sl = slice(LANES * hd, LANES * (hd + 1))
            cst, nst, m_prev = state[hd]
            ccol = g[:, hd:hd + 1] - bc[:, ML_HEADS + hd:ML_HEADS + hd + 1]
            brow = br[ML_HEADS + hd:ML_HEADS + hd + 1, :]
            irow = gt[hd:hd + 1, :]
            log_d = jnp.where(causal, brow + ccol, -jnp.inf)
            log_inter = brow + m_prev
            m_t = jnp.maximum(jnp.max(log_d, axis=0, keepdims=True), log_inter)
            w_intra = jnp.exp(log_d - m_t)
            w_inter = jnp.exp(log_inter - m_t)
            pair = slice(LANES * (hd // 2), LANES * (hd // 2 + 1))
            qc = q_ref[bi, pl.ds(r0, L), pair] * head_lanes[hd % 2]
            kc = k_ref[bi, pl.ds(r0, L), pair]
            vt = vt_ref[bi, sl, pl.ds(r0, L)]
            sc = _dot_nt(kc, qc) * w_intra
            num =_dot(vt, sc.astype(BF16)) + w_inter * _dot_nt(cst.astype(BF16), qc)
            qn = _dot_nt(jnp.broadcast_to(nst, (SUBLANES, LANES)).astype(BF16), qc)[0:1, :]
            den = jnp.sum(sc, axis=0, keepdims=True) + w_inter * qn
            hout = num / jnp.maximum(jnp.abs(den), jnp.exp(-m_t))
            b_last = brow[:, L - 1:L]
            lw = b_last - brow + irow
            m_new = jnp.maximum(b_last + m_prev, jnp.max(lw, axis=1, keepdims=True))
            w_in = jnp.exp(lw - m_new)
            decay = jnp.exp(b_last + m_prev - m_new)
            new_state.append((decay * cst + _dot((vt.astype(F32) * w_in).astype(BF16), kc),
                              decay * nst + _dot(jnp.broadcast_to(w_in, (SUBLANES, L)).astype(BF16), kc)[0:1, :],
                              m_new))
            ms = jnp.mean(hout * hout, axis=0, keepdims=True)
            y = (hout * lax.rsqrt(ms + EPS)).T * (og_ref[bi, pl.ds(r0, L), sl].astype(F32) * ng_ref[:, sl])
            o_ref[bi, pl.ds(r0, L), sl] = y.astype(BF16)
        return tuple(new_state)

    zero = (jnp.zeros((ML_DV, LANES), F32), jnp.zeros((1, LANES), F32), jnp.zeros((1, 1), F32))
    lax.fori_loop(0, s // (L * ML_UNROLL), chunk, tuple(tuple(zero for _ in range(ML_HEADS)) for _ in range(nb)))


ML_SEQS = 2
ML_UNROLL = 2


def mlstm(q, k, vt, og, gates, norm_g):
    b, w, s = vt.shape
    nb = ML_SEQS if b % ML_SEQS == 0 else 1
    gt = jnp.swapaxes(gates[:, :, :2 * ML_HEADS], 1, 2)
    seq = lambda n: pl.BlockSpec((nb, s, n), lambda i: (i, 0, 0))
    return pl.pallas_call(
        _mlstm_kernel,
        grid=(b // nb,),
        in_specs=[seq(ML_HEADS * ML_DQK), seq(ML_HEADS * ML_DQK), pl.BlockSpec((nb, w, s), lambda i: (i, 0, 0)), seq(w), seq(LANES),
                  pl.BlockSpec((nb, 2 * ML_HEADS, s), lambda i: (i, 0, 0)),
                  pl.BlockSpec((1, w), lambda i: (0, 0))],
        out_specs=seq(w),
        out_shape=jax.ShapeDtypeStruct((b, s, w), BF16),
        compiler_params=_cp(("parallel",)),
        name="mlstm",
    )(q, k, vt, og, gates, gt, norm_g.reshape(1, w))


def _hg_proj_kernel(h_ref, wf_ref, wq_ref, wi_ref, wg_ref, lb_ref, q_out, k_out, v_out, og_out, lf_out):
    h = h_ref[...]
    fz = _dot(h, wf_ref[...])
    log_lb = lb_ref[0:1, :]
    log_1m = lb_ref[1:2, :]
    one_m = lb_ref[2:3, :]
    u = jnp.exp(-jnp.abs(fz))
    a = log_lb
    bb = log_1m + (jnp.minimum(fz, 0.0) - jnp.log1p(u))
    lf_out[...] = jnp.maximum(a, bb) + jnp.log1p(jnp.exp(-jnp.abs(a - bb)))
    k_out[...] = (one_m * (jnp.where(fz >= 0.0, u, 1.0) / (1.0 + u))).astype(BF16)
    q_out[...] = _silu(_dot(h, wq_ref[...])).astype(BF16)
    v_out[...] = _dot(h, wi_ref[...]).astype(BF16)
    og_out[...] = _silu(_dot(h, wg_ref[...])).astype(BF16)


def hg_project(h, wf, wq, wi, wg, lb):
    b, s, d = h.shape
    tm = min(PROJ_TILE, s)
    w = BRANCH_WIDTH
    lbp = jnp.stack([jnp.log(lb), jnp.log1p(-lb), 1.0 - lb], axis=0)
    const = lambda shape: pl.BlockSpec(shape, lambda i, j: (0,) * len(shape))
    row = lambda n: pl.BlockSpec((None, tm, n), lambda i, j: (i, j, 0))
    return pl.pallas_call(
        _hg_proj_kernel,
        grid=(b, s // tm),
        in_specs=[row(d), const((d, w)), const((d, w)), const((d, w)), const((d, w)), const((3, w))],
        out_specs=[row(w), row(w), row(w), row(w), row(w)],
        out_shape=[jax.ShapeDtypeStruct((b, s, w), BF16)] * 4 + [jax.ShapeDtypeStruct((b, s, w), F32)],
        compiler_params=_cp(("parallel", "parallel")),
        name="hg_project",
    )(h, wf.astype(BF16), wq.astype(BF16), wi.astype(BF16), wg.astype(BF16), lbp)


HG_UNROLL = 4
HG_LEVELS = tuple(HG_CHUNK >> (i + 1) for i in range(HG_CHUNK.bit_length() - 1))


def _hg_tables():
    L = HG_CHUNK
    t = np.arange(L)
    tri = (t[None, :] <= t[:, None]).astype(np.float32)
    mats = [tri]
    x = t[:, None] ^ t[None, :]
    lvl = np.full((L, L), -1, np.int32)
    lvl[t[:, None] == t[None, :]] = 0
    for i, m in enumerate(HG_LEVELS):
        if m < SUBLANES:
            mats.append(tri[(t // (2 * m)) * (2 * m) + m - 1])
        lvl[(t[:, None] > t[None, :]) & (x >= m) & (x < 2 * m)] = i + 1
    return jnp.asarray(np.concatenate(mats, axis=0), BF16), jnp.asarray(lvl)


def _hgrn_kernel(q_ref, k_ref, v_ref, og_ref, lf_ref, ng_ref, tall_ref, lvl_ref, o_ref, st_ref):
    s, wd = q_ref.shape
    L = HG_CHUNK
    st_ref[...] = jnp.zeros_like(st_ref)
    rowi = lax.broadcasted_iota(I32, (L, LANES), 0)

    lvl = lvl_ref[...]
    level_masks = [lvl == i for i in range(len(HG_LEVELS) + 1)]

    def chunk(c, carry):
        r0 = pl.multiple_of(c * L, L)
        tall = tall_ref[...]
        g = lf_ref[pl.ds(r0, L), :]
        hi = g.astype(BF16)
        mid = (g - hi.astype(F32)).astype(BF16)
        cums = (_dot(tall, hi) + _dot(tall, mid)) * LOG2E
        a = cums[0:L]
        qb = q_ref[pl.ds(r0, L), :]
        kb = k_ref[pl.ds(r0, L), :]
        qf = qb.astype(F32)
        kf = kb.astype(F32)
        ws = []
        fine = 0
        for m in HG_LEVELS:
            if m >= SUBLANES:
                ref = jnp.concatenate([jnp.broadcast_to(a[g0 + m - 1:g0 + m, :], (2 * m, wd)) for g0 in range(0, L, 2 * m)], axis=0)
            else:
                fine += 1
                ref = cums[L * fine:L * (fine + 1)]
            e = jnp.exp2(-jnp.abs(a - ref))
            upper = (rowi & m) != 0
            qk = jnp.concatenate([jnp.where(upper, qf[:, LANES * hd:LANES * (hd + 1)], kf[:, LANES * hd:LANES * (hd + 1)])
                                  for hd in range(HG_HEADS)], axis=1)
            ws.append((e * qk).astype(BF16))
        a_last = a[L - 1:L, :]
        qa = (qf * jnp.exp2(a)).astype(BF16)
        kt = (kf * jnp.exp2(a_last - a)).astype(BF16)
        decay = jnp.exp2(a_last)
        for hd in range(HG_HEADS):
            sl = slice(LANES * hd, LANES * (hd + 1))
            vb = v_ref[pl.ds(r0, L), sl]
            sc = jnp.where(level_masks[0], _dot_nt(qb[:, sl], kb[:, sl]), 0.0)
            for i in range(len(HG_LEVELS)):
                w = ws[i][:, sl]
                sc = jnp.where(level_masks[i + 1], _dot_nt(w, w), sc)
            st = st_ref[hd]
            out = _dot(sc.astype(BF16), vb) + _dot_nt(qa[:, sl], st.astype(BF16))
            st_ref[hd] = st * decay[:, sl] + _dot_tn(vb, kt[:, sl])
            ms = jnp.mean(out * out, axis=-1, keepdims=True)
            y = (out * lax.rsqrt(ms + EPS)) * ng_ref[:, sl] * og_ref[pl.ds(r0, L), sl].astype(F32)
            o_ref[pl.ds(r0, L), sl] = y.astype(BF16)
        return carry

    def chunks(c2, carry):
        for u in range(HG_UNROLL):
            chunk(c2 * HG_UNROLL + u, carry)
        return carry

    lax.fori_loop(0, s // (L * HG_UNROLL), chunks, 0)


def hgrn(q, k, v, og, logf, norm_g):
    b, s, w = v.shape
    tall, lvl = _hg_tables()
    seq = pl.BlockSpec((None, s, w), lambda i: (i, 0, 0))
    const = lambda shape: pl.BlockSpec(shape, lambda i: (0,) * len(shape))
    return pl.pallas_call(
        _hgrn_kernel,
        grid=(b,),
        in_specs=[seq, seq, seq, seq, seq, const((1, w)), const(tall.shape), const(lvl.shape)],
        out_specs=seq,
        out_shape=jax.ShapeDtypeStruct((b, s, w), BF16),
        scratch_shapes=[pltpu.VMEM((HG_HEADS, LANES, HG_DK), F32)],
        compiler_params=_cp(("parallel",)),
        name="hgrn",
    )(q, k, v, og, logf, norm_g.reshape(1, w), tall, lvl)


def _merge_kernel(x_ref, h_ref, yf_ref, ym_ref, yh_ref, wg_ref, wb_ref, wo_ref, g1_ref, n2_ref, sc2_ref, sh2_ref, wr_ref, rb_ref,
                  x_out, h2_out, dest_out, w_out, cnt_out, prev_ref):
    d = x_ref.shape[1]

    @pl.when(pl.program_id(0) == 0)
    def _():
        prev_ref[...] = jnp.zeros_like(prev_ref)

    prev = prev_ref[...]
    picks = [_route_select(prev[SUB_TOKENS * u:SUB_TOKENS * (u + 1), :], wr_ref[...], rb_ref[...])
             for u in range(prev.shape[0] // SUB_TOKENS)]
    h = h_ref[...]
    merged = None
    for br, y_ref in enumerate((yf_ref, ym_ref, yh_ref)):
        gate = jax.nn.sigmoid(_dot(h, wg_ref[:, d * br:d * (br + 1)]))
        term = gate * _dot(y_ref[...], wb_ref[br])
        merged = term if merged is None else merged + term
    mixed = _dot(merged.astype(BF16), wo_ref[...])
    x1 = x_ref[...] + g1_ref[...] * mixed
    x_out[...] = x1
    h2 = _norm_mod(x1, n2_ref[...], sc2_ref[...], sh2_ref[...]).astype(BF16)
    h2_out[...] = h2
    prev_ref[...] = h2
    for u, (scores, sel) in enumerate(picks):
        _route_place(scores, sel, dest_out.at[u], w_out.at[u], cnt_out.at[u])


def merge_branches(x, h, y_fox, y_ml, y_hg, w_gates, w_branch, w_out, g1, norm2_g, sc2, sh2, router_w, router_bias):
    b, s, d = x.shape
    tm = MERGE_TILE
    per_seq = s // tm
    per_tile = tm // SUB_TOKENS
    nsub = b * s // SUB_TOKENS
    w = BRANCH_WIDTH
    n_tiles = b * per_seq
    tile = lambda g: jnp.minimum(g, n_tiles - 1)
    routed = lambda g: jnp.maximum(g - 1, 0)
    mat = pl.BlockSpec((per_tile, 2 * N_EXPERTS, SUB_TOKENS), lambda g: (routed(g), 0, 0))
    const = lambda shape: pl.BlockSpec(shape, lambda g: (0,) * len(shape))
    row = lambda n: pl.BlockSpec((None, tm, n), lambda g: (tile(g) // per_seq, tile(g) % per_seq, 0))
    per_b = pl.BlockSpec((None, 1, d), lambda g: (tile(g) // per_seq, 0, 0))
    x1, h2, dest, wts, cnt = pl.pallas_call(
        _merge_kernel,
        grid=(n_tiles + 1,),
        in_specs=[row(d), row(d), row(w), row(w), row(w), const((d, N_BRANCH * d)), const((N_BRANCH, w, d)), const((d, d)),
                  per_b, const((1, d)), per_b, per_b, const((N_EXPERTS, d)), const((N_EXPERTS, 1))],
        out_specs=[row(d), row(d), mat, mat, pl.BlockSpec((per_tile, N_EXPERTS, LANES), lambda g: (routed(g), 0, 0))],
        out_shape=[jax.ShapeDtypeStruct((b, s, d), F32), jax.ShapeDtypeStruct((b, s, d), BF16),
                   jax.ShapeDtypeStruct((nsub, 2 * N_EXPERTS, SUB_TOKENS), BF16),
                   jax.ShapeDtypeStruct((nsub, 2 * N_EXPERTS, SUB_TOKENS), BF16),
                   jax.ShapeDtypeStruct((nsub, N_EXPERTS, LANES), I32)],
        scratch_shapes=[pltpu.VMEM((tm, d), BF16)],
        compiler_params=_cp(("arbitrary",), MERGE_VMEM_LIMIT_MIB),
        name="merge_branches",
    )(x, h, y_fox, y_ml, y_hg, w_gates.astype(BF16), w_branch.astype(BF16), w_out.astype(BF16),
      g1.reshape(b, 1, d), norm2_g.reshape(1, d), sc2.reshape(b, 1, d), sh2.reshape(b, 1, d),
      router_w.T.astype(BF16), router_bias.reshape(N_EXPERTS, 1))
    return x1, h2, dest, wts, cnt[:, :, 0]


def _first_max(x, iota, size):
    m = jnp.max(x, axis=0, keepdims=True)
    idx = jnp.min(jnp.where(x == m, iota, size), axis=0, keepdims=True)
    return m, idx


def _route_select(h, wr, rb):
    n = h.shape[0]
    scores = jax.nn.sigmoid(_dot_nt(wr, h))
    choice = scores + rb
    e_iota = lax.broadcasted_iota(I32, (N_EXPERTS, n), 0)
    c3 = choice.reshape(N_GROUPS, GROUP_SIZE, n)
    i3 = lax.broadcasted_iota(I32, (N_GROUPS, GROUP_SIZE, n), 1)
    m1 = jnp.max(c3, axis=1, keepdims=True)
    i1 = jnp.min(jnp.where(c3 == m1, i3, GROUP_SIZE), axis=1, keepdims=True)
    m2 = jnp.max(jnp.where(i3 == i1, -jnp.inf, c3), axis=1, keepdims=True)
    gs = (m1 + m2).reshape(N_GROUPS, n)
    g_iota = lax.broadcasted_iota(I32, (N_GROUPS, n), 0)
    gsel = jnp.zeros((N_GROUPS, n), F32)
    for _ in range(TOPK_GROUPS):
        _, gi = _first_max(gs, g_iota, N_GROUPS)
        hit = g_iota == gi
        gsel = jnp.where(hit, 1.0, gsel)
        gs = jnp.where(hit, -jnp.inf, gs)
    gmask = jnp.broadcast_to(gsel.reshape(N_GROUPS, 1, n), (N_GROUPS, GROUP_SIZE, n)).reshape(N_EXPERTS, n)
    masked = jnp.where(gmask > 0.0, choice, -jnp.inf)
    sel = jnp.zeros((N_EXPERTS, n), F32)
    for _ in range(TOP_K):
        _, ei = _first_max(masked, e_iota, N_EXPERTS)
        hit = e_iota == ei
        sel = jnp.where(hit, 1.0, sel)
        masked = jnp.where(hit, -jnp.inf, masked)
    return scores, sel


def _route_place(scores, sel, dest_out, w_out, cnt_out):
    n = scores.shape[1]
    tr = lax.broadcasted_iota(I32, (n, n), 0)
    tc = lax.broadcasted_iota(I32, (n, n), 1)
    before = jnp.where(tr < tc, 1.0, 0.0).astype(BF16)
    pos = _dot(sel.astype(BF16), before)
    cnt = jnp.sum(sel, axis=1, keepdims=True)
    units = jnp.floor((cnt + (RUN_ALIGN - 1)) * (1.0 / RUN_ALIGN))
    er = lax.broadcasted_iota(I32, (N_EXPERTS, N_EXPERTS), 0)
    ec = lax.broadcasted_iota(I32, (N_EXPERTS, N_EXPERTS), 1)
    lower = jnp.where(ec < er, 1.0, 0.0).astype(BF16)
    off = _dot(lower, jnp.broadcast_to(units, (N_EXPERTS, LANES)).astype(BF16))[:, 0:1] * RUN_ALIGN
    dest = jnp.where(sel > 0.0, off + pos, float(DEST_NONE))
    dhi = jnp.floor(dest * (1.0 / DEST_RADIX))
    dest_out[0:N_EXPERTS, :] = dhi.astype(BF16)
    dest_out[N_EXPERTS:, :] = (dest - dhi * DEST_RADIX).astype(BF16)
    wsum = jnp.sum(scores * sel, axis=0, keepdims=True)
    w_out[0:N_EXPERTS, :] = jnp.zeros((N_EXPERTS, n), BF16)
    w_out[N_EXPERTS:, :] = (scores * sel / wsum * ROUTE_SCALE).astype(BF16)
    cnt_out[...] = jnp.broadcast_to(cnt, (N_EXPERTS, LANES)).astype(I32)


def _run_tables(cnt, n_blocks):
    units = (cnt + (RUN_ALIGN - 1)) // RUN_ALIGN
    src = jnp.cumsum(units, axis=1) - units
    per_block = MOE_BLOCK // RUN_ALIGN
    tot = jnp.sum(units, axis=0)
    tot_blocks = (tot + per_block - 1) // per_block
    blk_end = jnp.cumsum(tot_blocks)
    base = (blk_end - tot_blocks) * per_block
    dst = base[None, :] + jnp.cumsum(units, axis=0) - units
    n_used = blk_end[-1]
    tail_units = tot_blocks * per_block - tot
    tail_dst = base + tot
    blk = jnp.minimum(jnp.arange(n_blocks), n_used - 1)
    blk_exp = jnp.minimum(jnp.sum(blk[:, None] >= blk_end[None, :], axis=1), N_EXPERTS - 1)
    return (src.reshape(-1).astype(I32), dst.reshape(-1).astype(I32), units.reshape(-1).astype(I32),
            blk_exp.astype(I32), n_used.astype(I32).reshape(1), tail_dst.astype(I32), tail_units.astype(I32),
            jnp.sum(units, axis=1).astype(I32))


def _copy_lists(src, dst, units, nsub):
    s0 = src.reshape(nsub, 1, N_EXPERTS)
    d0 = dst.reshape(nsub, 1, N_EXPERTS)
    u = units.reshape(nsub, 1, N_EXPERTS)
    npair = u // 2
    poff = jnp.cumsum(npair, axis=2) - npair
    p = jnp.arange(SUB_UNITS // 2, dtype=I32).reshape(1, -1, 1)
    own = (p >= poff) & (p < poff + npair)
    psrc = jnp.sum(jnp.where(own, s0 - 2 * poff, 0), axis=2) + 2 * p[:, :, 0]
    pdst = jnp.sum(jnp.where(own, d0 - 2 * poff, 0), axis=2) + 2 * p[:, :, 0]
    odd = u & 1
    soff = jnp.cumsum(odd, axis=2) - odd
    q = jnp.arange(N_EXPERTS, dtype=I32).reshape(1, -1, 1)
    owns = (odd == 1) & (soff == q)
    ssrc = jnp.sum(jnp.where(owns, s0 + u - 1, 0), axis=2)
    sdst = jnp.sum(jnp.where(owns, d0 + u - 1, 0), axis=2)
    flat = lambda a: a.reshape(-1).astype(I32)
    return (flat(psrc), flat(pdst), flat(jnp.sum(npair, axis=2)), flat(ssrc), flat(sdst), flat(jnp.sum(odd, axis=2)))


def _run_bounds(src, units, nsub):
    lo = (src.reshape(nsub, N_EXPERTS) * RUN_ALIGN).astype(F32)
    hi = lo + (units.reshape(nsub, N_EXPERTS) * RUN_ALIGN).astype(F32)
    return jnp.stack([jnp.concatenate([lo, lo], axis=1), jnp.concatenate([hi, hi], axis=1)], axis=1)


def _max_blocks(t):
    nsub = t // SUB_TOKENS
    worst_units = t * TOP_K // RUN_ALIGN + nsub * N_EXPERTS
    per_block = MOE_BLOCK // RUN_ALIGN
    return -(-worst_units // per_block) + N_EXPERTS


SUB_UNITS = SUB_ROWS // RUN_ALIGN
COPY_UNROLL = 4


PAIR_SLOTS = SUB_UNITS // 2


def _piece_copies(lists, step, buf, hbm, sem, to_hbm):
    psrc_ref, pdst_ref, npair_ref, ssrc_ref, sdst_ref, nsingle_ref = lists

    def piece(src_ref, dst_ref, idx, units):
        rows = units * RUN_ALIGN
        v = buf.at[pl.ds(pl.multiple_of(src_ref[idx] * RUN_ALIGN, RUN_ALIGN), rows)]
        g = hbm.at[pl.ds(pl.multiple_of(dst_ref[idx] * RUN_ALIGN, RUN_ALIGN), rows)]
        cp = pltpu.make_async_copy(v, g, sem) if to_hbm else pltpu.make_async_copy(g, v, sem)
        cp.start()

    def issue(src_ref, dst_ref, base, count, units):
        def group(q, carry):
            for r in range(COPY_UNROLL):
                piece(src_ref, dst_ref, base + q * COPY_UNROLL + r, units)
            return carry

        groups = lax.shift_right_logical(count, COPY_UNROLL.bit_length() - 1)
        lax.fori_loop(0, groups, group, 0)
        lax.fori_loop(groups * COPY_UNROLL, count, lambda j, c: (piece(src_ref, dst_ref, base + j, units), c)[1], 0)

    issue(psrc_ref, pdst_ref, step * PAIR_SLOTS, npair_ref[step], 2)
    issue(ssrc_ref, sdst_ref, step * N_EXPERTS, nsingle_ref[step], 1)


TOTAL_BITS = tuple(1 << b for b in range(SUB_UNITS.bit_length()))


def _wait_runs(total_units, buf, hbm, sem, to_hbm):
    for bit in TOTAL_BITS:
        @pl.when((total_units & bit) != 0)
        def _():
            rows = bit * RUN_ALIGN
            v = buf.at[pl.ds(0, rows)]
            g = hbm.at[pl.ds(0, rows)]
            cp = pltpu.make_async_copy(v, g, sem) if to_hbm else pltpu.make_async_copy(g, v, sem)
            cp.wait()


def _dispatch_kernel(psrc_ref, pdst_ref, npair_ref, ssrc_ref, sdst_ref, nsingle_ref, tot_ref, tdst_ref, tunits_ref, nused_ref,
                     h_ref, dest_ref, lohi_ref, xs_out, buf_ref, zero_ref, sem):
    lists = (psrc_ref, pdst_ref, npair_ref, ssrc_ref, sdst_ref, nsingle_ref)
    i = pl.program_id(0)
    nsub = pl.num_programs(0)
    slot = i % 2
    n = h_ref.shape[0]
    chunk = SORT_CHUNK

    for sl in range(2):
        @pl.when((slot == sl) & (i >= 2))
        def _():
            _wait_runs(tot_ref[i - 2], buf_ref.at[sl], xs_out, sem.at[sl], True)

    h = h_ref[...]
    dest = dest_ref[...]
    lo = lohi_ref[0:1, :]
    hi = lohi_ref[1:2, :]
    radix = jnp.where(lax.broadcasted_iota(I32, (1, 2 * N_EXPERTS), 1) < N_EXPERTS, float(DEST_RADIX), 1.0)
    r_e = lax.broadcasted_iota(I32, (chunk, 2 * N_EXPERTS), 0).astype(F32)
    r_t = lax.broadcasted_iota(I32, (chunk, n), 0).astype(F32)
    for sl in range(2):
        @pl.when(slot == sl)
        def _():
            for c in range(SUB_ROWS // chunk):
                own = jnp.where(r_e + c * chunk >= lo, jnp.where(r_e + c * chunk < hi, radix, 0.0), 0.0)
                row_of = _dot(own.astype(BF16), dest)
                p = jnp.where(row_of == r_t + c * chunk, 1.0, 0.0)
                buf_ref[sl, c * chunk:(c + 1) * chunk, :] = _dot(p.astype(BF16), h).astype(BF16)
            _piece_copies(lists, i, buf_ref.at[sl], xs_out, sem.at[sl], True)

    @pl.when(i == nsub - 1)
    def _():
        zero_ref[...] = jnp.zeros_like(zero_ref)

        def tails(e, wait):
            u = tunits_ref[e]
            d0 = tdst_ref[e]
            for bit in TAIL_BITS:
                low = u & (bit - 1)

                @pl.when((u & bit) != 0)
                def _():
                    rows = bit * RUN_ALIGN
                    cp = pltpu.make_async_copy(zero_ref.at[pl.ds(0, rows)],
                                               xs_out.at[pl.ds(pl.multiple_of((d0 + low) * RUN_ALIGN, RUN_ALIGN), rows)], sem.at[2])
                    if wait:
                        cp.wait()
                    else:
                        cp.start()
            return wait

        def unused(b, wait):
            cp = pltpu.make_async_copy(zero_ref.at[pl.ds(0, MOE_BLOCK)],
                                       xs_out.at[pl.ds(pl.multiple_of(b * MOE_BLOCK, MOE_BLOCK), MOE_BLOCK)], sem.at[2])
            if wait:
                cp.wait()
            else:
                cp.start()
            return wait

        n_blocks = xs_out.shape[0] // MOE_BLOCK
        lax.fori_loop(0, N_EXPERTS, lambda e, c: (tails(e, False), c)[1], 0)
        lax.fori_loop(nused_ref[0], n_blocks, lambda b, c: (unused(b, False), c)[1], 0)
        for sl in range(2):
            @pl.when((slot != sl) & (i >= 1))
            def _():
                _wait_runs(tot_ref[i - 1], buf_ref.at[sl], xs_out, sem.at[sl], True)

            @pl.when(slot == sl)
            def _():
                _wait_runs(tot_ref[i], buf_ref.at[sl], xs_out, sem.at[sl], True)
        lax.fori_loop(0, N_EXPERTS, lambda e, c: (tails(e, True), c)[1], 0)
        lax.fori_loop(nused_ref[0], n_blocks, lambda b, c: (unused(b, True), c)[1], 0)


def moe_dispatch(h2, dest, tables, n_rows):
    t, d = h2.shape
    n = SUB_TOKENS
    nsub = t // n
    src, dst, units, _, n_used, tail_dst, tail_units, tot = tables
    grid_spec = pltpu.PrefetchScalarGridSpec(
        num_scalar_prefetch=10,
        grid=(nsub,),
        in_specs=[pl.BlockSpec((n, d), lambda i, *_: (i, 0)), pl.BlockSpec((None, 2 * N_EXPERTS, n), lambda i, *_: (i, 0, 0)),
                  pl.BlockSpec((None, 2, 2 * N_EXPERTS), lambda i, *_: (i, 0, 0))],
        out_specs=pl.BlockSpec(memory_space=pl.ANY),
        scratch_shapes=[pltpu.VMEM((2, SUB_ROWS, d), BF16), pltpu.VMEM((max(SUB_TOKENS, MOE_BLOCK), d), BF16), pltpu.SemaphoreType.DMA((3,))],
    )
    return pl.pallas_call(
        _dispatch_kernel,
        grid_spec=grid_spec,
        out_shape=jax.ShapeDtypeStruct((n_rows, d), BF16),
        compiler_params=_cp(("arbitrary",)),
        name="moe_dispatch",
    )(*_copy_lists(src, dst, units, nsub), tot, tail_dst, tail_units, n_used, h2, dest, _run_bounds(src, units, nsub))


X_SLOTS = 3


def _expert_kernel(blk_exp_ref, n_used_ref, x_hbm, w1_ref, w3_ref, w2_ref, y_ref, xbuf, sem):
    b = pl.program_id(0)
    n_used = n_used_ref[0]

    def x_copy(blk, slot):
        rows = pl.ds(pl.multiple_of(blk * MOE_BLOCK, MOE_BLOCK), MOE_BLOCK)
        return pltpu.make_async_copy(x_hbm.at[rows], xbuf.at[slot], sem.at[slot])

    @pl.when(b == 0)
    def _():
        x_copy(0, 0).start()

        @pl.when(n_used > 1)
        def _():
            x_copy(1, 1).start()

    ahead = b + (X_SLOTS - 1)

    @pl.when(ahead < n_used)
    def _():
        x_copy(ahead, lax.rem(ahead, X_SLOTS)).start()

    @pl.when(b < n_used)
    def _():
        slot = lax.rem(b, X_SLOTS)
        x_copy(b, slot).wait()
        x = xbuf[slot]
        hid = _silu(_dot(x, w1_ref[...].astype(BF16))) * _dot(x, w3_ref[...].astype(BF16))
        y_ref[...] = _dot(hid.astype(BF16), w2_ref[...].astype(BF16)).astype(BF16)

    @pl.when(b >= n_used)
    def _():
        y_ref[...] = jnp.zeros_like(y_ref)


def moe_experts(xs, w1, w3, w2, layer, tables, n_blocks):
    n_rows, d = xs.shape
    blk_exp, n_used = tables[3], tables[4]
    f = w1.shape[-1]

    def w_map(b, be, nu):
        return (layer, be[b], 0, 0)

    grid_spec = pltpu.PrefetchScalarGridSpec(
        num_scalar_prefetch=2,
        grid=(n_blocks,),
        in_specs=[pl.BlockSpec(memory_space=pl.ANY), pl.BlockSpec((None, None, d, f), w_map),
                  pl.BlockSpec((None, None, d, f), w_map), pl.BlockSpec((None, None, f, d), w_map)],
        out_specs=pl.BlockSpec((MOE_BLOCK, d), lambda b, be, nu: (b, 0)),
        scratch_shapes=[pltpu.VMEM((X_SLOTS, MOE_BLOCK, d), BF16), pltpu.SemaphoreType.DMA((X_SLOTS,))],
    )
    return pl.pallas_call(
        _expert_kernel,
        grid_spec=grid_spec,
        out_shape=jax.ShapeDtypeStruct((n_rows, d), BF16),
        compiler_params=_cp(("arbitrary",)),
        name="moe_experts",
    )(blk_exp, n_used, xs, w1, w3, w2)


def _combine_kernel(psrc_ref, pdst_ref, npair_ref, ssrc_ref, sdst_ref, nsingle_ref, tot_ref, ys_ref, dcol_ref, lohi_ref,
                    h_ref, x_ref, g2_ref, ws1_ref, ws3_ref, ws2_ref, o_ref, buf_ref, sem):
    lists = (psrc_ref, pdst_ref, npair_ref, ssrc_ref, sdst_ref, nsingle_ref)
    i = pl.program_id(0)
    nsub = pl.num_programs(0)
    slot = i % 2
    n = h_ref.shape[0]
    chunk = COMBINE_CHUNK

    def fetch(step, sl):
        always = SUB_TOKENS * TOP_K
        buf_ref[sl, always:, :] = jnp.zeros((SUB_ROWS - always, buf_ref.shape[2]), BF16)
        _piece_copies(lists, step, buf_ref.at[sl], ys_ref, sem.at[sl], False)

    @pl.when(i == 0)
    def _():
        fetch(0, 0)

    for sl in range(2):
        @pl.when((slot != sl) & (i + 1 < nsub))
        def _():
            fetch(i + 1, sl)

    h = h_ref[...]
    shared = _dot((_silu(_dot(h, ws1_ref[...])) * _dot(h, ws3_ref[...])).astype(BF16), ws2_ref[...])
    dw_t = dcol_ref[...]
    lo = lohi_ref[:, 0:1]
    hi = lohi_ref[:, 1:2]
    part = lax.broadcasted_iota(I32, (4 * N_EXPERTS, 1), 0)
    radix = jnp.where(part < N_EXPERTS, float(DEST_RADIX),
                      jnp.where(part < 2 * N_EXPERTS, 1.0, jnp.where(part < 3 * N_EXPERTS, WEIGHT_SHIFT, 0.0)))
    r_e = lax.broadcasted_iota(I32, (4 * N_EXPERTS, chunk), 1).astype(F32)
    r_t = lax.broadcasted_iota(I32, (n, chunk), 1).astype(F32)
    for sl in range(2):
        @pl.when(slot == sl)
        def _():
            _wait_runs(tot_ref[i], buf_ref.at[sl], ys_ref, sem.at[sl], False)
            acc = shared
            for c in range(SUB_ROWS // chunk):
                own = jnp.where(r_e + c * chunk >= lo, jnp.where(r_e + c * chunk < hi, radix, 0.0), 0.0).astype(BF16)
                val = _dot(dw_t, own)
                row_of = jnp.floor(val)
                pw = jnp.where(row_of == r_t + c * chunk, (val - row_of) * (1.0 / WEIGHT_SHIFT), 0.0)
                acc = acc + _dot(pw.astype(BF16), buf_ref[sl, c * chunk:(c + 1) * chunk, :])
            o_ref[...] = x_ref[...] + g2_ref[...] * acc


def moe_combine(ys, dest, wts, h2, x1, g2, ws1, ws3, ws2, tables, seq):
    t, d = h2.shape
    n = SUB_TOKENS
    nsub = t // n
    src, dst, units, tot = tables[0], tables[1], tables[2], tables[7]
    dw = jnp.concatenate([dest, wts[:, N_EXPERTS:, :], jnp.zeros((nsub, N_EXPERTS, n), BF16)], axis=1)
    dcol = jnp.swapaxes(dw, 1, 2)
    b2 = _run_bounds(src, units, nsub)
    bounds = jnp.swapaxes(jnp.concatenate([b2, b2], axis=2), 1, 2)
    per_seq = seq // n
    f = ws1.shape[-1]
    pair = pl.BlockSpec((None, n, 4 * N_EXPERTS), lambda i, *_: (i, 0, 0))
    grid_spec = pltpu.PrefetchScalarGridSpec(
        num_scalar_prefetch=7,
        grid=(nsub,),
        in_specs=[pl.BlockSpec(memory_space=pl.ANY), pair,
                  pl.BlockSpec((None, 4 * N_EXPERTS, 2), lambda i, *_: (i, 0, 0)),
                  pl.BlockSpec((n, d), lambda i, *_: (i, 0)), pl.BlockSpec((n, d), lambda i, *_: (i, 0)),
                  pl.BlockSpec((None, 1, d), lambda i, *_: (i // per_seq, 0, 0)),
                  pl.BlockSpec((d, f), lambda i, *_: (0, 0)), pl.BlockSpec((d, f), lambda i, *_: (0, 0)),
                  pl.BlockSpec((f, d), lambda i, *_: (0, 0))],
        out_specs=pl.BlockSpec((n, d), lambda i, *_: (i, 0)),
        scratch_shapes=[pltpu.VMEM((2, SUB_ROWS, d), BF16), pltpu.SemaphoreType.DMA((2,))],
    )
    return pl.pallas_call(
        _combine_kernel,
        grid_spec=grid_spec,
        out_shape=jax.ShapeDtypeStruct((t, d), F32),
        compiler_params=_cp(("arbitrary",)),
        name="moe_combine",
    )(*_copy_lists(src, dst, units, nsub), tot, ys, dcol, bounds, h2, x1, g2, ws1.astype(BF16), ws3.astype(BF16), ws2.astype(BF16))


def _split_w_in(w):
    fh = FOX_HEADS * FOX_HEAD_DIM
    sizes = (fh, fh, fh, FOX_HEADS,
             2 * ML_HEADS * ML_DQK, ML_HEADS * ML_DV, ML_HEADS, ML_HEADS, ML_HEADS * ML_DV,
             HG_HEADS * HG_DK, HG_HEADS * HG_DK, BRANCH_WIDTH, BRANCH_WIDTH,
             N_BRANCH * D_MODEL)
    outs, o = [], 0
    for sz in sizes:
        outs.append(w[:, o:o + sz])
        o += sz
    return outs


def moe_ffn(h2, x1, g2, dest, wts, cnt, w1, w3, w2, layer, ws1, ws3, ws2, seq):
    t, d = h2.shape
    n_blocks = _max_blocks(t)
    tables = _run_tables(cnt, n_blocks)
    xs = moe_dispatch(h2, dest, tables, n_blocks * MOE_BLOCK)
    ys = moe_experts(xs, w1, w3, w2, layer, tables, n_blocks)
    return moe_combine(ys, dest, wts, h2, x1, g2, ws1, ws3, ws2, tables, seq)


def kernel(x, c, ada_w, ada_b, norm1_g, norm2_g, w_in, fox_bf, fox_q_g, fox_k_g, mlstm_conv, mlstm_bi, mlstm_bf, mlstm_norm_g, hgrn_lower_bounds, hgrn_norm_g, w_branch, w_out, router_w, router_bias, exp_w1, exp_w3, exp_w2, sh_w1, sh_w3, sh_w2):
    b, s, d = x.shape
    depth = ada_w.shape[0]
    mod = adaln_mod(c, ada_w, ada_b)
    lb_all = jnp.cumsum(jax.nn.softmax(hgrn_lower_bounds.astype(F32), axis=0), axis=0)
    lb_all = lb_all - lb_all[0]
    for l in range(depth):
        sh1, sc1, g1, sh2, sc2, g2 = [mod[l][:, d * j:d * (j + 1)] for j in range(6)]
        (wfq, wfk, wfv, wff, wmqk, wmv, wmi, wmf, wmo, whf, whq, whi, whg, wgates) = _split_w_in(w_in[l])
        qp, kp, fv, h = fox_project(x, norm1_g[l], sc1, sh1, wfq, wfk, wfv, wff, fox_q_g[l], fox_k_g[l], fox_bf[l])
        y_fox = fox_attention(qp, kp, fv)
        mq, mk, mv, mog, mgates = ml_project(h, wmqk, wmv, wmi, wmf, wmo, mlstm_conv[l], mlstm_bi[l], mlstm_bf[l])
        y_ml = mlstm(mq, mk, mv, mog, mgates, mlstm_norm_g[l])
        hq, hk, hv, hog, hlf = hg_project(h, whf, whq, whi, whg, lb_all[l])
        y_hg = hgrn(hq, hk, hv, hog, hlf, hgrn_norm_g[l])
        x1, h2, dest, wts, cnt = merge_branches(x, h, y_fox, y_ml, y_hg, wgates, w_branch[l], w_out[l], g1, norm2_g[l], sc2, sh2,
                                                router_w[l], router_bias[l])
        x = moe_ffn(h2.reshape(b * s, d), x1.reshape(b * s, d), g2.reshape(b, 1, d), dest, wts, cnt,
                    exp_w1, exp_w3, exp_w2, l, sh_w1[l], sh_w3[l], sh_w2[l], s).reshape(b, s, d)
    return x
```
